```python
import functools
import jax, jax.numpy as jnp
from jax import lax
import numpy as np

D_MODEL = 2048
BATCH = 4
SEQ = 4096
DEPTH = 1
DEC_BATCH = 128
DEC_SEQ = 8
PAST_LEN = 16384
PAGE_SIZE = 128

ATTN_HEADS = 16
ATTN_KV_HEADS = 4
ATTN_HEAD_DIM = 64
ATTN_GROUP = ATTN_HEADS // ATTN_KV_HEADS
WINDOW = 128
ATTN_WIDTH = ATTN_HEADS * ATTN_HEAD_DIM
KV_WIDTH = ATTN_KV_HEADS * ATTN_HEAD_DIM
HGRN_HEADS = 8
HGRN_DK = 128
HGRN_DV = 128
HGRN_KW = HGRN_HEADS * HGRN_DK
HGRN_VW = HGRN_HEADS * HGRN_DV
HGRN_CHUNK = 64
MIX_WIDTH = ATTN_WIDTH + HGRN_VW
IN_PROJ_WIDTH = ATTN_WIDTH + 2 * KV_WIDTH + 2 * HGRN_KW + 2 * HGRN_VW
IN_PROJ_SPLITS = (ATTN_WIDTH,
                  ATTN_WIDTH + KV_WIDTH,
                  ATTN_WIDTH + 2 * KV_WIDTH,
                  ATTN_WIDTH + 2 * KV_WIDTH + HGRN_KW,
                  ATTN_WIDTH + 2 * KV_WIDTH + 2 * HGRN_KW,
                  ATTN_WIDTH + 2 * KV_WIDTH + 2 * HGRN_KW + HGRN_VW)
MEM_TOKENS = 256
MEM_HEADS = 4
MEM_HEAD_DIM = 128
MEM_WIDTH = MEM_HEADS * MEM_HEAD_DIM
D_FF = 5632
FFN_RESIDUAL = 0.5
EPS = 1e-6

kernel_name = 'hymba_swa_sink_hgrn2_macaron_memxattn_step'


def rmsnorm(x, g):
    xf = x.astype(jnp.float32)
    y = xf * lax.rsqrt(jnp.mean(xf * xf, axis=-1, keepdims=True) + EPS)
    return (y * g.astype(jnp.float32)).astype(x.dtype)


def half_ffn(x, g_pre, g_post, w_gate, w_up, w_down):
    h = rmsnorm(x, g_pre)
    f = (jax.nn.silu(h @ w_gate) * (h @ w_up)) @ w_down
    return x + FFN_RESIDUAL * rmsnorm(f, g_post)


def alibi_slopes():
    return 2.0 ** (-8.0 * jnp.arange(1, ATTN_HEADS + 1, dtype=jnp.float32) / ATTN_HEADS)


def sink_attention(q, k, v, qpos, kpos, sinks):
    s = jnp.einsum('bnqhgd,bnkhd->bnhgqk', q, k).astype(jnp.float32) * (ATTN_HEAD_DIM ** -0.5)
    dist = qpos[:, :, None] - kpos[:, None, :]
    valid = (kpos[:, None, :] >= 0) & (dist >= 0) & (dist < WINDOW)
    slopes = alibi_slopes().reshape(ATTN_KV_HEADS, ATTN_GROUP)[:, :, None, None]
    s = s - slopes * dist[None, :, None, None].astype(jnp.float32)
    s = jnp.where(valid[None, :, None, None], s, -jnp.inf)
    sink = sinks.astype(jnp.float32).reshape(ATTN_KV_HEADS, ATTN_GROUP)[:, :, None, None]
    m = jnp.maximum(jnp.max(s, axis=-1, keepdims=True), sink)
    p = jnp.exp(s - m)
    p = p / (jnp.sum(p, axis=-1, keepdims=True) + jnp.exp(sink - m))
    return jnp.einsum('bnhgqk,bnkhd->bnqhgd', p.astype(v.dtype), v)


def window_attention_prompt(q, k, v, sinks):
    B, S = q.shape[:2]
    nb = S // WINDOW
    qb = q.reshape(B, nb, WINDOW, ATTN_KV_HEADS, ATTN_GROUP, ATTN_HEAD_DIM)
    kb = k.reshape(B, nb, WINDOW, ATTN_KV_HEADS, ATTN_HEAD_DIM)
    vb = v.reshape(B, nb, WINDOW, ATTN_KV_HEADS, ATTN_HEAD_DIM)

    def band(a):
        prev = jnp.concatenate([jnp.zeros_like(a[:, :1]), a[:, :-1]], axis=1)
        return jnp.concatenate([prev, a], axis=2)

    start = jnp.arange(nb)[:, None] * WINDOW
    qpos = start + jnp.arange(WINDOW)[None, :]
    kpos = start - WINDOW + jnp.arange(2 * WINDOW)[None, :]
    o = sink_attention(qb, band(kb), band(vb), qpos, kpos, sinks)
    return o.reshape(B, S, ATTN_WIDTH), k[:, -WINDOW:], v[:, -WINDOW:]


def window_attention_sample(q, k, v, sinks, cache_k, cache_v):
    B, T = q.shape[:2]
    kk = jnp.concatenate([cache_k, k], axis=1)
    vv = jnp.concatenate([cache_v, v], axis=1)
    qpos = (PAST_LEN + jnp.arange(T))[None, :]
    kpos = (PAST_LEN - WINDOW + jnp.arange(WINDOW + T))[None, :]
    o = sink_attention(q[:, None], kk[:, None], vv[:, None], qpos, kpos, sinks)
    return o.reshape(B, T, ATTN_WIDTH), kk[:, -WINDOW:], vv[:, -WINDOW:]


def hgrn2_features(q_raw, f_raw, i_raw, lower_bound):
    B, T = q_raw.shape[:2]
    shp = (B, T, HGRN_HEADS, HGRN_DK)
    fr = f_raw.astype(jnp.float32).reshape(shp)
    lb = lower_bound.reshape(HGRN_HEADS, HGRN_DK)
    logf = jnp.log(lb + (1.0 - lb) * jax.nn.sigmoid(fr))
    k = (1.0 - lb) * jax.nn.sigmoid(-fr)
    q = jax.nn.silu(q_raw.astype(jnp.float32)).reshape(shp)
    v = i_raw.astype(jnp.float32).reshape(B, T, HGRN_HEADS, HGRN_DV)
    return q, k, v, logf


def hgrn2_chunked(q, k, v, logf, S0, chunk):
    B, T = q.shape[:2]
    n = T // chunk

    def to_chunks(a):
        return jnp.moveaxis(a.reshape(B, n, chunk, *a.shape[2:]), 1, 0)

    causal = jnp.tril(jnp.ones((chunk, chunk), dtype=bool))[None, :, :, None, None]

    def step(S, inp):
        qc, kc, vc, lc = inp
        L = jnp.cumsum(lc, axis=1)
        diff = L[:, :, None] - L[:, None, :]
        decay = jnp.exp(jnp.where(causal, diff, -jnp.inf))
        A = jnp.einsum('bthk,btshk,bshk->bhts', qc, decay, kc)
        o = jnp.einsum('bhts,bshv->bthv', A, vc) + jnp.einsum('bthk,bhkv->bthv', qc * jnp.exp(L), S)
        L_end = L[:, -1]
        S_new = jnp.exp(L_end)[..., None] * S + jnp.einsum(
            'bshk,bshv->bhkv', kc * jnp.exp(L_end[:, None] - L), vc)
        return S_new, o

    S_fin, o = lax.scan(step, S0.astype(jnp.float32),
                        (to_chunks(q), to_chunks(k), to_chunks(v), to_chunks(logf)))
    o = jnp.moveaxis(o, 0, 1).reshape(B, T, HGRN_HEADS, HGRN_DV)
    return o, S_fin


def memory_kv(mem, g, w_k, w_v):
    B = mem.shape[0]
    m = rmsnorm(mem, g)
    k = (m @ w_k).reshape(B, -1, MEM_HEADS, MEM_HEAD_DIM)
    v = (m @ w_v).reshape(B, -1, MEM_HEADS, MEM_HEAD_DIM)
    return k, v


def memory_attention(h, mem_k, mem_v, w_q, w_o):
    B, T = h.shape[:2]
    q = (h @ w_q).reshape(B, T, MEM_HEADS, MEM_HEAD_DIM)
    s = jnp.einsum('bthd,bmhd->bhtm', q, mem_k).astype(jnp.float32) * (MEM_HEAD_DIM ** -0.5)
    p = jax.nn.softmax(s, axis=-1).astype(mem_v.dtype)
    o = jnp.einsum('bhtm,bmhd->bthd', p, mem_v).reshape(B, T, MEM_WIDTH)
    return o @ w_o


def trunk_layer(x, mem_k, mem_v, window_mixer, recurrent_mixer, p):
    x = half_ffn(x, p['ffn1_norm_pre'], p['ffn1_norm_post'], p['ffn1_w_gate'], p['ffn1_w_up'], p['ffn1_w_down'])
    B, T = x.shape[:2]
    h = rmsnorm(x, p['mix_norm_pre'])
    q_a, k_a, v_a, q_h, f_h, i_h, g_h = jnp.split(h @ p['w_in'], IN_PROJ_SPLITS, axis=-1)
    q_a = q_a.reshape(B, T, ATTN_KV_HEADS, ATTN_GROUP, ATTN_HEAD_DIM)
    k_a = k_a.reshape(B, T, ATTN_KV_HEADS, ATTN_HEAD_DIM)
    v_a = v_a.reshape(B, T, ATTN_KV_HEADS, ATTN_HEAD_DIM)
    o_a, win_k, win_v = window_mixer(q_a, k_a, v_a, p['attn_sinks'])
    hq, hk, hv, hlogf = hgrn2_features(q_h, f_h, i_h, p['lower_bound'])
    o_h, S = recurrent_mixer(hq, hk, hv, hlogf)
    a = rmsnorm(o_a, p['attn_out_gain'])
    r = rmsnorm(o_h.astype(x.dtype), p['hgrn_out_gain']).reshape(B, T, HGRN_VW) * jax.nn.silu(g_h)
    mixed = jnp.concatenate([a, r], axis=-1) @ p['w_out']
    x = x + rmsnorm(mixed, p['mix_norm_post'])
    h = rmsnorm(x, p['mem_norm_pre'])
    x = x + rmsnorm(memory_attention(h, mem_k, mem_v, p['w_mem_q'], p['w_mem_o']), p['mem_norm_post'])
    x = half_ffn(x, p['ffn2_norm_pre'], p['ffn2_norm_post'], p['ffn2_w_gate'], p['ffn2_w_up'], p['ffn2_w_down'])
    return x, win_k, win_v, S


def setup_inputs(seed: int = 0) -> dict:
    key = jax.random.key(seed)
    keys = iter(jax.random.split(key, 40))

    def nrm(shape, scale):
        return scale * jax.random.normal(next(keys), shape, jnp.float32)

    def gain(width):
        return 1.0 + 0.01 * nrm((DEPTH, width), 1.0)

    return {
        'x_prompt': nrm((BATCH, SEQ, D_MODEL), 1.0),
        'x_sample': nrm((DEC_BATCH, DEC_SEQ, D_MODEL), 1.0),
        'mem_prompt': nrm((BATCH, MEM_TOKENS, D_MODEL), 1.0),
        'cache_win_k': nrm((DEPTH, DEC_BATCH, WINDOW, ATTN_KV_HEADS, ATTN_HEAD_DIM), 1.0),
        'cache_win_v': nrm((DEPTH, DEC_BATCH, WINDOW, ATTN_KV_HEADS, ATTN_HEAD_DIM), 1.0),
        'state_hgrn': nrm((DEPTH, DEC_BATCH, HGRN_HEADS, HGRN_DK, HGRN_DV), 0.5),
        'cache_mem_k': nrm((DEPTH, DEC_BATCH, MEM_TOKENS, MEM_HEADS, MEM_HEAD_DIM), 1.0),
        'cache_mem_v': nrm((DEPTH, DEC_BATCH, MEM_TOKENS, MEM_HEADS, MEM_HEAD_DIM), 1.0),
        'ffn1_norm_pre': gain(D_MODEL),
        'ffn1_norm_post': gain(D_MODEL),
        'ffn1_w_gate': nrm((DEPTH, D_MODEL, D_FF), D_MODEL ** -0.5),
        'ffn1_w_up': nrm((DEPTH, D_MODEL, D_FF), D_MODEL ** -0.5),
        'ffn1_w_down': nrm((DEPTH, D_FF, D_MODEL), D_FF ** -0.5),
        'mix_norm_pre': gain(D_MODEL),
        'mix_norm_post': gain(D_MODEL),
        'w_in': nrm((DEPTH, D_MODEL, IN_PROJ_WIDTH), D_MODEL ** -0.5),
        'attn_sinks': nrm((DEPTH, ATTN_HEADS), 1.0),
        'hgrn_lb_logits': nrm((DEPTH + 1, HGRN_KW), 0.5),
        'attn_out_gain': gain(ATTN_WIDTH),
        'hgrn_out_gain': gain(HGRN_DV),
        'w_out': nrm((DEPTH, MIX_WIDTH, D_MODEL), MIX_WIDTH ** -0.5),
        'mem_norm_pre': gain(D_MODEL),
        'mem_norm_post': gain(D_MODEL),
        'mem_norm_kv': gain(D_MODEL),
        'w_mem_q': nrm((DEPTH, D_MODEL, MEM_WIDTH), D_MODEL ** -0.5),
        'w_mem_k': nrm((DEPTH, D_MODEL, MEM_WIDTH), D_MODEL ** -0.5),
        'w_mem_v': nrm((DEPTH, D_MODEL, MEM_WIDTH), D_MODEL ** -0.5),
        'w_mem_o': nrm((DEPTH, MEM_WIDTH, D_MODEL), MEM_WIDTH ** -0.5),
        'ffn2_norm_pre': gain(D_MODEL),
        'ffn2_norm_post': gain(D_MODEL),
        'ffn2_w_gate': nrm((DEPTH, D_MODEL, D_FF), D_MODEL ** -0.5),
        'ffn2_w_up': nrm((DEPTH, D_MODEL, D_FF), D_MODEL ** -0.5),
        'ffn2_w_down': nrm((DEPTH, D_FF, D_MODEL), D_FF ** -0.5),
    }


def reference(x_prompt, x_sample, mem_prompt, cache_win_k, cache_win_v, state_hgrn, cache_mem_k, cache_mem_v,
              ffn1_norm_pre, ffn1_norm_post, ffn1_w_gate, ffn1_w_up, ffn1_w_down,
              mix_norm_pre, mix_norm_post, w_in, attn_sinks, hgrn_lb_logits, attn_out_gain, hgrn_out_gain, w_out,
              mem_norm_pre, mem_norm_post, mem_norm_kv, w_mem_q, w_mem_k, w_mem_v, w_mem_o,
              ffn2_norm_pre, ffn2_norm_post, ffn2_w_gate, ffn2_w_up, ffn2_w_down):
    lower_bounds = jnp.cumsum(jax.nn.softmax(hgrn_lb_logits.astype(jnp.float32), axis=0), axis=0)
    y_p = x_prompt
    y_s = x_sample
    p_wk, p_wv, p_S, p_mk, p_mv, s_wk, s_wv, s_S = [], [], [], [], [], [], [], []
    for l in range(DEPTH):
        p = dict(ffn1_norm_pre=ffn1_norm_pre[l], ffn1_norm_post=ffn1_norm_post[l], ffn1_w_gate=ffn1_w_gate[l],
                 ffn1_w_up=ffn1_w_up[l], ffn1_w_down=ffn1_w_down[l],
                 mix_norm_pre=mix_norm_pre[l], mix_norm_post=mix_norm_post[l], w_in=w_in[l],
                 attn_sinks=attn_sinks[l], lower_bound=lower_bounds[l], attn_out_gain=attn_out_gain[l],
                 hgrn_out_gain=hgrn_out_gain[l], w_out=w_out[l],
                 mem_norm_pre=mem_norm_pre[l], mem_norm_post=mem_norm_post[l], w_mem_q=w_mem_q[l], w_mem_o=w_mem_o[l],
                 ffn2_norm_pre=ffn2_norm_pre[l], ffn2_norm_post=ffn2_norm_post[l], ffn2_w_gate=ffn2_w_gate[l],
                 ffn2_w_up=ffn2_w_up[l], ffn2_w_down=ffn2_w_down[l])
        mk, mv = memory_kv(mem_prompt, mem_norm_kv[l], w_mem_k[l], w_mem_v[l])
        S0_p = jnp.zeros((x_prompt.shape[0], HGRN_HEADS, HGRN_DK, HGRN_DV), jnp.float32)
        y_p, wk, wv, S = trunk_layer(y_p, mk, mv, window_attention_prompt,
                                     functools.partial(hgrn2_chunked, S0=S0_p, chunk=HGRN_CHUNK), p)
        p_wk.append(wk)
        p_wv.append(wv)
        p_S.append(S)
        p_mk.append(mk)
        p_mv.append(mv)
        y_s, wk, wv, S = trunk_layer(
            y_s, cache_mem_k[l], cache_mem_v[l],
            functools.partial(window_attention_sample, cache_k=cache_win_k[l], cache_v=cache_win_v[l]),
            functools.partial(hgrn2_chunked, S0=state_hgrn[l], chunk=x_sample.shape[1]), p)
        s_wk.append(wk)
        s_wv.append(wv)
        s_S.append(S)
    return (y_p, y_s, jnp.stack(p_wk), jnp.stack(p_wv), jnp.stack(p_S), jnp.stack(p_mk), jnp.stack(p_mv),
            jnp.stack(s_wk), jnp.stack(s_wv), jnp.stack(s_S))
```

```python
import functools

import jax
import jax.numpy as jnp
from jax import lax
from jax.experimental import pallas as pl
from jax.experimental.pallas import tpu as pltpu

F32 = jnp.float32
BF16 = jnp.bfloat16

D_MODEL = 2048
D_FF = 5632
ATTN_HEADS = 16
ATTN_KV_HEADS = 4
ATTN_GROUP = ATTN_HEADS // ATTN_KV_HEADS
ATTN_HEAD_DIM = 64
WINDOW = 128
ATTN_WIDTH = ATTN_HEADS * ATTN_HEAD_DIM
KV_WIDTH = ATTN_KV_HEADS * ATTN_HEAD_DIM
HGRN_HEADS = 8
HGRN_DK = 128
HGRN_DV = 128
HGRN_KW = HGRN_HEADS * HGRN_DK
HGRN_VW = HGRN_HEADS * HGRN_DV
IN_PROJ_WIDTH = ATTN_WIDTH + 2 * KV_WIDTH + 2 * HGRN_KW + 2 * HGRN_VW
MEM_HEADS = 4
MEM_HEAD_DIM = 128
MEM_WIDTH = MEM_HEADS * MEM_HEAD_DIM
FFN_RESIDUAL = 0.5
EPS = 1e-6
PAST_LEN = 16384

Z_QA, Z_QH, Z_FH, Z_IH, Z_GH = 0, 1, 2, 3, 4
Z_KA, Z_VA = 20, 21

VMEM_LIMIT_BYTES = 56 * 1024 * 1024

TOKEN_TILE = 512
FF_TILE = 512
HGRN_CHUNK = 64
HGRN_SUB = 16
MAX_EXP = 80.0

ALIBI_SLOPES = tuple(2.0 ** (-8.0 * (h + 1) / ATTN_HEADS) for h in range(ATTN_HEADS))


def _params(*semantics):
    return pltpu.CompilerParams(dimension_semantics=semantics,
                                vmem_limit_bytes=VMEM_LIMIT_BYTES)


def _rms(x, g):
    return x * lax.rsqrt(jnp.mean(x * x, axis=-1, keepdims=True) + EPS) * g


def _silu(x):
    return x * jax.nn.sigmoid(x)


def _dot(a, b):
    return jnp.dot(a, b, preferred_element_type=F32)


def _dot_nt(a, b):
    return lax.dot_general(a, b, (((1,), (1,)), ((), ())), preferred_element_type=F32)


def _dot_tn(a, b):
    return lax.dot_general(a, b, (((0,), (0,)), ((), ())), preferred_element_type=F32)


def _ffn_kernel(x_ref, gpre_ref, gpost_ref, wg_ref, wu_ref, wd_ref, o_ref, h_scr, acc_scr):
    j = pl.program_id(1)

    @pl.when(j == 0)
    def _():
        h_scr[...] = _rms(x_ref[...], gpre_ref[...]).astype(BF16)
        acc_scr[...] = jnp.zeros_like(acc_scr)

    h = h_scr[...]
    g = _dot(h, wg_ref[...])
    u = _dot(h, wu_ref[...])
    a = (_silu(g) * u).astype(BF16)
    acc_scr[...] += _dot(a, wd_ref[...])

    @pl.when(j == pl.num_programs(1) - 1)
    def _():
        o_ref[...] = x_ref[...] + FFN_RESIDUAL * _rms(acc_scr[...], gpost_ref[...])


def _ffn(x, g_pre, g_post, wg, wu, wd):
    t = x.shape[0]
    tm, tf = TOKEN_TILE, FF_TILE
    return pl.pallas_call(
        _ffn_kernel,
        grid=(t // tm, D_FF // tf),
        in_specs=[
            pl.BlockSpec((tm, D_MODEL), lambda i, j: (i, 0)),
            pl.BlockSpec((1, D_MODEL), lambda i, j: (0, 0)),
            pl.BlockSpec((1, D_MODEL), lambda i, j: (0, 0)),
            pl.BlockSpec((D_MODEL, tf), lambda i, j: (0, j)),
            pl.BlockSpec((D_MODEL, tf), lambda i, j: (0, j)),
            pl.BlockSpec((tf, D_MODEL), lambda i, j: (j, 0)),
        ],
        out_specs=pl.BlockSpec((tm, D_MODEL), lambda i, j: (i, 0)),
        out_shape=jax.ShapeDtypeStruct((t, D_MODEL), F32),
        scratch_shapes=[pltpu.VMEM((tm, D_MODEL), BF16), pltpu.VMEM((tm, D_MODEL), F32)],
        compiler_params=_params("parallel", "arbitrary"),
        name="ffn_half",
    )(x, g_pre, g_post, wg, wu, wd)


def _in_proj_kernel(x_ref, g_ref, w_ref, z_ref, h_scr):
    @pl.when(pl.program_id(1) == 0)
    def _():
        h_scr[...] = _rms(x_ref[...], g_ref[...]).astype(BF16)

    z_ref[...] = _dot(h_scr[...], w_ref[...])


def _in_proj(x, g, w):
    t = x.shape[0]
    tm, tn = TOKEN_TILE, 512
    return pl.pallas_call(
        _in_proj_kernel,
        grid=(t // tm, IN_PROJ_WIDTH // tn),
        in_specs=[
            pl.BlockSpec((tm, D_MODEL), lambda i, j: (i, 0)),
            pl.BlockSpec((1, D_MODEL), lambda i, j: (0, 0)),
            pl.BlockSpec((D_MODEL, tn), lambda i, j: (0, j)),
        ],
        out_specs=pl.BlockSpec((tm, tn), lambda i, j: (i, j)),
        out_shape=jax.ShapeDtypeStruct((t, IN_PROJ_WIDTH), F32),
        scratch_shapes=[pltpu.VMEM((tm, D_MODEL), BF16)],
        compiler_params=_params("parallel", "arbitrary"),
        name="in_proj",
    )(x, g, w)


def _attn_prompt_kernel(sink_ref, q_ref, kc_ref, kp_ref, vc_ref, vp_ref, gain_ref, o_ref):
    n = pl.program_id(1)
    k = jnp.concatenate([kp_ref[...], kc_ref[...]], axis=0).astype(BF16)
    v = jnp.concatenate([vp_ref[...], vc_ref[...]], axis=0).astype(BF16)
    row = lax.broadcasted_iota(jnp.int32, (WINDOW, 2 * WINDOW), 0)
    col = lax.broadcasted_iota(jnp.int32, (WINDOW, 2 * WINDOW), 1)
    dist = row + WINDOW - col
    valid = (dist >= 0) & (dist < WINDOW) & ((col >= WINDOW) | (n > 0))
    distf = dist.astype(F32)
    scale = ATTN_HEAD_DIM ** -0.5
    outs = []
    for hd in range(ATTN_HEADS):
        kvh = hd // ATTN_GROUP
        kh = k[:, kvh * ATTN_HEAD_DIM:(kvh + 1) * ATTN_HEAD_DIM]
        vh = v[:, kvh * ATTN_HEAD_DIM:(kvh + 1) * ATTN_HEAD_DIM]
        qh = q_ref[:, hd * ATTN_HEAD_DIM:(hd + 1) * ATTN_HEAD_DIM].astype(BF16)
        s = _dot_nt(qh, kh) * scale - ALIBI_SLOPES[hd] * distf
        s = jnp.where(valid, s, -jnp.inf)
        sink = sink_ref[hd]
        m = jnp.maximum(jnp.max(s, axis=-1, keepdims=True), sink)
        p = jnp.exp(s - m)
        denom = jnp.sum(p, axis=-1, keepdims=True) + jnp.exp(sink - m)
        outs.append(_dot(p.astype(BF16), vh) / denom)
    o = jnp.concatenate(outs, axis=1)
    o_ref[...] = _rms(o, gain_ref[...]).astype(BF16)


def _attn_prompt(z3, sinks, gain):
    b, s, _ = z3.shape
    nb = s // WINDOW

    def cur(width_idx):
        return lambda bi, n: (bi, n, width_idx)

    def prev(width_idx):
        return lambda bi, n: (bi, jnp.maximum(n - 1, 0), width_idx)

    return pl.pallas_call(
        _attn_prompt_kernel,
        grid=(b, nb),
        in_specs=[
            pl.BlockSpec(memory_space=pltpu.SMEM),
            pl.BlockSpec((None, WINDOW, ATTN_WIDTH), cur(Z_QA)),
            pl.BlockSpec((None, WINDOW, KV_WIDTH), cur(Z_KA)),
            pl.BlockSpec((None, WINDOW, KV_WIDTH), prev(Z_KA)),
            pl.BlockSpec((None, WINDOW, KV_WIDTH), cur(Z_VA)),
            pl.BlockSpec((None, WINDOW, KV_WIDTH), prev(Z_VA)),
            pl.BlockSpec((1, ATTN_WIDTH), lambda bi, n: (0, 0)),
        ],
        out_specs=pl.BlockSpec((None, WINDOW, ATTN_WIDTH), lambda bi, n: (bi, n, 0)),
        out_shape=jax.ShapeDtypeStruct((b, s, ATTN_WIDTH), BF16),
        compiler_params=_params("parallel", "arbitrary"),
        name="attn_prompt",
    )(sinks, z3, z3, z3, z3, z3, gain)


ATTN_SAMPLE_SEQS = 8


def _attn_sample_kernel(sink_ref, q_ref, kn_ref, vn_ref, ck_ref, cv_ref, gain_ref,
                        o_ref, wk_ref, wv_ref):
    _, t_new, _ = q_ref.shape
    assert t_new & (t_new - 1) == 0
    t_bits = t_new.bit_length() - 1
    rows = ATTN_GROUP * t_new
    kn = kn_ref[...]
    vn = vn_ref[...]
    ck = ck_ref[...]
    cv = cv_ref[...]
    wk_ref[:, :WINDOW - t_new, :] = ck[:, t_new:, :]
    wk_ref[:, WINDOW - t_new:, :] = kn
    wv_ref[:, :WINDOW - t_new, :] = cv[:, t_new:, :]
    wv_ref[:, WINDOW - t_new:, :] = vn

    r_c = lax.broadcasted_iota(jnp.int32, (rows, WINDOW), 0)
    j_c = lax.broadcasted_iota(jnp.int32, (rows, WINDOW), 1)
    t_c = r_c & (t_new - 1)
    dist_c = t_c + WINDOW - j_c
    valid_c = dist_c < WINDOW
    r_n = lax.broadcasted_iota(jnp.int32, (rows, t_new), 0)
    j_n = lax.broadcasted_iota(jnp.int32, (rows, t_new), 1)
    dist_n = (r_n & (t_new - 1)) - j_n
    valid_n = dist_n >= 0
    head_of_row = lax.broadcasted_iota(jnp.int32, (rows, 1), 0) >> t_bits
    scale = ATTN_HEAD_DIM ** -0.5

    outs = [None] * ATTN_HEADS
    for kvh in range(ATTN_KV_HEADS):
        slope = jnp.zeros((rows, 1), F32)
        sink = jnp.zeros((rows, 1), F32)
        for g in range(ATTN_GROUP):
            hd = kvh * ATTN_GROUP + g
            slope = jnp.where(head_of_row == g, ALIBI_SLOPES[hd], slope)
            sink = jnp.where(head_of_row == g, sink_ref[hd], sink)
        q = jnp.concatenate(
            [q_ref[:, :, (kvh * ATTN_GROUP + g) * ATTN_HEAD_DIM:(kvh * ATTN_GROUP + g + 1) * ATTN_HEAD_DIM]
             for g in range(ATTN_GROUP)], axis=1).astype(BF16)
        sl = slice(kvh * ATTN_HEAD_DIM, (kvh + 1) * ATTN_HEAD_DIM)
        ck_h = ck[:, :, sl].astype(BF16)
        cv_h = cv[:, :, sl].astype(BF16)
        kn_h = kn[:, :, sl].astype(BF16)
        vn_h = vn[:, :, sl].astype(BF16)
        s_c = jnp.einsum('gqd,gkd->gqk', q, ck_h, preferred_element_type=F32) * scale
        s_n = jnp.einsum('gqd,gkd->gqk', q, kn_h, preferred_element_type=F32) * scale
        s_c = jnp.where(valid_c[None], s_c - (slope * dist_c.astype(F32))[None], -jnp.inf)
        s_n = jnp.where(valid_n[None], s_n - (slope * dist_n.astype(F32))[None], -jnp.inf)
        m = jnp.maximum(jnp.maximum(jnp.max(s_c, axis=-1, keepdims=True),
                                    jnp.max(s_n, axis=-1, keepdims=True)), sink[None])
        p_c = jnp.exp(s_c - m)
        p_n = jnp.exp(s_n - m)
        denom = (jnp.sum(p_c, axis=-1, keepdims=True) + jnp.sum(p_n, axis=-1, keepdims=True)
                 + jnp.exp(sink[None] - m))
        o = (jnp.einsum('gqk,gkd->gqd', p_c.astype(BF16), cv_h, preferred_element_type=F32)
             + jnp.einsum('gqk,gkd->gqd', p_n.astype(BF16), vn_h, preferred_element_type=F32)) / denom
        for g in range(ATTN_GROUP):
            outs[kvh * ATTN_GROUP + g] = o[:, g * t_new:(g + 1) * t_new, :]
    o_all = jnp.concatenate(outs, axis=2)
    o_ref[...] = _rms(o_all, gain_ref[...][None])


def _attn_sample(z3, cache_k, cache_v, sinks, gain):
    b, t_new, _ = z3.shape
    gs = ATTN_SAMPLE_SEQS
    new_spec = lambda idx: pl.BlockSpec((gs, t_new, KV_WIDTH), lambda i: (i, 0, idx))
    cache_spec = pl.BlockSpec((gs, WINDOW, KV_WIDTH), lambda i: (i, 0, 0))
    return pl.pallas_call(
        _attn_sample_kernel,
        grid=(b // gs,),
        in_specs=[
            pl.BlockSpec(memory_space=pltpu.SMEM),
            pl.BlockSpec((gs, t_new, ATTN_WIDTH), lambda i: (i, 0, Z_QA)),
            new_spec(Z_KA),
            new_spec(Z_VA),
            cache_spec,
            cache_spec,
            pl.BlockSpec((1, ATTN_WIDTH), lambda i: (0, 0)),
        ],
        out_specs=[
            pl.BlockSpec((gs, t_new, ATTN_WIDTH), lambda i: (i, 0, 0)),
            cache_spec,
            cache_spec,
        ],
        out_shape=[
            jax.ShapeDtypeStruct((b, t_new, ATTN_WIDTH), F32),
            jax.ShapeDtypeStruct((b, WINDOW, KV_WIDTH), F32),
            jax.ShapeDtypeStruct((b, WINDOW, KV_WIDTH), F32),
        ],
        compiler_params=_params("parallel"),
        name="attn_sample",
    )(sinks, z3, z3, z3, cache_k, cache_v, gain)


def _lower_bound(lbl_ref):
    logits = lbl_ref[...]
    e = jnp.exp(logits - jnp.max(logits, axis=0, keepdims=True))
    return e[0:1, :] / jnp.sum(e, axis=0, keepdims=True)


def _hgrn_features(q_raw, f_raw, lb):
    sg = jax.nn.sigmoid(f_raw)
    logf = jnp.log(lb + (1.0 - lb) * sg)
    k = (1.0 - lb) * jax.nn.sigmoid(-f_raw)
    return _silu(q_raw), k, logf


def _cumsum_rows(x):
    n = x.shape[0]
    row = lax.broadcasted_iota(jnp.int32, x.shape, 0)
    shift = 1
    while shift < n:
        x = x + jnp.where(row >= shift, pltpu.roll(x, shift, axis=0), 0.0)
        shift *= 2
    return x


def _column_broadcast(row_vec, n):
    return jnp.transpose(jnp.broadcast_to(row_vec, (n, n)))


def _hgrn_out(o, gain, g_raw):
    return _rms(o, gain) * _silu(g_raw)


def _hgrn_prompt_kernel(lbl_ref, q_ref, f_ref, i_ref, g_ref, gain_ref, r_ref, s_ref):
    c = HGRN_CHUNK
    sub = HGRN_SUB

    @pl.when(pl.program_id(1) == 0)
    def _():
        s_ref[...] = jnp.zeros_like(s_ref)

    lb = _lower_bound(lbl_ref)
    q_all, k_all, logf_all = _hgrn_features(q_ref[...], f_ref[...], lb)
    l_all = _cumsum_rows(logf_all)
    row = lax.broadcasted_iota(jnp.int32, (c, HGRN_DK), 0)
    trow = lax.broadcasted_iota(jnp.int32, (c, c), 0)
    tcol = lax.broadcasted_iota(jnp.int32, (c, c), 1)
    sub_bits = sub.bit_length() - 1
    diag_mask = ((trow >> sub_bits) == (tcol >> sub_bits)) & (tcol <= trow)
    gain = gain_ref[...]

    for h in range(HGRN_HEADS):
        sl = slice(h * HGRN_DK, (h + 1) * HGRN_DK)
        q, k, l = q_all[:, sl], k_all[:, sl], l_all[:, sl]
        v = i_ref[:, sl]
        vb = v.astype(BF16)
        q_parts, k_parts = [], []
        for r in range(sub, c, sub):
            l_r = l[r - 1:r, :]
            q_parts.append(jnp.where(row >= r, q * jnp.exp(jnp.minimum(l - l_r, 0.0)), 0.0))
            k_blk = k[r - sub:r, :] * jnp.exp(l_r - l[r - sub:r, :])
            pieces = [k_blk, jnp.zeros((c - r, HGRN_DK), F32)]
            if r > sub:
                pieces.insert(0, jnp.zeros((r - sub, HGRN_DK), F32))
            k_parts.append(jnp.concatenate(pieces, axis=0))
        a_off = _dot_nt(jnp.concatenate(q_parts, axis=1).astype(BF16),
                        jnp.concatenate(k_parts, axis=1).astype(BF16))
        l_mid = jnp.concatenate(
            [jnp.broadcast_to(l[b0 + sub // 2 - 1:b0 + sub // 2, :], (sub, HGRN_DK))
             for b0 in range(0, c, sub)], axis=0)
        q_d = q * jnp.exp(jnp.minimum(l - l_mid, MAX_EXP))
        k_d = k * jnp.exp(jnp.minimum(l_mid - l, MAX_EXP))
        a_d = _dot_nt(q_d.astype(BF16), k_d.astype(BF16))
        a = jnp.where(diag_mask, a_d, a_off)

        s_old = s_ref[h]
        o = _dot(a.astype(BF16), vb) + _dot((q * jnp.exp(l)).astype(BF16), s_old.astype(BF16))
        l_end = l[c - 1:c, :]
        k_end = k * jnp.exp(l_end - l)
        s_ref[h] = (s_old * _column_broadcast(jnp.exp(l_end), HGRN_DK)
                    + _dot_tn(k_end.astype(BF16), vb))
        r_ref[:, sl] = _hgrn_out(o, gain, g_ref[:, sl]).astype(r_ref.dtype)


def _hgrn_prompt(z3, lb_logits, gain):
    b, s, _ = z3.shape
    c = HGRN_CHUNK
    seg = lambda idx: pl.BlockSpec((None, c, HGRN_KW), lambda bi, ci: (bi, ci, idx))
    return pl.pallas_call(
        _hgrn_prompt_kernel,
        grid=(b, s // c),
        in_specs=[
            pl.BlockSpec(lb_logits.shape, lambda bi, ci: (0, 0)),
            seg(Z_QH), seg(Z_FH), seg(Z_IH), seg(Z_GH),
            pl.BlockSpec((1, HGRN_DV), lambda bi, ci: (0, 0)),
        ],
        out_specs=[
            pl.BlockSpec((None, c, HGRN_VW), lambda bi, ci: (bi, ci, 0)),
            pl.BlockSpec((None, HGRN_HEADS, HGRN_DK, HGRN_DV), lambda bi, ci: (bi, 0, 0, 0)),
        ],
        out_shape=[
            jax.ShapeDtypeStruct((b, s, HGRN_VW), BF16),
            jax.ShapeDtypeStruct((b, HGRN_HEADS, HGRN_DK, HGRN_DV), F32),
        ],
        compiler_params=_params("parallel", "arbitrary"),
        name="hgrn_prompt",
    )(lb_logits, z3, z3, z3, z3, gain)


def _hgrn_sample_kernel(lbl_ref, q_ref, f_ref, i_ref, g_ref, gain_ref, s0_ref, r_ref, s_ref):
    t_new = q_ref.shape[0]
    lb = _lower_bound(lbl_ref)
    q_all, k_all, logf_all = _hgrn_features(q_ref[...], f_ref[...], lb)
    l_all = _cumsum_rows(logf_all)
    srow = lax.broadcasted_iota(jnp.int32, (t_new, HGRN_DK), 0)
    gain = gain_ref[...]

    for h in range(HGRN_HEADS):
        sl = slice(h * HGRN_DK, (h + 1) * HGRN_DK)
        q, k, l = q_all[:, sl], k_all[:, sl], l_all[:, sl]
        v = i_ref[:, sl]
        vb = v.astype(BF16)
        o_rows = []
        for t in range(t_new):
            w = jnp.where(srow <= t,
                          jnp.exp(jnp.minimum(l[t:t + 1, :] - l, 0.0)) * k * q[t:t + 1, :], 0.0)
            a_col = jnp.sum(w, axis=-1, keepdims=True)
            o_rows.append(jnp.sum(a_col * v, axis=0, keepdims=True))
        s_old = s0_ref[h]
        o = jnp.concatenate(o_rows, axis=0) + _dot((q * jnp.exp(l)).astype(BF16), s_old.astype(BF16))
        l_end = l[t_new - 1:t_new, :]
        k_end = k * jnp.exp(l_end - l)
        s_ref[h] = (s_old * _column_broadcast(jnp.exp(l_end), HGRN_DK)
                    + _dot_tn(k_end.astype(BF16), vb))
        r_ref[:, sl] = _hgrn_out(o, gain, g_ref[:, sl]).astype(r_ref.dtype)


def _hgrn_sample(z3, state, lb_logits, gain):
    b, t_new, _ = z3.shape
    seg = lambda idx: pl.BlockSpec((None, t_new, HGRN_KW), lambda bi: (bi, 0, idx))
    state_spec = pl.BlockSpec((None, HGRN_HEADS, HGRN_DK, HGRN_DV), lambda bi: (bi, 0, 0, 0))
    return pl.pallas_call(
        _hgrn_sample_kernel,
        grid=(b,),
        in_specs=[
            pl.BlockSpec(lb_logits.shape, lambda bi: (0, 0)),
            seg(Z_QH), seg(Z_FH), seg(Z_IH), seg(Z_GH),
            pl.BlockSpec((1, HGRN_DV), lambda bi: (0, 0)),
            state_spec,
        ],
        out_specs=[
            pl.BlockSpec((None, t_new, HGRN_VW), lambda bi: (bi, 0, 0)),
            state_spec,
        ],
        out_shape=[
            jax.ShapeDtypeStruct((b, t_new, HGRN_VW), F32),
            jax.ShapeDtypeStruct((b, HGRN_HEADS, HGRN_DK, HGRN_DV), F32),
        ],
        compiler_params=_params("parallel"),
        name="hgrn_sample",
    )(lb_logits, z3, z3, z3, z3, gain, state)


def _mix_out_kernel(x_ref, a_ref, r_ref, wa_ref, wr_ref, gpost_ref, gmem_ref, wq_ref,
                    x2_ref, qm_ref):
    mixed = (_dot(a_ref[...].astype(BF16), wa_ref[...])
             + _dot(r_ref[...].astype(BF16), wr_ref[...]))
    x2 = x_ref[...] + _rms(mixed, gpost_ref[...])
    x2_ref[...] = x2
    qm_ref[...] = _dot(_rms(x2, gmem_ref[...]).astype(BF16), wq_ref[...])


def _mix_out(x, a, r, w_out, g_post, g_mem, w_q):
    t = x.shape[0]
    tm = 256
    tok = lambda width: pl.BlockSpec((tm, width), lambda i: (i, 0))
    const = lambda shape, idx=(0, 0): pl.BlockSpec(shape, lambda i: idx)
    return pl.pallas_call(
        _mix_out_kernel,
        grid=(t // tm,),
        in_specs=[
            tok(D_MODEL), tok(ATTN_WIDTH), tok(HGRN_VW),
            const((ATTN_WIDTH, D_MODEL), (0, 0)),
            const((HGRN_VW, D_MODEL), (1, 0)),
            const((1, D_MODEL)), const((1, D_MODEL)),
            const((D_MODEL, MEM_WIDTH)),
        ],
        out_specs=[tok(D_MODEL), tok(MEM_WIDTH)],
        out_shape=[jax.ShapeDtypeStruct((t, D_MODEL), F32),
                   jax.ShapeDtypeStruct((t, MEM_WIDTH), F32)],
        compiler_params=_params("parallel"),
        name="mix_out",
    )(x, a, r, w_out, w_out, g_post, g_mem, w_q)


def _mem_kv_kernel(m_ref, g_ref, wk_ref, wv_ref, k_ref, v_ref):
    h = _rms(m_ref[...], g_ref[...]).astype(BF16)
    k_ref[...] = _dot(h, wk_ref[...])
    v_ref[...] = _dot(h, wv_ref[...])


def _mem_kv(mem, g, w_k, w_v):
    t = mem.shape[0]
    tm = 256
    const = lambda shape: pl.BlockSpec(shape, lambda i: (0, 0))
    tok = lambda width: pl.BlockSpec((tm, width), lambda i: (i, 0))
    return pl.pallas_call(
        _mem_kv_kernel,
        grid=(t // tm,),
        in_specs=[tok(D_MODEL), const((1, D_MODEL)),
                  const((D_MODEL, MEM_WIDTH)), const((D_MODEL, MEM_WIDTH))],
        out_specs=[tok(MEM_WIDTH), tok(MEM_WIDTH)],
        out_shape=[jax.ShapeDtypeStruct((t, MEM_WIDTH), F32)] * 2,
        compiler_params=_params("parallel"),
        name="mem_kv",
    )(mem, g, w_k, w_v)


def _mem_attn_kernel(x_ref, q_ref, mk_ref, mv_ref, wo_ref, g_ref, o_ref):
    seqs, tq, _ = q_ref.shape
    scale = MEM_HEAD_DIM ** -0.5
    outs = []
    for h in range(MEM_HEADS):
        sl = slice(h * MEM_HEAD_DIM, (h + 1) * MEM_HEAD_DIM)
        q = q_ref[:, :, sl].astype(BF16)
        k = mk_ref[:, :, sl].astype(BF16)
        v = mv_ref[:, :, sl].astype(BF16)
        s = jnp.einsum('gqd,gkd->gqk', q, k, preferred_element_type=F32) * scale
        p = jnp.exp(s - jnp.max(s, axis=-1, keepdims=True))
        denom = jnp.sum(p, axis=-1, keepdims=True)
        outs.append(jnp.einsum('gqk,gkd->gqd', p.astype(BF16), v, preferred_element_type=F32) / denom)
    o = jnp.concatenate(outs, axis=2).reshape(seqs * tq, MEM_WIDTH)
    y = _dot(o.astype(BF16), wo_ref[...])
    x = x_ref[...].reshape(seqs * tq, D_MODEL)
    o_ref[...] = (x + _rms(y, g_ref[...])).reshape(seqs, tq, D_MODEL)


def _mem_attn(x3, q3, mem_k, mem_v, w_o, g_post, seqs, tq):
    nseq, slen, _ = x3.shape
    m_tokens = mem_k.shape[1]
    tok = lambda width: pl.BlockSpec((seqs, tq, width), lambda i, j: (i, j, 0))
    mem = pl.BlockSpec((seqs, m_tokens, MEM_WIDTH), lambda i, j: (i, 0, 0))
    const = lambda shape: pl.BlockSpec(shape, lambda i, j: (0, 0))
    return pl.pallas_call(
        _mem_attn_kernel,
        grid=(nseq // seqs, slen // tq),
        in_specs=[tok(D_MODEL), tok(MEM_WIDTH), mem, mem,
                  const((MEM_WIDTH, D_MODEL)), const((1, D_MODEL))],
        out_specs=tok(D_MODEL),
        out_shape=jax.ShapeDtypeStruct((nseq, slen, D_MODEL), F32),
        compiler_params=_params("parallel", "arbitrary"),
        name="mem_attn",
    )(x3, q3, mem_k, mem_v, w_o, g_post)


def _row(g):
    return g.reshape(1, -1)


def _trunk_front(x2d, w):
    x1 = _ffn(x2d, w['ffn1_pre'], w['ffn1_post'], w['ffn1_wg'], w['ffn1_wu'], w['ffn1_wd'])
    z = _in_proj(x1, w['mix_pre'], w['w_in'])
    return x1, z


def _trunk_back(x1, a, r, mem_k, mem_v, w, nseq, seqs, tq):
    t = x1.shape[0]
    x2, qm = _mix_out(x1, a.reshape(t, ATTN_WIDTH), r.reshape(t, HGRN_VW), w['w_out'],
                      w['mix_post'], w['mem_pre'], w['w_mem_q'])
    x3 = _mem_attn(x2.reshape(nseq, t // nseq, D_MODEL), qm.reshape(nseq, t // nseq, MEM_WIDTH),
                   mem_k, mem_v, w['w_mem_o'], w['mem_post'], seqs, tq)
    return _ffn(x3.reshape(t, D_MODEL), w['ffn2_pre'], w['ffn2_post'],
                w['ffn2_wg'], w['ffn2_wu'], w['ffn2_wd'])


def kernel(x_prompt, x_sample, mem_prompt, cache_win_k, cache_win_v, state_hgrn, cache_mem_k, cache_mem_v, ffn1_norm_pre, ffn1_norm_post, ffn1_w_gate, ffn1_w_up, ffn1_w_down, mix_norm_pre, mix_norm_post, w_in, attn_sinks, hgrn_lb_logits, attn_out_gain, hgrn_out_gain, w_out, mem_norm_pre, mem_norm_post, mem_norm_kv, w_mem_q, w_mem_k, w_mem_v, w_mem_o, ffn2_norm_pre, ffn2_norm_post, ffn2_w_gate, ffn2_w_up, ffn2_w_down):
    bp, sp, _ = x_prompt.shape
    bs, ts, _ = x_sample.shape
    mt = mem_prompt.shape[1]
    l = 0

    w_in_l = w_in[l]
    kv_end = ATTN_WIDTH + 2 * KV_WIDTH
    w_in_perm = jnp.concatenate(
        [w_in_l[:, :ATTN_WIDTH], w_in_l[:, kv_end:], w_in_l[:, ATTN_WIDTH:kv_end]], axis=1)
    w = dict(
        ffn1_pre=_row(ffn1_norm_pre[l]), ffn1_post=_row(ffn1_norm_post[l]),
        ffn1_wg=ffn1_w_gate[l].astype(BF16), ffn1_wu=ffn1_w_up[l].astype(BF16),
        ffn1_wd=ffn1_w_down[l].astype(BF16),
        mix_pre=_row(mix_norm_pre[l]), mix_post=_row(mix_norm_post[l]),
        w_in=w_in_perm.astype(BF16), w_out=w_out[l].astype(BF16),
        mem_pre=_row(mem_norm_pre[l]), mem_post=_row(mem_norm_post[l]),
        w_mem_q=w_mem_q[l].astype(BF16), w_mem_o=w_mem_o[l].astype(BF16),
        ffn2_pre=_row(ffn2_norm_pre[l]), ffn2_post=_row(ffn2_norm_post[l]),
        ffn2_wg=ffn2_w_gate[l].astype(BF16), ffn2_wu=ffn2_w_up[l].astype(BF16),
        ffn2_wd=ffn2_w_down[l].astype(BF16),
    )
    sinks = attn_sinks[l]
    attn_gain = _row(attn_out_gain[l])
    hgrn_gain = _row(hgrn_out_gain[l])

    mk, mv = _mem_kv(mem_prompt.reshape(bp * mt, D_MODEL), _row(mem_norm_kv[l]),
                     w_mem_k[l].astype(BF16), w_mem_v[l].astype(BF16))
    mk = mk.reshape(bp, mt, MEM_WIDTH)
    mv = mv.reshape(bp, mt, MEM_WIDTH)
    x1, z = _trunk_front(x_prompt.reshape(bp * sp, D_MODEL), w)
    z3 = z.reshape(bp, sp, IN_PROJ_WIDTH)
    a = _attn_prompt(z3, sinks, attn_gain)
    r, p_state = _hgrn_prompt(z3, hgrn_lb_logits, hgrn_gain)
    y_p = _trunk_back(x1, a, r, mk, mv, w, bp, 1, 256).reshape(bp, sp, D_MODEL)
    k_off = Z_KA * KV_WIDTH
    v_off = Z_VA * KV_WIDTH
    p_wk = z3[:, sp - WINDOW:, k_off:k_off + KV_WIDTH]
    p_wv = z3[:, sp - WINDOW:, v_off:v_off + KV_WIDTH]

    x1s, zs = _trunk_front(x_sample.reshape(bs * ts, D_MODEL), w)
    zs3 = zs.reshape(bs, ts, IN_PROJ_WIDTH)
    a_s, s_wk, s_wv = _attn_sample(zs3, cache_win_k[l].reshape(bs, WINDOW, KV_WIDTH),
                                   cache_win_v[l].reshape(bs, WINDOW, KV_WIDTH), sinks, attn_gain)
    r_s, s_state = _hgrn_sample(zs3, state_hgrn[l], hgrn_lb_logits, hgrn_gain)
    y_s = _trunk_back(x1s, a_s, r_s, cache_mem_k[l].reshape(bs, -1, MEM_WIDTH),
                      cache_mem_v[l].reshape(bs, -1, MEM_WIDTH), w, bs, 8, ts).reshape(bs, ts, D_MODEL)

    kv5 = lambda t, n: t.reshape(1, n, WINDOW, ATTN_KV_HEADS, ATTN_HEAD_DIM)
    mem5 = lambda t: t.reshape(1, bp, mt, MEM_HEADS, MEM_HEAD_DIM)
    return (y_p, y_s, kv5(p_wk, bp), kv5(p_wv, bp), p_state[None], mem5(mk), mem5(mv),
            kv5(s_wk, bs), kv5(s_wv, bs), s_state[None])
```

```python
import functools

import jax
import jax.numpy as jnp
from jax import lax
from jax.experimental import pallas as pl
from jax.experimental.pallas import tpu as pltpu

F32 = jnp.float32
BF16 = jnp.bfloat16

D_MODEL = 2048
D_FF = 5632
ATTN_HEADS = 16
ATTN_KV_HEADS = 4
ATTN_GROUP = ATTN_HEADS // ATTN_KV_HEADS
ATTN_HEAD_DIM = 64
WINDOW = 128
ATTN_WIDTH = ATTN_HEADS * ATTN_HEAD_DIM
KV_WIDTH = ATTN_KV_HEADS * ATTN_HEAD_DIM
HGRN_HEADS = 8
HGRN_DK = 128
HGRN_DV = 128
HGRN_KW = HGRN_HEADS * HGRN_DK
HGRN_VW = HGRN_HEADS * HGRN_DV
IN_PROJ_WIDTH = ATTN_WIDTH + 2 * KV_WIDTH + 2 * HGRN_KW + 2 * HGRN_VW
MEM_HEADS = 4
MEM_HEAD_DIM = 128
MEM_WIDTH = MEM_HEADS * MEM_HEAD_DIM
FFN_RESIDUAL = 0.5
EPS = 1e-6
PAST_LEN = 16384

Z_QA, Z_QH, Z_FH, Z_IH, Z_GH = 0, 1, 2, 3, 4
Z_KA, Z_VA = 20, 21

VMEM_LIMIT_BYTES = 60 * 1024 * 1024

FFN_TOKEN_TILE = 1024
MIX_TOKEN_TILE = 512
FF_TILE = 512
HGRN_CHUNK = 64
HGRN_SUB = 16
MAX_EXP = 80.0

ALIBI_SLOPES = tuple(2.0 ** (-8.0 * (h + 1) / ATTN_HEADS) for h in range(ATTN_HEADS))


def _params(*semantics):
    return pltpu.CompilerParams(dimension_semantics=semantics,
                                vmem_limit_bytes=VMEM_LIMIT_BYTES)


def _rms(x, g):
    return x * lax.rsqrt(jnp.mean(x * x, axis=-1, keepdims=True) + EPS) * g


def _silu(x):
    return x * jax.nn.sigmoid(x)


def _dot(a, b):
    return jnp.dot(a, b, preferred_element_type=F32)


def _dot_nt(a, b):
    return lax.dot_general(a, b, (((1,), (1,)), ((), ())), preferred_element_type=F32)


def _dot_tn(a, b):
    return lax.dot_general(a, b, (((0,), (0,)), ((), ())), preferred_element_type=F32)


def _ffn_kernel(x_ref, gpre_ref, gpost_ref, wg_ref, wu_ref, wd_ref, o_ref, h_scr):
    j = pl.program_id(1)

    @pl.when(j == 0)
    def _():
        h_scr[...] = _rms(x_ref[...], gpre_ref[...]).astype(BF16)
        o_ref[...] = jnp.zeros_like(o_ref)

    h = h_scr[...]
    g = _dot(h, wg_ref[...])
    u = _dot(h, wu_ref[...])
    a = (_silu(g) * u).astype(BF16)
    o_ref[...] += _dot(a, wd_ref[...])

    @pl.when(j == pl.num_programs(1) - 1)
    def _():
        o_ref[...] = x_ref[...] + FFN_RESIDUAL * _rms(o_ref[...], gpost_ref[...])


def _ffn(x, g_pre, g_post, wg, wu, wd):
    t = x.shape[0]
    tm, tf = FFN_TOKEN_TILE, FF_TILE
    return pl.pallas_call(
        _ffn_kernel,
        grid=(t // tm, D_FF // tf),
        in_specs=[
            pl.BlockSpec((tm, D_MODEL), lambda i, j: (i, 0), pipeline_mode=pl.Buffered(1)),
            pl.BlockSpec((1, D_MODEL), lambda i, j: (0, 0)),
            pl.BlockSpec((1, D_MODEL), lambda i, j: (0, 0)),
            pl.BlockSpec((D_MODEL, tf), lambda i, j: (0, j)),
            pl.BlockSpec((D_MODEL, tf), lambda i, j: (0, j)),
            pl.BlockSpec((tf, D_MODEL), lambda i, j: (j, 0)),
        ],
        out_specs=pl.BlockSpec((tm, D_MODEL), lambda i, j: (i, 0)),
        out_shape=jax.ShapeDtypeStruct((t, D_MODEL), F32),
        scratch_shapes=[pltpu.VMEM((tm, D_MODEL), BF16)],
        compiler_params=_params("parallel", "arbitrary"),
        name="ffn_half",
    )(x, g_pre, g_post, wg, wu, wd)


def _in_proj_kernel(x_ref, g_ref, w_ref, z_ref, h_scr):
    @pl.when(pl.program_id(1) == 0)
    def _():
        h_scr[...] = _rms(x_ref[...], g_ref[...]).astype(BF16)

    z_ref[...] = _dot(h_scr[...], w_ref[...])


def _in_proj(x, g, w):
    t = x.shape[0]
    tm, tn = FFN_TOKEN_TILE, 512
    return pl.pallas_call(
        _in_proj_kernel,
        grid=(t // tm, IN_PROJ_WIDTH // tn),
        in_specs=[
            pl.BlockSpec((tm, D_MODEL), lambda i, j: (i, 0)),
            pl.BlockSpec((1, D_MODEL), lambda i, j: (0, 0)),
            pl.BlockSpec((D_MODEL, tn), lambda i, j: (0, j)),
        ],
        out_specs=pl.BlockSpec((tm, tn), lambda i, j: (i, j)),
        out_shape=jax.ShapeDtypeStruct((t, IN_PROJ_WIDTH), F32),
        scratch_shapes=[pltpu.VMEM((tm, D_MODEL), BF16)],
        compiler_params=_params("parallel", "arbitrary"),
        name="in_proj",
    )(x, g, w)


def _attn_prompt_kernel(sink_ref, q_ref, kc_ref, kp_ref, vc_ref, vp_ref, gain_ref, o_ref):
    n = pl.program_id(1)
    k = jnp.concatenate([kp_ref[...], kc_ref[...]], axis=0).astype(BF16)
    v = jnp.concatenate([vp_ref[...], vc_ref[...]], axis=0).astype(BF16)
    row = lax.broadcasted_iota(jnp.int32, (WINDOW, 2 * WINDOW), 0)
    col = lax.broadcasted_iota(jnp.int32, (WINDOW, 2 * WINDOW), 1)
    dist = row + WINDOW - col
    valid = (dist >= 0) & (dist < WINDOW) & ((col >= WINDOW) | (n > 0))
    distf = dist.astype(F32)
    scale = ATTN_HEAD_DIM ** -0.5
    outs = []
    for hd in range(ATTN_HEADS):
        kvh = hd // ATTN_GROUP
        kh = k[:, kvh * ATTN_HEAD_DIM:(kvh + 1) * ATTN_HEAD_DIM]
        vh = v[:, kvh * ATTN_HEAD_DIM:(kvh + 1) * ATTN_HEAD_DIM]
        qh = q_ref[:, hd * ATTN_HEAD_DIM:(hd + 1) * ATTN_HEAD_DIM].astype(BF16)
        s = _dot_nt(qh, kh) * scale - ALIBI_SLOPES[hd] * distf
        s = jnp.where(valid, s, -jnp.inf)
        sink = sink_ref[hd]
        m = jnp.maximum(jnp.max(s, axis=-1, keepdims=True), sink)
        p = jnp.exp(s - m)
        denom = jnp.sum(p, axis=-1, keepdims=True) + jnp.exp(sink - m)
        outs.append(_dot(p.astype(BF16), vh) / denom)
    o = jnp.concatenate(outs, axis=1)
    o_ref[...] = _rms(o, gain_ref[...]).astype(BF16)


def _attn_prompt(z3, sinks, gain):
    b, s, _ = z3.shape
    nb = s // WINDOW

    def cur(width_idx):
        return lambda bi, n: (bi, n, width_idx)

    def prev(width_idx):
        return lambda bi, n: (bi, jnp.maximum(n - 1, 0), width_idx)

    return pl.pallas_call(
        _attn_prompt_kernel,
        grid=(b, nb),
        in_specs=[
            pl.BlockSpec(memory_space=pltpu.SMEM),
            pl.BlockSpec((None, WINDOW, ATTN_WIDTH), cur(Z_QA)),
            pl.BlockSpec((None, WINDOW, KV_WIDTH), cur(Z_KA)),
            pl.BlockSpec((None, WINDOW, KV_WIDTH), prev(Z_KA)),
            pl.BlockSpec((None, WINDOW, KV_WIDTH), cur(Z_VA)),
            pl.BlockSpec((None, WINDOW, KV_WIDTH), prev(Z_VA)),
            pl.BlockSpec((1, ATTN_WIDTH), lambda bi, n: (0, 0)),
        ],
        out_specs=pl.BlockSpec((None, WINDOW, ATTN_WIDTH), lambda bi, n: (bi, n, 0)),
        out_shape=jax.ShapeDtypeStruct((b, s, ATTN_WIDTH), BF16),
        compiler_params=_params("parallel", "arbitrary"),
        name="attn_prompt",
    )(sinks, z3, z3, z3, z3, z3, gain)


ATTN_SAMPLE_SEQS = 8


def _attn_sample_kernel(sink_ref, q_ref, kn_ref, vn_ref, ck_ref, cv_ref, gain_ref,
                        o_ref, wk_ref, wv_ref):
    _, t_new, _ = q_ref.shape
    assert t_new & (t_new - 1) == 0
    t_bits = t_new.bit_length() - 1
    rows = ATTN_GROUP * t_new
    kn = kn_ref[...]
    vn = vn_ref[...]
    ck = ck_ref[...]
    cv = cv_ref[...]
    wk_ref[:, :WINDOW - t_new, :] = ck[:, t_new:, :]
    wk_ref[:, WINDOW - t_new:, :] = kn
    wv_ref[:, :WINDOW - t_new, :] = cv[:, t_new:, :]
    wv_ref[:, WINDOW - t_new:, :] = vn

    r_c = lax.broadcasted_iota(jnp.int32, (rows, WINDOW), 0)
    j_c = lax.broadcasted_iota(jnp.int32, (rows, WINDOW), 1)
    t_c = r_c & (t_new - 1)
    dist_c = t_c + WINDOW - j_c
    valid_c = dist_c < WINDOW
    r_n = lax.broadcasted_iota(jnp.int32, (rows, t_new), 0)
    j_n = lax.broadcasted_iota(jnp.int32, (rows, t_new), 1)
    dist_n = (r_n & (t_new - 1)) - j_n
    valid_n = dist_n >= 0
    head_of_row = lax.broadcasted_iota(jnp.int32, (rows, 1), 0) >> t_bits
    scale = ATTN_HEAD_DIM ** -0.5

    outs = [None] * ATTN_HEADS
    for kvh in range(ATTN_KV_HEADS):
        slope = jnp.zeros((rows, 1), F32)
        sink = jnp.zeros((rows, 1), F32)
        for g in range(ATTN_GROUP):
            hd = kvh * ATTN_GROUP + g
            slope = jnp.where(head_of_row == g, ALIBI_SLOPES[hd], slope)
            sink = jnp.where(head_of_row == g, sink_ref[hd], sink)
        q = jnp.concatenate(
            [q_ref[:, :, (kvh * ATTN_GROUP + g) * ATTN_HEAD_DIM:(kvh * ATTN_GROUP + g + 1) * ATTN_HEAD_DIM]
             for g in range(ATTN_GROUP)], axis=1).astype(BF16)
        sl = slice(kvh * ATTN_HEAD_DIM, (kvh + 1) * ATTN_HEAD_DIM)
        ck_h = ck[:, :, sl].astype(BF16)
        cv_h = cv[:, :, sl].astype(BF16)
        kn_h = kn[:, :, sl].astype(BF16)
        vn_h = vn[:, :, sl].astype(BF16)
        s_c = jnp.einsum('gqd,gkd->gqk', q, ck_h, preferred_element_type=F32) * scale
        s_n = jnp.einsum('gqd,gkd->gqk', q, kn_h, preferred_element_type=F32) * scale
        s_c = jnp.where(valid_c[None], s_c - (slope * dist_c.astype(F32))[None], -jnp.inf)
        s_n = jnp.where(valid_n[None], s_n - (slope * dist_n.astype(F32))[None], -jnp.inf)
        m = jnp.maximum(jnp.maximum(jnp.max(s_c, axis=-1, keepdims=True),
                                    jnp.max(s_n, axis=-1, keepdims=True)), sink[None])
        p_c = jnp.exp(s_c - m)
        p_n = jnp.exp(s_n - m)
        denom = (jnp.sum(p_c, axis=-1, keepdims=True) + jnp.sum(p_n, axis=-1, keepdims=True)
                 + jnp.exp(sink[None] - m))
        o = (jnp.einsum('gqk,gkd->gqd', p_c.astype(BF16), cv_h, preferred_element_type=F32)
             + jnp.einsum('gqk,gkd->gqd', p_n.astype(BF16), vn_h, preferred_element_type=F32)) / denom
        for g in range(ATTN_GROUP):
            outs[kvh * ATTN_GROUP + g] = o[:, g * t_new:(g + 1) * t_new, :]
    o_all = jnp.concatenate(outs, axis=2)
    o_ref[...] = _rms(o_all, gain_ref[...][None])


def _attn_sample(z3, cache_k, cache_v, sinks, gain):
    b, t_new, _ = z3.shape
    gs = ATTN_SAMPLE_SEQS
    new_spec = lambda idx: pl.BlockSpec((gs, t_new, KV_WIDTH), lambda i: (i, 0, idx))
    cache_spec = pl.BlockSpec((gs, WINDOW, KV_WIDTH), lambda i: (i, 0, 0))
    return pl.pallas_call(
        _attn_sample_kernel,
        grid=(b // gs,),
        in_specs=[
            pl.BlockSpec(memory_space=pltpu.SMEM),
            pl.BlockSpec((gs, t_new, ATTN_WIDTH), lambda i: (i, 0, Z_QA)),
            new_spec(Z_KA),
            new_spec(Z_VA),
            cache_spec,
            cache_spec,
            pl.BlockSpec((1, ATTN_WIDTH), lambda i: (0, 0)),
        ],
        out_specs=[
            pl.BlockSpec((gs, t_new, ATTN_WIDTH), lambda i: (i, 0, 0)),
            cache_spec,
            cache_spec,
        ],
        out_shape=[
            jax.ShapeDtypeStruct((b, t_new, ATTN_WIDTH), F32),
            jax.ShapeDtypeStruct((b, WINDOW, KV_WIDTH), F32),
            jax.ShapeDtypeStruct((b, WINDOW, KV_WIDTH), F32),
        ],
        compiler_params=_params("parallel"),
        name="attn_sample",
    )(sinks, z3, z3, z3, cache_k, cache_v, gain)


def _lower_bound(lbl_ref):
    logits = lbl_ref[...]
    e = jnp.exp(logits - jnp.max(logits, axis=0, keepdims=True))
    return e[0:1, :] / jnp.sum(e, axis=0, keepdims=True)


def _hgrn_features(q_raw, f_raw, lb):
    sg = jax.nn.sigmoid(f_raw)
    logf = jnp.log(lb + (1.0 - lb) * sg)
    k = (1.0 - lb) * jax.nn.sigmoid(-f_raw)
    return _silu(q_raw), k, logf


def _cumsum_rows(x):
    n = x.shape[0]
    row = lax.broadcasted_iota(jnp.int32, x.shape, 0)
    shift = 1
    while shift < n:
        x = x + jnp.where(row >= shift, pltpu.roll(x, shift, axis=0), 0.0)
        shift *= 2
    return x


def _column_broadcast(row_vec, n):
    return jnp.transpose(jnp.broadcast_to(row_vec, (n, n)))


def _hgrn_out(o, gain, g_raw):
    return _rms(o, gain) * _silu(g_raw)


def _hgrn_prompt_kernel(lbl_ref, q_ref, f_ref, i_ref, g_ref, gain_ref, r_ref, s_ref):
    c = HGRN_CHUNK
    sub = HGRN_SUB

    @pl.when(pl.program_id(1) == 0)
    def _():
        s_ref[...] = jnp.zeros_like(s_ref)

    lb = _lower_bound(lbl_ref)
    q_all, k_all, logf_all = _hgrn_features(q_ref[...], f_ref[...], lb)
    l_all = _cumsum_rows(logf_all)
    row = lax.broadcasted_iota(jnp.int32, (c, HGRN_DK), 0)
    trow = lax.broadcasted_iota(jnp.int32, (c, c), 0)
    tcol = lax.broadcasted_iota(jnp.int32, (c, c), 1)
    sub_bits = sub.bit_length() - 1
    diag_mask = ((trow >> sub_bits) == (tcol >> sub_bits)) & (tcol <= trow)
    gain = gain_ref[...]

    for h in range(HGRN_HEADS):
        sl = slice(h * HGRN_DK, (h + 1) * HGRN_DK)
        q, k, l = q_all[:, sl], k_all[:, sl], l_all[:, sl]
        v = i_ref[:, sl]
        vb = v.astype(BF16)
        q_parts, k_parts = [], []
        for r in range(sub, c, sub):
            l_r = l[r - 1:r, :]
            q_parts.append(jnp.where(row >= r, q * jnp.exp(jnp.minimum(l - l_r, 0.0)), 0.0))
            k_blk = k[r - sub:r, :] * jnp.exp(l_r - l[r - sub:r, :])
            pieces = [k_blk, jnp.zeros((c - r, HGRN_DK), F32)]
            if r > sub:
                pieces.insert(0, jnp.zeros((r - sub, HGRN_DK), F32))
            k_parts.append(jnp.concatenate(pieces, axis=0))
        a_off = _dot_nt(jnp.concatenate(q_parts, axis=1).astype(BF16),
                        jnp.concatenate(k_parts, axis=1).astype(BF16))
        l_mid = jnp.concatenate(
            [jnp.broadcast_to(l[b0 + sub // 2 - 1:b0 + sub // 2, :], (sub, HGRN_DK))
             for b0 in range(0, c, sub)], axis=0)
        q_d = q * jnp.exp(jnp.minimum(l - l_mid, MAX_EXP))
        k_d = k * jnp.exp(jnp.minimum(l_mid - l, MAX_EXP))
        a_d = _dot_nt(q_d.astype(BF16), k_d.astype(BF16))
        a = jnp.where(diag_mask, a_d, a_off)

        s_old = s_ref[h]
        o = _dot(a.astype(BF16), vb) + _dot((q * jnp.exp(l)).astype(BF16), s_old.astype(BF16))
        l_end = l[c - 1:c, :]
        k_end = k * jnp.exp(l_end - l)
        s_ref[h] = (s_old * _column_broadcast(jnp.exp(l_end), HGRN_DK)
                    + _dot_tn(k_end.astype(BF16), vb))
        r_ref[:, sl] = _hgrn_out(o, gain, g_ref[:, sl]).astype(r_ref.dtype)


def _hgrn_prompt(z3, lb_logits, gain):
    b, s, _ = z3.shape
    c = HGRN_CHUNK
    seg = lambda idx: pl.BlockSpec((None, c, HGRN_KW), lambda bi, ci: (bi, ci, idx))
    return pl.pallas_call(
        _hgrn_prompt_kernel,
        grid=(b, s // c),
        in_specs=[
            pl.BlockSpec(lb_logits.shape, lambda bi, ci: (0, 0)),
            seg(Z_QH), seg(Z_FH), seg(Z_IH), seg(Z_GH),
            pl.BlockSpec((1, HGRN_DV), lambda bi, ci: (0, 0)),
        ],
        out_specs=[
            pl.BlockSpec((None, c, HGRN_VW), lambda bi, ci: (bi, ci, 0)),
            pl.BlockSpec((None, HGRN_HEADS, HGRN_DK, HGRN_DV), lambda bi, ci: (bi, 0, 0, 0)),
        ],
        out_shape=[
            jax.ShapeDtypeStruct((b, s, HGRN_VW), BF16),
            jax.ShapeDtypeStruct((b, HGRN_HEADS, HGRN_DK, HGRN_DV), F32),
        ],
        compiler_params=_params("parallel", "arbitrary"),
        name="hgrn_prompt",
    )(lb_logits, z3, z3, z3, z3, gain)


def _hgrn_sample_kernel(lbl_ref, q_ref, f_ref, i_ref, g_ref, gain_ref, s0_ref, r_ref, s_ref):
    t_new = q_ref.shape[0]
    lb = _lower_bound(lbl_ref)
    q_all, k_all, logf_all = _hgrn_features(q_ref[...], f_ref[...], lb)
    l_all = _cumsum_rows(logf_all)
    srow = lax.broadcasted_iota(jnp.int32, (t_new, HGRN_DK), 0)
    gain = gain_ref[...]

    for h in range(HGRN_HEADS):
        sl = slice(h * HGRN_DK, (h + 1) * HGRN_DK)
        q, k, l = q_all[:, sl], k_all[:, sl], l_all[:, sl]
        v = i_ref[:, sl]
        vb = v.astype(BF16)
        o_rows = []
        for t in range(t_new):
            w = jnp.where(srow <= t,
                          jnp.exp(jnp.minimum(l[t:t + 1, :] - l, 0.0)) * k * q[t:t + 1, :], 0.0)
            a_col = jnp.sum(w, axis=-1, keepdims=True)
            o_rows.append(jnp.sum(a_col * v, axis=0, keepdims=True))
        s_old = s0_ref[h]
        o = jnp.concatenate(o_rows, axis=0) + _dot((q * jnp.exp(l)).astype(BF16), s_old.astype(BF16))
        l_end = l[t_new - 1:t_new, :]
        k_end = k * jnp.exp(l_end - l)
        s_ref[h] = (s_old * _column_broadcast(jnp.exp(l_end), HGRN_DK)
                    + _dot_tn(k_end.astype(BF16), vb))
        r_ref[:, sl] = _hgrn_out(o, gain, g_ref[:, sl]).astype(r_ref.dtype)


def _hgrn_sample(z3, state, lb_logits, gain):
    b, t_new, _ = z3.shape
    seg = lambda idx: pl.BlockSpec((None, t_new, HGRN_KW), lambda bi: (bi, 0, idx))
    state_spec = pl.BlockSpec((None, HGRN_HEADS, HGRN_DK, HGRN_DV), lambda bi: (bi, 0, 0, 0))
    return pl.pallas_call(
        _hgrn_sample_kernel,
        grid=(b,),
        in_specs=[
            pl.BlockSpec(lb_logits.shape, lambda bi: (0, 0)),
            seg(Z_QH), seg(Z_FH), seg(Z_IH), seg(Z_GH),
            pl.BlockSpec((1, HGRN_DV), lambda bi: (0, 0)),
            state_spec,
        ],
        out_specs=[
            pl.BlockSpec((None, t_new, HGRN_VW), lambda bi: (bi, 0, 0)),
            state_spec,
        ],
        out_shape=[
            jax.ShapeDtypeStruct((b, t_new, HGRN_VW), F32),
            jax.ShapeDtypeStruct((b, HGRN_HEADS, HGRN_DK, HGRN_DV), F32),
        ],
        compiler_params=_params("parallel"),
        name="hgrn_sample",
    )(lb_logits, z3, z3, z3, z3, gain, state)


def _mix_out_kernel(x_ref, a_ref, r_ref, wa_ref, wr_ref, gpost_ref, gmem_ref, wq_ref,
                    x2_ref, qm_ref):
    mixed = (_dot(a_ref[...].astype(BF16), wa_ref[...])
             + _dot(r_ref[...].astype(BF16), wr_ref[...]))
    x2 = x_ref[...] + _rms(mixed, gpost_ref[...])
    x2_ref[...] = x2
    qm_ref[...] = _dot(_rms(x2, gmem_ref[...]).astype(BF16), wq_ref[...])


def _mix_out(x, a, r, w_out, g_post, g_mem, w_q):
    t = x.shape[0]
    tm = MIX_TOKEN_TILE
    tok = lambda width: pl.BlockSpec((tm, width), lambda i: (i, 0))
    const = lambda shape, idx=(0, 0): pl.BlockSpec(shape, lambda i: idx)
    return pl.pallas_call(
        _mix_out_kernel,
        grid=(t // tm,),
        in_specs=[
            tok(D_MODEL), tok(ATTN_WIDTH), tok(HGRN_VW),
            const((ATTN_WIDTH, D_MODEL), (0, 0)),
            const((HGRN_VW, D_MODEL), (1, 0)),
            const((1, D_MODEL)), const((1, D_MODEL)),
            const((D_MODEL, MEM_WIDTH)),
        ],
        out_specs=[tok(D_MODEL), tok(MEM_WIDTH)],
        out_shape=[jax.ShapeDtypeStruct((t, D_MODEL), F32),
                   jax.ShapeDtypeStruct((t, MEM_WIDTH), F32)],
        compiler_params=_params("parallel"),
        name="mix_out",
    )(x, a, r, w_out, w_out, g_post, g_mem, w_q)


def _mem_kv_kernel(m_ref, g_ref, wk_ref, wv_ref, k_ref, v_ref, kb_ref, vb_ref):
    h = _rms(m_ref[...], g_ref[...]).astype(BF16)
    k = _dot(h, wk_ref[...])
    v = _dot(h, wv_ref[...])
    k_ref[...] = k
    v_ref[...] = v
    kb_ref[...] = k.astype(BF16)
    vb_ref[...] = v.astype(BF16)


def _mem_kv(mem, g, w_k, w_v):
    b, m, _ = mem.shape
    const = lambda shape: pl.BlockSpec(shape, lambda i: (0, 0))
    out = pl.BlockSpec((None, m, MEM_WIDTH), lambda i: (i, 0, 0))
    return pl.pallas_call(
        _mem_kv_kernel,
        grid=(b,),
        in_specs=[pl.BlockSpec((None, m, D_MODEL), lambda i: (i, 0, 0)), const((1, D_MODEL)),
                  const((D_MODEL, MEM_WIDTH)), const((D_MODEL, MEM_WIDTH))],
        out_specs=[out] * 4,
        out_shape=[jax.ShapeDtypeStruct((b, m, MEM_WIDTH), F32)] * 2
        + [jax.ShapeDtypeStruct((b, m, MEM_WIDTH), BF16)] * 2,
        compiler_params=_params("parallel"),
        name="mem_kv",
    )(mem, g, w_k, w_v)


def _mem_attn_finish(o, x_ref, wo_ref, g_ref, o_ref):
    seqs, tq, _ = x_ref.shape
    y = _dot(o.reshape(seqs * tq, MEM_WIDTH).astype(BF16), wo_ref[...])
    x = x_ref[...].reshape(seqs * tq, D_MODEL)
    o_ref[...] = (x + _rms(y, g_ref[...])).reshape(seqs, tq, D_MODEL)


def _mem_attn_kernel(x_ref, q_ref, mk_ref, mv_ref, wo_ref, g_ref, o_ref):
    scale = MEM_HEAD_DIM ** -0.5
    outs = []
    for h in range(MEM_HEADS):
        sl = slice(h * MEM_HEAD_DIM, (h + 1) * MEM_HEAD_DIM)
        q = q_ref[:, :, sl].astype(BF16)
        s = jnp.einsum('gqd,gkd->gqk', q, mk_ref[:, :, sl], preferred_element_type=F32) * scale
        p = jnp.exp(s - jnp.max(s, axis=-1, keepdims=True))
        denom = jnp.sum(p, axis=-1, keepdims=True)
        outs.append(jnp.einsum('gqk,gkd->gqd', p.astype(BF16), mv_ref[:, :, sl],
                               preferred_element_type=F32) / denom)
    _mem_attn_finish(jnp.concatenate(outs, axis=2), x_ref, wo_ref, g_ref, o_ref)


def _mem_attn_interleaved_kernel(x_ref, q_ref, mk_ref, mv_ref, wo_ref, g_ref, o_ref):
    seqs, tq, _ = q_ref.shape
    assert tq & (tq - 1) == 0 and MEM_HEADS & (MEM_HEADS - 1) == 0
    rows = MEM_HEADS * tq
    cols = mk_ref.shape[1]
    scale = MEM_HEAD_DIM ** -0.5
    q = jnp.concatenate([q_ref[:, :, h * MEM_HEAD_DIM:(h + 1) * MEM_HEAD_DIM]
                         for h in range(MEM_HEADS)], axis=1).astype(BF16)
    s = jnp.einsum('gqd,gkd->gqk', q, mk_ref[...].astype(BF16), preferred_element_type=F32) * scale
    row_head = lax.broadcasted_iota(jnp.int32, (rows, cols), 0) >> (tq.bit_length() - 1)
    col_head = lax.broadcasted_iota(jnp.int32, (rows, cols), 1) & (MEM_HEADS - 1)
    s = jnp.where((row_head == col_head)[None], s, -jnp.inf)
    p = jnp.exp(s - jnp.max(s, axis=-1, keepdims=True))
    denom = jnp.sum(p, axis=-1, keepdims=True)
    o = jnp.einsum('gqk,gkd->gqd', p.astype(BF16), mv_ref[...].astype(BF16),
                   preferred_element_type=F32) / denom
    o = jnp.concatenate([o[:, h * tq:(h + 1) * tq, :] for h in range(MEM_HEADS)], axis=2)
    _mem_attn_finish(o, x_ref, wo_ref, g_ref, o_ref)


def _mem_attn(body, x3, q3, mem_k, mem_v, w_o, g_post, seqs, tq):
    nseq, slen, _ = x3.shape
    tok = lambda width: pl.BlockSpec((seqs, tq, width), lambda i, j: (i, j, 0))
    mem = pl.BlockSpec((seqs,) + mem_k.shape[1:], lambda i, j: (i, 0, 0))
    const = lambda shape: pl.BlockSpec(shape, lambda i, j: (0, 0))
    return pl.pallas_call(
        body,
        grid=(nseq // seqs, slen // tq),
        in_specs=[tok(D_MODEL), tok(MEM_WIDTH), mem, mem,
                  const((MEM_WIDTH, D_MODEL)), const((1, D_MODEL))],
        out_specs=tok(D_MODEL),
        out_shape=jax.ShapeDtypeStruct((nseq, slen, D_MODEL), F32),
        compiler_params=_params("parallel", "arbitrary"),
        name="mem_attn",
    )(x3, q3, mem_k, mem_v, w_o, g_post)


def _row(g):
    return g.reshape(1, -1)


def _trunk_front(x2d, w):
    x1 = _ffn(x2d, w['ffn1_pre'], w['ffn1_post'], w['ffn1_wg'], w['ffn1_wu'], w['ffn1_wd'])
    z = _in_proj(x1, w['mix_pre'], w['w_in'])
    return x1, z


def _trunk_back(x1, a, r, mem_body, mem_k, mem_v, w, nseq, seqs, tq):
    t = x1.shape[0]
    x2, qm = _mix_out(x1, a.reshape(t, ATTN_WIDTH), r.reshape(t, HGRN_VW), w['w_out'],
                      w['mix_post'], w['mem_pre'], w['w_mem_q'])
    x3 = _mem_attn(mem_body, x2.reshape(nseq, t // nseq, D_MODEL),
                   qm.reshape(nseq, t // nseq, MEM_WIDTH),
                   mem_k, mem_v, w['w_mem_o'], w['mem_post'], seqs, tq)
    return _ffn(x3.reshape(t, D_MODEL), w['ffn2_pre'], w['ffn2_post'],
                w['ffn2_wg'], w['ffn2_wu'], w['ffn2_wd'])


def kernel(x_prompt, x_sample, mem_prompt, cache_win_k, cache_win_v, state_hgrn, cache_mem_k, cache_mem_v, ffn1_norm_pre, ffn1_norm_post, ffn1_w_gate, ffn1_w_up, ffn1_w_down, mix_norm_pre, mix_norm_post, w_in, attn_sinks, hgrn_lb_logits, attn_out_gain, hgrn_out_gain, w_out, mem_norm_pre, mem_norm_post, mem_norm_kv, w_mem_q, w_mem_k, w_mem_v, w_mem_o, ffn2_norm_pre, ffn2_norm_post, ffn2_w_gate, ffn2_w_up, ffn2_w_down):
    bp, sp, _ = x_prompt.shape
    bs, ts, _ = x_sample.shape
    mt = mem_prompt.shape[1]
    l = 0

    w_in_l = w_in[l]
    kv_end = ATTN_WIDTH + 2 * KV_WIDTH
    w_in_perm = jnp.concatenate(
        [w_in_l[:, :ATTN_WIDTH], w_in_l[:, kv_end:], w_in_l[:, ATTN_WIDTH:kv_end]], axis=1)
    w = dict(
        ffn1_pre=_row(ffn1_norm_pre[l]), ffn1_post=_row(ffn1_norm_post[l]),
        ffn1_wg=ffn1_w_gate[l].astype(BF16), ffn1_wu=ffn1_w_up[l].astype(BF16),
        ffn1_wd=ffn1_w_down[l].astype(BF16),
        mix_pre=_row(mix_norm_pre[l]), mix_post=_row(mix_norm_post[l]),
        w_in=w_in_perm.astype(BF16), w_out=w_out[l].astype(BF16),
        mem_pre=_row(mem_norm_pre[l]), mem_post=_row(mem_norm_post[l]),
        w_mem_q=w_mem_q[l].astype(BF16), w_mem_o=w_mem_o[l].astype(BF16),
        ffn2_pre=_row(ffn2_norm_pre[l]), ffn2_post=_row(ffn2_norm_post[l]),
        ffn2_wg=ffn2_w_gate[l].astype(BF16), ffn2_wu=ffn2_w_up[l].astype(BF16),
        ffn2_wd=ffn2_w_down[l].astype(BF16),
    )
    sinks = attn_sinks[l]
    attn_gain = _row(attn_out_gain[l])
    hgrn_gain = _row(hgrn_out_gain[l])

    mk, mv, mk_b, mv_b = _mem_kv(mem_prompt, _row(mem_norm_kv[l]),
                                 w_mem_k[l].astype(BF16), w_mem_v[l].astype(BF16))
    x1, z = _trunk_front(x_prompt.reshape(bp * sp, D_MODEL), w)
    z3 = z.reshape(bp, sp, IN_PROJ_WIDTH)
    a = _attn_prompt(z3, sinks, attn_gain)
    r, p_state = _hgrn_prompt(z3, hgrn_lb_logits, hgrn_gain)
    y_p = _trunk_back(x1, a, r, _mem_attn_kernel, mk_b, mv_b, w, bp, 1, 256).reshape(bp, sp, D_MODEL)
    k_off = Z_KA * KV_WIDTH
    v_off = Z_VA * KV_WIDTH
    p_wk = z3[:, sp - WINDOW:, k_off:k_off + KV_WIDTH]
    p_wv = z3[:, sp - WINDOW:, v_off:v_off + KV_WIDTH]

    x1s, zs = _trunk_front(x_sample.reshape(bs * ts, D_MODEL), w)
    zs3 = zs.reshape(bs, ts, IN_PROJ_WIDTH)
    a_s, s_wk, s_wv = _attn_sample(zs3, cache_win_k[l].reshape(bs, WINDOW, KV_WIDTH),
                                   cache_win_v[l].reshape(bs, WINDOW, KV_WIDTH), sinks, attn_gain)
    r_s, s_state = _hgrn_sample(zs3, state_hgrn[l], hgrn_lb_logits, hgrn_gain)
    mem_rows = cache_mem_k.shape[2] * MEM_HEADS
    y_s = _trunk_back(x1s, a_s, r_s, _mem_attn_interleaved_kernel,
                      cache_mem_k[l].reshape(bs, mem_rows, MEM_HEAD_DIM),
                      cache_mem_v[l].reshape(bs, mem_rows, MEM_HEAD_DIM),
                      w, bs, 8, ts).reshape(bs, ts, D_MODEL)

    kv5 = lambda t, n: t.reshape(1, n, WINDOW, ATTN_KV_HEADS, ATTN_HEAD_DIM)
    mem5 = lambda t: t.reshape(1, bp, mt, MEM_HEADS, MEM_HEAD_DIM)
    return (y_p, y_s, kv5(p_wk, bp), kv5(p_wv, bp), p_state[None], mem5(mk), mem5(mv),
            kv5(s_wk, bs), kv5(s_wv, bs), s_state[None])
```

```python
import functools

import jax
import jax.numpy as jnp
from jax import lax
from jax.experimental import pallas as pl
from jax.experimental.pallas import tpu as pltpu

F32 = jnp.float32
BF16 = jnp.bfloat16

D_MODEL = 2048
D_FF = 5632
ATTN_HEADS = 16
ATTN_KV_HEADS = 4
ATTN_GROUP = ATTN_HEADS // ATTN_KV_HEADS
ATTN_HEAD_DIM = 64
WINDOW = 128
ATTN_WIDTH = ATTN_HEADS * ATTN_HEAD_DIM
KV_WIDTH = ATTN_KV_HEADS * ATTN_HEAD_DIM
HGRN_HEADS = 8
HGRN_DK = 128
HGRN_DV = 128
HGRN_KW = HGRN_HEADS * HGRN_DK
HGRN_VW = HGRN_HEADS * HGRN_DV
IN_PROJ_WIDTH = ATTN_WIDTH + 2 * KV_WIDTH + 2 * HGRN_KW + 2 * HGRN_VW
MEM_HEADS = 4
MEM_HEAD_DIM = 128
MEM_WIDTH = MEM_HEADS * MEM_HEAD_DIM
FFN_RESIDUAL = 0.5
EPS = 1e-6

ZB_WIDTH = ATTN_WIDTH + HGRN_KW + 2 * HGRN_VW
ZF_WIDTH = HGRN_KW + 2 * KV_WIDTH
ZB_QA, ZB_QH, ZB_IH, ZB_GH = 0, 1, 2, 3
ZF_FH = 0
ZF_KA, ZF_VA = 4, 5

VMEM_LIMIT_BYTES = 56 * 1024 * 1024

FFN_TOKEN_TILE = 512
PROJ_TOKEN_TILE = 1024
MIX_TOKEN_TILE = 512
FF_TILE = 512
PROJ_COL_TILE = 512
HGRN_CHUNK = 64
HGRN_STEP_CHUNKS = 2
HGRN_SUB = 16
HGRN_SAMPLE_SEQS = 2
ATTN_SAMPLE_SEQS = 8
MAX_EXP = 80.0
MASKED = -1e30

ALIBI_SLOPES = tuple(2.0 ** (-8.0 * (h + 1) / ATTN_HEADS) for h in range(ATTN_HEADS))


def _params(*semantics):
    return pltpu.CompilerParams(dimension_semantics=semantics,
                                vmem_limit_bytes=VMEM_LIMIT_BYTES)


def _rms(x, g):
    return x * lax.rsqrt(jnp.mean(x * x, axis=-1, keepdims=True) + EPS) * g


def _silu(x):
    return x * jax.nn.sigmoid(x)


def _dot(a, b):
    return jnp.dot(a, b, preferred_element_type=F32)


def _dot_nt(a, b):
    return lax.dot_general(a, b, (((1,), (1,)), ((), ())), preferred_element_type=F32)


def _dot_tn(a, b):
    return lax.dot_general(a, b, (((0,), (0,)), ((), ())), preferred_element_type=F32)


def _log2(n):
    assert n & (n - 1) == 0
    return n.bit_length() - 1


def _ffn_kernel(x_ref, gpre_ref, gpost_ref, wg_ref, wu_ref, wd_ref, o_ref, h_scr):
    j = pl.program_id(1)

    @pl.when(j == 0)
    def _():
        h_scr[...] = _rms(x_ref[...], gpre_ref[...]).astype(BF16)
        o_ref[...] = jnp.zeros_like(o_ref)

    h = h_scr[...]
    g = _dot(h, wg_ref[...])
    u = _dot(h, wu_ref[...])
    a = (_silu(g) * u).astype(BF16)
    o_ref[...] += _dot(a, wd_ref[...])

    @pl.when(j == pl.num_programs(1) - 1)
    def _():
        o_ref[...] = x_ref[...] + FFN_RESIDUAL * _rms(o_ref[...], gpost_ref[...])


def _ffn(x, g_pre, g_post, wg, wu, wd):
    t = x.shape[0]
    tm, tf = FFN_TOKEN_TILE, FF_TILE
    return pl.pallas_call(
        _ffn_kernel,
        grid=(t // tm, D_FF // tf),
        in_specs=[
            pl.BlockSpec((tm, D_MODEL), lambda i, j: (i, 0)),
            pl.BlockSpec((1, D_MODEL), lambda i, j: (0, 0)),
            pl.BlockSpec((1, D_MODEL), lambda i, j: (0, 0)),
            pl.BlockSpec((D_MODEL, tf), lambda i, j: (0, j)),
            pl.BlockSpec((D_MODEL, tf), lambda i, j: (0, j)),
            pl.BlockSpec((tf, D_MODEL), lambda i, j: (j, 0)),
        ],
        out_specs=pl.BlockSpec((tm, D_MODEL), lambda i, j: (i, 0)),
        out_shape=jax.ShapeDtypeStruct((t, D_MODEL), F32),
        scratch_shapes=[pltpu.VMEM((tm, D_MODEL), BF16)],
        compiler_params=_params("parallel", "arbitrary"),
        name="ffn_half",
    )(x, g_pre, g_post, wg, wu, wd)


def _in_proj_kernel(x_ref, g_ref, w_ref, zb_ref, zf_ref, h_scr, *, bf16_steps):
    j = pl.program_id(1)

    @pl.when(j == 0)
    def _():
        h_scr[...] = _rms(x_ref[...], g_ref[...]).astype(BF16)

    z = _dot(h_scr[...], w_ref[...])

    @pl.when(j < bf16_steps)
    def _():
        zb_ref[...] = z.astype(BF16)

    @pl.when(j >= bf16_steps)
    def _():
        zf_ref[...] = z


def _in_proj(x, g, w):
    t = x.shape[0]
    tm, tn = min(PROJ_TOKEN_TILE, t), PROJ_COL_TILE
    nb = ZB_WIDTH // tn
    return pl.pallas_call(
        functools.partial(_in_proj_kernel, bf16_steps=nb),
        grid=(t // tm, IN_PROJ_WIDTH // tn),
        in_specs=[
            pl.BlockSpec((tm, D_MODEL), lambda i, j: (i, 0)),
            pl.BlockSpec((1, D_MODEL), lambda i, j: (0, 0)),
            pl.BlockSpec((D_MODEL, tn), lambda i, j: (0, j)),
        ],
        out_specs=[
            pl.BlockSpec((tm, tn), lambda i, j: (i, jnp.minimum(j, nb - 1))),
            pl.BlockSpec((tm, tn), lambda i, j: (i, jnp.maximum(j - nb, 0))),
        ],
        out_shape=[jax.ShapeDtypeStruct((t, ZB_WIDTH), BF16),
                   jax.ShapeDtypeStruct((t, ZF_WIDTH), F32)],
        scratch_shapes=[pltpu.VMEM((tm, D_MODEL), BF16)],
        compiler_params=_params("parallel", "arbitrary"),
        name="in_proj",
    )(x, g, w)


def _attn_prompt_kernel(sink_ref, q_ref, kc_ref, kp_ref, vc_ref, vp_ref, gain_ref, o_ref):
    n = pl.program_id(1)
    scale = ATTN_HEAD_DIM ** -0.5
    k = (jnp.concatenate([kp_ref[...], kc_ref[...]], axis=0) * scale).astype(BF16)
    v = jnp.concatenate([vp_ref[...], vc_ref[...]], axis=0).astype(BF16)
    row = lax.broadcasted_iota(jnp.int32, (WINDOW, 2 * WINDOW), 0)
    col = lax.broadcasted_iota(jnp.int32, (WINDOW, 2 * WINDOW), 1)
    dist = row + WINDOW - col
    valid = (dist >= 0) & (dist < WINDOW) & ((col >= WINDOW) | (n > 0))
    neg_dist = jnp.where(valid, -dist.astype(F32), MASKED)
    outs = []
    for hd in range(ATTN_HEADS):
        kvh = hd // ATTN_GROUP
        kh = k[:, kvh * ATTN_HEAD_DIM:(kvh + 1) * ATTN_HEAD_DIM]
        vh = v[:, kvh * ATTN_HEAD_DIM:(kvh + 1) * ATTN_HEAD_DIM]
        qh = q_ref[:, hd * ATTN_HEAD_DIM:(hd + 1) * ATTN_HEAD_DIM]
        s = _dot_nt(qh, kh) + ALIBI_SLOPES[hd] * neg_dist
        sink = sink_ref[hd]
        m = jnp.maximum(jnp.max(s, axis=-1, keepdims=True), sink)
        p = jnp.exp(s - m)
        denom = jnp.sum(p, axis=-1, keepdims=True) + jnp.exp(sink - m)
        outs.append(_dot(p.astype(BF16), vh) / denom)
    o = jnp.concatenate(outs, axis=1)
    o_ref[...] = _rms(o, gain_ref[...]).astype(BF16)


def _attn_prompt(zb3, zf3, sinks, gain):
    b, s, _ = zb3.shape
    nb = s // WINDOW

    def cur(width_idx):
        return lambda bi, n: (bi, n, width_idx)

    def prev(width_idx):
        return lambda bi, n: (bi, jnp.maximum(n - 1, 0), width_idx)

    return pl.pallas_call(
        _attn_prompt_kernel,
        grid=(b, nb),
        in_specs=[
            pl.BlockSpec(memory_space=pltpu.SMEM),
            pl.BlockSpec((None, WINDOW, ATTN_WIDTH), cur(ZB_QA)),
            pl.BlockSpec((None, WINDOW, KV_WIDTH), cur(ZF_KA)),
            pl.BlockSpec((None, WINDOW, KV_WIDTH), prev(ZF_KA)),
            pl.BlockSpec((None, WINDOW, KV_WIDTH), cur(ZF_VA)),
            pl.BlockSpec((None, WINDOW, KV_WIDTH), prev(ZF_VA)),
            pl.BlockSpec((1, ATTN_WIDTH), lambda bi, n: (0, 0)),
        ],
        out_specs=pl.BlockSpec((None, WINDOW, ATTN_WIDTH), lambda bi, n: (bi, n, 0)),
        out_shape=jax.ShapeDtypeStruct((b, s, ATTN_WIDTH), BF16),
        compiler_params=_params("parallel", "arbitrary"),
        name="attn_prompt",
    )(sinks, zb3, zf3, zf3, zf3, zf3, gain)


def _attn_sample_kernel(sink_ref, q_ref, kn_ref, vn_ref, ck_ref, cv_ref, gain_ref,
                        o_ref, wk_ref, wv_ref, *, t_new):
    seqs = ck_ref.shape[0]
    t_bits = _log2(t_new)
    rows = ATTN_GROUP * t_new
    q_all = q_ref[...].astype(F32).reshape(seqs, t_new, ATTN_WIDTH)
    kn = kn_ref[...].reshape(seqs, t_new, KV_WIDTH)
    vn = vn_ref[...].reshape(seqs, t_new, KV_WIDTH)
    ck = ck_ref[...]
    cv = cv_ref[...]
    wk_ref[:, :WINDOW - t_new, :] = ck[:, t_new:, :]
    wk_ref[:, WINDOW - t_new:, :] = kn
    wv_ref[:, :WINDOW - t_new, :] = cv[:, t_new:, :]
    wv_ref[:, WINDOW - t_new:, :] = vn

    r_c = lax.broadcasted_iota(jnp.int32, (rows, WINDOW), 0)
    j_c = lax.broadcasted_iota(jnp.int32, (rows, WINDOW), 1)
    dist_c = (r_c & (t_new - 1)) + WINDOW - j_c
    neg_c = jnp.where(dist_c < WINDOW, -dist_c.astype(F32), MASKED)
    r_n = lax.broadcasted_iota(jnp.int32, (rows, t_new), 0)
    j_n = lax.broadcasted_iota(jnp.int32, (rows, t_new), 1)
    dist_n = (r_n & (t_new - 1)) - j_n
    neg_n = jnp.where(dist_n >= 0, -dist_n.astype(F32), MASKED)
    head_of_row = lax.broadcasted_iota(jnp.int32, (rows, 1), 0) >> t_bits
    scale = ATTN_HEAD_DIM ** -0.5

    outs = [None] * ATTN_HEADS
    for kvh in range(ATTN_KV_HEADS):
        slope = jnp.zeros((rows, 1), F32)
        sink = jnp.zeros((rows, 1), F32)
        for g in range(ATTN_GROUP):
            hd = kvh * ATTN_GROUP + g
            slope = jnp.where(head_of_row == g, ALIBI_SLOPES[hd], slope)
            sink = jnp.where(head_of_row == g, sink_ref[hd], sink)
        q = jnp.concatenate(
            [q_all[:, :, (kvh * ATTN_GROUP + g) * ATTN_HEAD_DIM:(kvh * ATTN_GROUP + g + 1) * ATTN_HEAD_DIM]
             for g in range(ATTN_GROUP)], axis=1).astype(BF16)
        sl = slice(kvh * ATTN_HEAD_DIM, (kvh + 1) * ATTN_HEAD_DIM)
        ck_h = (ck[:, :, sl] * scale).astype(BF16)
        kn_h = (kn[:, :, sl] * scale).astype(BF16)
        cv_h = cv[:, :, sl].astype(BF16)
        vn_h = vn[:, :, sl].astype(BF16)
        s_c = jnp.einsum('gqd,gkd->gqk', q, ck_h, preferred_element_type=F32) + (slope * neg_c)[None]
        s_n = jnp.einsum('gqd,gkd->gqk', q, kn_h, preferred_element_type=F32) + (slope * neg_n)[None]
        m = jnp.maximum(jnp.maximum(jnp.max(s_c, axis=-1, keepdims=True),
                                    jnp.max(s_n, axis=-1, keepdims=True)), sink[None])
        p_c = jnp.exp(s_c - m)
        p_n = jnp.exp(s_n - m)
        denom = (jnp.sum(p_c, axis=-1, keepdims=True) + jnp.sum(p_n, axis=-1, keepdims=True)
                 + jnp.exp(sink[None] - m))
        o = (jnp.einsum('gqk,gkd->gqd', p_c.astype(BF16), cv_h, preferred_element_type=F32)
             + jnp.einsum('gqk,gkd->gqd', p_n.astype(BF16), vn_h, preferred_element_type=F32)) / denom
        for g in range(ATTN_GROUP):
            outs[kvh * ATTN_GROUP + g] = o[:, g * t_new:(g + 1) * t_new, :]
    o_all = jnp.concatenate(outs, axis=2)
    o_ref[...] = _rms(o_all, gain_ref[...][None]).reshape(seqs * t_new, ATTN_WIDTH)


def _attn_sample(zb, zf, cache_k, cache_v, sinks, gain, t_new):
    b = cache_k.shape[0]
    gs = ATTN_SAMPLE_SEQS
    tok = lambda width, idx: pl.BlockSpec((gs * t_new, width), lambda i: (i, idx))
    cache_spec = pl.BlockSpec((gs, WINDOW, KV_WIDTH), lambda i: (i, 0, 0))
    return pl.pallas_call(
        functools.partial(_attn_sample_kernel, t_new=t_new),
        grid=(b // gs,),
        in_specs=[
            pl.BlockSpec(memory_space=pltpu.SMEM),
            tok(ATTN_WIDTH, ZB_QA), tok(KV_WIDTH, ZF_KA), tok(KV_WIDTH, ZF_VA),
            cache_spec, cache_spec,
            pl.BlockSpec((1, ATTN_WIDTH), lambda i: (0, 0)),
        ],
        out_specs=[tok(ATTN_WIDTH, 0), cache_spec, cache_spec],
        out_shape=[
            jax.ShapeDtypeStruct((b * t_new, ATTN_WIDTH), F32),
            jax.ShapeDtypeStruct((b, WINDOW, KV_WIDTH), F32),
            jax.ShapeDtypeStruct((b, WINDOW, KV_WIDTH), F32),
        ],
        compiler_params=_params("parallel"),
        name="attn_sample",
    )(sinks, zb, zf, zf, cache_k, cache_v, gain)


def _lower_bound(lbl_ref):
    logits = lbl_ref[...]
    e = jnp.exp(logits - jnp.max(logits, axis=0, keepdims=True))
    return e[0:1, :] / jnp.sum(e, axis=0, keepdims=True)


def _hgrn_features(q_raw, f_raw, lb):
    sg = jax.nn.sigmoid(f_raw)
    logf = jnp.log(lb + (1.0 - lb) * sg)
    k = (1.0 - lb) * jax.nn.sigmoid(-f_raw)
    return _silu(q_raw), k, logf


def _cumsum_rows(x, segment):
    row = lax.broadcasted_iota(jnp.int32, x.shape, 0) & (segment - 1)
    shift = 1
    while shift < segment:
        x = x + jnp.where(row >= shift, pltpu.roll(x, shift, axis=0), 0.0)
        shift *= 2
    return x


def _column_broadcast(row_vec, n):
    return jnp.transpose(jnp.broadcast_to(row_vec, (n, n)))


def _hgrn_out(o, gain, g_raw):
    return _rms(o, gain) * _silu(g_raw)


def _hgrn_state_update(s_old, k, l, vb):
    n = l.shape[0]
    l_end = l[n - 1:n, :]
    k_end = k * jnp.exp(l_end - l)
    return (s_old * _column_broadcast(jnp.exp(l_end), HGRN_DK)
            + _dot_tn(k_end.astype(BF16), vb))


def _hgrn_prompt_kernel(lbl_ref, q_ref, f_ref, i_ref, g_ref, gain_ref, r_ref, s_ref):
    c = HGRN_CHUNK
    sub = HGRN_SUB

    @pl.when(pl.program_id(1) == 0)
    def _():
        s_ref[...] = jnp.zeros_like(s_ref)

    lb = _lower_bound(lbl_ref)
    q_all, k_all, logf_all = _hgrn_features(q_ref[...].astype(F32), f_ref[...], lb)
    l_all = _cumsum_rows(logf_all, c)
    row = lax.broadcasted_iota(jnp.int32, (c, HGRN_DK), 0)
    trow = lax.broadcasted_iota(jnp.int32, (c, c), 0)
    tcol = lax.broadcasted_iota(jnp.int32, (c, c), 1)
    sub_bits = _log2(sub)
    diag_mask = ((trow >> sub_bits) == (tcol >> sub_bits)) & (tcol <= trow)
    gain = gain_ref[...]

    for ci in range(HGRN_STEP_CHUNKS):
        rows = slice(ci * c, (ci + 1) * c)
        for h in range(HGRN_HEADS):
            sl = slice(h * HGRN_DK, (h + 1) * HGRN_DK)
            q, k, l = q_all[rows, sl], k_all[rows, sl], l_all[rows, sl]
            vb = i_ref[rows, sl]
            q_parts, k_parts = [], []
            for r in range(sub, c, sub):
                l_r = l[r - 1:r, :]
                q_parts.append(jnp.where(row >= r, q * jnp.exp(jnp.minimum(l - l_r, 0.0)), 0.0))
                k_blk = k[r - sub:r, :] * jnp.exp(l_r - l[r - sub:r, :])
                pieces = [k_blk, jnp.zeros((c - r, HGRN_DK), F32)]
                if r > sub:
                    pieces.insert(0, jnp.zeros((r - sub, HGRN_DK), F32))
                k_parts.append(jnp.concatenate(pieces, axis=0))
            a_off = _dot_nt(jnp.concatenate(q_parts, axis=1).astype(BF16),
                            jnp.concatenate(k_parts, axis=1).astype(BF16))
            l_mid = jnp.concatenate(
                [jnp.broadcast_to(l[b0 + sub // 2 - 1:b0 + sub // 2, :], (sub, HGRN_DK))
                 for b0 in range(0, c, sub)], axis=0)
            q_d = q * jnp.exp(jnp.minimum(l - l_mid, MAX_EXP))
            k_d = k * jnp.exp(jnp.minimum(l_mid - l, MAX_EXP))
            a_d = _dot_nt(q_d.astype(BF16), k_d.astype(BF16))
            a = jnp.where(diag_mask, a_d, a_off)

            s_old = s_ref[h]
            o = _dot(a.astype(BF16), vb) + _dot((q * jnp.exp(l)).astype(BF16), s_old.astype(BF16))
            s_ref[h] = _hgrn_state_update(s_old, k, l, vb)
            r_ref[rows, sl] = _hgrn_out(o, gain, g_ref[rows, sl].astype(F32)).astype(r_ref.dtype)


def _hgrn_prompt(zb3, zf3, lb_logits, gain):
    b, s, _ = zb3.shape
    rows = HGRN_CHUNK * HGRN_STEP_CHUNKS
    seg = lambda idx: pl.BlockSpec((None, rows, HGRN_KW), lambda bi, ci: (bi, ci, idx))
    return pl.pallas_call(
        _hgrn_prompt_kernel,
        grid=(b, s // rows),
        in_specs=[
            pl.BlockSpec(lb_logits.shape, lambda bi, ci: (0, 0)),
            seg(ZB_QH), seg(ZF_FH), seg(ZB_IH), seg(ZB_GH),
            pl.BlockSpec((1, HGRN_DV), lambda bi, ci: (0, 0)),
        ],
        out_specs=[
            pl.BlockSpec((None, rows, HGRN_VW), lambda bi, ci: (bi, ci, 0)),
            pl.BlockSpec((None, HGRN_HEADS, HGRN_DK, HGRN_DV), lambda bi, ci: (bi, 0, 0, 0)),
        ],
        out_shape=[
            jax.ShapeDtypeStruct((b, s, HGRN_VW), BF16),
            jax.ShapeDtypeStruct((b, HGRN_HEADS, HGRN_DK, HGRN_DV), F32),
        ],
        compiler_params=_params("parallel", "arbitrary"),
        name="hgrn_prompt",
    )(lb_logits, zb3, zf3, zb3, zb3, gain)


def _hgrn_sample_kernel(lbl_ref, q_ref, f_ref, i_ref, g_ref, gain_ref, s0_ref, r_ref, s_ref,
                        *, t_new):
    seqs = s0_ref.shape[0]
    lb = _lower_bound(lbl_ref)
    q_all, k_all, logf_all = _hgrn_features(q_ref[...].astype(F32), f_ref[...], lb)
    l_all = _cumsum_rows(logf_all, t_new)
    v_all = i_ref[...].astype(F32)
    g_all = g_ref[...].astype(F32)
    srow = lax.broadcasted_iota(jnp.int32, (t_new, HGRN_DK), 0)
    gain = gain_ref[...]

    for sq in range(seqs):
        rows = slice(sq * t_new, (sq + 1) * t_new)
        for h in range(HGRN_HEADS):
            sl = slice(h * HGRN_DK, (h + 1) * HGRN_DK)
            q, k, l, v = q_all[rows, sl], k_all[rows, sl], l_all[rows, sl], v_all[rows, sl]
            vb = v.astype(BF16)
            o_rows = []
            for t in range(t_new):
                w = jnp.where(srow <= t,
                              jnp.exp(jnp.minimum(l[t:t + 1, :] - l, 0.0)) * k * q[t:t + 1, :], 0.0)
                a_col = jnp.sum(w, axis=-1, keepdims=True)
                o_rows.append(jnp.sum(a_col * v, axis=0, keepdims=True))
            s_old = s0_ref[sq, h]
            o = (jnp.concatenate(o_rows, axis=0)
                 + _dot((q * jnp.exp(l)).astype(BF16), s_old.astype(BF16)))
            s_ref[sq, h] = _hgrn_state_update(s_old, k, l, vb)
            r_ref[rows, sl] = _hgrn_out(o, gain, g_all[rows, sl])


def _hgrn_sample(zb, zf, state, lb_logits, gain, t_new):
    b = state.shape[0]
    gs = HGRN_SAMPLE_SEQS
    seg = lambda idx: pl.BlockSpec((gs * t_new, HGRN_KW), lambda i: (i, idx))
    state_spec = pl.BlockSpec((gs, HGRN_HEADS, HGRN_DK, HGRN_DV), lambda i: (i, 0, 0, 0))
    return pl.pallas_call(
        functools.partial(_hgrn_sample_kernel, t_new=t_new),
        grid=(b // gs,),
        in_specs=[
            pl.BlockSpec(lb_logits.shape, lambda i: (0, 0)),
            seg(ZB_QH), seg(ZF_FH), seg(ZB_IH), seg(ZB_GH),
            pl.BlockSpec((1, HGRN_DV), lambda i: (0, 0)),
            state_spec,
        ],
        out_specs=[seg(0), state_spec],
        out_shape=[
            jax.ShapeDtypeStruct((b * t_new, HGRN_VW), F32),
            jax.ShapeDtypeStruct((b, HGRN_HEADS, HGRN_DK, HGRN_DV), F32),
        ],
        compiler_params=_params("parallel"),
        name="hgrn_sample",
    )(lb_logits, zb, zf, zb, zb, gain, state)


def _mix_out_kernel(x_ref, a_ref, r_ref, wa_ref, wr_ref, gpost_ref, gmem_ref, wq_ref,
                    x2_ref, qm_ref):
    mixed = (_dot(a_ref[...].astype(BF16), wa_ref[...])
             + _dot(r_ref[...].astype(BF16), wr_ref[...]))
    x2 = x_ref[...] + _rms(mixed, gpost_ref[...])
    x2_ref[...] = x2
    qm_ref[...] = _dot(_rms(x2, gmem_ref[...]).astype(BF16), wq_ref[...])


def _mix_out(x, a, r, w_out, g_post, g_mem, w_q):
    t = x.shape[0]
    tm = MIX_TOKEN_TILE
    tok = lambda width: pl.BlockSpec((tm, width), lambda i: (i, 0))
    const = lambda shape, idx=(0, 0): pl.BlockSpec(shape, lambda i: idx)
    return pl.pallas_call(
        _mix_out_kernel,
        grid=(t // tm,),
        in_specs=[
            tok(D_MODEL), tok(ATTN_WIDTH), tok(HGRN_VW),
            const((ATTN_WIDTH, D_MODEL), (0, 0)),
            const((HGRN_VW, D_MODEL), (1, 0)),
            const((1, D_MODEL)), const((1, D_MODEL)),
            const((D_MODEL, MEM_WIDTH)),
        ],
        out_specs=[tok(D_MODEL), tok(MEM_WIDTH)],
        out_shape=[jax.ShapeDtypeStruct((t, D_MODEL), F32),
                   jax.ShapeDtypeStruct((t, MEM_WIDTH), F32)],
        compiler_params=_params("parallel"),
        name="mix_out",
    )(x, a, r, w_out, w_out, g_post, g_mem, w_q)


def _mem_kv_kernel(m_ref, g_ref, wk_ref, wv_ref, k_ref, v_ref, kb_ref, vb_ref):
    h = _rms(m_ref[...], g_ref[...]).astype(BF16)
    k = _dot(h, wk_ref[...])
    v = _dot(h, wv_ref[...])
    k_ref[...] = k
    v_ref[...] = v
    kb_ref[...] = k.astype(BF16)
    vb_ref[...] = v.astype(BF16)


def _mem_kv(mem, g, w_k, w_v):
    b, m, _ = mem.shape
    const = lambda shape: pl.BlockSpec(shape, lambda i: (0, 0))
    out = pl.BlockSpec((None, m, MEM_WIDTH), lambda i: (i, 0, 0))
    return pl.pallas_call(
        _mem_kv_kernel,
        grid=(b,),
        in_specs=[pl.BlockSpec((None, m, D_MODEL), lambda i: (i, 0, 0)), const((1, D_MODEL)),
                  const((D_MODEL, MEM_WIDTH)), const((D_MODEL, MEM_WIDTH))],
        out_specs=[out] * 4,
        out_shape=[jax.ShapeDtypeStruct((b, m, MEM_WIDTH), F32)] * 2
        + [jax.ShapeDtypeStruct((b, m, MEM_WIDTH), BF16)] * 2,
        compiler_params=_params("parallel"),
        name="mem_kv",
    )(mem, g, w_k, w_v)


def _mem_attn_finish(o, x_ref, wo_ref, g_ref, o_ref):
    seqs, tq, _ = x_ref.shape
    y = _dot(o.reshape(seqs * tq, MEM_WIDTH).astype(BF16), wo_ref[...])
    x = x_ref[...].reshape(seqs * tq, D_MODEL)
    o_ref[...] = (x + _rms(y, g_ref[...])).reshape(seqs, tq, D_MODEL)


def _mem_attn_kernel(x_ref, q_ref, mk_ref, mv_ref, wo_ref, g_ref, o_ref):
    scale = MEM_HEAD_DIM ** -0.5
    outs = []
    for h in range(MEM_HEADS):
        sl = slice(h * MEM_HEAD_DIM, (h + 1) * MEM_HEAD_DIM)
        q = q_ref[:, :, sl].astype(BF16)
        s = jnp.einsum('gqd,gkd->gqk', q, mk_ref[:, :, sl], preferred_element_type=F32) * scale
        p = jnp.exp(s - jnp.max(s, axis=-1, keepdims=True))
        denom = jnp.sum(p, axis=-1, keepdims=True)
        outs.append(jnp.einsum('gqk,gkd->gqd', p.astype(BF16), mv_ref[:, :, sl],
                               preferred_element_type=F32) / denom)
    _mem_attn_finish(jnp.concatenate(outs, axis=2), x_ref, wo_ref, g_ref, o_ref)


def _mem_attn_interleaved_kernel(x_ref, q_ref, mk_ref, mv_ref, wo_ref, g_ref, o_ref):
    seqs, tq, _ = q_ref.shape
    rows = MEM_HEADS * tq
    cols = mk_ref.shape[1]
    scale = MEM_HEAD_DIM ** -0.5
    q = jnp.concatenate([q_ref[:, :, h * MEM_HEAD_DIM:(h + 1) * MEM_HEAD_DIM]
                         for h in range(MEM_HEADS)], axis=1).astype(BF16)
    s = jnp.einsum('gqd,gkd->gqk', q, mk_ref[...].astype(BF16), preferred_element_type=F32) * scale
    row_head = lax.broadcasted_iota(jnp.int32, (rows, cols), 0) >> _log2(tq)
    col_head = lax.broadcasted_iota(jnp.int32, (rows, cols), 1) & (MEM_HEADS - 1)
    s = jnp.where((row_head == col_head)[None], s, -jnp.inf)
    p = jnp.exp(s - jnp.max(s, axis=-1, keepdims=True))
    denom = jnp.sum(p, axis=-1, keepdims=True)
    o = jnp.einsum('gqk,gkd->gqd', p.astype(BF16), mv_ref[...].astype(BF16),
                   preferred_element_type=F32) / denom
    o = jnp.concatenate([o[:, h * tq:(h + 1) * tq, :] for h in range(MEM_HEADS)], axis=2)
    _mem_attn_finish(o, x_ref, wo_ref, g_ref, o_ref)


def _mem_attn(body, x3, q3, mem_k, mem_v, w_o, g_post, seqs, tq):
    nseq, slen, _ = x3.shape
    tok = lambda width: pl.BlockSpec((seqs, tq, width), lambda i, j: (i, j, 0))
    mem = pl.BlockSpec((seqs,) + mem_k.shape[1:], lambda i, j: (i, 0, 0))
    const = lambda shape: pl.BlockSpec(shape, lambda i, j: (0, 0))
    return pl.pallas_call(
        body,
        grid=(nseq // seqs, slen // tq),
        in_specs=[tok(D_MODEL), tok(MEM_WIDTH), mem, mem,
                  const((MEM_WIDTH, D_MODEL)), const((1, D_MODEL))],
        out_specs=tok(D_MODEL),
        out_shape=jax.ShapeDtypeStruct((nseq, slen, D_MODEL), F32),
        compiler_params=_params("parallel", "arbitrary"),
        name="mem_attn",
    )(x3, q3, mem_k, mem_v, w_o, g_post)


def _row(g):
    return g.reshape(1, -1)


def _trunk_front(x2d, w):
    x1 = _ffn(x2d, w['ffn1_pre'], w['ffn1_post'], w['ffn1_wg'], w['ffn1_wu'], w['ffn1_wd'])
    zb, zf = _in_proj(x1, w['mix_pre'], w['w_in'])
    return x1, zb, zf


def _trunk_back(x1, a, r, mem_body, mem_k, mem_v, w, nseq, seqs, tq):
    t = x1.shape[0]
    x2, qm = _mix_out(x1, a.reshape(t, ATTN_WIDTH), r.reshape(t, HGRN_VW), w['w_out'],
                      w['mix_post'], w['mem_pre'], w['w_mem_q'])
    x3 = _mem_attn(mem_body, x2.reshape(nseq, t // nseq, D_MODEL),
                   qm.reshape(nseq, t // nseq, MEM_WIDTH),
                   mem_k, mem_v, w['w_mem_o'], w['mem_post'], seqs, tq)
    return _ffn(x3.reshape(t, D_MODEL), w['ffn2_pre'], w['ffn2_post'],
                w['ffn2_wg'], w['ffn2_wu'], w['ffn2_wd'])


def kernel(x_prompt, x_sample, mem_prompt, cache_win_k, cache_win_v, state_hgrn, cache_mem_k, cache_mem_v, ffn1_norm_pre, ffn1_norm_post, ffn1_w_gate, ffn1_w_up, ffn1_w_down, mix_norm_pre, mix_norm_post, w_in, attn_sinks, hgrn_lb_logits, attn_out_gain, hgrn_out_gain, w_out, mem_norm_pre, mem_norm_post, mem_norm_kv, w_mem_q, w_mem_k, w_mem_v, w_mem_o, ffn2_norm_pre, ffn2_norm_post, ffn2_w_gate, ffn2_w_up, ffn2_w_down):
    bp, sp, _ = x_prompt.shape
    bs, ts, _ = x_sample.shape
    mt = mem_prompt.shape[1]
    l = 0

    w_in_l = w_in[l]
    c_k = ATTN_WIDTH
    c_qh = c_k + 2 * KV_WIDTH
    c_f = c_qh + HGRN_KW
    c_i = c_f + HGRN_KW
    w_in_perm = jnp.concatenate(
        [w_in_l[:, :c_k], w_in_l[:, c_qh:c_f], w_in_l[:, c_i:],
         w_in_l[:, c_f:c_i], w_in_l[:, c_k:c_qh]], axis=1)
    w = dict(
        ffn1_pre=_row(ffn1_norm_pre[l]), ffn1_post=_row(ffn1_norm_post[l]),
        ffn1_wg=ffn1_w_gate[l].astype(BF16), ffn1_wu=ffn1_w_up[l].astype(BF16),
        ffn1_wd=ffn1_w_down[l].astype(BF16),
        mix_pre=_row(mix_norm_pre[l]), mix_post=_row(mix_norm_post[l]),
        w_in=w_in_perm.astype(BF16), w_out=w_out[l].astype(BF16),
        mem_pre=_row(mem_norm_pre[l]), mem_post=_row(mem_norm_post[l]),
        w_mem_q=w_mem_q[l].astype(BF16), w_mem_o=w_mem_o[l].astype(BF16),
        ffn2_pre=_row(ffn2_norm_pre[l]), ffn2_post=_row(ffn2_norm_post[l]),
        ffn2_wg=ffn2_w_gate[l].astype(BF16), ffn2_wu=ffn2_w_up[l].astype(BF16),
        ffn2_wd=ffn2_w_down[l].astype(BF16),
    )
    sinks = attn_sinks[l]
    attn_gain = _row(attn_out_gain[l])
    hgrn_gain = _row(hgrn_out_gain[l])

    mk, mv, mk_b, mv_b = _mem_kv(mem_prompt, _row(mem_norm_kv[l]),
                                 w_mem_k[l].astype(BF16), w_mem_v[l].astype(BF16))
    x1, zb, zf = _trunk_front(x_prompt.reshape(bp * sp, D_MODEL), w)
    zb3 = zb.reshape(bp, sp, ZB_WIDTH)
    zf3 = zf.reshape(bp, sp, ZF_WIDTH)
    a = _attn_prompt(zb3, zf3, sinks, attn_gain)
    r, p_state = _hgrn_prompt(zb3, zf3, hgrn_lb_logits, hgrn_gain)
    y_p = _trunk_back(x1, a, r, _mem_attn_kernel, mk_b, mv_b, w, bp, 1, 512).reshape(bp, sp, D_MODEL)
    k_off = ZF_KA * KV_WIDTH
    v_off = ZF_VA * KV_WIDTH
    p_wk = zf3[:, sp - WINDOW:, k_off:k_off + KV_WIDTH]
    p_wv = zf3[:, sp - WINDOW:, v_off:v_off + KV_WIDTH]

    x1s, zbs, zfs = _trunk_front(x_sample.reshape(bs * ts, D_MODEL), w)
    a_s, s_wk, s_wv = _attn_sample(zbs, zfs, cache_win_k[l].reshape(bs, WINDOW, KV_WIDTH),
                                   cache_win_v[l].reshape(bs, WINDOW, KV_WIDTH), sinks, attn_gain, ts)
    r_s, s_state = _hgrn_sample(zbs, zfs, state_hgrn[l], hgrn_lb_logits, hgrn_gain, ts)
    mem_rows = cache_mem_k.shape[2] * MEM_HEADS
    y_s = _trunk_back(x1s, a_s, r_s, _mem_attn_interleaved_kernel,
                      cache_mem_k[l].reshape(bs, mem_rows, MEM_HEAD_DIM),
                      cache_mem_v[l].reshape(bs, mem_rows, MEM_HEAD_DIM),
                      w, bs, 8, ts).reshape(bs, ts, D_MODEL)

    kv5 = lambda t, n: t.reshape(1, n, WINDOW, ATTN_KV_HEADS, ATTN_HEAD_DIM)
    mem5 = lambda t: t.reshape(1, bp, mt, MEM_HEADS, MEM_HEAD_DIM)
    return (y_p, y_s, kv5(p_wk, bp), kv5(p_wv, bp), p_state[None], mem5(mk), mem5(mv),
            kv5(s_wk, bs), kv5(s_wv, bs), s_state[None])
```

```python
import functools

import jax
import jax.numpy as jnp
from jax import lax
from jax.experimental import pallas as pl
from jax.experimental.pallas import tpu as pltpu

F32 = jnp.float32
BF16 = jnp.bfloat16

D_MODEL = 2048
D_FF = 5632
ATTN_HEADS = 16
ATTN_KV_HEADS = 4
ATTN_GROUP = ATTN_HEADS // ATTN_KV_HEADS
ATTN_HEAD_DIM = 64
WINDOW = 128
ATTN_WIDTH = ATTN_HEADS * ATTN_HEAD_DIM
KV_WIDTH = ATTN_KV_HEADS * ATTN_HEAD_DIM
HGRN_HEADS = 8
HGRN_DK = 128
HGRN_DV = 128
HGRN_KW = HGRN_HEADS * HGRN_DK
HGRN_VW = HGRN_HEADS * HGRN_DV
IN_PROJ_WIDTH = ATTN_WIDTH + 2 * KV_WIDTH + 2 * HGRN_KW + 2 * HGRN_VW
MEM_HEADS = 4
MEM_HEAD_DIM = 128
MEM_WIDTH = MEM_HEADS * MEM_HEAD_DIM
FFN_RESIDUAL = 0.5
EPS = 1e-6

ZB_WIDTH = ATTN_WIDTH + HGRN_KW + 2 * HGRN_VW
ZF_WIDTH = HGRN_KW + 2 * KV_WIDTH
ZB_QA, ZB_QH, ZB_IH, ZB_GH = 0, 1, 2, 3
ZF_FH = 0
ZF_KA, ZF_VA = 4, 5

VMEM_LIMIT_BYTES = 56 * 1024 * 1024

FFN_TOKEN_TILE = 512
PROJ_TOKEN_TILE = 1024
MIX_TOKEN_TILE = 512
FF_TILE = 512
PROJ_COL_TILE = 512
HGRN_CHUNK = 128
HGRN_STEP_CHUNKS = 2
HGRN_SAMPLE_SEQS = 4
ATTN_SAMPLE_SEQS = 8
MASKED = -1e30

ALIBI_SLOPES = tuple(2.0 ** (-8.0 * (h + 1) / ATTN_HEADS) for h in range(ATTN_HEADS))


def _params(*semantics):
    return pltpu.CompilerParams(dimension_semantics=semantics,
                                vmem_limit_bytes=VMEM_LIMIT_BYTES)


def _rms(x, g):
    return x * lax.rsqrt(jnp.mean(x * x, axis=-1, keepdims=True) + EPS) * g


def _silu(x):
    return x * jax.nn.sigmoid(x)


def _dot(a, b):
    return jnp.dot(a, b, preferred_element_type=F32)


def _dot_nt(a, b):
    return lax.dot_general(a, b, (((1,), (1,)), ((), ())), preferred_element_type=F32)


def _dot_tn(a, b):
    return lax.dot_general(a, b, (((0,), (0,)), ((), ())), preferred_element_type=F32)


def _log2(n):
    assert n & (n - 1) == 0
    return n.bit_length() - 1


def _ffn_kernel(x_ref, gpre_ref, gpost_ref, wg_ref, wu_ref, wd_ref, o_ref, h_scr):
    j = pl.program_id(1)

    def partial_down(h):
        g = _dot(h, wg_ref[...])
        u = _dot(h, wu_ref[...])
        return _dot((_silu(g) * u).astype(BF16), wd_ref[...])

    @pl.when(j == 0)
    def _():
        h = _rms(x_ref[...], gpre_ref[...]).astype(BF16)
        h_scr[...] = h
        o_ref[...] = partial_down(h)

    @pl.when(j > 0)
    def _():
        o_ref[...] += partial_down(h_scr[...])

    @pl.when(j == pl.num_programs(1) - 1)
    def _():
        o_ref[...] = x_ref[...] + _rms(o_ref[...], FFN_RESIDUAL * gpost_ref[...])


def _ffn(x, g_pre, g_post, wg, wu, wd):
    t = x.shape[0]
    tm, tf = FFN_TOKEN_TILE, FF_TILE
    return pl.pallas_call(
        _ffn_kernel,
        grid=(t // tm, D_FF // tf),
        in_specs=[
            pl.BlockSpec((tm, D_MODEL), lambda i, j: (i, 0)),
            pl.BlockSpec((1, D_MODEL), lambda i, j: (0, 0)),
            pl.BlockSpec((1, D_MODEL), lambda i, j: (0, 0)),
            pl.BlockSpec((D_MODEL, tf), lambda i, j: (0, j)),
            pl.BlockSpec((D_MODEL, tf), lambda i, j: (0, j)),
            pl.BlockSpec((tf, D_MODEL), lambda i, j: (j, 0)),
        ],
        out_specs=pl.BlockSpec((tm, D_MODEL), lambda i, j: (i, 0)),
        out_shape=jax.ShapeDtypeStruct((t, D_MODEL), F32),
        scratch_shapes=[pltpu.VMEM((tm, D_MODEL), BF16)],
        compiler_params=_params("parallel", "arbitrary"),
        name="ffn_half",
    )(x, g_pre, g_post, wg, wu, wd)


def _in_proj_kernel(x_ref, g_ref, w_ref, zb_ref, zf_ref, h_scr, *, bf16_steps):
    j = pl.program_id(1)

    @pl.when(j == 0)
    def _():
        h_scr[...] = _rms(x_ref[...], g_ref[...]).astype(BF16)

    z = _dot(h_scr[...], w_ref[...])

    @pl.when(j < bf16_steps)
    def _():
        zb_ref[...] = z.astype(BF16)

    @pl.when(j >= bf16_steps)
    def _():
        zf_ref[...] = z


def _in_proj(x, g, w):
    t = x.shape[0]
    tm, tn = min(PROJ_TOKEN_TILE, t), PROJ_COL_TILE
    nb = ZB_WIDTH // tn
    return pl.pallas_call(
        functools.partial(_in_proj_kernel, bf16_steps=nb),
        grid=(t // tm, IN_PROJ_WIDTH // tn),
        in_specs=[
            pl.BlockSpec((tm, D_MODEL), lambda i, j: (i, 0)),
            pl.BlockSpec((1, D_MODEL), lambda i, j: (0, 0)),
            pl.BlockSpec((D_MODEL, tn), lambda i, j: (0, j)),
        ],
        out_specs=[
            pl.BlockSpec((tm, tn), lambda i, j: (i, jnp.minimum(j, nb - 1))),
            pl.BlockSpec((tm, tn), lambda i, j: (i, jnp.maximum(j - nb, 0))),
        ],
        out_shape=[jax.ShapeDtypeStruct((t, ZB_WIDTH), BF16),
                   jax.ShapeDtypeStruct((t, ZF_WIDTH), F32)],
        scratch_shapes=[pltpu.VMEM((tm, D_MODEL), BF16)],
        compiler_params=_params("parallel", "arbitrary"),
        name="in_proj",
    )(x, g, w)


def _attn_prompt_kernel(sink_ref, q_ref, kc_ref, kp_ref, vc_ref, vp_ref, gain_ref, o_ref):
    n = pl.program_id(1)
    scale = ATTN_HEAD_DIM ** -0.5
    k = (jnp.concatenate([kp_ref[...], kc_ref[...]], axis=0) * scale).astype(BF16)
    v = jnp.concatenate([vp_ref[...], vc_ref[...]], axis=0).astype(BF16)
    row = lax.broadcasted_iota(jnp.int32, (WINDOW, 2 * WINDOW), 0)
    col = lax.broadcasted_iota(jnp.int32, (WINDOW, 2 * WINDOW), 1)
    dist = row + WINDOW - col
    valid = (dist >= 0) & (dist < WINDOW) & ((col >= WINDOW) | (n > 0))
    neg_dist = jnp.where(valid, -dist.astype(F32), MASKED)
    outs = []
    for hd in range(ATTN_HEADS):
        kvh = hd // ATTN_GROUP
        kh = k[:, kvh * ATTN_HEAD_DIM:(kvh + 1) * ATTN_HEAD_DIM]
        vh = v[:, kvh * ATTN_HEAD_DIM:(kvh + 1) * ATTN_HEAD_DIM]
        qh = q_ref[:, hd * ATTN_HEAD_DIM:(hd + 1) * ATTN_HEAD_DIM]
        s = _dot_nt(qh, kh) + ALIBI_SLOPES[hd] * neg_dist
        sink = sink_ref[hd]
        m = jnp.maximum(jnp.max(s, axis=-1, keepdims=True), sink)
        p = jnp.exp(s - m)
        denom = jnp.sum(p, axis=-1, keepdims=True) + jnp.exp(sink - m)
        outs.append(_dot(p.astype(BF16), vh) / denom)
    o = jnp.concatenate(outs, axis=1)
    o_ref[...] = _rms(o, gain_ref[...]).astype(BF16)


def _attn_prompt(zb3, zf3, sinks, gain):
    b, s, _ = zb3.shape
    nb = s // WINDOW

    def cur(width_idx):
        return lambda bi, n: (bi, n, width_idx)

    def prev(width_idx):
        return lambda bi, n: (bi, jnp.maximum(n - 1, 0), width_idx)

    return pl.pallas_call(
        _attn_prompt_kernel,
        grid=(b, nb),
        in_specs=[
            pl.BlockSpec(memory_space=pltpu.SMEM),
            pl.BlockSpec((None, WINDOW, ATTN_WIDTH), cur(ZB_QA)),
            pl.BlockSpec((None, WINDOW, KV_WIDTH), cur(ZF_KA)),
            pl.BlockSpec((None, WINDOW, KV_WIDTH), prev(ZF_KA)),
            pl.BlockSpec((None, WINDOW, KV_WIDTH), cur(ZF_VA)),
            pl.BlockSpec((None, WINDOW, KV_WIDTH), prev(ZF_VA)),
            pl.BlockSpec((1, ATTN_WIDTH), lambda bi, n: (0, 0)),
        ],
        out_specs=pl.BlockSpec((None, WINDOW, ATTN_WIDTH), lambda bi, n: (bi, n, 0)),
        out_shape=jax.ShapeDtypeStruct((b, s, ATTN_WIDTH), BF16),
        compiler_params=_params("parallel", "arbitrary"),
        name="attn_prompt",
    )(sinks, zb3, zf3, zf3, zf3, zf3, gain)


def _attn_sample_kernel(sink_ref, q_ref, kn_ref, vn_ref, ck_ref, cv_ref, gain_ref,
                        o_ref, wk_ref, wv_ref, *, t_new):
    seqs = ck_ref.shape[0]
    t_bits = _log2(t_new)
    rows = ATTN_GROUP * t_new
    q_all = q_ref[...].astype(F32).reshape(seqs, t_new, ATTN_WIDTH)
    kn = kn_ref[...].reshape(seqs, t_new, KV_WIDTH)
    vn = vn_ref[...].reshape(seqs, t_new, KV_WIDTH)
    ck = ck_ref[...]
    cv = cv_ref[...]
    wk_ref[:, :WINDOW - t_new, :] = ck[:, t_new:, :]
    wk_ref[:, WINDOW - t_new:, :] = kn
    wv_ref[:, :WINDOW - t_new, :] = cv[:, t_new:, :]
    wv_ref[:, WINDOW - t_new:, :] = vn

    r_c = lax.broadcasted_iota(jnp.int32, (rows, WINDOW), 0)
    j_c = lax.broadcasted_iota(jnp.int32, (rows, WINDOW), 1)
    dist_c = (r_c & (t_new - 1)) + WINDOW - j_c
    neg_c = jnp.where(dist_c < WINDOW, -dist_c.astype(F32), MASKED)
    r_n = lax.broadcasted_iota(jnp.int32, (rows, t_new), 0)
    j_n = lax.broadcasted_iota(jnp.int32, (rows, t_new), 1)
    dist_n = (r_n & (t_new - 1)) - j_n
    neg_n = jnp.where(dist_n >= 0, -dist_n.astype(F32), MASKED)
    head_of_row = lax.broadcasted_iota(jnp.int32, (rows, 1), 0) >> t_bits
    scale = ATTN_HEAD_DIM ** -0.5

    outs = [None] * ATTN_HEADS
    for kvh in range(ATTN_KV_HEADS):
        slope = jnp.zeros((rows, 1), F32)
        sink = jnp.zeros((rows, 1), F32)
        for g in range(ATTN_GROUP):
            hd = kvh * ATTN_GROUP + g
            slope = jnp.where(head_of_row == g, ALIBI_SLOPES[hd], slope)
            sink = jnp.where(head_of_row == g, sink_ref[hd], sink)
        q = jnp.concatenate(
            [q_all[:, :, (kvh * ATTN_GROUP + g) * ATTN_HEAD_DIM:(kvh * ATTN_GROUP + g + 1) * ATTN_HEAD_DIM]
             for g in range(ATTN_GROUP)], axis=1).astype(BF16)
        sl = slice(kvh * ATTN_HEAD_DIM, (kvh + 1) * ATTN_HEAD_DIM)
        ck_h = (ck[:, :, sl] * scale).astype(BF16)
        kn_h = (kn[:, :, sl] * scale).astype(BF16)
        cv_h = cv[:, :, sl].astype(BF16)
        vn_h = vn[:, :, sl].astype(BF16)
        s_c = jnp.einsum('gqd,gkd->gqk', q, ck_h, preferred_element_type=F32) + (slope * neg_c)[None]
        s_n = jnp.einsum('gqd,gkd->gqk', q, kn_h, preferred_element_type=F32) + (slope * neg_n)[None]
        m = jnp.maximum(jnp.maximum(jnp.max(s_c, axis=-1, keepdims=True),
                                    jnp.max(s_n, axis=-1, keepdims=True)), sink[None])
        p_c = jnp.exp(s_c - m)
        p_n = jnp.exp(s_n - m)
        denom = (jnp.sum(p_c, axis=-1, keepdims=True) + jnp.sum(p_n, axis=-1, keepdims=True)
                 + jnp.exp(sink[None] - m))
        o = (jnp.einsum('gqk,gkd->gqd', p_c.astype(BF16), cv_h, preferred_element_type=F32)
             + jnp.einsum('gqk,gkd->gqd', p_n.astype(BF16), vn_h, preferred_element_type=F32)) / denom
        for g in range(ATTN_GROUP):
            outs[kvh * ATTN_GROUP + g] = o[:, g * t_new:(g + 1) * t_new, :]
    o_all = jnp.concatenate(outs, axis=2)
    o_ref[...] = _rms(o_all, gain_ref[...][None]).reshape(seqs * t_new, ATTN_WIDTH)


def _attn_sample(zb, zf, cache_k, cache_v, sinks, gain, t_new):
    b = cache_k.shape[0]
    gs = ATTN_SAMPLE_SEQS
    tok = lambda width, idx: pl.BlockSpec((gs * t_new, width), lambda i: (i, idx))
    cache_spec = pl.BlockSpec((gs, WINDOW, KV_WIDTH), lambda i: (i, 0, 0))
    return pl.pallas_call(
        functools.partial(_attn_sample_kernel, t_new=t_new),
        grid=(b // gs,),
        in_specs=[
            pl.BlockSpec(memory_space=pltpu.SMEM),
            tok(ATTN_WIDTH, ZB_QA), tok(KV_WIDTH, ZF_KA), tok(KV_WIDTH, ZF_VA),
            cache_spec, cache_spec,
            pl.BlockSpec((1, ATTN_WIDTH), lambda i: (0, 0)),
        ],
        out_specs=[tok(ATTN_WIDTH, 0), cache_spec, cache_spec],
        out_shape=[
            jax.ShapeDtypeStruct((b * t_new, ATTN_WIDTH), F32),
            jax.ShapeDtypeStruct((b, WINDOW, KV_WIDTH), F32),
            jax.ShapeDtypeStruct((b, WINDOW, KV_WIDTH), F32),
        ],
        compiler_params=_params("parallel"),
        name="attn_sample",
    )(sinks, zb, zf, zf, cache_k, cache_v, gain)


def _lower_bound(lbl_ref):
    logits = lbl_ref[...]
    e = jnp.exp(logits - jnp.max(logits, axis=0, keepdims=True))
    return e[0:1, :] / jnp.sum(e, axis=0, keepdims=True)


def _hgrn_features(q_raw, f_raw, lb):
    sg = jax.nn.sigmoid(f_raw)
    f = lb + (1.0 - lb) * sg
    k = (1.0 - lb) * (1.0 - sg)
    return _silu(q_raw), k, f, jnp.log2(f)


def _cumsum_rows(x, segment):
    row = lax.broadcasted_iota(jnp.int32, x.shape, 0) & (segment - 1)
    shift = 1
    while shift < segment:
        x = x + jnp.where(row >= shift, pltpu.roll(x, shift, axis=0), 0.0)
        shift *= 2
    return x


def _split3(x):
    hi = x.astype(BF16)
    r = x - hi.astype(F32)
    mid = r.astype(BF16)
    lo = (r - mid.astype(F32)).astype(BF16)
    return hi, mid, lo


def _cumsum_rows_mxu(x, tri):
    hi, mid, lo = _split3(x)
    return _dot(tri, hi) + _dot(tri, mid) + _dot(tri, lo)


def _column_broadcast(row_vec, n):
    rows = 16
    hi, mid, lo = _split3(row_vec)
    terms = jnp.concatenate([hi, mid, lo, jnp.zeros((rows - 3, n), BF16)], axis=0)
    ones = (lax.broadcasted_iota(jnp.int32, (rows, n), 0) < 3).astype(BF16)
    return _dot_tn(terms, ones)


def _hgrn_out(o, gain, g_raw):
    return _rms(o, gain) * _silu(g_raw)


def _hgrn_state_update(s_old, k, l2, vb):
    n = l2.shape[0]
    l_end = l2[n - 1:n, :]
    k_end = k * jnp.exp2(l_end - l2)
    return (s_old * _column_broadcast(jnp.exp2(l_end), HGRN_DK)
            + _dot_tn(k_end.astype(BF16), vb))


def _pair_levels(c):
    t = lax.broadcasted_iota(jnp.int32, (c, c), 0)
    s = lax.broadcasted_iota(jnp.int32, (c, c), 1)
    masks = [t == s]
    half = 1
    while half < c:
        block = 2 * half
        same_block = (t >> _log2(block)) == (s >> _log2(block))
        masks.append(same_block & ((t & (block - 1)) >= half) & ((s & (block - 1)) < half))
        half = block
    return masks


def _pair_factors(q, k, f, log2f, l2):
    c, w = q.shape
    pos = lax.broadcasted_iota(jnp.int32, (c, w), 0) & 3
    e2 = jnp.exp2(jnp.where(pos == 0, pltpu.roll(log2f, c - 1, axis=0),
                            jnp.where(pos == 1, 0.0,
                                      jnp.where(pos == 2, log2f,
                                                log2f + pltpu.roll(log2f, 1, axis=0)))))
    qs = [q, q * f, q * e2]
    ks = [k, k, k * e2]
    half = 4
    while half < c:
        block = 2 * half
        ref = l2.reshape(c // block, block, w)[:, half - 1:half, :]
        ref = jnp.broadcast_to(ref, (c // block, block, w)).reshape(c, w)
        e = jnp.exp2(-jnp.abs(l2 - ref))
        qs.append(q * e)
        ks.append(k * e)
        half = block
    return [x.astype(BF16) for x in qs], [x.astype(BF16) for x in ks]


def _hgrn_prompt_kernel(lbl_ref, q_ref, f_ref, i_ref, g_ref, gain_ref, r_ref, s_ref):
    c = HGRN_CHUNK

    @pl.when(pl.program_id(1) == 0)
    def _():
        s_ref[...] = jnp.zeros_like(s_ref)

    lb = _lower_bound(lbl_ref)
    q_all, k_all, f_all, log2f_all = _hgrn_features(q_ref[...].astype(F32), f_ref[...], lb)
    masks = _pair_levels(c)
    tri = (lax.broadcasted_iota(jnp.int32, (c, c), 1)
           <= lax.broadcasted_iota(jnp.int32, (c, c), 0)).astype(BF16)
    gain = gain_ref[...]

    chunks = []
    for ci in range(HGRN_STEP_CHUNKS):
        rows = slice(ci * c, (ci + 1) * c)
        l2_c = _cumsum_rows_mxu(log2f_all[rows], tri)
        q_lv, k_lv = _pair_factors(q_all[rows], k_all[rows], f_all[rows], log2f_all[rows], l2_c)
        l_end = l2_c[c - 1:c, :]
        q_hat = (q_all[rows] * jnp.exp2(l2_c)).astype(BF16)
        k_end = (k_all[rows] * jnp.exp2(l_end - l2_c)).astype(BF16)
        chunks.append((rows, q_lv, k_lv, q_hat, k_end, jnp.exp2(l_end)))

    for rows, q_lv, k_lv, q_hat, k_end, decay_end in chunks:
        for h in range(HGRN_HEADS):
            sl = slice(h * HGRN_DK, (h + 1) * HGRN_DK)
            vb = i_ref[rows, sl]
            a = jnp.zeros((c, c), F32)
            for mask, q_f, k_f in zip(masks, q_lv, k_lv):
                a = jnp.where(mask, _dot_nt(q_f[:, sl], k_f[:, sl]), a)
            s_old = s_ref[h]
            o = _dot(a.astype(BF16), vb) + _dot(q_hat[:, sl], s_old.astype(BF16))
            s_ref[h] = (s_old * _column_broadcast(decay_end[:, sl], HGRN_DK)
                        + _dot_tn(k_end[:, sl], vb))
            r_ref[rows, sl] = _hgrn_out(o, gain, g_ref[rows, sl].astype(F32)).astype(r_ref.dtype)


def _hgrn_prompt(zb3, zf3, lb_logits, gain):
    b, s, _ = zb3.shape
    rows = HGRN_CHUNK * HGRN_STEP_CHUNKS
    seg = lambda idx: pl.BlockSpec((None, rows, HGRN_KW), lambda bi, ci: (bi, ci, idx))
    return pl.pallas_call(
        _hgrn_prompt_kernel,
        grid=(b, s // rows),
        in_specs=[
            pl.BlockSpec(lb_logits.shape, lambda bi, ci: (0, 0)),
            seg(ZB_QH), seg(ZF_FH), seg(ZB_IH), seg(ZB_GH),
            pl.BlockSpec((1, HGRN_DV), lambda bi, ci: (0, 0)),
        ],
        out_specs=[
            pl.BlockSpec((None, rows, HGRN_VW), lambda bi, ci: (bi, ci, 0)),
            pl.BlockSpec((None, HGRN_HEADS, HGRN_DK, HGRN_DV), lambda bi, ci: (bi, 0, 0, 0)),
        ],
        out_shape=[
            jax.ShapeDtypeStruct((b, s, HGRN_VW), BF16),
            jax.ShapeDtypeStruct((b, HGRN_HEADS, HGRN_DK, HGRN_DV), F32),
        ],
        compiler_params=_params("parallel", "arbitrary"),
        name="hgrn_prompt",
    )(lb_logits, zb3, zf3, zb3, zb3, gain)


def _hgrn_sample_kernel(lbl_ref, seg_ref, segt_ref, q_ref, f_ref, i_ref, g_ref, gain_ref, s0_ref,
                        r_ref, s_ref, *, t_new):
    seqs = s0_ref.shape[0]
    n = seqs * t_new
    width = HGRN_KW
    lb = _lower_bound(lbl_ref)
    q_all, k_all, _, log2f_all = _hgrn_features(q_ref[...].astype(F32), f_ref[...], lb)
    l2_all = _cumsum_rows(log2f_all, t_new)
    v_all = i_ref[...].astype(F32)
    g_all = g_ref[...].astype(F32)
    gain = gain_ref[...]

    def row_of_seq(x, t):
        x3 = x.reshape(seqs, t_new, width)
        return jnp.broadcast_to(x3[:, t:t + 1, :], (seqs, t_new, width)).reshape(n, width)

    src = lax.broadcasted_iota(jnp.int32, (n, width), 0) & (t_new - 1)
    w_parts = []
    for t in range(t_new):
        decay = jnp.exp2(jnp.minimum(row_of_seq(l2_all, t) - l2_all, 0.0))
        w_parts.append(jnp.where(src <= t, decay * k_all * row_of_seq(q_all, t), 0.0))
    w = jnp.concatenate(w_parts, axis=0).astype(BF16)
    a_heads = _dot(w, seg_ref[...])
    a_lanes = _dot(a_heads.astype(BF16), segt_ref[...])
    o_rows = []
    for t in range(t_new):
        p = (a_lanes[t * n:(t + 1) * n] * v_all).reshape(seqs, t_new, width)
        o_rows.append(jnp.sum(p, axis=1, keepdims=True))
    o_intra = jnp.concatenate(o_rows, axis=1).reshape(n, width)

    l_end = row_of_seq(l2_all, t_new - 1)
    q_hat = q_all * jnp.exp2(l2_all)
    k_end = k_all * jnp.exp2(l_end - l2_all)
    decay_end = jnp.exp2(l_end)
    for h in range(HGRN_HEADS):
        sl = slice(h * HGRN_DK, (h + 1) * HGRN_DK)
        o_parts = []
        for sq in range(seqs):
            rows = slice(sq * t_new, (sq + 1) * t_new)
            s_old = s0_ref[sq, h]
            o_parts.append(_dot(q_hat[rows, sl].astype(BF16), s_old.astype(BF16)))
            s_ref[sq, h] = (s_old * _column_broadcast(decay_end[sq * t_new:sq * t_new + 1, sl], HGRN_DK)
                            + _dot_tn(k_end[rows, sl].astype(BF16), v_all[rows, sl].astype(BF16)))
        o = o_intra[:, sl] + jnp.concatenate(o_parts, axis=0)
        r_ref[:, sl] = _hgrn_out(o, gain, g_all[:, sl])


def _hgrn_sample(zb, zf, state, lb_logits, gain, t_new):
    b = state.shape[0]
    gs = HGRN_SAMPLE_SEQS
    seg = lambda idx: pl.BlockSpec((gs * t_new, HGRN_KW), lambda i: (i, idx))
    state_spec = pl.BlockSpec((gs, HGRN_HEADS, HGRN_DK, HGRN_DV), lambda i: (i, 0, 0, 0))
    lane_head = jnp.arange(HGRN_KW, dtype=jnp.int32) // HGRN_DK
    head_seg = (lane_head[:, None] == jnp.arange(HGRN_DK, dtype=jnp.int32)[None, :]).astype(BF16)
    return pl.pallas_call(
        functools.partial(_hgrn_sample_kernel, t_new=t_new),
        grid=(b // gs,),
        in_specs=[
            pl.BlockSpec(lb_logits.shape, lambda i: (0, 0)),
            pl.BlockSpec((HGRN_KW, HGRN_DK), lambda i: (0, 0)),
            pl.BlockSpec((HGRN_DK, HGRN_KW), lambda i: (0, 0)),
            seg(ZB_QH), seg(ZF_FH), seg(ZB_IH), seg(ZB_GH),
            pl.BlockSpec((1, HGRN_DV), lambda i: (0, 0)),
            state_spec,
        ],
        out_specs=[seg(0), state_spec],
        out_shape=[
            jax.ShapeDtypeStruct((b * t_new, HGRN_VW), F32),
            jax.ShapeDtypeStruct((b, HGRN_HEADS, HGRN_DK, HGRN_DV), F32),
        ],
        compiler_params=_params("parallel"),
        name="hgrn_sample",
    )(lb_logits, head_seg, head_seg.T, zb, zf, zb, zb, gain, state)


def _mix_out_kernel(x_ref, a_ref, r_ref, wa_ref, wr_ref, gpost_ref, gmem_ref, wq_ref,
                    x2_ref, qm_ref):
    mixed = (_dot(a_ref[...].astype(BF16), wa_ref[...])
             + _dot(r_ref[...].astype(BF16), wr_ref[...]))
    x2 = x_ref[...] + _rms(mixed, gpost_ref[...])
    x2_ref[...] = x2
    qm_ref[...] = _dot(_rms(x2, gmem_ref[...]).astype(BF16), wq_ref[...])


def _mix_out(x, a, r, w_out, g_post, g_mem, w_q):
    t = x.shape[0]
    tm = MIX_TOKEN_TILE
    tok = lambda width: pl.BlockSpec((tm, width), lambda i: (i, 0))
    const = lambda shape, idx=(0, 0): pl.BlockSpec(shape, lambda i: idx)
    return pl.pallas_call(
        _mix_out_kernel,
        grid=(t // tm,),
        in_specs=[
            tok(D_MODEL), tok(ATTN_WIDTH), tok(HGRN_VW),
            const((ATTN_WIDTH, D_MODEL), (0, 0)),
            const((HGRN_VW, D_MODEL), (1, 0)),
            const((1, D_MODEL)), const((1, D_MODEL)),
            const((D_MODEL, MEM_WIDTH)),
        ],
        out_specs=[tok(D_MODEL), tok(MEM_WIDTH)],
        out_shape=[jax.ShapeDtypeStruct((t, D_MODEL), F32),
                   jax.ShapeDtypeStruct((t, MEM_WIDTH), F32)],
        compiler_params=_params("parallel"),
        name="mix_out",
    )(x, a, r, w_out, w_out, g_post, g_mem, w_q)


def _mem_kv_kernel(m_ref, g_ref, wk_ref, wv_ref, k_ref, v_ref, kb_ref, vb_ref):
    h = _rms(m_ref[...], g_ref[...]).astype(BF16)
    k = _dot(h, wk_ref[...])
    v = _dot(h, wv_ref[...])
    k_ref[...] = k
    v_ref[...] = v
    kb_ref[...] = k.astype(BF16)
    vb_ref[...] = v.astype(BF16)


def _mem_kv(mem, g, w_k, w_v):
    b, m, _ = mem.shape
    const = lambda shape: pl.BlockSpec(shape, lambda i: (0, 0))
    out = pl.BlockSpec((None, m, MEM_WIDTH), lambda i: (i, 0, 0))
    return pl.pallas_call(
        _mem_kv_kernel,
        grid=(b,),
        in_specs=[pl.BlockSpec((None, m, D_MODEL), lambda i: (i, 0, 0)), const((1, D_MODEL)),
                  const((D_MODEL, MEM_WIDTH)), const((D_MODEL, MEM_WIDTH))],
        out_specs=[out] * 4,
        out_shape=[jax.ShapeDtypeStruct((b, m, MEM_WIDTH), F32)] * 2
        + [jax.ShapeDtypeStruct((b, m, MEM_WIDTH), BF16)] * 2,
        compiler_params=_params("parallel"),
        name="mem_kv",
    )(mem, g, w_k, w_v)


def _mem_attn_finish(o, x_ref, wo_ref, g_ref, o_ref):
    seqs, tq, _ = x_ref.shape
    y = _dot(o.reshape(seqs * tq, MEM_WIDTH).astype(BF16), wo_ref[...])
    x = x_ref[...].reshape(seqs * tq, D_MODEL)
    o_ref[...] = (x + _rms(y, g_ref[...])).reshape(seqs, tq, D_MODEL)


def _mem_attn_kernel(x_ref, q_ref, mk_ref, mv_ref, wo_ref, g_ref, o_ref):
    scale = MEM_HEAD_DIM ** -0.5
    outs = []
    for h in range(MEM_HEADS):
        sl = slice(h * MEM_HEAD_DIM, (h + 1) * MEM_HEAD_DIM)
        q = q_ref[:, :, sl].astype(BF16)
        s = jnp.einsum('gqd,gkd->gqk', q, mk_ref[:, :, sl], preferred_element_type=F32) * scale
        p = jnp.exp(s - jnp.max(s, axis=-1, keepdims=True))
        denom = jnp.sum(p, axis=-1, keepdims=True)
        outs.append(jnp.einsum('gqk,gkd->gqd', p.astype(BF16), mv_ref[:, :, sl],
                               preferred_element_type=F32) / denom)
    _mem_attn_finish(jnp.concatenate(outs, axis=2), x_ref, wo_ref, g_ref, o_ref)


def _mem_attn_interleaved_kernel(x_ref, q_ref, mk_ref, mv_ref, wo_ref, g_ref, o_ref):
    seqs, tq, _ = q_ref.shape
    rows = MEM_HEADS * tq
    cols = mk_ref.shape[1]
    scale = MEM_HEAD_DIM ** -0.5
    q = jnp.concatenate([q_ref[:, :, h * MEM_HEAD_DIM:(h + 1) * MEM_HEAD_DIM]
                         for h in range(MEM_HEADS)], axis=1).astype(BF16)
    s = jnp.einsum('gqd,gkd->gqk', q, mk_ref[...].astype(BF16), preferred_element_type=F32) * scale
    row_head = lax.broadcasted_iota(jnp.int32, (rows, cols), 0) >> _log2(tq)
    col_head = lax.broadcasted_iota(jnp.int32, (rows, cols), 1) & (MEM_HEADS - 1)
    s = jnp.where((row_head == col_head)[None], s, -jnp.inf)
    p = jnp.exp(s - jnp.max(s, axis=-1, keepdims=True))
    denom = jnp.sum(p, axis=-1, keepdims=True)
    o = jnp.einsum('gqk,gkd->gqd', p.astype(BF16), mv_ref[...].astype(BF16),
                   preferred_element_type=F32) / denom
    o = jnp.concatenate([o[:, h * tq:(h + 1) * tq, :] for h in range(MEM_HEADS)], axis=2)
    _mem_attn_finish(o, x_ref, wo_ref, g_ref, o_ref)


def _mem_attn(body, x3, q3, mem_k, mem_v, w_o, g_post, seqs, tq):
    nseq, slen, _ = x3.shape
    tok = lambda width: pl.BlockSpec((seqs, tq, width), lambda i, j: (i, j, 0))
    mem = pl.BlockSpec((seqs,) + mem_k.shape[1:], lambda i, j: (i, 0, 0))
    const = lambda shape: pl.BlockSpec(shape, lambda i, j: (0, 0))
    return pl.pallas_call(
        body,
        grid=(nseq // seqs, slen // tq),
        in_specs=[tok(D_MODEL), tok(MEM_WIDTH), mem, mem,
                  const((MEM_WIDTH, D_MODEL)), const((1, D_MODEL))],
        out_specs=tok(D_MODEL),
        out_shape=jax.ShapeDtypeStruct((nseq, slen, D_MODEL), F32),
        compiler_params=_params("parallel", "arbitrary"),
        name="mem_attn",
    )(x3, q3, mem_k, mem_v, w_o, g_post)


def _row(g):
    return g.reshape(1, -1)


def _trunk_front(x2d, w):
    x1 = _ffn(x2d, w['ffn1_pre'], w['ffn1_post'], w['ffn1_wg'], w['ffn1_wu'], w['ffn1_wd'])
    zb, zf = _in_proj(x1, w['mix_pre'], w['w_in'])
    return x1, zb, zf


def _trunk_back(x1, a, r, mem_body, mem_k, mem_v, w, nseq, seqs, tq):
    t = x1.shape[0]
    x2, qm = _mix_out(x1, a.reshape(t, ATTN_WIDTH), r.reshape(t, HGRN_VW), w['w_out'],
                      w['mix_post'], w['mem_pre'], w['w_mem_q'])
    x3 = _mem_attn(mem_body, x2.reshape(nseq, t // nseq, D_MODEL),
                   qm.reshape(nseq, t // nseq, MEM_WIDTH),
                   mem_k, mem_v, w['w_mem_o'], w['mem_post'], seqs, tq)
    return _ffn(x3.reshape(t, D_MODEL), w['ffn2_pre'], w['ffn2_post'],
                w['ffn2_wg'], w['ffn2_wu'], w['ffn2_wd'])


def kernel(x_prompt, x_sample, mem_prompt, cache_win_k, cache_win_v, state_hgrn, cache_mem_k, cache_mem_v, ffn1_norm_pre, ffn1_norm_post, ffn1_w_gate, ffn1_w_up, ffn1_w_down, mix_norm_pre, mix_norm_post, w_in, attn_sinks, hgrn_lb_logits, attn_out_gain, hgrn_out_gain, w_out, mem_norm_pre, mem_norm_post, mem_norm_kv, w_mem_q, w_mem_k, w_mem_v, w_mem_o, ffn2_norm_pre, ffn2_norm_post, ffn2_w_gate, ffn2_w_up, ffn2_w_down):
    bp, sp, _ = x_prompt.shape
    bs, ts, _ = x_sample.shape
    mt = mem_prompt.shape[1]
    l = 0

    w_in_l = w_in[l]
    c_k = ATTN_WIDTH
    c_qh = c_k + 2 * KV_WIDTH
    c_f = c_qh + HGRN_KW
    c_i = c_f + HGRN_KW
    w_in_perm = jnp.concatenate(
        [w_in_l[:, :c_k], w_in_l[:, c_qh:c_f], w_in_l[:, c_i:],
         w_in_l[:, c_f:c_i], w_in_l[:, c_k:c_qh]], axis=1)
    w = dict(
        ffn1_pre=_row(ffn1_norm_pre[l]), ffn1_post=_row(ffn1_norm_post[l]),
        ffn1_wg=ffn1_w_gate[l].astype(BF16), ffn1_wu=ffn1_w_up[l].astype(BF16),
        ffn1_wd=ffn1_w_down[l].astype(BF16),
        mix_pre=_row(mix_norm_pre[l]), mix_post=_row(mix_norm_post[l]),
        w_in=w_in_perm.astype(BF16), w_out=w_out[l].astype(BF16),
        mem_pre=_row(mem_norm_pre[l]), mem_post=_row(mem_norm_post[l]),
        w_mem_q=w_mem_q[l].astype(BF16), w_mem_o=w_mem_o[l].astype(BF16),
        ffn2_pre=_row(ffn2_norm_pre[l]), ffn2_post=_row(ffn2_norm_post[l]),
        ffn2_wg=ffn2_w_gate[l].astype(BF16), ffn2_wu=ffn2_w_up[l].astype(BF16),
        ffn2_wd=ffn2_w_down[l].astype(BF16),
    )
    sinks = attn_sinks[l]
    attn_gain = _row(attn_out_gain[l])
    hgrn_gain = _row(hgrn_out_gain[l])

    mk, mv, mk_b, mv_b = _mem_kv(mem_prompt, _row(mem_norm_kv[l]),
                                 w_mem_k[l].astype(BF16), w_mem_v[l].astype(BF16))
    x1, zb, zf = _trunk_front(x_prompt.reshape(bp * sp, D_MODEL), w)
    zb3 = zb.reshape(bp, sp, ZB_WIDTH)
    zf3 = zf.reshape(bp, sp, ZF_WIDTH)
    a = _attn_prompt(zb3, zf3, sinks, attn_gain)
    r, p_state = _hgrn_prompt(zb3, zf3, hgrn_lb_logits, hgrn_gain)
    y_p = _trunk_back(x1, a, r, _mem_attn_kernel, mk_b, mv_b, w, bp, 1, 512).reshape(bp, sp, D_MODEL)
    k_off = ZF_KA * KV_WIDTH
    v_off = ZF_VA * KV_WIDTH
    p_wk = zf3[:, sp - WINDOW:, k_off:k_off + KV_WIDTH]
    p_wv = zf3[:, sp - WINDOW:, v_off:v_off + KV_WIDTH]

    x1s, zbs, zfs = _trunk_front(x_sample.reshape(bs * ts, D_MODEL), w)
    a_s, s_wk, s_wv = _attn_sample(zbs, zfs, cache_win_k[l].reshape(bs, WINDOW, KV_WIDTH),
                                   cache_win_v[l].reshape(bs, WINDOW, KV_WIDTH), sinks, attn_gain, ts)
    r_s, s_state = _hgrn_sample(zbs, zfs, state_hgrn[l], hgrn_lb_logits, hgrn_gain, ts)
    mem_rows = cache_mem_k.shape[2] * MEM_HEADS
    y_s = _trunk_back(x1s, a_s, r_s, _mem_attn_interleaved_kernel,
                      cache_mem_k[l].reshape(bs, mem_rows, MEM_HEAD_DIM),
                      cache_mem_v[l].reshape(bs, mem_rows, MEM_HEAD_DIM),
                      w, bs, 8, ts).reshape(bs, ts, D_MODEL)

    kv5 = lambda t, n: t.reshape(1, n, WINDOW, ATTN_KV_HEADS, ATTN_HEAD_DIM)
    mem5 = lambda t: t.reshape(1, bp, mt, MEM_HEADS, MEM_HEAD_DIM)
    return (y_p, y_s, kv5(p_wk, bp), kv5(p_wv, bp), p_state[None], mem5(mk), mem5(mv),
            kv5(s_wk, bs), kv5(s_wv, bs), s_state[None])
```

```python
import functools

import jax
import jax.numpy as jnp
from jax import lax
from jax.experimental import pallas as pl
from jax.experimental.pallas import tpu as pltpu

F32 = jnp.float32
BF16 = jnp.bfloat16

D_MODEL = 2048
D_FF = 5632
ATTN_HEADS = 16
ATTN_KV_HEADS = 4
ATTN_GROUP = ATTN_HEADS // ATTN_KV_HEADS
ATTN_HEAD_DIM = 64
WINDOW = 128
ATTN_WIDTH = ATTN_HEADS * ATTN_HEAD_DIM
KV_WIDTH = ATTN_KV_HEADS * ATTN_HEAD_DIM
HGRN_HEADS = 8
HGRN_DK = 128
HGRN_DV = 128
HGRN_KW = HGRN_HEADS * HGRN_DK
HGRN_VW = HGRN_HEADS * HGRN_DV
IN_PROJ_WIDTH = ATTN_WIDTH + 2 * KV_WIDTH + 2 * HGRN_KW + 2 * HGRN_VW
MEM_HEADS = 4
MEM_HEAD_DIM = 128
MEM_WIDTH = MEM_HEADS * MEM_HEAD_DIM
FFN_RESIDUAL = 0.5
EPS = 1e-6

ZB_WIDTH = ATTN_WIDTH + HGRN_KW + 2 * HGRN_VW
ZF_WIDTH = HGRN_KW + 2 * KV_WIDTH
ZB_QA, ZB_QH, ZB_IH, ZB_GH = 0, 1, 2, 3
ZF_FH = 0
ZF_KA, ZF_VA = 4, 5

VMEM_LIMIT_BYTES = 56 * 1024 * 1024

FFN_TOKEN_TILE = 512
PROJ_TOKEN_TILE = 1024
MIX_TOKEN_TILE = 512
FF_TILE = 512
PROJ_COL_TILE = 512
HGRN_CHUNK = 128
HGRN_STEP_CHUNKS = 2
HGRN_SAMPLE_SEQS = 4
ATTN_SAMPLE_SEQS = 8
MASKED = -1e30

ALIBI_SLOPES = tuple(2.0 ** (-8.0 * (h + 1) / ATTN_HEADS) for h in range(ATTN_HEADS))


def _params(*semantics):
    return pltpu.CompilerParams(dimension_semantics=semantics,
                                vmem_limit_bytes=VMEM_LIMIT_BYTES)


def _rms(x, g):
    return x * lax.rsqrt(jnp.mean(x * x, axis=-1, keepdims=True) + EPS) * g


def _silu(x):
    return x * jax.nn.sigmoid(x)


def _dot(a, b):
    return jnp.dot(a, b, preferred_element_type=F32)


def _dot_nt(a, b):
    return lax.dot_general(a, b, (((1,), (1,)), ((), ())), preferred_element_type=F32)


def _dot_tn(a, b):
    return lax.dot_general(a, b, (((0,), (0,)), ((), ())), preferred_element_type=F32)


def _log2(n):
    assert n & (n - 1) == 0
    return n.bit_length() - 1


def _ffn_kernel(x_ref, gpre_ref, gpost_ref, wg_ref, wu_ref, wd_ref, o_ref, h_scr):
    j = pl.program_id(1)

    def partial_down(h):
        g = _dot(h, wg_ref[...])
        u = _dot(h, wu_ref[...])
        return _dot((_silu(g) * u).astype(BF16), wd_ref[...])

    @pl.when(j == 0)
    def _():
        h = _rms(x_ref[...], gpre_ref[...]).astype(BF16)
        h_scr[...] = h
        o_ref[...] = partial_down(h)

    @pl.when(j > 0)
    def _():
        o_ref[...] += partial_down(h_scr[...])

    @pl.when(j == pl.num_programs(1) - 1)
    def _():
        o_ref[...] = x_ref[...] + _rms(o_ref[...], FFN_RESIDUAL * gpost_ref[...])


def _ffn(x, g_pre, g_post, wg, wu, wd):
    t = x.shape[0]
    tm, tf = FFN_TOKEN_TILE, FF_TILE
    return pl.pallas_call(
        _ffn_kernel,
        grid=(t // tm, D_FF // tf),
        in_specs=[
            pl.BlockSpec((tm, D_MODEL), lambda i, j: (i, 0)),
            pl.BlockSpec((1, D_MODEL), lambda i, j: (0, 0)),
            pl.BlockSpec((1, D_MODEL), lambda i, j: (0, 0)),
            pl.BlockSpec((None, D_MODEL, tf), lambda i, j: (j, 0, 0)),
            pl.BlockSpec((None, D_MODEL, tf), lambda i, j: (j, 0, 0)),
            pl.BlockSpec((tf, D_MODEL), lambda i, j: (j, 0)),
        ],
        out_specs=pl.BlockSpec((tm, D_MODEL), lambda i, j: (i, 0)),
        out_shape=jax.ShapeDtypeStruct((t, D_MODEL), F32),
        scratch_shapes=[pltpu.VMEM((tm, D_MODEL), BF16)],
        compiler_params=_params("parallel", "arbitrary"),
        name="ffn_half",
    )(x, g_pre, g_post, wg, wu, wd)


def _in_proj_kernel(x_ref, g_ref, w_ref, zb_ref, zf_ref, h_scr, *, bf16_steps):
    j = pl.program_id(1)

    @pl.when(j == 0)
    def _():
        h_scr[...] = _rms(x_ref[...], g_ref[...]).astype(BF16)

    z = _dot(h_scr[...], w_ref[...])

    @pl.when(j < bf16_steps)
    def _():
        zb_ref[...] = z.astype(BF16)

    @pl.when(j >= bf16_steps)
    def _():
        zf_ref[...] = z


def _in_proj(x, g, w):
    t = x.shape[0]
    tm, tn = min(PROJ_TOKEN_TILE, t), PROJ_COL_TILE
    nb = ZB_WIDTH // tn
    return pl.pallas_call(
        functools.partial(_in_proj_kernel, bf16_steps=nb),
        grid=(t // tm, IN_PROJ_WIDTH // tn),
        in_specs=[
            pl.BlockSpec((tm, D_MODEL), lambda i, j: (i, 0)),
            pl.BlockSpec((1, D_MODEL), lambda i, j: (0, 0)),
            pl.BlockSpec((None, D_MODEL, tn), lambda i, j: (j, 0, 0)),
        ],
        out_specs=[
            pl.BlockSpec((tm, tn), lambda i, j: (i, jnp.minimum(j, nb - 1))),
            pl.BlockSpec((tm, tn), lambda i, j: (i, jnp.maximum(j - nb, 0))),
        ],
        out_shape=[jax.ShapeDtypeStruct((t, ZB_WIDTH), BF16),
                   jax.ShapeDtypeStruct((t, ZF_WIDTH), F32)],
        scratch_shapes=[pltpu.VMEM((tm, D_MODEL), BF16)],
        compiler_params=_params("parallel", "arbitrary"),
        name="in_proj",
    )(x, g, w)


def _attn_prompt_kernel(sink_ref, q_ref, kc_ref, kp_ref, vc_ref, vp_ref, gain_ref, o_ref):
    n = pl.program_id(1)
    scale = ATTN_HEAD_DIM ** -0.5
    k = (jnp.concatenate([kp_ref[...], kc_ref[...]], axis=0) * scale).astype(BF16)
    v = jnp.concatenate([vp_ref[...], vc_ref[...]], axis=0).astype(BF16)
    row = lax.broadcasted_iota(jnp.int32, (WINDOW, 2 * WINDOW), 0)
    col = lax.broadcasted_iota(jnp.int32, (WINDOW, 2 * WINDOW), 1)
    dist = row + WINDOW - col
    valid = (dist >= 0) & (dist < WINDOW) & ((col >= WINDOW) | (n > 0))
    neg_dist = jnp.where(valid, -dist.astype(F32), MASKED)
    outs = []
    for hd in range(ATTN_HEADS):
        kvh = hd // ATTN_GROUP
        kh = k[:, kvh * ATTN_HEAD_DIM:(kvh + 1) * ATTN_HEAD_DIM]
        vh = v[:, kvh * ATTN_HEAD_DIM:(kvh + 1) * ATTN_HEAD_DIM]
        qh = q_ref[:, hd * ATTN_HEAD_DIM:(hd + 1) * ATTN_HEAD_DIM]
        s = _dot_nt(qh, kh) + ALIBI_SLOPES[hd] * neg_dist
        sink = sink_ref[hd]
        m = jnp.maximum(jnp.max(s, axis=-1, keepdims=True), sink)
        p = jnp.exp(s - m)
        denom = jnp.sum(p, axis=-1, keepdims=True) + jnp.exp(sink - m)
        outs.append(_dot(p.astype(BF16), vh) / denom)
    o = jnp.concatenate(outs, axis=1)
    o_ref[...] = _rms(o, gain_ref[...]).astype(BF16)


def _attn_prompt(zb3, zf3, sinks, gain):
    b, s, _ = zb3.shape
    nb = s // WINDOW

    def cur(width_idx):
        return lambda bi, n: (bi, n, width_idx)

    def prev(width_idx):
        return lambda bi, n: (bi, jnp.maximum(n - 1, 0), width_idx)

    return pl.pallas_call(
        _attn_prompt_kernel,
        grid=(b, nb),
        in_specs=[
            pl.BlockSpec(memory_space=pltpu.SMEM),
            pl.BlockSpec((None, WINDOW, ATTN_WIDTH), cur(ZB_QA)),
            pl.BlockSpec((None, WINDOW, KV_WIDTH), cur(ZF_KA)),
            pl.BlockSpec((None, WINDOW, KV_WIDTH), prev(ZF_KA)),
            pl.BlockSpec((None, WINDOW, KV_WIDTH), cur(ZF_VA)),
            pl.BlockSpec((None, WINDOW, KV_WIDTH), prev(ZF_VA)),
            pl.BlockSpec((1, ATTN_WIDTH), lambda bi, n: (0, 0)),
        ],
        out_specs=pl.BlockSpec((None, WINDOW, ATTN_WIDTH), lambda bi, n: (bi, n, 0)),
        out_shape=jax.ShapeDtypeStruct((b, s, ATTN_WIDTH), BF16),
        compiler_params=_params("parallel", "arbitrary"),
        name="attn_prompt",
    )(sinks, zb3, zf3, zf3, zf3, zf3, gain)


def _attn_sample_kernel(sink_ref, q_ref, kn_ref, vn_ref, ck_ref, cv_ref, gain_ref,
                        o_ref, wk_ref, wv_ref, *, t_new):
    seqs = ck_ref.shape[0]
    t_bits = _log2(t_new)
    rows = ATTN_GROUP * t_new
    q_all = q_ref[...].astype(F32).reshape(seqs, t_new, ATTN_WIDTH)
    kn = kn_ref[...].reshape(seqs, t_new, KV_WIDTH)
    vn = vn_ref[...].reshape(seqs, t_new, KV_WIDTH)
    ck = ck_ref[...]
    cv = cv_ref[...]
    wk_ref[:, :WINDOW - t_new, :] = ck[:, t_new:, :]
    wk_ref[:, WINDOW - t_new:, :] = kn
    wv_ref[:, :WINDOW - t_new, :] = cv[:, t_new:, :]
    wv_ref[:, WINDOW - t_new:, :] = vn

    r_c = lax.broadcasted_iota(jnp.int32, (rows, WINDOW), 0)
    j_c = lax.broadcasted_iota(jnp.int32, (rows, WINDOW), 1)
    dist_c = (r_c & (t_new - 1)) + WINDOW - j_c
    neg_c = jnp.where(dist_c < WINDOW, -dist_c.astype(F32), MASKED)
    r_n = lax.broadcasted_iota(jnp.int32, (rows, t_new), 0)
    j_n = lax.broadcasted_iota(jnp.int32, (rows, t_new), 1)
    dist_n = (r_n & (t_new - 1)) - j_n
    neg_n = jnp.where(dist_n >= 0, -dist_n.astype(F32), MASKED)
    head_of_row = lax.broadcasted_iota(jnp.int32, (rows, 1), 0) >> t_bits
    scale = ATTN_HEAD_DIM ** -0.5

    outs = [None] * ATTN_HEADS
    for kvh in range(ATTN_KV_HEADS):
        slope = jnp.zeros((rows, 1), F32)
        sink = jnp.zeros((rows, 1), F32)
        for g in range(ATTN_GROUP):
            hd = kvh * ATTN_GROUP + g
            slope = jnp.where(head_of_row == g, ALIBI_SLOPES[hd], slope)
            sink = jnp.where(head_of_row == g, sink_ref[hd], sink)
        q = jnp.concatenate(
            [q_all[:, :, (kvh * ATTN_GROUP + g) * ATTN_HEAD_DIM:(kvh * ATTN_GROUP + g + 1) * ATTN_HEAD_DIM]
             for g in range(ATTN_GROUP)], axis=1).astype(BF16)
        sl = slice(kvh * ATTN_HEAD_DIM, (kvh + 1) * ATTN_HEAD_DIM)
        ck_h = (ck[:, :, sl] * scale).astype(BF16)
        kn_h = (kn[:, :, sl] * scale).astype(BF16)
        cv_h = cv[:, :, sl].astype(BF16)
        vn_h = vn[:, :, sl].astype(BF16)
        s_c = jnp.einsum('gqd,gkd->gqk', q, ck_h, preferred_element_type=F32) + (slope * neg_c)[None]
        s_n = jnp.einsum('gqd,gkd->gqk', q, kn_h, preferred_element_type=F32) + (slope * neg_n)[None]
        m = jnp.maximum(jnp.maximum(jnp.max(s_c, axis=-1, keepdims=True),
                                    jnp.max(s_n, axis=-1, keepdims=True)), sink[None])
        p_c = jnp.exp(s_c - m)
        p_n = jnp.exp(s_n - m)
        denom = (jnp.sum(p_c, axis=-1, keepdims=True) + jnp.sum(p_n, axis=-1, keepdims=True)
                 + jnp.exp(sink[None] - m))
        o = (jnp.einsum('gqk,gkd->gqd', p_c.astype(BF16), cv_h, preferred_element_type=F32)
             + jnp.einsum('gqk,gkd->gqd', p_n.astype(BF16), vn_h, preferred_element_type=F32)) / denom
        for g in range(ATTN_GROUP):
            outs[kvh * ATTN_GROUP + g] = o[:, g * t_new:(g + 1) * t_new, :]
    o_all = jnp.concatenate(outs, axis=2)
    o_ref[...] = _rms(o_all, gain_ref[...][None]).reshape(seqs * t_new, ATTN_WIDTH)


def _attn_sample(zb, zf, cache_k, cache_v, sinks, gain, t_new):
    b = cache_k.shape[0]
    gs = ATTN_SAMPLE_SEQS
    tok = lambda width, idx: pl.BlockSpec((gs * t_new, width), lambda i: (i, idx))
    cache_spec = pl.BlockSpec((gs, WINDOW, KV_WIDTH), lambda i: (i, 0, 0))
    return pl.pallas_call(
        functools.partial(_attn_sample_kernel, t_new=t_new),
        grid=(b // gs,),
        in_specs=[
            pl.BlockSpec(memory_space=pltpu.SMEM),
            tok(ATTN_WIDTH, ZB_QA), tok(KV_WIDTH, ZF_KA), tok(KV_WIDTH, ZF_VA),
            cache_spec, cache_spec,
            pl.BlockSpec((1, ATTN_WIDTH), lambda i: (0, 0)),
        ],
        out_specs=[tok(ATTN_WIDTH, 0), cache_spec, cache_spec],
        out_shape=[
            jax.ShapeDtypeStruct((b * t_new, ATTN_WIDTH), F32),
            jax.ShapeDtypeStruct((b, WINDOW, KV_WIDTH), F32),
            jax.ShapeDtypeStruct((b, WINDOW, KV_WIDTH), F32),
        ],
        compiler_params=_params("parallel"),
        name="attn_sample",
    )(sinks, zb, zf, zf, cache_k, cache_v, gain)


def _lower_bound(lbl_ref):
    logits = lbl_ref[...]
    e = jnp.exp(logits - jnp.max(logits, axis=0, keepdims=True))
    return e[0:1, :] / jnp.sum(e, axis=0, keepdims=True)


def _hgrn_features(q_raw, f_raw, lb):
    sg = jax.nn.sigmoid(f_raw)
    f = lb + (1.0 - lb) * sg
    k = (1.0 - lb) * (1.0 - sg)
    return _silu(q_raw), k, f, jnp.log2(f)


def _cumsum_rows(x, segment):
    row = lax.broadcasted_iota(jnp.int32, x.shape, 0) & (segment - 1)
    shift = 1
    while shift < segment:
        x = x + jnp.where(row >= shift, pltpu.roll(x, shift, axis=0), 0.0)
        shift *= 2
    return x


def _split3(x):
    hi = x.astype(BF16)
    r = x - hi.astype(F32)
    mid = r.astype(BF16)
    lo = (r - mid.astype(F32)).astype(BF16)
    return hi, mid, lo


def _cumsum_rows_mxu(x, tri):
    hi, mid, lo = _split3(x)
    return _dot(tri, hi) + _dot(tri, mid) + _dot(tri, lo)


def _column_broadcast(row_vec, n):
    rows = 16
    hi, mid, lo = _split3(row_vec)
    terms = jnp.concatenate([hi, mid, lo, jnp.zeros((rows - 3, n), BF16)], axis=0)
    ones = (lax.broadcasted_iota(jnp.int32, (rows, n), 0) < 3).astype(BF16)
    return _dot_tn(terms, ones)


def _hgrn_out(o, gain, g_raw):
    return _rms(o, gain) * _silu(g_raw)


def _pair_levels(c):
    t = lax.broadcasted_iota(jnp.int32, (c, c), 0)
    s = lax.broadcasted_iota(jnp.int32, (c, c), 1)
    masks = [t == s]
    half = 1
    while half < c:
        block = 2 * half
        same_block = (t >> _log2(block)) == (s >> _log2(block))
        masks.append(same_block & ((t & (block - 1)) >= half) & ((s & (block - 1)) < half))
        half = block
    return masks


def _pair_factors(q, k, f, log2f, l2):
    c, w = q.shape
    pos = lax.broadcasted_iota(jnp.int32, (c, w), 0) & 3
    e2 = jnp.exp2(jnp.where(pos == 0, pltpu.roll(log2f, c - 1, axis=0),
                            jnp.where(pos == 1, 0.0,
                                      jnp.where(pos == 2, log2f,
                                                log2f + pltpu.roll(log2f, 1, axis=0)))))
    qs = [q, q * f, q * e2]
    ks = [k, k, k * e2]
    half = 4
    while half < c:
        block = 2 * half
        ref = l2.reshape(c // block, block, w)[:, half - 1:half, :]
        ref = jnp.broadcast_to(ref, (c // block, block, w)).reshape(c, w)
        e = jnp.exp2(-jnp.abs(l2 - ref))
        qs.append(q * e)
        ks.append(k * e)
        half = block
    return [x.astype(BF16) for x in qs], [x.astype(BF16) for x in ks]


def _hgrn_prompt_kernel(lbl_ref, q_ref, f_ref, i_ref, g_ref, gain_ref, r_ref, s_ref):
    c = HGRN_CHUNK

    @pl.when(pl.program_id(1) == 0)
    def _():
        s_ref[...] = jnp.zeros_like(s_ref)

    lb = _lower_bound(lbl_ref)
    q_all, k_all, f_all, log2f_all = _hgrn_features(q_ref[...].astype(F32), f_ref[...], lb)
    masks = _pair_levels(c)
    tri = (lax.broadcasted_iota(jnp.int32, (c, c), 1)
           <= lax.broadcasted_iota(jnp.int32, (c, c), 0)).astype(BF16)
    gain = gain_ref[...]

    chunks = []
    for ci in range(HGRN_STEP_CHUNKS):
        rows = slice(ci * c, (ci + 1) * c)
        l2_c = _cumsum_rows_mxu(log2f_all[rows], tri)
        q_lv, k_lv = _pair_factors(q_all[rows], k_all[rows], f_all[rows], log2f_all[rows], l2_c)
        l_end = l2_c[c - 1:c, :]
        q_hat = (q_all[rows] * jnp.exp2(l2_c)).astype(BF16)
        k_end = (k_all[rows] * jnp.exp2(l_end - l2_c)).astype(BF16)
        chunks.append((rows, q_lv, k_lv, q_hat, k_end, jnp.exp2(l_end)))

    for rows, q_lv, k_lv, q_hat, k_end, decay_end in chunks:
        for h in range(HGRN_HEADS):
            sl = slice(h * HGRN_DK, (h + 1) * HGRN_DK)
            vb = i_ref[rows, sl]
            a = jnp.zeros((c, c), F32)
            for mask, q_f, k_f in zip(masks, q_lv, k_lv):
                a = jnp.where(mask, _dot_nt(q_f[:, sl], k_f[:, sl]), a)
            s_old = s_ref[h]
            o = _dot(a.astype(BF16), vb) + _dot(q_hat[:, sl], s_old.astype(BF16))
            s_ref[h] = (s_old * _column_broadcast(decay_end[:, sl], HGRN_DK)
                        + _dot_tn(k_end[:, sl], vb))
            r_ref[rows, sl] = _hgrn_out(o, gain, g_ref[rows, sl].astype(F32)).astype(r_ref.dtype)


def _hgrn_prompt(zb3, zf3, lb_logits, gain):
    b, s, _ = zb3.shape
    rows = HGRN_CHUNK * HGRN_STEP_CHUNKS
    seg = lambda idx: pl.BlockSpec((None, rows, HGRN_KW), lambda bi, ci: (bi, ci, idx))
    return pl.pallas_call(
        _hgrn_prompt_kernel,
        grid=(b, s // rows),
        in_specs=[
            pl.BlockSpec(lb_logits.shape, lambda bi, ci: (0, 0)),
            seg(ZB_QH), seg(ZF_FH), seg(ZB_IH), seg(ZB_GH),
            pl.BlockSpec((1, HGRN_DV), lambda bi, ci: (0, 0)),
        ],
        out_specs=[
            pl.BlockSpec((None, rows, HGRN_VW), lambda bi, ci: (bi, ci, 0)),
            pl.BlockSpec((None, HGRN_HEADS, HGRN_DK, HGRN_DV), lambda bi, ci: (bi, 0, 0, 0)),
        ],
        out_shape=[
            jax.ShapeDtypeStruct((b, s, HGRN_VW), BF16),
            jax.ShapeDtypeStruct((b, HGRN_HEADS, HGRN_DK, HGRN_DV), F32),
        ],
        compiler_params=_params("parallel", "arbitrary"),
        name="hgrn_prompt",
    )(lb_logits, zb3, zf3, zb3, zb3, gain)


def _hgrn_sample_kernel(lbl_ref, seg_ref, segt_ref, q_ref, f_ref, i_ref, g_ref, gain_ref, s0_ref,
                        r_ref, s_ref, *, t_new):
    seqs = s0_ref.shape[0]
    n = seqs * t_new
    width = HGRN_KW
    lb = _lower_bound(lbl_ref)
    q_all, k_all, _, log2f_all = _hgrn_features(q_ref[...].astype(F32), f_ref[...], lb)
    l2_all = _cumsum_rows(log2f_all, t_new)
    v_all = i_ref[...].astype(F32)
    g_all = g_ref[...].astype(F32)
    gain = gain_ref[...]

    def row_of_seq(x, t):
        x3 = x.reshape(seqs, t_new, width)
        return jnp.broadcast_to(x3[:, t:t + 1, :], (seqs, t_new, width)).reshape(n, width)

    src = lax.broadcasted_iota(jnp.int32, (n, width), 0) & (t_new - 1)
    w_parts = []
    for t in range(t_new):
        decay = jnp.exp2(jnp.minimum(row_of_seq(l2_all, t) - l2_all, 0.0))
        w_parts.append(jnp.where(src <= t, decay * k_all * row_of_seq(q_all, t), 0.0))
    w = jnp.concatenate(w_parts, axis=0).astype(BF16)
    a_heads = _dot(w, seg_ref[...])
    a_lanes = _dot(a_heads.astype(BF16), segt_ref[...])
    o_rows = []
    for t in range(t_new):
        p = (a_lanes[t * n:(t + 1) * n] * v_all).reshape(seqs, t_new, width)
        o_rows.append(jnp.sum(p, axis=1, keepdims=True))
    o_intra = jnp.concatenate(o_rows, axis=1).reshape(n, width)

    l_end = row_of_seq(l2_all, t_new - 1)
    q_hat = q_all * jnp.exp2(l2_all)
    k_end = k_all * jnp.exp2(l_end - l2_all)
    decay_end = jnp.exp2(l_end)
    for h in range(HGRN_HEADS):
        sl = slice(h * HGRN_DK, (h + 1) * HGRN_DK)
        o_parts = []
        for sq in range(seqs):
            rows = slice(sq * t_new, (sq + 1) * t_new)
            s_old = s0_ref[sq, h]
            o_parts.append(_dot(q_hat[rows, sl].astype(BF16), s_old.astype(BF16)))
            s_ref[sq, h] = (s_old * _column_broadcast(decay_end[sq * t_new:sq * t_new + 1, sl], HGRN_DK)
                            + _dot_tn(k_end[rows, sl].astype(BF16), v_all[rows, sl].astype(BF16)))
        o = o_intra[:, sl] + jnp.concatenate(o_parts, axis=0)
        r_ref[:, sl] = _hgrn_out(o, gain, g_all[:, sl])


def _hgrn_sample(zb, zf, state, lb_logits, gain, t_new):
    b = state.shape[0]
    gs = HGRN_SAMPLE_SEQS
    seg = lambda idx: pl.BlockSpec((gs * t_new, HGRN_KW), lambda i: (i, idx))
    state_spec = pl.BlockSpec((gs, HGRN_HEADS, HGRN_DK, HGRN_DV), lambda i: (i, 0, 0, 0))
    lane_head = jnp.arange(HGRN_KW, dtype=jnp.int32) // HGRN_DK
    head_seg = (lane_head[:, None] == jnp.arange(HGRN_DK, dtype=jnp.int32)[None, :]).astype(BF16)
    return pl.pallas_call(
        functools.partial(_hgrn_sample_kernel, t_new=t_new),
        grid=(b // gs,),
        in_specs=[
            pl.BlockSpec(lb_logits.shape, lambda i: (0, 0)),
            pl.BlockSpec((HGRN_KW, HGRN_DK), lambda i: (0, 0)),
            pl.BlockSpec((HGRN_DK, HGRN_KW), lambda i: (0, 0)),
            seg(ZB_QH), seg(ZF_FH), seg(ZB_IH), seg(ZB_GH),
            pl.BlockSpec((1, HGRN_DV), lambda i: (0, 0)),
            state_spec,
        ],
        out_specs=[seg(0), state_spec],
        out_shape=[
            jax.ShapeDtypeStruct((b * t_new, HGRN_VW), F32),
            jax.ShapeDtypeStruct((b, HGRN_HEADS, HGRN_DK, HGRN_DV), F32),
        ],
        compiler_params=_params("parallel"),
        name="hgrn_sample",
    )(lb_logits, head_seg, head_seg.T, zb, zf, zb, zb, gain, state)


def _mix_out_kernel(x_ref, a_ref, r_ref, wa_ref, wr_ref, gpost_ref, gmem_ref, wq_ref,
                    x2_ref, qm_ref):
    mixed = (_dot(a_ref[...].astype(BF16), wa_ref[...])
             + _dot(r_ref[...].astype(BF16), wr_ref[...]))
    x2 = x_ref[...] + _rms(mixed, gpost_ref[...])
    x2_ref[...] = x2
    qm_ref[...] = _dot(_rms(x2, gmem_ref[...]).astype(BF16), wq_ref[...])


def _mix_out(x, a, r, w_out, g_post, g_mem, w_q):
    t = x.shape[0]
    tm = MIX_TOKEN_TILE
    tok = lambda width: pl.BlockSpec((tm, width), lambda i: (i, 0))
    const = lambda shape, idx=(0, 0): pl.BlockSpec(shape, lambda i: idx)
    return pl.pallas_call(
        _mix_out_kernel,
        grid=(t // tm,),
        in_specs=[
            tok(D_MODEL), tok(ATTN_WIDTH), tok(HGRN_VW),
            const((ATTN_WIDTH, D_MODEL), (0, 0)),
            const((HGRN_VW, D_MODEL), (1, 0)),
            const((1, D_MODEL)), const((1, D_MODEL)),
            const((D_MODEL, MEM_WIDTH)),
        ],
        out_specs=[tok(D_MODEL), tok(MEM_WIDTH)],
        out_shape=[jax.ShapeDtypeStruct((t, D_MODEL), F32),
                   jax.ShapeDtypeStruct((t, MEM_WIDTH), F32)],
        compiler_params=_params("parallel"),
        name="mix_out",
    )(x, a, r, w_out, w_out, g_post, g_mem, w_q)


def _mem_kv_kernel(m_ref, g_ref, wk_ref, wv_ref, k_ref, v_ref, kb_ref, vb_ref):
    h = _rms(m_ref[...], g_ref[...]).astype(BF16)
    k = _dot(h, wk_ref[...])
    v = _dot(h, wv_ref[...])
    k_ref[...] = k
    v_ref[...] = v
    kb_ref[...] = k.astype(BF16)
    vb_ref[...] = v.astype(BF16)


def _mem_kv(mem, g, w_k, w_v):
    b, m, _ = mem.shape
    const = lambda shape: pl.BlockSpec(shape, lambda i: (0, 0))
    out = pl.BlockSpec((None, m, MEM_WIDTH), lambda i: (i, 0, 0))
    return pl.pallas_call(
        _mem_kv_kernel,
        grid=(b,),
        in_specs=[pl.BlockSpec((None, m, D_MODEL), lambda i: (i, 0, 0)), const((1, D_MODEL)),
                  const((D_MODEL, MEM_WIDTH)), const((D_MODEL, MEM_WIDTH))],
        out_specs=[out] * 4,
        out_shape=[jax.ShapeDtypeStruct((b, m, MEM_WIDTH), F32)] * 2
        + [jax.ShapeDtypeStruct((b, m, MEM_WIDTH), BF16)] * 2,
        compiler_params=_params("parallel"),
        name="mem_kv",
    )(mem, g, w_k, w_v)


def _mem_attn_finish(o, x_ref, wo_ref, g_ref, o_ref):
    seqs, tq, _ = x_ref.shape
    y = _dot(o.reshape(seqs * tq, MEM_WIDTH).astype(BF16), wo_ref[...])
    x = x_ref[...].reshape(seqs * tq, D_MODEL)
    o_ref[...] = (x + _rms(y, g_ref[...])).reshape(seqs, tq, D_MODEL)


def _mem_attn_kernel(x_ref, q_ref, mk_ref, mv_ref, wo_ref, g_ref, o_ref):
    scale = MEM_HEAD_DIM ** -0.5
    outs = []
    for h in range(MEM_HEADS):
        sl = slice(h * MEM_HEAD_DIM, (h + 1) * MEM_HEAD_DIM)
        q = q_ref[:, :, sl].astype(BF16)
        s = jnp.einsum('gqd,gkd->gqk', q, mk_ref[:, :, sl], preferred_element_type=F32) * scale
        p = jnp.exp(s - jnp.max(s, axis=-1, keepdims=True))
        denom = jnp.sum(p, axis=-1, keepdims=True)
        outs.append(jnp.einsum('gqk,gkd->gqd', p.astype(BF16), mv_ref[:, :, sl],
                               preferred_element_type=F32) / denom)
    _mem_attn_finish(jnp.concatenate(outs, axis=2), x_ref, wo_ref, g_ref, o_ref)


def _mem_attn_interleaved_kernel(x_ref, q_ref, mk_ref, mv_ref, wo_ref, g_ref, o_ref):
    seqs, tq, _ = q_ref.shape
    rows = MEM_HEADS * tq
    cols = mk_ref.shape[1]
    scale = MEM_HEAD_DIM ** -0.5
    q = jnp.concatenate([q_ref[:, :, h * MEM_HEAD_DIM:(h + 1) * MEM_HEAD_DIM]
                         for h in range(MEM_HEADS)], axis=1).astype(BF16)
    s = jnp.einsum('gqd,gkd->gqk', q, mk_ref[...].astype(BF16), preferred_element_type=F32) * scale
    row_head = lax.broadcasted_iota(jnp.int32, (rows, cols), 0) >> _log2(tq)
    col_head = lax.broadcasted_iota(jnp.int32, (rows, cols), 1) & (MEM_HEADS - 1)
    s = jnp.where((row_head == col_head)[None], s, -jnp.inf)
    p = jnp.exp(s - jnp.max(s, axis=-1, keepdims=True))
    denom = jnp.sum(p, axis=-1, keepdims=True)
    o = jnp.einsum('gqk,gkd->gqd', p.astype(BF16), mv_ref[...].astype(BF16),
                   preferred_element_type=F32) / denom
    o = jnp.concatenate([o[:, h * tq:(h + 1) * tq, :] for h in range(MEM_HEADS)], axis=2)
    _mem_attn_finish(o, x_ref, wo_ref, g_ref, o_ref)


def _mem_attn(body, x3, q3, mem_k, mem_v, w_o, g_post, seqs, tq):
    nseq, slen, _ = x3.shape
    tok = lambda width: pl.BlockSpec((seqs, tq, width), lambda i, j: (i, j, 0))
    mem = pl.BlockSpec((seqs,) + mem_k.shape[1:], lambda i, j: (i, 0, 0))
    const = lambda shape: pl.BlockSpec(shape, lambda i, j: (0, 0))
    return pl.pallas_call(
        body,
        grid=(nseq // seqs, slen // tq),
        in_specs=[tok(D_MODEL), tok(MEM_WIDTH), mem, mem,
                  const((MEM_WIDTH, D_MODEL)), const((1, D_MODEL))],
        out_specs=tok(D_MODEL),
        out_shape=jax.ShapeDtypeStruct((nseq, slen, D_MODEL), F32),
        compiler_params=_params("parallel", "arbitrary"),
        name="mem_attn",
    )(x3, q3, mem_k, mem_v, w_o, g_post)


def _row(g):
    return g.reshape(1, -1)


def _cast_tiles_kernel(order_ref, w_ref, o_ref):
    del order_ref
    o_ref[...] = w_ref[...].astype(BF16)


def _cast_col_tiles(w, tile, order=None):
    k, n = w.shape
    nt = n // tile
    order = jnp.arange(nt, dtype=jnp.int32) if order is None else jnp.asarray(order, jnp.int32)
    return pl.pallas_call(
        _cast_tiles_kernel,
        grid_spec=pltpu.PrefetchScalarGridSpec(
            num_scalar_prefetch=1,
            grid=(nt,),
            in_specs=[pl.BlockSpec((k, tile), lambda j, order_ref: (0, order_ref[j]))],
            out_specs=pl.BlockSpec((None, k, tile), lambda j, order_ref: (j, 0, 0)),
        ),
        out_shape=jax.ShapeDtypeStruct((nt, k, tile), BF16),
        compiler_params=_params("parallel"),
        name="cast_col_tiles",
    )(order, w)


def _trunk_front(x2d, w):
    x1 = _ffn(x2d, w['ffn1_pre'], w['ffn1_post'], w['ffn1_wg'], w['ffn1_wu'], w['ffn1_wd'])
    zb, zf = _in_proj(x1, w['mix_pre'], w['w_in'])
    return x1, zb, zf


def _trunk_back(x1, a, r, mem_body, mem_k, mem_v, w, nseq, seqs, tq):
    t = x1.shape[0]
    x2, qm = _mix_out(x1, a.reshape(t, ATTN_WIDTH), r.reshape(t, HGRN_VW), w['w_out'],
                      w['mix_post'], w['mem_pre'], w['w_mem_q'])
    x3 = _mem_attn(mem_body, x2.reshape(nseq, t // nseq, D_MODEL),
                   qm.reshape(nseq, t // nseq, MEM_WIDTH),
                   mem_k, mem_v, w['w_mem_o'], w['mem_post'], seqs, tq)
    return _ffn(x3.reshape(t, D_MODEL), w['ffn2_pre'], w['ffn2_post'],
                w['ffn2_wg'], w['ffn2_wu'], w['ffn2_wd'])


def kernel(x_prompt, x_sample, mem_prompt, cache_win_k, cache_win_v, state_hgrn, cache_mem_k, cache_mem_v, ffn1_norm_pre, ffn1_norm_post, ffn1_w_gate, ffn1_w_up, ffn1_w_down, mix_norm_pre, mix_norm_post, w_in, attn_sinks, hgrn_lb_logits, attn_out_gain, hgrn_out_gain, w_out, mem_norm_pre, mem_norm_post, mem_norm_kv, w_mem_q, w_mem_k, w_mem_v, w_mem_o, ffn2_norm_pre, ffn2_norm_post, ffn2_w_gate, ffn2_w_up, ffn2_w_down):
    bp, sp, _ = x_prompt.shape
    bs, ts, _ = x_sample.shape
    mt = mem_prompt.shape[1]
    l = 0

    seg_widths = dict(qa=ATTN_WIDTH, kv=2 * KV_WIDTH, qh=HGRN_KW, fh=HGRN_KW, ih=HGRN_VW, gh=HGRN_VW)
    seg_tiles, start = {}, 0
    for name in ('qa', 'kv', 'qh', 'fh', 'ih', 'gh'):
        n_tiles = seg_widths[name] // PROJ_COL_TILE
        seg_tiles[name] = list(range(start, start + n_tiles))
        start += n_tiles
    w_in_order = sum((seg_tiles[name] for name in ('qa', 'qh', 'ih', 'gh', 'fh', 'kv')), [])
    w = dict(
        ffn1_pre=_row(ffn1_norm_pre[l]), ffn1_post=_row(ffn1_norm_post[l]),
        ffn1_wg=_cast_col_tiles(ffn1_w_gate[l], FF_TILE), ffn1_wu=_cast_col_tiles(ffn1_w_up[l], FF_TILE),
        ffn1_wd=ffn1_w_down[l].astype(BF16),
        mix_pre=_row(mix_norm_pre[l]), mix_post=_row(mix_norm_post[l]),
        w_in=_cast_col_tiles(w_in[l], PROJ_COL_TILE, w_in_order), w_out=w_out[l].astype(BF16),
        mem_pre=_row(mem_norm_pre[l]), mem_post=_row(mem_norm_post[l]),
        w_mem_q=w_mem_q[l].astype(BF16), w_mem_o=w_mem_o[l].astype(BF16),
        ffn2_pre=_row(ffn2_norm_pre[l]), ffn2_post=_row(ffn2_norm_post[l]),
        ffn2_wg=_cast_col_tiles(ffn2_w_gate[l], FF_TILE), ffn2_wu=_cast_col_tiles(ffn2_w_up[l], FF_TILE),
        ffn2_wd=ffn2_w_down[l].astype(BF16),
    )
    sinks = attn_sinks[l]
    attn_gain = _row(attn_out_gain[l])
    hgrn_gain = _row(hgrn_out_gain[l])

    mk, mv, mk_b, mv_b = _mem_kv(mem_prompt, _row(mem_norm_kv[l]),
                                 w_mem_k[l].astype(BF16), w_mem_v[l].astype(BF16))
    x1, zb, zf = _trunk_front(x_prompt.reshape(bp * sp, D_MODEL), w)
    zb3 = zb.reshape(bp, sp, ZB_WIDTH)
    zf3 = zf.reshape(bp, sp, ZF_WIDTH)
    a = _attn_prompt(zb3, zf3, sinks, attn_gain)
    r, p_state = _hgrn_prompt(zb3, zf3, hgrn_lb_logits, hgrn_gain)
    y_p = _trunk_back(x1, a, r, _mem_attn_kernel, mk_b, mv_b, w, bp, 1, 512).reshape(bp, sp, D_MODEL)
    k_off = ZF_KA * KV_WIDTH
    v_off = ZF_VA * KV_WIDTH
    p_wk = zf3[:, sp - WINDOW:, k_off:k_off + KV_WIDTH]
    p_wv = zf3[:, sp - WINDOW:, v_off:v_off + KV_WIDTH]

    x1s, zbs, zfs = _trunk_front(x_sample.reshape(bs * ts, D_MODEL), w)
    a_s, s_wk, s_wv = _attn_sample(zbs, zfs, cache_win_k[l].reshape(bs, WINDOW, KV_WIDTH),
                                   cache_win_v[l].reshape(bs, WINDOW, KV_WIDTH), sinks, attn_gain, ts)
    r_s, s_state = _hgrn_sample(zbs, zfs, state_hgrn[l], hgrn_lb_logits, hgrn_gain, ts)
    mem_rows = cache_mem_k.shape[2] * MEM_HEADS
    y_s = _trunk_back(x1s, a_s, r_s, _mem_attn_interleaved_kernel,
                      cache_mem_k[l].reshape(bs, mem_rows, MEM_HEAD_DIM),
                      cache_mem_v[l].reshape(bs, mem_rows, MEM_HEAD_DIM),
                      w, bs, 8, ts).reshape(bs, ts, D_MODEL)

    kv5 = lambda t, n: t.reshape(1, n, WINDOW, ATTN_KV_HEADS, ATTN_HEAD_DIM)
    mem5 = lambda t: t.reshape(1, bp, mt, MEM_HEADS, MEM_HEAD_DIM)
    return (y_p, y_s, kv5(p_wk, bp), kv5(p_wv, bp), p_state[None], mem5(mk), mem5(mv),
            kv5(s_wk, bs), kv5(s_wv, bs), s_state[None])
```

```python
import functools

import jax
import jax.numpy as jnp
from jax import lax
from jax.experimental import pallas as pl
from jax.experimental.pallas import tpu as pltpu

F32 = jnp.float32
BF16 = jnp.bfloat16

D_MODEL = 2048
D_FF = 5632
ATTN_HEADS = 16
ATTN_KV_HEADS = 4
ATTN_GROUP = ATTN_HEADS // ATTN_KV_HEADS
ATTN_HEAD_DIM = 64
WINDOW = 128
ATTN_WIDTH = ATTN_HEADS * ATTN_HEAD_DIM
KV_WIDTH = ATTN_KV_HEADS * ATTN_HEAD_DIM
HGRN_HEADS = 8
HGRN_DK = 128
HGRN_DV = 128
HGRN_KW = HGRN_HEADS * HGRN_DK
HGRN_VW = HGRN_HEADS * HGRN_DV
IN_PROJ_WIDTH = ATTN_WIDTH + 2 * KV_WIDTH + 2 * HGRN_KW + 2 * HGRN_VW
MEM_HEADS = 4
MEM_HEAD_DIM = 128
MEM_WIDTH = MEM_HEADS * MEM_HEAD_DIM
FFN_RESIDUAL = 0.5
EPS = 1e-6

ZB_WIDTH = ATTN_WIDTH + HGRN_KW + 2 * HGRN_VW
ZF_WIDTH = HGRN_KW + 2 * KV_WIDTH
ZB_QA, ZB_QH, ZB_IH, ZB_GH = 0, 1, 2, 3
ZF_FH = 0
ZF_KA, ZF_VA = 4, 5

VMEM_LIMIT_BYTES = 56 * 1024 * 1024

FFN_TOKEN_TILE = 512
PROJ_TOKEN_TILE = 1024
MIX_TOKEN_TILE = 512
FF_TILE = 512
PROJ_COL_TILE = 512
HGRN_CHUNK = 128
HGRN_STEP_CHUNKS = 2
HGRN_SAMPLE_SEQS = 4
ATTN_SAMPLE_SEQS = 8
ATTN_STEP_BLOCKS = 1
MASKED = -1e30

ALIBI_SLOPES = tuple(2.0 ** (-8.0 * (h + 1) / ATTN_HEADS) for h in range(ATTN_HEADS))


def _params(*semantics):
    return pltpu.CompilerParams(dimension_semantics=semantics,
                                vmem_limit_bytes=VMEM_LIMIT_BYTES)


def _rms(x, g):
    return x * lax.rsqrt(jnp.mean(x * x, axis=-1, keepdims=True) + EPS) * g


def _silu(x):
    return x * jax.nn.sigmoid(x)


def _dot(a, b):
    return jnp.dot(a, b, preferred_element_type=F32)


def _dot_nt(a, b):
    return lax.dot_general(a, b, (((1,), (1,)), ((), ())), preferred_element_type=F32)


def _dot_tn(a, b):
    return lax.dot_general(a, b, (((0,), (0,)), ((), ())), preferred_element_type=F32)


def _log2(n):
    assert n & (n - 1) == 0
    return n.bit_length() - 1


def _ffn_kernel(x_ref, gpre_ref, gpost_ref, wg_ref, wu_ref, wd_ref, o_ref, h_scr):
    j = pl.program_id(1)

    def partial_down(h):
        g = _dot(h, wg_ref[...])
        u = _dot(h, wu_ref[...])
        return _dot((_silu(g) * u).astype(BF16), wd_ref[...])

    @pl.when(j == 0)
    def _():
        h = _rms(x_ref[...], gpre_ref[...]).astype(BF16)
        h_scr[...] = h
        o_ref[...] = partial_down(h)

    @pl.when(j > 0)
    def _():
        o_ref[...] += partial_down(h_scr[...])

    @pl.when(j == pl.num_programs(1) - 1)
    def _():
        o_ref[...] = x_ref[...] + _rms(o_ref[...], FFN_RESIDUAL * gpost_ref[...])


def _ffn(x, g_pre, g_post, wg, wu, wd):
    t = x.shape[0]
    tm, tf = FFN_TOKEN_TILE, wg.shape[2]
    return pl.pallas_call(
        _ffn_kernel,
        grid=(t // tm, D_FF // tf),
        in_specs=[
            pl.BlockSpec((tm, D_MODEL), lambda i, j: (i, 0)),
            pl.BlockSpec((1, D_MODEL), lambda i, j: (0, 0)),
            pl.BlockSpec((1, D_MODEL), lambda i, j: (0, 0)),
            pl.BlockSpec((None, D_MODEL, tf), lambda i, j: (j, 0, 0)),
            pl.BlockSpec((None, D_MODEL, tf), lambda i, j: (j, 0, 0)),
            pl.BlockSpec((tf, D_MODEL), lambda i, j: (j, 0)),
        ],
        out_specs=pl.BlockSpec((tm, D_MODEL), lambda i, j: (i, 0)),
        out_shape=jax.ShapeDtypeStruct((t, D_MODEL), F32),
        scratch_shapes=[pltpu.VMEM((tm, D_MODEL), BF16)],
        compiler_params=_params("parallel", "arbitrary"),
        name="ffn_half",
    )(x, g_pre, g_post, wg, wu, wd)


def _in_proj_kernel(x_ref, g_ref, w_ref, zb_ref, zf_ref, h_scr, *, bf16_steps):
    j = pl.program_id(1)

    @pl.when(j == 0)
    def _():
        h_scr[...] = _rms(x_ref[...], g_ref[...]).astype(BF16)

    z = _dot(h_scr[...], w_ref[...])

    @pl.when(j < bf16_steps)
    def _():
        zb_ref[...] = z.astype(BF16)

    @pl.when(j >= bf16_steps)
    def _():
        zf_ref[...] = z


def _in_proj(x, g, w):
    t = x.shape[0]
    tm, tn = min(PROJ_TOKEN_TILE, t), PROJ_COL_TILE
    nb = ZB_WIDTH // tn
    return pl.pallas_call(
        functools.partial(_in_proj_kernel, bf16_steps=nb),
        grid=(t // tm, IN_PROJ_WIDTH // tn),
        in_specs=[
            pl.BlockSpec((tm, D_MODEL), lambda i, j: (i, 0)),
            pl.BlockSpec((1, D_MODEL), lambda i, j: (0, 0)),
            pl.BlockSpec((None, D_MODEL, tn), lambda i, j: (j, 0, 0)),
        ],
        out_specs=[
            pl.BlockSpec((tm, tn), lambda i, j: (i, jnp.minimum(j, nb - 1))),
            pl.BlockSpec((tm, tn), lambda i, j: (i, jnp.maximum(j - nb, 0))),
        ],
        out_shape=[jax.ShapeDtypeStruct((t, ZB_WIDTH), BF16),
                   jax.ShapeDtypeStruct((t, ZF_WIDTH), F32)],
        scratch_shapes=[pltpu.VMEM((tm, D_MODEL), BF16)],
        compiler_params=_params("parallel", "arbitrary"),
        name="in_proj",
    )(x, g, w)


def _attn_prompt_kernel(sink_ref, q_ref, kc_ref, kp_ref, vc_ref, vp_ref, gain_ref, o_ref):
    n = pl.program_id(1)
    scale = ATTN_HEAD_DIM ** -0.5
    k_all = (jnp.concatenate([kp_ref[...], kc_ref[...]], axis=0) * scale).astype(BF16)
    v_all = jnp.concatenate([vp_ref[...], vc_ref[...]], axis=0).astype(BF16)
    row = lax.broadcasted_iota(jnp.int32, (WINDOW, 2 * WINDOW), 0)
    col = lax.broadcasted_iota(jnp.int32, (WINDOW, 2 * WINDOW), 1)
    dist = row + WINDOW - col
    in_window = (dist >= 0) & (dist < WINDOW)
    gain = gain_ref[...]
    for blk in range(ATTN_STEP_BLOCKS):
        valid = in_window
        if blk == 0:
            valid = valid & ((col >= WINDOW) | (n > 0))
        neg_dist = jnp.where(valid, -dist.astype(F32), MASKED)
        k = k_all[blk * WINDOW:(blk + 2) * WINDOW]
        v = v_all[blk * WINDOW:(blk + 2) * WINDOW]
        qrows = slice(blk * WINDOW, (blk + 1) * WINDOW)
        outs = []
        for hd in range(ATTN_HEADS):
            kvh = hd // ATTN_GROUP
            kh = k[:, kvh * ATTN_HEAD_DIM:(kvh + 1) * ATTN_HEAD_DIM]
            vh = v[:, kvh * ATTN_HEAD_DIM:(kvh + 1) * ATTN_HEAD_DIM]
            qh = q_ref[qrows, hd * ATTN_HEAD_DIM:(hd + 1) * ATTN_HEAD_DIM]
            s = _dot_nt(qh, kh) + ALIBI_SLOPES[hd] * neg_dist
            sink = sink_ref[hd]
            m = jnp.maximum(jnp.max(s, axis=-1, keepdims=True), sink)
            p = jnp.exp(s - m)
            denom = jnp.sum(p, axis=-1, keepdims=True) + jnp.exp(sink - m)
            outs.append(_dot(p.astype(BF16), vh) / denom)
        o = jnp.concatenate(outs, axis=1)
        o_ref[qrows, :] = _rms(o, gain).astype(BF16)


def _attn_prompt(zb3, zf3, sinks, gain):
    b, s, _ = zb3.shape
    nblk = ATTN_STEP_BLOCKS
    rows = nblk * WINDOW

    def cur(width_idx):
        return lambda bi, n: (bi, n, width_idx)

    def prev(width_idx):
        return lambda bi, n: (bi, jnp.maximum(n * nblk - 1, 0), width_idx)

    return pl.pallas_call(
        _attn_prompt_kernel,
        grid=(b, s // rows),
        in_specs=[
            pl.BlockSpec(memory_space=pltpu.SMEM),
            pl.BlockSpec((None, rows, ATTN_WIDTH), cur(ZB_QA)),
            pl.BlockSpec((None, rows, KV_WIDTH), cur(ZF_KA)),
            pl.BlockSpec((None, WINDOW, KV_WIDTH), prev(ZF_KA)),
            pl.BlockSpec((None, rows, KV_WIDTH), cur(ZF_VA)),
            pl.BlockSpec((None, WINDOW, KV_WIDTH), prev(ZF_VA)),
            pl.BlockSpec((1, ATTN_WIDTH), lambda bi, n: (0, 0)),
        ],
        out_specs=pl.BlockSpec((None, rows, ATTN_WIDTH), lambda bi, n: (bi, n, 0)),
        out_shape=jax.ShapeDtypeStruct((b, s, ATTN_WIDTH), BF16),
        compiler_params=_params("parallel", "arbitrary"),
        name="attn_prompt",
    )(sinks, zb3, zf3, zf3, zf3, zf3, gain)


def _attn_sample_kernel(sink_ref, q_ref, kn_ref, vn_ref, ck_ref, cv_ref, gain_ref,
                        o_ref, wk_ref, wv_ref, *, t_new):
    seqs = ck_ref.shape[0]
    t_bits = _log2(t_new)
    rows = ATTN_GROUP * t_new
    q_all = q_ref[...].astype(F32).reshape(seqs, t_new, ATTN_WIDTH)
    kn = kn_ref[...].reshape(seqs, t_new, KV_WIDTH)
    vn = vn_ref[...].reshape(seqs, t_new, KV_WIDTH)
    ck = ck_ref[...]
    cv = cv_ref[...]
    wk_ref[:, :WINDOW - t_new, :] = ck[:, t_new:, :]
    wk_ref[:, WINDOW - t_new:, :] = kn
    wv_ref[:, :WINDOW - t_new, :] = cv[:, t_new:, :]
    wv_ref[:, WINDOW - t_new:, :] = vn

    r_c = lax.broadcasted_iota(jnp.int32, (rows, WINDOW), 0)
    j_c = lax.broadcasted_iota(jnp.int32, (rows, WINDOW), 1)
    dist_c = (r_c & (t_new - 1)) + WINDOW - j_c
    neg_c = jnp.where(dist_c < WINDOW, -dist_c.astype(F32), MASKED)
    r_n = lax.broadcasted_iota(jnp.int32, (rows, t_new), 0)
    j_n = lax.broadcasted_iota(jnp.int32, (rows, t_new), 1)
    dist_n = (r_n & (t_new - 1)) - j_n
    neg_n = jnp.where(dist_n >= 0, -dist_n.astype(F32), MASKED)
    head_of_row = lax.broadcasted_iota(jnp.int32, (rows, 1), 0) >> t_bits
    scale = ATTN_HEAD_DIM ** -0.5

    outs = [None] * ATTN_HEADS
    for kvh in range(ATTN_KV_HEADS):
        slope = jnp.zeros((rows, 1), F32)
        sink = jnp.zeros((rows, 1), F32)
        for g in range(ATTN_GROUP):
            hd = kvh * ATTN_GROUP + g
            slope = jnp.where(head_of_row == g, ALIBI_SLOPES[hd], slope)
            sink = jnp.where(head_of_row == g, sink_ref[hd], sink)
        q = jnp.concatenate(
            [q_all[:, :, (kvh * ATTN_GROUP + g) * ATTN_HEAD_DIM:(kvh * ATTN_GROUP + g + 1) * ATTN_HEAD_DIM]
             for g in range(ATTN_GROUP)], axis=1).astype(BF16)
        sl = slice(kvh * ATTN_HEAD_DIM, (kvh + 1) * ATTN_HEAD_DIM)
        ck_h = (ck[:, :, sl] * scale).astype(BF16)
        kn_h = (kn[:, :, sl] * scale).astype(BF16)
        cv_h = cv[:, :, sl].astype(BF16)
        vn_h = vn[:, :, sl].astype(BF16)
        s_c = jnp.einsum('gqd,gkd->gqk', q, ck_h, preferred_element_type=F32) + (slope * neg_c)[None]
        s_n = jnp.einsum('gqd,gkd->gqk', q, kn_h, preferred_element_type=F32) + (slope * neg_n)[None]
        m = jnp.maximum(jnp.maximum(jnp.max(s_c, axis=-1, keepdims=True),
                                    jnp.max(s_n, axis=-1, keepdims=True)), sink[None])
        p_c = jnp.exp(s_c - m)
        p_n = jnp.exp(s_n - m)
        denom = (jnp.sum(p_c, axis=-1, keepdims=True) + jnp.sum(p_n, axis=-1, keepdims=True)
                 + jnp.exp(sink[None] - m))
        o = (jnp.einsum('gqk,gkd->gqd', p_c.astype(BF16), cv_h, preferred_element_type=F32)
             + jnp.einsum('gqk,gkd->gqd', p_n.astype(BF16), vn_h, preferred_element_type=F32)) / denom
        for g in range(ATTN_GROUP):
            outs[kvh * ATTN_GROUP + g] = o[:, g * t_new:(g + 1) * t_new, :]
    o_all = jnp.concatenate(outs, axis=2)
    o_ref[...] = _rms(o_all, gain_ref[...][None]).reshape(seqs * t_new, ATTN_WIDTH)


def _attn_sample(zb, zf, cache_k, cache_v, sinks, gain, t_new):
    b = cache_k.shape[0]
    gs = ATTN_SAMPLE_SEQS
    tok = lambda width, idx: pl.BlockSpec((gs * t_new, width), lambda i: (i, idx))
    cache_spec = pl.BlockSpec((gs, WINDOW, KV_WIDTH), lambda i: (i, 0, 0))
    return pl.pallas_call(
        functools.partial(_attn_sample_kernel, t_new=t_new),
        grid=(b // gs,),
        in_specs=[
            pl.BlockSpec(memory_space=pltpu.SMEM),
            tok(ATTN_WIDTH, ZB_QA), tok(KV_WIDTH, ZF_KA), tok(KV_WIDTH, ZF_VA),
            cache_spec, cache_spec,
            pl.BlockSpec((1, ATTN_WIDTH), lambda i: (0, 0)),
        ],
        out_specs=[tok(ATTN_WIDTH, 0), cache_spec, cache_spec],
        out_shape=[
            jax.ShapeDtypeStruct((b * t_new, ATTN_WIDTH), F32),
            jax.ShapeDtypeStruct((b, WINDOW, KV_WIDTH), F32),
            jax.ShapeDtypeStruct((b, WINDOW, KV_WIDTH), F32),
        ],
        compiler_params=_params("parallel"),
        name="attn_sample",
    )(sinks, zb, zf, zf, cache_k, cache_v, gain)


def _lower_bound(lbl_ref):
    logits = lbl_ref[...]
    e = jnp.exp(logits - jnp.max(logits, axis=0, keepdims=True))
    return e[0:1, :] / jnp.sum(e, axis=0, keepdims=True)


def _hgrn_features(q_raw, f_raw, lb):
    sg = jax.nn.sigmoid(f_raw)
    f = lb + (1.0 - lb) * sg
    k = (1.0 - lb) * (1.0 - sg)
    return _silu(q_raw), k, f, jnp.log2(f)


def _cumsum_rows(x, segment):
    row = lax.broadcasted_iota(jnp.int32, x.shape, 0) & (segment - 1)
    shift = 1
    while shift < segment:
        x = x + jnp.where(row >= shift, pltpu.roll(x, shift, axis=0), 0.0)
        shift *= 2
    return x


def _split3(x):
    hi = x.astype(BF16)
    r = x - hi.astype(F32)
    mid = r.astype(BF16)
    lo = (r - mid.astype(F32)).astype(BF16)
    return hi, mid, lo


def _cumsum_rows_mxu(x, tri):
    hi, mid, lo = _split3(x)
    return _dot(tri, hi) + _dot(tri, mid) + _dot(tri, lo)


def _column_broadcast(row_vec, n):
    rows = 16
    hi, mid, lo = _split3(row_vec)
    terms = jnp.concatenate([hi, mid, lo, jnp.zeros((rows - 3, n), BF16)], axis=0)
    ones = (lax.broadcasted_iota(jnp.int32, (rows, n), 0) < 3).astype(BF16)
    return _dot_tn(terms, ones)


def _hgrn_out(o, gain, g_raw):
    return _rms(o, gain) * _silu(g_raw)


def _pair_levels(c):
    t = lax.broadcasted_iota(jnp.int32, (c, c), 0)
    s = lax.broadcasted_iota(jnp.int32, (c, c), 1)
    masks = [t == s]
    half = 1
    while half < c:
        block = 2 * half
        same_block = (t >> _log2(block)) == (s >> _log2(block))
        masks.append(same_block & ((t & (block - 1)) >= half) & ((s & (block - 1)) < half))
        half = block
    return masks


def _pair_factors(q, k, f, log2f, l2):
    c, w = q.shape
    pos = lax.broadcasted_iota(jnp.int32, (c, w), 0) & 3
    e2 = jnp.exp2(jnp.where(pos == 0, pltpu.roll(log2f, c - 1, axis=0),
                            jnp.where(pos == 1, 0.0,
                                      jnp.where(pos == 2, log2f,
                                                log2f + pltpu.roll(log2f, 1, axis=0)))))
    qs = [q, q * f, q * e2]
    ks = [k, k, k * e2]
    half = 4
    while half < c:
        block = 2 * half
        ref = l2.reshape(c // block, block, w)[:, half - 1:half, :]
        ref = jnp.broadcast_to(ref, (c // block, block, w)).reshape(c, w)
        e = jnp.exp2(-jnp.abs(l2 - ref))
        qs.append(q * e)
        ks.append(k * e)
        half = block
    return [x.astype(BF16) for x in qs], [x.astype(BF16) for x in ks]


def _hgrn_prompt_kernel(lbl_ref, q_ref, f_ref, i_ref, g_ref, gain_ref, r_ref, s_ref):
    c = HGRN_CHUNK

    @pl.when(pl.program_id(1) == 0)
    def _():
        s_ref[...] = jnp.zeros_like(s_ref)

    lb = _lower_bound(lbl_ref)
    q_all, k_all, f_all, log2f_all = _hgrn_features(q_ref[...].astype(F32), f_ref[...], lb)
    masks = _pair_levels(c)
    tri = (lax.broadcasted_iota(jnp.int32, (c, c), 1)
           <= lax.broadcasted_iota(jnp.int32, (c, c), 0)).astype(BF16)
    gain = gain_ref[...]

    chunks = []
    for ci in range(HGRN_STEP_CHUNKS):
        rows = slice(ci * c, (ci + 1) * c)
        l2_c = _cumsum_rows_mxu(log2f_all[rows], tri)
        q_lv, k_lv = _pair_factors(q_all[rows], k_all[rows], f_all[rows], log2f_all[rows], l2_c)
        l_end = l2_c[c - 1:c, :]
        q_hat = (q_all[rows] * jnp.exp2(l2_c)).astype(BF16)
        k_end = (k_all[rows] * jnp.exp2(l_end - l2_c)).astype(BF16)
        chunks.append((rows, q_lv, k_lv, q_hat, k_end, jnp.exp2(l_end)))

    for rows, q_lv, k_lv, q_hat, k_end, decay_end in chunks:
        for h in range(HGRN_HEADS):
            sl = slice(h * HGRN_DK, (h + 1) * HGRN_DK)
            vb = i_ref[rows, sl]
            a = jnp.zeros((c, c), F32)
            for mask, q_f, k_f in zip(masks, q_lv, k_lv):
                a = jnp.where(mask, _dot_nt(q_f[:, sl], k_f[:, sl]), a)
            s_old = s_ref[h]
            o = _dot(a.astype(BF16), vb) + _dot(q_hat[:, sl], s_old.astype(BF16))
            s_ref[h] = (s_old * _column_broadcast(decay_end[:, sl], HGRN_DK)
                        + _dot_tn(k_end[:, sl], vb))
            r_ref[rows, sl] = _hgrn_out(o, gain, g_ref[rows, sl].astype(F32)).astype(r_ref.dtype)


def _hgrn_prompt(zb3, zf3, lb_logits, gain):
    b, s, _ = zb3.shape
    rows = HGRN_CHUNK * HGRN_STEP_CHUNKS
    seg = lambda idx: pl.BlockSpec((None, rows, HGRN_KW), lambda bi, ci: (bi, ci, idx))
    return pl.pallas_call(
        _hgrn_prompt_kernel,
        grid=(b, s // rows),
        in_specs=[
            pl.BlockSpec(lb_logits.shape, lambda bi, ci: (0, 0)),
            seg(ZB_QH), seg(ZF_FH), seg(ZB_IH), seg(ZB_GH),
            pl.BlockSpec((1, HGRN_DV), lambda bi, ci: (0, 0)),
        ],
        out_specs=[
            pl.BlockSpec((None, rows, HGRN_VW), lambda bi, ci: (bi, ci, 0)),
            pl.BlockSpec((None, HGRN_HEADS, HGRN_DK, HGRN_DV), lambda bi, ci: (bi, 0, 0, 0)),
        ],
        out_shape=[
            jax.ShapeDtypeStruct((b, s, HGRN_VW), BF16),
            jax.ShapeDtypeStruct((b, HGRN_HEADS, HGRN_DK, HGRN_DV), F32),
        ],
        compiler_params=_params("parallel", "arbitrary"),
        name="hgrn_prompt",
    )(lb_logits, zb3, zf3, zb3, zb3, gain)


def _hgrn_sample_kernel(lbl_ref, seg_ref, segt_ref, q_ref, f_ref, i_ref, g_ref, gain_ref, s0_ref,
                        r_ref, s_ref, *, t_new):
    seqs = s0_ref.shape[0]
    n = seqs * t_new
    width = HGRN_KW
    lb = _lower_bound(lbl_ref)
    q_all, k_all, _, log2f_all = _hgrn_features(q_ref[...].astype(F32), f_ref[...], lb)
    l2_all = _cumsum_rows(log2f_all, t_new)
    v_all = i_ref[...].astype(F32)
    g_all = g_ref[...].astype(F32)
    gain = gain_ref[...]

    def row_of_seq(x, t):
        x3 = x.reshape(seqs, t_new, width)
        return jnp.broadcast_to(x3[:, t:t + 1, :], (seqs, t_new, width)).reshape(n, width)

    src = lax.broadcasted_iota(jnp.int32, (n, width), 0) & (t_new - 1)
    w_parts = []
    for t in range(t_new):
        decay = jnp.exp2(jnp.minimum(row_of_seq(l2_all, t) - l2_all, 0.0))
        w_parts.append(jnp.where(src <= t, decay * k_all * row_of_seq(q_all, t), 0.0))
    w = jnp.concatenate(w_parts, axis=0).astype(BF16)
    a_heads = _dot(w, seg_ref[...])
    a_lanes = _dot(a_heads.astype(BF16), segt_ref[...])
    o_rows = []
    for t in range(t_new):
        p = (a_lanes[t * n:(t + 1) * n] * v_all).reshape(seqs, t_new, width)
        o_rows.append(jnp.sum(p, axis=1, keepdims=True))
    o_intra = jnp.concatenate(o_rows, axis=1).reshape(n, width)

    l_end = row_of_seq(l2_all, t_new - 1)
    q_hat = q_all * jnp.exp2(l2_all)
    k_end = k_all * jnp.exp2(l_end - l2_all)
    decay_end = jnp.exp2(l_end)
    for h in range(HGRN_HEADS):
        sl = slice(h * HGRN_DK, (h + 1) * HGRN_DK)
        o_parts = []
        for sq in range(seqs):
            rows = slice(sq * t_new, (sq + 1) * t_new)
            s_old = s0_ref[sq, h]
            o_parts.append(_dot(q_hat[rows, sl].astype(BF16), s_old.astype(BF16)))
            s_ref[sq, h] = (s_old * _column_broadcast(decay_end[sq * t_new:sq * t_new + 1, sl], HGRN_DK)
                            + _dot_tn(k_end[rows, sl].astype(BF16), v_all[rows, sl].astype(BF16)))
        o = o_intra[:, sl] + jnp.concatenate(o_parts, axis=0)
        r_ref[:, sl] = _hgrn_out(o, gain, g_all[:, sl])


def _hgrn_sample(zb, zf, state, lb_logits, gain, t_new):
    b = state.shape[0]
    gs = HGRN_SAMPLE_SEQS
    seg = lambda idx: pl.BlockSpec((gs * t_new, HGRN_KW), lambda i: (i, idx))
    state_spec = pl.BlockSpec((gs, HGRN_HEADS, HGRN_DK, HGRN_DV), lambda i: (i, 0, 0, 0))
    lane_head = jnp.arange(HGRN_KW, dtype=jnp.int32) // HGRN_DK
    head_seg = (lane_head[:, None] == jnp.arange(HGRN_DK, dtype=jnp.int32)[None, :]).astype(BF16)
    return pl.pallas_call(
        functools.partial(_hgrn_sample_kernel, t_new=t_new),
        grid=(b // gs,),
        in_specs=[
            pl.BlockSpec(lb_logits.shape, lambda i: (0, 0)),
            pl.BlockSpec((HGRN_KW, HGRN_DK), lambda i: (0, 0)),
            pl.BlockSpec((HGRN_DK, HGRN_KW), lambda i: (0, 0)),
            seg(ZB_QH), seg(ZF_FH), seg(ZB_IH), seg(ZB_GH),
            pl.BlockSpec((1, HGRN_DV), lambda i: (0, 0)),
            state_spec,
        ],
        out_specs=[seg(0), state_spec],
        out_shape=[
            jax.ShapeDtypeStruct((b * t_new, HGRN_VW), F32),
            jax.ShapeDtypeStruct((b, HGRN_HEADS, HGRN_DK, HGRN_DV), F32),
        ],
        compiler_params=_params("parallel"),
        name="hgrn_sample",
    )(lb_logits, head_seg, head_seg.T, zb, zf, zb, zb, gain, state)


def _mix_out_kernel(x_ref, a_ref, r_ref, wa_ref, wr_ref, gpost_ref, gmem_ref, wq_ref,
                    x2_ref, qm_ref):
    mixed = (_dot(a_ref[...].astype(BF16), wa_ref[...])
             + _dot(r_ref[...].astype(BF16), wr_ref[...]))
    x2 = x_ref[...] + _rms(mixed, gpost_ref[...])
    x2_ref[...] = x2
    qm_ref[...] = _dot(_rms(x2, gmem_ref[...]).astype(BF16), wq_ref[...])


def _mix_out(x, a, r, w_out, g_post, g_mem, w_q):
    t = x.shape[0]
    tm = MIX_TOKEN_TILE
    tok = lambda width: pl.BlockSpec((tm, width), lambda i: (i, 0))
    const = lambda shape, idx=(0, 0): pl.BlockSpec(shape, lambda i: idx)
    return pl.pallas_call(
        _mix_out_kernel,
        grid=(t // tm,),
        in_specs=[
            tok(D_MODEL), tok(ATTN_WIDTH), tok(HGRN_VW),
            const((ATTN_WIDTH, D_MODEL), (0, 0)),
            const((HGRN_VW, D_MODEL), (1, 0)),
            const((1, D_MODEL)), const((1, D_MODEL)),
            const((D_MODEL, MEM_WIDTH)),
        ],
        out_specs=[tok(D_MODEL), tok(MEM_WIDTH)],
        out_shape=[jax.ShapeDtypeStruct((t, D_MODEL), F32),
                   jax.ShapeDtypeStruct((t, MEM_WIDTH), F32)],
        compiler_params=_params("parallel"),
        name="mix_out",
    )(x, a, r, w_out, w_out, g_post, g_mem, w_q)


def _mem_kv_kernel(m_ref, g_ref, wk_ref, wv_ref, k_ref, v_ref, kb_ref, vb_ref):
    h = _rms(m_ref[...], g_ref[...]).astype(BF16)
    k = _dot(h, wk_ref[...])
    v = _dot(h, wv_ref[...])
    k_ref[...] = k
    v_ref[...] = v
    kb_ref[...] = k.astype(BF16)
    vb_ref[...] = v.astype(BF16)


def _mem_kv(mem, g, w_k, w_v):
    b, m, _ = mem.shape
    const = lambda shape: pl.BlockSpec(shape, lambda i: (0, 0))
    out = pl.BlockSpec((None, m, MEM_WIDTH), lambda i: (i, 0, 0))
    return pl.pallas_call(
        _mem_kv_kernel,
        grid=(b,),
        in_specs=[pl.BlockSpec((None, m, D_MODEL), lambda i: (i, 0, 0)), const((1, D_MODEL)),
                  const((D_MODEL, MEM_WIDTH)), const((D_MODEL, MEM_WIDTH))],
        out_specs=[out] * 4,
        out_shape=[jax.ShapeDtypeStruct((b, m, MEM_WIDTH), F32)] * 2
        + [jax.ShapeDtypeStruct((b, m, MEM_WIDTH), BF16)] * 2,
        compiler_params=_params("parallel"),
        name="mem_kv",
    )(mem, g, w_k, w_v)


def _mem_attn_finish(o, x_ref, wo_ref, g_ref, o_ref):
    seqs, tq, _ = x_ref.shape
    y = _dot(o.reshape(seqs * tq, MEM_WIDTH).astype(BF16), wo_ref[...])
    x = x_ref[...].reshape(seqs * tq, D_MODEL)
    o_ref[...] = (x + _rms(y, g_ref[...])).reshape(seqs, tq, D_MODEL)


def _mem_attn_kernel(x_ref, q_ref, mk_ref, mv_ref, wo_ref, g_ref, o_ref):
    scale = MEM_HEAD_DIM ** -0.5
    outs = []
    for h in range(MEM_HEADS):
        sl = slice(h * MEM_HEAD_DIM, (h + 1) * MEM_HEAD_DIM)
        q = q_ref[:, :, sl].astype(BF16)
        s = jnp.einsum('gqd,gkd->gqk', q, mk_ref[:, :, sl], preferred_element_type=F32) * scale
        p = jnp.exp(s - jnp.max(s, axis=-1, keepdims=True))
        denom = jnp.sum(p, axis=-1, keepdims=True)
        outs.append(jnp.einsum('gqk,gkd->gqd', p.astype(BF16), mv_ref[:, :, sl],
                               preferred_element_type=F32) / denom)
    _mem_attn_finish(jnp.concatenate(outs, axis=2), x_ref, wo_ref, g_ref, o_ref)


def _mem_attn_interleaved_kernel(x_ref, q_ref, mk_ref, mv_ref, wo_ref, g_ref, o_ref):
    seqs, tq, _ = q_ref.shape
    rows = MEM_HEADS * tq
    cols = mk_ref.shape[1]
    scale = MEM_HEAD_DIM ** -0.5
    q = jnp.concatenate([q_ref[:, :, h * MEM_HEAD_DIM:(h + 1) * MEM_HEAD_DIM]
                         for h in range(MEM_HEADS)], axis=1).astype(BF16)
    s = jnp.einsum('gqd,gkd->gqk', q, mk_ref[...].astype(BF16), preferred_element_type=F32) * scale
    row_head = lax.broadcasted_iota(jnp.int32, (rows, cols), 0) >> _log2(tq)
    col_head = lax.broadcasted_iota(jnp.int32, (rows, cols), 1) & (MEM_HEADS - 1)
    s = jnp.where((row_head == col_head)[None], s, -jnp.inf)
    p = jnp.exp(s - jnp.max(s, axis=-1, keepdims=True))
    denom = jnp.sum(p, axis=-1, keepdims=True)
    o = jnp.einsum('gqk,gkd->gqd', p.astype(BF16), mv_ref[...].astype(BF16),
                   preferred_element_type=F32) / denom
    o = jnp.concatenate([o[:, h * tq:(h + 1) * tq, :] for h in range(MEM_HEADS)], axis=2)
    _mem_attn_finish(o, x_ref, wo_ref, g_ref, o_ref)


def _mem_attn(body, x3, q3, mem_k, mem_v, w_o, g_post, seqs, tq):
    nseq, slen, _ = x3.shape
    tok = lambda width: pl.BlockSpec((seqs, tq, width), lambda i, j: (i, j, 0))
    mem = pl.BlockSpec((seqs,) + mem_k.shape[1:], lambda i, j: (i, 0, 0))
    const = lambda shape: pl.BlockSpec(shape, lambda i, j: (0, 0))
    return pl.pallas_call(
        body,
        grid=(nseq // seqs, slen // tq),
        in_specs=[tok(D_MODEL), tok(MEM_WIDTH), mem, mem,
                  const((MEM_WIDTH, D_MODEL)), const((1, D_MODEL))],
        out_specs=tok(D_MODEL),
        out_shape=jax.ShapeDtypeStruct((nseq, slen, D_MODEL), F32),
        compiler_params=_params("parallel", "arbitrary"),
        name="mem_attn",
    )(x3, q3, mem_k, mem_v, w_o, g_post)


def _row(g):
    return g.reshape(1, -1)


def _cast_tiles_kernel(order_ref, w_ref, o_ref):
    del order_ref
    o_ref[...] = w_ref[...].astype(BF16)


def _cast_col_tiles(w, tile, order=None):
    k, n = w.shape
    nt = n // tile
    order = jnp.arange(nt, dtype=jnp.int32) if order is None else jnp.asarray(order, jnp.int32)
    return pl.pallas_call(
        _cast_tiles_kernel,
        grid_spec=pltpu.PrefetchScalarGridSpec(
            num_scalar_prefetch=1,
            grid=(nt,),
            in_specs=[pl.BlockSpec((k, tile), lambda j, order_ref: (0, order_ref[j]))],
            out_specs=pl.BlockSpec((None, k, tile), lambda j, order_ref: (j, 0, 0)),
        ),
        out_shape=jax.ShapeDtypeStruct((nt, k, tile), BF16),
        compiler_params=_params("parallel"),
        name="cast_col_tiles",
    )(order, w)


def _trunk_front(x2d, w):
    x1 = _ffn(x2d, w['ffn1_pre'], w['ffn1_post'], w['ffn1_wg'], w['ffn1_wu'], w['ffn1_wd'])
    zb, zf = _in_proj(x1, w['mix_pre'], w['w_in'])
    return x1, zb, zf


def _trunk_back(x1, a, r, mem_body, mem_k, mem_v, w, nseq, seqs, tq):
    t = x1.shape[0]
    x2, qm = _mix_out(x1, a.reshape(t, ATTN_WIDTH), r.reshape(t, HGRN_VW), w['w_out'],
                      w['mix_post'], w['mem_pre'], w['w_mem_q'])
    x3 = _mem_attn(mem_body, x2.reshape(nseq, t // nseq, D_MODEL),
                   qm.reshape(nseq, t // nseq, MEM_WIDTH),
                   mem_k, mem_v, w['w_mem_o'], w['mem_post'], seqs, tq)
    return _ffn(x3.reshape(t, D_MODEL), w['ffn2_pre'], w['ffn2_post'],
                w['ffn2_wg'], w['ffn2_wu'], w['ffn2_wd'])


def kernel(x_prompt, x_sample, mem_prompt, cache_win_k, cache_win_v, state_hgrn, cache_mem_k, cache_mem_v, ffn1_norm_pre, ffn1_norm_post, ffn1_w_gate, ffn1_w_up, ffn1_w_down, mix_norm_pre, mix_norm_post, w_in, attn_sinks, hgrn_lb_logits, attn_out_gain, hgrn_out_gain, w_out, mem_norm_pre, mem_norm_post, mem_norm_kv, w_mem_q, w_mem_k, w_mem_v, w_mem_o, ffn2_norm_pre, ffn2_norm_post, ffn2_w_gate, ffn2_w_up, ffn2_w_down):
    bp, sp, _ = x_prompt.shape
    bs, ts, _ = x_sample.shape
    mt = mem_prompt.shape[1]
    l = 0

    seg_widths = dict(qa=ATTN_WIDTH, kv=2 * KV_WIDTH, qh=HGRN_KW, fh=HGRN_KW, ih=HGRN_VW, gh=HGRN_VW)
    seg_tiles, start = {}, 0
    for name in ('qa', 'kv', 'qh', 'fh', 'ih', 'gh'):
        n_tiles = seg_widths[name] // PROJ_COL_TILE
        seg_tiles[name] = list(range(start, start + n_tiles))
        start += n_tiles
    w_in_order = sum((seg_tiles[name] for name in ('qa', 'qh', 'ih', 'gh', 'fh', 'kv')), [])
    w = dict(
        ffn1_pre=_row(ffn1_norm_pre[l]), ffn1_post=_row(ffn1_norm_post[l]),
        ffn1_wg=_cast_col_tiles(ffn1_w_gate[l], FF_TILE // 2),
        ffn1_wu=_cast_col_tiles(ffn1_w_up[l], FF_TILE // 2),
        ffn1_wd=ffn1_w_down[l].astype(BF16),
        mix_pre=_row(mix_norm_pre[l]), mix_post=_row(mix_norm_post[l]),
        w_in=_cast_col_tiles(w_in[l], PROJ_COL_TILE, w_in_order), w_out=w_out[l].astype(BF16),
        mem_pre=_row(mem_norm_pre[l]), mem_post=_row(mem_norm_post[l]),
        w_mem_q=w_mem_q[l].astype(BF16), w_mem_o=w_mem_o[l].astype(BF16),
        ffn2_pre=_row(ffn2_norm_pre[l]), ffn2_post=_row(ffn2_norm_post[l]),
        ffn2_wg=_cast_col_tiles(ffn2_w_gate[l], FF_TILE), ffn2_wu=_cast_col_tiles(ffn2_w_up[l], FF_TILE),
        ffn2_wd=ffn2_w_down[l].astype(BF16),
    )
    sinks = attn_sinks[l]
    attn_gain = _row(attn_out_gain[l])
    hgrn_gain = _row(hgrn_out_gain[l])

    mk, mv, mk_b, mv_b = _mem_kv(mem_prompt, _row(mem_norm_kv[l]),
                                 w_mem_k[l].astype(BF16), w_mem_v[l].astype(BF16))
    x1, zb, zf = _trunk_front(x_prompt.reshape(bp * sp, D_MODEL), w)
    zb3 = zb.reshape(bp, sp, ZB_WIDTH)
    zf3 = zf.reshape(bp, sp, ZF_WIDTH)
    a = _attn_prompt(zb3, zf3, sinks, attn_gain)
    r, p_state = _hgrn_prompt(zb3, zf3, hgrn_lb_logits, hgrn_gain)
    y_p = _trunk_back(x1, a, r, _mem_attn_kernel, mk_b, mv_b, w, bp, 1, 512).reshape(bp, sp, D_MODEL)
    k_off = ZF_KA * KV_WIDTH
    v_off = ZF_VA * KV_WIDTH
    p_wk = zf3[:, sp - WINDOW:, k_off:k_off + KV_WIDTH]
    p_wv = zf3[:, sp - WINDOW:, v_off:v_off + KV_WIDTH]

    x1s, zbs, zfs = _trunk_front(x_sample.reshape(bs * ts, D_MODEL), w)
    a_s, s_wk, s_wv = _attn_sample(zbs, zfs, cache_win_k[l].reshape(bs, WINDOW, KV_WIDTH),
                                   cache_win_v[l].reshape(bs, WINDOW, KV_WIDTH), sinks, attn_gain, ts)
    r_s, s_state = _hgrn_sample(zbs, zfs, state_hgrn[l], hgrn_lb_logits, hgrn_gain, ts)
    mem_rows = cache_mem_k.shape[2] * MEM_HEADS
    y_s = _trunk_back(x1s, a_s, r_s, _mem_attn_interleaved_kernel,
                      cache_mem_k[l].reshape(bs, mem_rows, MEM_HEAD_DIM),
                      cache_mem_v[l].reshape(bs, mem_rows, MEM_HEAD_DIM),
                      w, bs, 8, ts).reshape(bs, ts, D_MODEL)

    kv5 = lambda t, n: t.reshape(1, n, WINDOW, ATTN_KV_HEADS, ATTN_HEAD_DIM)
    mem5 = lambda t: t.reshape(1, bp, mt, MEM_HEADS, MEM_HEAD_DIM)
    return (y_p, y_s, kv5(p_wk, bp), kv5(p_wv, bp), p_state[None], mem5(mk), mem5(mv),
            kv5(s_wk, bs), kv5(s_wv, bs), s_state[None])
```

```python
import functools

import jax
import jax.numpy as jnp
from jax import lax
from jax.experimental import pallas as pl
from jax.experimental.pallas import tpu as pltpu

F32 = jnp.float32
BF16 = jnp.bfloat16

D_MODEL = 2048
D_FF = 5632
ATTN_HEADS = 16
ATTN_KV_HEADS = 4
ATTN_GROUP = ATTN_HEADS // ATTN_KV_HEADS
ATTN_HEAD_DIM = 64
WINDOW = 128
ATTN_WIDTH = ATTN_HEADS * ATTN_HEAD_DIM
KV_WIDTH = ATTN_KV_HEADS * ATTN_HEAD_DIM
HGRN_HEADS = 8
HGRN_DK = 128
HGRN_DV = 128
HGRN_KW = HGRN_HEADS * HGRN_DK
HGRN_VW = HGRN_HEADS * HGRN_DV
IN_PROJ_WIDTH = ATTN_WIDTH + 2 * KV_WIDTH + 2 * HGRN_KW + 2 * HGRN_VW
MEM_HEADS = 4
MEM_HEAD_DIM = 128
MEM_WIDTH = MEM_HEADS * MEM_HEAD_DIM
FFN_RESIDUAL = 0.5
EPS = 1e-6

ZB_WIDTH = ATTN_WIDTH + HGRN_KW + 2 * HGRN_VW
ZF_WIDTH = HGRN_KW + 2 * KV_WIDTH
ZB_QA, ZB_QH, ZB_IH, ZB_GH = 0, 1, 2, 3
ZF_FH = 0
ZF_KA, ZF_VA = 4, 5

VMEM_LIMIT_BYTES = 56 * 1024 * 1024

FFN_TOKEN_TILE = 512
PROJ_TOKEN_TILE = 1024
MIX_TOKEN_TILE = 512
FF_TILE = 512
FFN_SLOTS = 4
FFN_AHEAD = 2
PROJ_COL_TILE = 512
HGRN_CHUNK = 128
HGRN_STEP_CHUNKS = 2
HGRN_SAMPLE_SEQS = 4
ATTN_SAMPLE_SEQS = 8
ATTN_STEP_BLOCKS = 1
MASKED = -1e30

ALIBI_SLOPES = tuple(2.0 ** (-8.0 * (h + 1) / ATTN_HEADS) for h in range(ATTN_HEADS))


def _params(*semantics):
    return pltpu.CompilerParams(dimension_semantics=semantics,
                                vmem_limit_bytes=VMEM_LIMIT_BYTES)


def _rms(x, g):
    return x * lax.rsqrt(jnp.mean(x * x, axis=-1, keepdims=True) + EPS) * g


def _silu(x):
    return x * jax.nn.sigmoid(x)


def _dot(a, b):
    return jnp.dot(a, b, preferred_element_type=F32)


def _dot_nt(a, b):
    return lax.dot_general(a, b, (((1,), (1,)), ((), ())), preferred_element_type=F32)


def _dot_tn(a, b):
    return lax.dot_general(a, b, (((0,), (0,)), ((), ())), preferred_element_type=F32)


def _log2(n):
    assert n & (n - 1) == 0
    return n.bit_length() - 1


def _ffn_weight_copies(wg_hbm, wu_hbm, wd_hbm, wg_buf, wu_buf, wd_buf, sem, tile, slot):
    return (pltpu.make_async_copy(wg_hbm.at[tile], wg_buf.at[slot], sem.at[0, slot]),
            pltpu.make_async_copy(wu_hbm.at[tile], wu_buf.at[slot], sem.at[1, slot]),
            pltpu.make_async_copy(wd_hbm.at[tile], wd_buf.at[slot], sem.at[2, slot]))


def _ffn_kernel(x_ref, gpre_ref, gpost_ref, wg_hbm, wu_hbm, wd_hbm, o_ref,
                wg_buf, wu_buf, wd_buf, sem, *, token_tiles):
    i = pl.program_id(0)
    n = wg_hbm.shape[0]
    slots, ahead = FFN_SLOTS, FFN_AHEAD
    copies = functools.partial(_ffn_weight_copies, wg_hbm, wu_hbm, wd_hbm, wg_buf, wu_buf, wd_buf, sem)

    def start(tile):
        for c in copies(tile, tile % slots):
            c.start()

    @pl.when(i == 0)
    def _():
        for tile in range(ahead):
            start(tile)

    h = _rms(x_ref[...], gpre_ref[...]).astype(BF16)
    for j in range(n):
        nxt = j + ahead
        if nxt < n:
            assert nxt % slots not in {jj % slots for jj in range(j, nxt)}
            start(nxt)
        else:
            assert (nxt - n) % slots not in {jj % slots for jj in range(j, n)}
            pl.when(i + 1 < token_tiles)(functools.partial(start, nxt - n))
        slot = j % slots
        for c in copies(j, slot):
            c.wait()
        g = _dot(h, wg_buf[slot])
        u = _dot(h, wu_buf[slot])
        part = _dot((_silu(g) * u).astype(BF16), wd_buf[slot])
        if j == 0:
            o_ref[...] = part
        else:
            o_ref[...] += part
    o_ref[...] = x_ref[...] + _rms(o_ref[...], FFN_RESIDUAL * gpost_ref[...])


def _ffn(x, g_pre, g_post, wg, wu, wd):
    t = x.shape[0]
    tm = FFN_TOKEN_TILE
    n, _, tf = wg.shape
    assert FFN_AHEAD < FFN_SLOTS <= n
    any_space = pl.BlockSpec(memory_space=pl.ANY)
    return pl.pallas_call(
        functools.partial(_ffn_kernel, token_tiles=t // tm),
        grid=(t // tm,),
        in_specs=[
            pl.BlockSpec((tm, D_MODEL), lambda i: (i, 0)),
            pl.BlockSpec((1, D_MODEL), lambda i: (0, 0)),
            pl.BlockSpec((1, D_MODEL), lambda i: (0, 0)),
            any_space, any_space, any_space,
        ],
        out_specs=pl.BlockSpec((tm, D_MODEL), lambda i: (i, 0)),
        out_shape=jax.ShapeDtypeStruct((t, D_MODEL), F32),
        scratch_shapes=[
            pltpu.VMEM((FFN_SLOTS, D_MODEL, tf), BF16),
            pltpu.VMEM((FFN_SLOTS, D_MODEL, tf), BF16),
            pltpu.VMEM((FFN_SLOTS, tf, D_MODEL), BF16),
            pltpu.SemaphoreType.DMA((3, FFN_SLOTS)),
        ],
        compiler_params=_params("arbitrary"),
        name="ffn_half",
    )(x, g_pre, g_post, wg, wu, wd)


def _in_proj_kernel(x_ref, g_ref, w_ref, zb_ref, zf_ref, h_scr, *, bf16_steps):
    j = pl.program_id(1)

    @pl.when(j == 0)
    def _():
        h_scr[...] = _rms(x_ref[...], g_ref[...]).astype(BF16)

    z = _dot(h_scr[...], w_ref[...])

    @pl.when(j < bf16_steps)
    def _():
        zb_ref[...] = z.astype(BF16)

    @pl.when(j >= bf16_steps)
    def _():
        zf_ref[...] = z


def _in_proj(x, g, w):
    t = x.shape[0]
    tm, tn = min(PROJ_TOKEN_TILE, t), PROJ_COL_TILE
    nb = ZB_WIDTH // tn
    return pl.pallas_call(
        functools.partial(_in_proj_kernel, bf16_steps=nb),
        grid=(t // tm, IN_PROJ_WIDTH // tn),
        in_specs=[
            pl.BlockSpec((tm, D_MODEL), lambda i, j: (i, 0)),
            pl.BlockSpec((1, D_MODEL), lambda i, j: (0, 0)),
            pl.BlockSpec((None, D_MODEL, tn), lambda i, j: (j, 0, 0)),
        ],
        out_specs=[
            pl.BlockSpec((tm, tn), lambda i, j: (i, jnp.minimum(j, nb - 1))),
            pl.BlockSpec((tm, tn), lambda i, j: (i, jnp.maximum(j - nb, 0))),
        ],
        out_shape=[jax.ShapeDtypeStruct((t, ZB_WIDTH), BF16),
                   jax.ShapeDtypeStruct((t, ZF_WIDTH), F32)],
        scratch_shapes=[pltpu.VMEM((tm, D_MODEL), BF16)],
        compiler_params=_params("parallel", "arbitrary"),
        name="in_proj",
    )(x, g, w)


def _attn_prompt_kernel(sink_ref, q_ref, kc_ref, kp_ref, vc_ref, vp_ref, gain_ref, o_ref):
    n = pl.program_id(1)
    scale = ATTN_HEAD_DIM ** -0.5
    k_all = (jnp.concatenate([kp_ref[...], kc_ref[...]], axis=0) * scale).astype(BF16)
    v_all = jnp.concatenate([vp_ref[...], vc_ref[...]], axis=0).astype(BF16)
    row = lax.broadcasted_iota(jnp.int32, (WINDOW, 2 * WINDOW), 0)
    col = lax.broadcasted_iota(jnp.int32, (WINDOW, 2 * WINDOW), 1)
    dist = row + WINDOW - col
    in_window = (dist >= 0) & (dist < WINDOW)
    gain = gain_ref[...]
    for blk in range(ATTN_STEP_BLOCKS):
        valid = in_window
        if blk == 0:
            valid = valid & ((col >= WINDOW) | (n > 0))
        neg_dist = jnp.where(valid, -dist.astype(F32), MASKED)
        k = k_all[blk * WINDOW:(blk + 2) * WINDOW]
        v = v_all[blk * WINDOW:(blk + 2) * WINDOW]
        qrows = slice(blk * WINDOW, (blk + 1) * WINDOW)
        outs = []
        for hd in range(ATTN_HEADS):
            kvh = hd // ATTN_GROUP
            kh = k[:, kvh * ATTN_HEAD_DIM:(kvh + 1) * ATTN_HEAD_DIM]
            vh = v[:, kvh * ATTN_HEAD_DIM:(kvh + 1) * ATTN_HEAD_DIM]
            qh = q_ref[qrows, hd * ATTN_HEAD_DIM:(hd + 1) * ATTN_HEAD_DIM]
            s = _dot_nt(qh, kh) + ALIBI_SLOPES[hd] * neg_dist
            sink = sink_ref[hd]
            m = jnp.maximum(jnp.max(s, axis=-1, keepdims=True), sink)
            p = jnp.exp(s - m)
            denom = jnp.sum(p, axis=-1, keepdims=True) + jnp.exp(sink - m)
            outs.append(_dot(p.astype(BF16), vh) / denom)
        o = jnp.concatenate(outs, axis=1)
        o_ref[qrows, :] = _rms(o, gain).astype(BF16)


def _attn_prompt(zb3, zf3, sinks, gain):
    b, s, _ = zb3.shape
    nblk = ATTN_STEP_BLOCKS
    rows = nblk * WINDOW

    def cur(width_idx):
        return lambda bi, n: (bi, n, width_idx)

    def prev(width_idx):
        return lambda bi, n: (bi, jnp.maximum(n * nblk - 1, 0), width_idx)

    return pl.pallas_call(
        _attn_prompt_kernel,
        grid=(b, s // rows),
        in_specs=[
            pl.BlockSpec(memory_space=pltpu.SMEM),
            pl.BlockSpec((None, rows, ATTN_WIDTH), cur(ZB_QA)),
            pl.BlockSpec((None, rows, KV_WIDTH), cur(ZF_KA)),
            pl.BlockSpec((None, WINDOW, KV_WIDTH), prev(ZF_KA)),
            pl.BlockSpec((None, rows, KV_WIDTH), cur(ZF_VA)),
            pl.BlockSpec((None, WINDOW, KV_WIDTH), prev(ZF_VA)),
            pl.BlockSpec((1, ATTN_WIDTH), lambda bi, n: (0, 0)),
        ],
        out_specs=pl.BlockSpec((None, rows, ATTN_WIDTH), lambda bi, n: (bi, n, 0)),
        out_shape=jax.ShapeDtypeStruct((b, s, ATTN_WIDTH), BF16),
        compiler_params=_params("parallel", "arbitrary"),
        name="attn_prompt",
    )(sinks, zb3, zf3, zf3, zf3, zf3, gain)


def _attn_sample_kernel(sink_ref, q_ref, kn_ref, vn_ref, ck_ref, cv_ref, gain_ref,
                        o_ref, wk_ref, wv_ref, *, t_new):
    seqs = ck_ref.shape[0]
    t_bits = _log2(t_new)
    rows = ATTN_GROUP * t_new
    q_all = q_ref[...].astype(F32).reshape(seqs, t_new, ATTN_WIDTH)
    kn = kn_ref[...].reshape(seqs, t_new, KV_WIDTH)
    vn = vn_ref[...].reshape(seqs, t_new, KV_WIDTH)
    ck = ck_ref[...]
    cv = cv_ref[...]
    wk_ref[:, :WINDOW - t_new, :] = ck[:, t_new:, :]
    wk_ref[:, WINDOW - t_new:, :] = kn
    wv_ref[:, :WINDOW - t_new, :] = cv[:, t_new:, :]
    wv_ref[:, WINDOW - t_new:, :] = vn

    r_c = lax.broadcasted_iota(jnp.int32, (rows, WINDOW), 0)
    j_c = lax.broadcasted_iota(jnp.int32, (rows, WINDOW), 1)
    dist_c = (r_c & (t_new - 1)) + WINDOW - j_c
    neg_c = jnp.where(dist_c < WINDOW, -dist_c.astype(F32), MASKED)
    r_n = lax.broadcasted_iota(jnp.int32, (rows, t_new), 0)
    j_n = lax.broadcasted_iota(jnp.int32, (rows, t_new), 1)
    dist_n = (r_n & (t_new - 1)) - j_n
    neg_n = jnp.where(dist_n >= 0, -dist_n.astype(F32), MASKED)
    head_of_row = lax.broadcasted_iota(jnp.int32, (rows, 1), 0) >> t_bits
    scale = ATTN_HEAD_DIM ** -0.5

    outs = [None] * ATTN_HEADS
    for kvh in range(ATTN_KV_HEADS):
        slope = jnp.zeros((rows, 1), F32)
        sink = jnp.zeros((rows, 1), F32)
        for g in range(ATTN_GROUP):
            hd = kvh * ATTN_GROUP + g
            slope = jnp.where(head_of_row == g, ALIBI_SLOPES[hd], slope)
            sink = jnp.where(head_of_row == g, sink_ref[hd], sink)
        q = jnp.concatenate(
            [q_all[:, :, (kvh * ATTN_GROUP + g) * ATTN_HEAD_DIM:(kvh * ATTN_GROUP + g + 1) * ATTN_HEAD_DIM]
             for g in range(ATTN_GROUP)], axis=1).astype(BF16)
        sl = slice(kvh * ATTN_HEAD_DIM, (kvh + 1) * ATTN_HEAD_DIM)
        ck_h = (ck[:, :, sl] * scale).astype(BF16)
        kn_h = (kn[:, :, sl] * scale).astype(BF16)
        cv_h = cv[:, :, sl].astype(BF16)
        vn_h = vn[:, :, sl].astype(BF16)
        s_c = jnp.einsum('gqd,gkd->gqk', q, ck_h, preferred_element_type=F32) + (slope * neg_c)[None]
        s_n = jnp.einsum('gqd,gkd->gqk', q, kn_h, preferred_element_type=F32) + (slope * neg_n)[None]
        m = jnp.maximum(jnp.maximum(jnp.max(s_c, axis=-1, keepdims=True),
                                    jnp.max(s_n, axis=-1, keepdims=True)), sink[None])
        p_c = jnp.exp(s_c - m)
        p_n = jnp.exp(s_n - m)
        denom = (jnp.sum(p_c, axis=-1, keepdims=True) + jnp.sum(p_n, axis=-1, keepdims=True)
                 + jnp.exp(sink[None] - m))
        o = (jnp.einsum('gqk,gkd->gqd', p_c.astype(BF16), cv_h, preferred_element_type=F32)
             + jnp.einsum('gqk,gkd->gqd', p_n.astype(BF16), vn_h, preferred_element_type=F32)) / denom
        for g in range(ATTN_GROUP):
            outs[kvh * ATTN_GROUP + g] = o[:, g * t_new:(g + 1) * t_new, :]
    o_all = jnp.concatenate(outs, axis=2)
    o_ref[...] = _rms(o_all, gain_ref[...][None]).reshape(seqs * t_new, ATTN_WIDTH)


def _attn_sample(zb, zf, cache_k, cache_v, sinks, gain, t_new):
    b = cache_k.shape[0]
    gs = ATTN_SAMPLE_SEQS
    tok = lambda width, idx: pl.BlockSpec((gs * t_new, width), lambda i: (i, idx))
    cache_spec = pl.BlockSpec((gs, WINDOW, KV_WIDTH), lambda i: (i, 0, 0))
    return pl.pallas_call(
        functools.partial(_attn_sample_kernel, t_new=t_new),
        grid=(b // gs,),
        in_specs=[
            pl.BlockSpec(memory_space=pltpu.SMEM),
            tok(ATTN_WIDTH, ZB_QA), tok(KV_WIDTH, ZF_KA), tok(KV_WIDTH, ZF_VA),
            cache_spec, cache_spec,
            pl.BlockSpec((1, ATTN_WIDTH), lambda i: (0, 0)),
        ],
        out_specs=[tok(ATTN_WIDTH, 0), cache_spec, cache_spec],
        out_shape=[
            jax.ShapeDtypeStruct((b * t_new, ATTN_WIDTH), F32),
            jax.ShapeDtypeStruct((b, WINDOW, KV_WIDTH), F32),
            jax.ShapeDtypeStruct((b, WINDOW, KV_WIDTH), F32),
        ],
        compiler_params=_params("parallel"),
        name="attn_sample",
    )(sinks, zb, zf, zf, cache_k, cache_v, gain)


def _lower_bound(lbl_ref):
    logits = lbl_ref[...]
    e = jnp.exp(logits - jnp.max(logits, axis=0, keepdims=True))
    return e[0:1, :] / jnp.sum(e, axis=0, keepdims=True)


def _hgrn_features(q_raw, f_raw, lb):
    sg = jax.nn.sigmoid(f_raw)
    f = lb + (1.0 - lb) * sg
    k = (1.0 - lb) * (1.0 - sg)
    return _silu(q_raw), k, f, jnp.log2(f)


def _cumsum_rows(x, segment):
    row = lax.broadcasted_iota(jnp.int32, x.shape, 0) & (segment - 1)
    shift = 1
    while shift < segment:
        x = x + jnp.where(row >= shift, pltpu.roll(x, shift, axis=0), 0.0)
        shift *= 2
    return x


def _split3(x):
    hi = x.astype(BF16)
    r = x - hi.astype(F32)
    mid = r.astype(BF16)
    lo = (r - mid.astype(F32)).astype(BF16)
    return hi, mid, lo


def _cumsum_rows_mxu(x, tri):
    hi, mid, lo = _split3(x)
    return _dot(tri, hi) + _dot(tri, mid) + _dot(tri, lo)


def _column_broadcast(row_vec, n):
    rows = 16
    hi, mid, lo = _split3(row_vec)
    terms = jnp.concatenate([hi, mid, lo, jnp.zeros((rows - 3, n), BF16)], axis=0)
    ones = (lax.broadcasted_iota(jnp.int32, (rows, n), 0) < 3).astype(BF16)
    return _dot_tn(terms, ones)


def _hgrn_out(o, gain, g_raw):
    return _rms(o, gain) * _silu(g_raw)


def _pair_levels(c):
    t = lax.broadcasted_iota(jnp.int32, (c, c), 0)
    s = lax.broadcasted_iota(jnp.int32, (c, c), 1)
    masks = [t == s]
    half = 1
    while half < c:
        block = 2 * half
        same_block = (t >> _log2(block)) == (s >> _log2(block))
        masks.append(same_block & ((t & (block - 1)) >= half) & ((s & (block - 1)) < half))
        half = block
    return masks


def _pair_factors(q, k, f, log2f, l2):
    c, w = q.shape
    pos = lax.broadcasted_iota(jnp.int32, (c, w), 0) & 3
    e2 = jnp.exp2(jnp.where(pos == 0, pltpu.roll(log2f, c - 1, axis=0),
                            jnp.where(pos == 1, 0.0,
                                      jnp.where(pos == 2, log2f,
                                                log2f + pltpu.roll(log2f, 1, axis=0)))))
    qs = [q, q * f, q * e2]
    ks = [k, k, k * e2]
    half = 4
    while half < c:
        block = 2 * half
        ref = l2.reshape(c // block, block, w)[:, half - 1:half, :]
        ref = jnp.broadcast_to(ref, (c // block, block, w)).reshape(c, w)
        e = jnp.exp2(-jnp.abs(l2 - ref))
        qs.append(q * e)
        ks.append(k * e)
        half = block
    return [x.astype(BF16) for x in qs], [x.astype(BF16) for x in ks]


def _hgrn_prompt_kernel(lbl_ref, q_ref, f_ref, i_ref, g_ref, gain_ref, r_ref, s_ref):
    c = HGRN_CHUNK

    @pl.when(pl.program_id(1) == 0)
    def _():
        s_ref[...] = jnp.zeros_like(s_ref)

    lb = _lower_bound(lbl_ref)
    q_all, k_all, f_all, log2f_all = _hgrn_features(q_ref[...].astype(F32), f_ref[...], lb)
    masks = _pair_levels(c)
    tri = (lax.broadcasted_iota(jnp.int32, (c, c), 1)
           <= lax.broadcasted_iota(jnp.int32, (c, c), 0)).astype(BF16)
    gain = gain_ref[...]

    chunks = []
    for ci in range(HGRN_STEP_CHUNKS):
        rows = slice(ci * c, (ci + 1) * c)
        l2_c = _cumsum_rows_mxu(log2f_all[rows], tri)
        q_lv, k_lv = _pair_factors(q_all[rows], k_all[rows], f_all[rows], log2f_all[rows], l2_c)
        l_end = l2_c[c - 1:c, :]
        q_hat = (q_all[rows] * jnp.exp2(l2_c)).astype(BF16)
        k_end = (k_all[rows] * jnp.exp2(l_end - l2_c)).astype(BF16)
        chunks.append((rows, q_lv, k_lv, q_hat, k_end, jnp.exp2(l_end)))

    for rows, q_lv, k_lv, q_hat, k_end, decay_end in chunks:
        for h in range(HGRN_HEADS):
            sl = slice(h * HGRN_DK, (h + 1) * HGRN_DK)
            vb = i_ref[rows, sl]
            a = jnp.zeros((c, c), F32)
            for mask, q_f, k_f in zip(masks, q_lv, k_lv):
                a = jnp.where(mask, _dot_nt(q_f[:, sl], k_f[:, sl]), a)
            s_old = s_ref[h]
            o = _dot(a.astype(BF16), vb) + _dot(q_hat[:, sl], s_old.astype(BF16))
            s_ref[h] = (s_old * _column_broadcast(decay_end[:, sl], HGRN_DK)
                        + _dot_tn(k_end[:, sl], vb))
            r_ref[rows, sl] = _hgrn_out(o, gain, g_ref[rows, sl].astype(F32)).astype(r_ref.dtype)


def _hgrn_prompt(zb3, zf3, lb_logits, gain):
    b, s, _ = zb3.shape
    rows = HGRN_CHUNK * HGRN_STEP_CHUNKS
    seg = lambda idx: pl.BlockSpec((None, rows, HGRN_KW), lambda bi, ci: (bi, ci, idx))
    return pl.pallas_call(
        _hgrn_prompt_kernel,
        grid=(b, s // rows),
        in_specs=[
            pl.BlockSpec(lb_logits.shape, lambda bi, ci: (0, 0)),
            seg(ZB_QH), seg(ZF_FH), seg(ZB_IH), seg(ZB_GH),
            pl.BlockSpec((1, HGRN_DV), lambda bi, ci: (0, 0)),
        ],
        out_specs=[
            pl.BlockSpec((None, rows, HGRN_VW), lambda bi, ci: (bi, ci, 0)),
            pl.BlockSpec((None, HGRN_HEADS, HGRN_DK, HGRN_DV), lambda bi, ci: (bi, 0, 0, 0)),
        ],
        out_shape=[
            jax.ShapeDtypeStruct((b, s, HGRN_VW), BF16),
            jax.ShapeDtypeStruct((b, HGRN_HEADS, HGRN_DK, HGRN_DV), F32),
        ],
        compiler_params=_params("parallel", "arbitrary"),
        name="hgrn_prompt",
    )(lb_logits, zb3, zf3, zb3, zb3, gain)


def _hgrn_sample_kernel(lbl_ref, seg_ref, segt_ref, q_ref, f_ref, i_ref, g_ref, gain_ref, s0_ref,
                        r_ref, s_ref, *, t_new):
    seqs = s0_ref.shape[0]
    n = seqs * t_new
    width = HGRN_KW
    lb = _lower_bound(lbl_ref)
    q_all, k_all, _, log2f_all = _hgrn_features(q_ref[...].astype(F32), f_ref[...], lb)
    l2_all = _cumsum_rows(log2f_all, t_new)
    v_all = i_ref[...].astype(F32)
    g_all = g_ref[...].astype(F32)
    gain = gain_ref[...]

    def row_of_seq(x, t):
        x3 = x.reshape(seqs, t_new, width)
        return jnp.broadcast_to(x3[:, t:t + 1, :], (seqs, t_new, width)).reshape(n, width)

    src = lax.broadcasted_iota(jnp.int32, (n, width), 0) & (t_new - 1)
    w_parts = []
    for t in range(t_new):
        decay = jnp.exp2(jnp.minimum(row_of_seq(l2_all, t) - l2_all, 0.0))
        w_parts.append(jnp.where(src <= t, decay * k_all * row_of_seq(q_all, t), 0.0))
    w = jnp.concatenate(w_parts, axis=0).astype(BF16)
    a_heads = _dot(w, seg_ref[...])
    a_lanes = _dot(a_heads.astype(BF16), segt_ref[...])
    o_rows = []
    for t in range(t_new):
        p = (a_lanes[t * n:(t + 1) * n] * v_all).reshape(seqs, t_new, width)
        o_rows.append(jnp.sum(p, axis=1, keepdims=True))
    o_intra = jnp.concatenate(o_rows, axis=1).reshape(n, width)

    l_end = row_of_seq(l2_all, t_new - 1)
    q_hat = q_all * jnp.exp2(l2_all)
    k_end = k_all * jnp.exp2(l_end - l2_all)
    decay_end = jnp.exp2(l_end)
    for h in range(HGRN_HEADS):
        sl = slice(h * HGRN_DK, (h + 1) * HGRN_DK)
        o_parts = []
        for sq in range(seqs):
            rows = slice(sq * t_new, (sq + 1) * t_new)
            s_old = s0_ref[sq, h]
            o_parts.append(_dot(q_hat[rows, sl].astype(BF16), s_old.astype(BF16)))
            s_ref[sq, h] = (s_old * _column_broadcast(decay_end[sq * t_new:sq * t_new + 1, sl], HGRN_DK)
                            + _dot_tn(k_end[rows, sl].astype(BF16), v_all[rows, sl].astype(BF16)))
        o = o_intra[:, sl] + jnp.concatenate(o_parts, axis=0)
        r_ref[:, sl] = _hgrn_out(o, gain, g_all[:, sl])


def _hgrn_sample(zb, zf, state, lb_logits, gain, t_new):
    b = state.shape[0]
    gs = HGRN_SAMPLE_SEQS
    seg = lambda idx: pl.BlockSpec((gs * t_new, HGRN_KW), lambda i: (i, idx))
    state_spec = pl.BlockSpec((gs, HGRN_HEADS, HGRN_DK, HGRN_DV), lambda i: (i, 0, 0, 0))
    lane_head = jnp.arange(HGRN_KW, dtype=jnp.int32) // HGRN_DK
    head_seg = (lane_head[:, None] == jnp.arange(HGRN_DK, dtype=jnp.int32)[None, :]).astype(BF16)
    return pl.pallas_call(
        functools.partial(_hgrn_sample_kernel, t_new=t_new),
        grid=(b // gs,),
        in_specs=[
            pl.BlockSpec(lb_logits.shape, lambda i: (0, 0)),
            pl.BlockSpec((HGRN_KW, HGRN_DK), lambda i: (0, 0)),
            pl.BlockSpec((HGRN_DK, HGRN_KW), lambda i: (0, 0)),
            seg(ZB_QH), seg(ZF_FH), seg(ZB_IH), seg(ZB_GH),
            pl.BlockSpec((1, HGRN_DV), lambda i: (0, 0)),
            state_spec,
        ],
        out_specs=[seg(0), state_spec],
        out_shape=[
            jax.ShapeDtypeStruct((b * t_new, HGRN_VW), F32),
            jax.ShapeDtypeStruct((b, HGRN_HEADS, HGRN_DK, HGRN_DV), F32),
        ],
        compiler_params=_params("parallel"),
        name="hgrn_sample",
    )(lb_logits, head_seg, head_seg.T, zb, zf, zb, zb, gain, state)


def _mix_out_kernel(x_ref, a_ref, r_ref, wa_ref, wr_ref, gpost_ref, gmem_ref, wq_ref,
                    x2_ref, qm_ref):
    mixed = (_dot(a_ref[...].astype(BF16), wa_ref[...])
             + _dot(r_ref[...].astype(BF16), wr_ref[...]))
    x2 = x_ref[...] + _rms(mixed, gpost_ref[...])
    x2_ref[...] = x2
    qm_ref[...] = _dot(_rms(x2, gmem_ref[...]).astype(BF16), wq_ref[...])


def _mix_out(x, a, r, w_out, g_post, g_mem, w_q):
    t = x.shape[0]
    tm = MIX_TOKEN_TILE
    tok = lambda width: pl.BlockSpec((tm, width), lambda i: (i, 0))
    const = lambda shape, idx=(0, 0): pl.BlockSpec(shape, lambda i: idx)
    return pl.pallas_call(
        _mix_out_kernel,
        grid=(t // tm,),
        in_specs=[
            tok(D_MODEL), tok(ATTN_WIDTH), tok(HGRN_VW),
            const((ATTN_WIDTH, D_MODEL), (0, 0)),
            const((HGRN_VW, D_MODEL), (1, 0)),
            const((1, D_MODEL)), const((1, D_MODEL)),
            const((D_MODEL, MEM_WIDTH)),
        ],
        out_specs=[tok(D_MODEL), tok(MEM_WIDTH)],
        out_shape=[jax.ShapeDtypeStruct((t, D_MODEL), F32),
                   jax.ShapeDtypeStruct((t, MEM_WIDTH), F32)],
        compiler_params=_params("parallel"),
        name="mix_out",
    )(x, a, r, w_out, w_out, g_post, g_mem, w_q)


def _mem_kv_kernel(m_ref, g_ref, wk_ref, wv_ref, k_ref, v_ref, kb_ref, vb_ref):
    h = _rms(m_ref[...], g_ref[...]).astype(BF16)
    k = _dot(h, wk_ref[...])
    v = _dot(h, wv_ref[...])
    k_ref[...] = k
    v_ref[...] = v
    kb_ref[...] = k.astype(BF16)
    vb_ref[...] = v.astype(BF16)


def _mem_kv(mem, g, w_k, w_v):
    b, m, _ = mem.shape
    const = lambda shape: pl.BlockSpec(shape, lambda i: (0, 0))
    out = pl.BlockSpec((None, m, MEM_WIDTH), lambda i: (i, 0, 0))
    return pl.pallas_call(
        _mem_kv_kernel,
        grid=(b,),
        in_specs=[pl.BlockSpec((None, m, D_MODEL), lambda i: (i, 0, 0)), const((1, D_MODEL)),
                  const((D_MODEL, MEM_WIDTH)), const((D_MODEL, MEM_WIDTH))],
        out_specs=[out] * 4,
        out_shape=[jax.ShapeDtypeStruct((b, m, MEM_WIDTH), F32)] * 2
        + [jax.ShapeDtypeStruct((b, m, MEM_WIDTH), BF16)] * 2,
        compiler_params=_params("parallel"),
        name="mem_kv",
    )(mem, g, w_k, w_v)


def _mem_attn_finish(o, x_ref, wo_ref, g_ref, o_ref):
    seqs, tq, _ = x_ref.shape
    y = _dot(o.reshape(seqs * tq, MEM_WIDTH).astype(BF16), wo_ref[...])
    x = x_ref[...].reshape(seqs * tq, D_MODEL)
    o_ref[...] = (x + _rms(y, g_ref[...])).reshape(seqs, tq, D_MODEL)


def _mem_attn_kernel(x_ref, q_ref, mk_ref, mv_ref, wo_ref, g_ref, o_ref):
    scale = MEM_HEAD_DIM ** -0.5
    outs = []
    for h in range(MEM_HEADS):
        sl = slice(h * MEM_HEAD_DIM, (h + 1) * MEM_HEAD_DIM)
        q = q_ref[:, :, sl].astype(BF16)
        s = jnp.einsum('gqd,gkd->gqk', q, mk_ref[:, :, sl], preferred_element_type=F32) * scale
        p = jnp.exp(s - jnp.max(s, axis=-1, keepdims=True))
        denom = jnp.sum(p, axis=-1, keepdims=True)
        outs.append(jnp.einsum('gqk,gkd->gqd', p.astype(BF16), mv_ref[:, :, sl],
                               preferred_element_type=F32) / denom)
    _mem_attn_finish(jnp.concatenate(outs, axis=2), x_ref, wo_ref, g_ref, o_ref)


def _mem_attn_interleaved_kernel(x_ref, q_ref, mk_ref, mv_ref, wo_ref, g_ref, o_ref):
    seqs, tq, _ = q_ref.shape
    rows = MEM_HEADS * tq
    cols = mk_ref.shape[1]
    scale = MEM_HEAD_DIM ** -0.5
    q = jnp.concatenate([q_ref[:, :, h * MEM_HEAD_DIM:(h + 1) * MEM_HEAD_DIM]
                         for h in range(MEM_HEADS)], axis=1).astype(BF16)
    s = jnp.einsum('gqd,gkd->gqk', q, mk_ref[...].astype(BF16), preferred_element_type=F32) * scale
    row_head = lax.broadcasted_iota(jnp.int32, (rows, cols), 0) >> _log2(tq)
    col_head = lax.broadcasted_iota(jnp.int32, (rows, cols), 1) & (MEM_HEADS - 1)
    s = jnp.where((row_head == col_head)[None], s, -jnp.inf)
    p = jnp.exp(s - jnp.max(s, axis=-1, keepdims=True))
    denom = jnp.sum(p, axis=-1, keepdims=True)
    o = jnp.einsum('gqk,gkd->gqd', p.astype(BF16), mv_ref[...].astype(BF16),
                   preferred_element_type=F32) / denom
    o = jnp.concatenate([o[:, h * tq:(h + 1) * tq, :] for h in range(MEM_HEADS)], axis=2)
    _mem_attn_finish(o, x_ref, wo_ref, g_ref, o_ref)


def _mem_attn(body, x3, q3, mem_k, mem_v, w_o, g_post, seqs, tq):
    nseq, slen, _ = x3.shape
    tok = lambda width: pl.BlockSpec((seqs, tq, width), lambda i, j: (i, j, 0))
    mem = pl.BlockSpec((seqs,) + mem_k.shape[1:], lambda i, j: (i, 0, 0))
    const = lambda shape: pl.BlockSpec(shape, lambda i, j: (0, 0))
    return pl.pallas_call(
        body,
        grid=(nseq // seqs, slen // tq),
        in_specs=[tok(D_MODEL), tok(MEM_WIDTH), mem, mem,
                  const((MEM_WIDTH, D_MODEL)), const((1, D_MODEL))],
        out_specs=tok(D_MODEL),
        out_shape=jax.ShapeDtypeStruct((nseq, slen, D_MODEL), F32),
        compiler_params=_params("parallel", "arbitrary"),
        name="mem_attn",
    )(x3, q3, mem_k, mem_v, w_o, g_post)


def _row(g):
    return g.reshape(1, -1)


def _cast_tiles_kernel(order_ref, w_ref, o_ref):
    del order_ref
    o_ref[...] = w_ref[...].astype(BF16)


def _cast_col_tiles(w, tile, order=None):
    k, n = w.shape
    nt = n // tile
    order = jnp.arange(nt, dtype=jnp.int32) if order is None else jnp.asarray(order, jnp.int32)
    return pl.pallas_call(
        _cast_tiles_kernel,
        grid_spec=pltpu.PrefetchScalarGridSpec(
            num_scalar_prefetch=1,
            grid=(nt,),
            in_specs=[pl.BlockSpec((k, tile), lambda j, order_ref: (0, order_ref[j]))],
            out_specs=pl.BlockSpec((None, k, tile), lambda j, order_ref: (j, 0, 0)),
        ),
        out_shape=jax.ShapeDtypeStruct((nt, k, tile), BF16),
        compiler_params=_params("parallel"),
        name="cast_col_tiles",
    )(order, w)


def _trunk_front(x2d, w):
    x1 = _ffn(x2d, w['ffn1_pre'], w['ffn1_post'], w['ffn1_wg'], w['ffn1_wu'], w['ffn1_wd'])
    zb, zf = _in_proj(x1, w['mix_pre'], w['w_in'])
    return x1, zb, zf


def _trunk_back(x1, a, r, mem_body, mem_k, mem_v, w, nseq, seqs, tq):
    t = x1.shape[0]
    x2, qm = _mix_out(x1, a.reshape(t, ATTN_WIDTH), r.reshape(t, HGRN_VW), w['w_out'],
                      w['mix_post'], w['mem_pre'], w['w_mem_q'])
    x3 = _mem_attn(mem_body, x2.reshape(nseq, t // nseq, D_MODEL),
                   qm.reshape(nseq, t // nseq, MEM_WIDTH),
                   mem_k, mem_v, w['w_mem_o'], w['mem_post'], seqs, tq)
    return _ffn(x3.reshape(t, D_MODEL), w['ffn2_pre'], w['ffn2_post'],
                w['ffn2_wg'], w['ffn2_wu'], w['ffn2_wd'])


def kernel(x_prompt, x_sample, mem_prompt, cache_win_k, cache_win_v, state_hgrn, cache_mem_k, cache_mem_v, ffn1_norm_pre, ffn1_norm_post, ffn1_w_gate, ffn1_w_up, ffn1_w_down, mix_norm_pre, mix_norm_post, w_in, attn_sinks, hgrn_lb_logits, attn_out_gain, hgrn_out_gain, w_out, mem_norm_pre, mem_norm_post, mem_norm_kv, w_mem_q, w_mem_k, w_mem_v, w_mem_o, ffn2_norm_pre, ffn2_norm_post, ffn2_w_gate, ffn2_w_up, ffn2_w_down):
    bp, sp, _ = x_prompt.shape
    bs, ts, _ = x_sample.shape
    mt = mem_prompt.shape[1]
    l = 0

    seg_widths = dict(qa=ATTN_WIDTH, kv=2 * KV_WIDTH, qh=HGRN_KW, fh=HGRN_KW, ih=HGRN_VW, gh=HGRN_VW)
    seg_tiles, start = {}, 0
    for name in ('qa', 'kv', 'qh', 'fh', 'ih', 'gh'):
        n_tiles = seg_widths[name] // PROJ_COL_TILE
        seg_tiles[name] = list(range(start, start + n_tiles))
        start += n_tiles
    w_in_order = sum((seg_tiles[name] for name in ('qa', 'qh', 'ih', 'gh', 'fh', 'kv')), [])
    w = dict(
        ffn1_pre=_row(ffn1_norm_pre[l]), ffn1_post=_row(ffn1_norm_post[l]),
        ffn1_wg=_cast_col_tiles(ffn1_w_gate[l], FF_TILE), ffn1_wu=_cast_col_tiles(ffn1_w_up[l], FF_TILE),
        ffn1_wd=ffn1_w_down[l].astype(BF16).reshape(D_FF // FF_TILE, FF_TILE, D_MODEL),
        mix_pre=_row(mix_norm_pre[l]), mix_post=_row(mix_norm_post[l]),
        w_in=_cast_col_tiles(w_in[l], PROJ_COL_TILE, w_in_order), w_out=w_out[l].astype(BF16),
        mem_pre=_row(mem_norm_pre[l]), mem_post=_row(mem_norm_post[l]),
        w_mem_q=w_mem_q[l].astype(BF16), w_mem_o=w_mem_o[l].astype(BF16),
        ffn2_pre=_row(ffn2_norm_pre[l]), ffn2_post=_row(ffn2_norm_post[l]),
        ffn2_wg=_cast_col_tiles(ffn2_w_gate[l], FF_TILE), ffn2_wu=_cast_col_tiles(ffn2_w_up[l], FF_TILE),
        ffn2_wd=ffn2_w_down[l].astype(BF16).reshape(D_FF // FF_TILE, FF_TILE, D_MODEL),
    )
    sinks = attn_sinks[l]
    attn_gain = _row(attn_out_gain[l])
    hgrn_gain = _row(hgrn_out_gain[l])

    mk, mv, mk_b, mv_b = _mem_kv(mem_prompt, _row(mem_norm_kv[l]),
                                 w_mem_k[l].astype(BF16), w_mem_v[l].astype(BF16))
    x1, zb, zf = _trunk_front(x_prompt.reshape(bp * sp, D_MODEL), w)
    zb3 = zb.reshape(bp, sp, ZB_WIDTH)
    zf3 = zf.reshape(bp, sp, ZF_WIDTH)
    a = _attn_prompt(zb3, zf3, sinks, attn_gain)
    r, p_state = _hgrn_prompt(zb3, zf3, hgrn_lb_logits, hgrn_gain)
    y_p = _trunk_back(x1, a, r, _mem_attn_kernel, mk_b, mv_b, w, bp, 1, 512).reshape(bp, sp, D_MODEL)
    k_off = ZF_KA * KV_WIDTH
    v_off = ZF_VA * KV_WIDTH
    p_wk = zf3[:, sp - WINDOW:, k_off:k_off + KV_WIDTH]
    p_wv = zf3[:, sp - WINDOW:, v_off:v_off + KV_WIDTH]

    x1s, zbs, zfs = _trunk_front(x_sample.reshape(bs * ts, D_MODEL), w)
    a_s, s_wk, s_wv = _attn_sample(zbs, zfs, cache_win_k[l].reshape(bs, WINDOW, KV_WIDTH),
                                   cache_win_v[l].reshape(bs, WINDOW, KV_WIDTH), sinks, attn_gain, ts)
    r_s, s_state = _hgrn_sample(zbs, zfs, state_hgrn[l], hgrn_lb_logits, hgrn_gain, ts)
    mem_rows = cache_mem_k.shape[2] * MEM_HEADS
    y_s = _trunk_back(x1s, a_s, r_s, _mem_attn_interleaved_kernel,
                      cache_mem_k[l].reshape(bs, mem_rows, MEM_HEAD_DIM),
                      cache_mem_v[l].reshape(bs, mem_rows, MEM_HEAD_DIM),
                      w, bs, 8, ts).reshape(bs, ts, D_MODEL)

    kv5 = lambda t, n: t.reshape(1, n, WINDOW, ATTN_KV_HEADS, ATTN_HEAD_DIM)
    mem5 = lambda t: t.reshape(1, bp, mt, MEM_HEADS, MEM_HEAD_DIM)
    return (y_p, y_s, kv5(p_wk, bp), kv5(p_wv, bp), p_state[None], mem5(mk), mem5(mv),
            kv5(s_wk, bs), kv5(s_wv, bs), s_state[None])
```

```python
import functools

import jax
import jax.numpy as jnp
from jax import lax
from jax.experimental import pallas as pl
from jax.experimental.pallas import tpu as pltpu

F32 = jnp.float32
BF16 = jnp.bfloat16

D_MODEL = 2048
D_FF = 5632
ATTN_HEADS = 16
ATTN_KV_HEADS = 4
ATTN_GROUP = ATTN_HEADS // ATTN_KV_HEADS
ATTN_HEAD_DIM = 64
WINDOW = 128
ATTN_WIDTH = ATTN_HEADS * ATTN_HEAD_DIM
KV_WIDTH = ATTN_KV_HEADS * ATTN_HEAD_DIM
HGRN_HEADS = 8
HGRN_DK = 128
HGRN_DV = 128
HGRN_KW = HGRN_HEADS * HGRN_DK
HGRN_VW = HGRN_HEADS * HGRN_DV
IN_PROJ_WIDTH = ATTN_WIDTH + 2 * KV_WIDTH + 2 * HGRN_KW + 2 * HGRN_VW
MEM_HEADS = 4
MEM_HEAD_DIM = 128
MEM_WIDTH = MEM_HEADS * MEM_HEAD_DIM
FFN_RESIDUAL = 0.5
EPS = 1e-6

ZB_WIDTH = ATTN_WIDTH + HGRN_KW + 2 * HGRN_VW
ZF_WIDTH = HGRN_KW + 2 * KV_WIDTH
ZB_QA, ZB_QH, ZB_IH, ZB_GH = 0, 1, 2, 3
ZF_FH = 0
ZF_KA, ZF_VA = 4, 5

VMEM_LIMIT_BYTES = 56 * 1024 * 1024

FFN_TOKEN_TILE = 512
PROJ_TOKEN_TILE = 512
MIX_TOKEN_TILE = 512
FF_TILE = 512
RING_SLOTS = 4
RING_AHEAD = 2
PROJ_COL_TILE = 512
HGRN_CHUNK = 128
HGRN_STEP_CHUNKS = 2
HGRN_SAMPLE_SEQS = 4
ATTN_SAMPLE_SEQS = 8
ATTN_STEP_BLOCKS = 4
MASKED = -1e30

ALIBI_SLOPES = tuple(2.0 ** (-8.0 * (h + 1) / ATTN_HEADS) for h in range(ATTN_HEADS))


def _params(*semantics):
    return pltpu.CompilerParams(dimension_semantics=semantics,
                                vmem_limit_bytes=VMEM_LIMIT_BYTES)


def _rms(x, g):
    return x * lax.rsqrt(jnp.mean(x * x, axis=-1, keepdims=True) + EPS) * g


def _silu(x):
    return x * jax.nn.sigmoid(x)


def _dot(a, b):
    return jnp.dot(a, b, preferred_element_type=F32)


def _dot_nt(a, b):
    return lax.dot_general(a, b, (((1,), (1,)), ((), ())), preferred_element_type=F32)


def _dot_tn(a, b):
    return lax.dot_general(a, b, (((0,), (0,)), ((), ())), preferred_element_type=F32)


def _log2(n):
    assert n & (n - 1) == 0
    return n.bit_length() - 1


def _stream_weight_tiles(n, token_tiles, copies, step):
    i = pl.program_id(0)
    slots, ahead = RING_SLOTS, RING_AHEAD
    assert ahead < slots <= n

    def start(tile):
        for c in copies(tile, tile % slots):
            c.start()

    @pl.when(i == 0)
    def _():
        for tile in range(ahead):
            start(tile)

    for j in range(n):
        nxt = j + ahead
        if nxt < n:
            assert nxt % slots not in {jj % slots for jj in range(j, nxt)}
            start(nxt)
        else:
            assert (nxt - n) % slots not in {jj % slots for jj in range(j, n)}
            pl.when(i + 1 < token_tiles)(functools.partial(start, nxt - n))
        for c in copies(j, j % slots):
            c.wait()
        step(j, j % slots)


def _ffn_kernel(x_ref, gpre_ref, gpost_ref, wg_hbm, wu_hbm, wd_hbm, o_ref,
                wg_buf, wu_buf, wd_buf, sem, *, token_tiles):
    def copies(tile, slot):
        return (pltpu.make_async_copy(wg_hbm.at[tile], wg_buf.at[slot], sem.at[0, slot]),
                pltpu.make_async_copy(wu_hbm.at[tile], wu_buf.at[slot], sem.at[1, slot]),
                pltpu.make_async_copy(wd_hbm.at[tile], wd_buf.at[slot], sem.at[2, slot]))

    h = _rms(x_ref[...], gpre_ref[...]).astype(BF16)

    def step(j, slot):
        g = _dot(h, wg_buf[slot])
        u = _dot(h, wu_buf[slot])
        part = _dot((_silu(g) * u).astype(BF16), wd_buf[slot])
        if j == 0:
            o_ref[...] = part
        else:
            o_ref[...] += part

    _stream_weight_tiles(wg_hbm.shape[0], token_tiles, copies, step)
    o_ref[...] = x_ref[...] + _rms(o_ref[...], FFN_RESIDUAL * gpost_ref[...])


def _ffn(x, g_pre, g_post, wg, wu, wd):
    t = x.shape[0]
    tm = FFN_TOKEN_TILE
    n, _, tf = wg.shape
    any_space = pl.BlockSpec(memory_space=pl.ANY)
    return pl.pallas_call(
        functools.partial(_ffn_kernel, token_tiles=t // tm),
        grid=(t // tm,),
        in_specs=[
            pl.BlockSpec((tm, D_MODEL), lambda i: (i, 0)),
            pl.BlockSpec((1, D_MODEL), lambda i: (0, 0)),
            pl.BlockSpec((1, D_MODEL), lambda i: (0, 0)),
            any_space, any_space, any_space,
        ],
        out_specs=pl.BlockSpec((tm, D_MODEL), lambda i: (i, 0)),
        out_shape=jax.ShapeDtypeStruct((t, D_MODEL), F32),
        scratch_shapes=[
            pltpu.VMEM((RING_SLOTS, D_MODEL, tf), BF16),
            pltpu.VMEM((RING_SLOTS, D_MODEL, tf), BF16),
            pltpu.VMEM((RING_SLOTS, tf, D_MODEL), BF16),
            pltpu.SemaphoreType.DMA((3, RING_SLOTS)),
        ],
        compiler_params=_params("arbitrary"),
        name="ffn_half",
    )(x, g_pre, g_post, wg, wu, wd)


def _in_proj_kernel(x_ref, g_ref, w_hbm, zb_ref, zf_ref, w_buf, sem, *, token_tiles):
    tn = w_hbm.shape[2]
    nb = zb_ref.shape[1] // tn

    def copies(tile, slot):
        return (pltpu.make_async_copy(w_hbm.at[tile], w_buf.at[slot], sem.at[slot]),)

    h = _rms(x_ref[...], g_ref[...]).astype(BF16)

    def step(j, slot):
        z = _dot(h, w_buf[slot])
        if j < nb:
            zb_ref[:, j * tn:(j + 1) * tn] = z.astype(BF16)
        else:
            zf_ref[:, (j - nb) * tn:(j - nb + 1) * tn] = z

    _stream_weight_tiles(w_hbm.shape[0], token_tiles, copies, step)


def _in_proj(x, g, w):
    t = x.shape[0]
    tm = PROJ_TOKEN_TILE
    n, _, tn = w.shape
    return pl.pallas_call(
        functools.partial(_in_proj_kernel, token_tiles=t // tm),
        grid=(t // tm,),
        in_specs=[
            pl.BlockSpec((tm, D_MODEL), lambda i: (i, 0)),
            pl.BlockSpec((1, D_MODEL), lambda i: (0, 0)),
            pl.BlockSpec(memory_space=pl.ANY),
        ],
        out_specs=[
            pl.BlockSpec((tm, ZB_WIDTH), lambda i: (i, 0)),
            pl.BlockSpec((tm, ZF_WIDTH), lambda i: (i, 0)),
        ],
        out_shape=[jax.ShapeDtypeStruct((t, ZB_WIDTH), BF16),
                   jax.ShapeDtypeStruct((t, ZF_WIDTH), F32)],
        scratch_shapes=[pltpu.VMEM((RING_SLOTS, D_MODEL, tn), BF16),
                        pltpu.SemaphoreType.DMA((RING_SLOTS,))],
        compiler_params=_params("arbitrary"),
        name="in_proj",
    )(x, g, w)


def _attn_prompt_kernel(sink_ref, q_ref, kc_ref, kp_ref, vc_ref, vp_ref, gain_ref, o_ref):
    n = pl.program_id(1)
    nblk = q_ref.shape[0] // WINDOW
    scale = ATTN_HEAD_DIM ** -0.5
    row = lax.broadcasted_iota(jnp.int32, (WINDOW, 2 * WINDOW), 0)
    col = lax.broadcasted_iota(jnp.int32, (WINDOW, 2 * WINDOW), 1)
    dist = row + WINDOW - col
    in_window = (dist >= 0) & (dist < WINDOW)
    gain = gain_ref[...]

    def block(qrows, k, v, valid):
        k = (k * scale).astype(BF16)
        v = v.astype(BF16)
        neg_dist = jnp.where(valid, -dist.astype(F32), MASKED)
        outs = []
        for hd in range(ATTN_HEADS):
            kvh = hd // ATTN_GROUP
            kh = k[:, kvh * ATTN_HEAD_DIM:(kvh + 1) * ATTN_HEAD_DIM]
            vh = v[:, kvh * ATTN_HEAD_DIM:(kvh + 1) * ATTN_HEAD_DIM]
            qh = q_ref[qrows, hd * ATTN_HEAD_DIM:(hd + 1) * ATTN_HEAD_DIM]
            s = _dot_nt(qh, kh) + ALIBI_SLOPES[hd] * neg_dist
            sink = sink_ref[hd]
            m = jnp.maximum(jnp.max(s, axis=-1, keepdims=True), sink)
            p = jnp.exp(s - m)
            denom = jnp.sum(p, axis=-1, keepdims=True) + jnp.exp(sink - m)
            outs.append(_dot(p.astype(BF16), vh) / denom)
        o = jnp.concatenate(outs, axis=1)
        o_ref[qrows, :] = _rms(o, gain).astype(BF16)

    block(slice(0, WINDOW),
          jnp.concatenate([kp_ref[...], kc_ref[0:WINDOW, :]], axis=0),
          jnp.concatenate([vp_ref[...], vc_ref[0:WINDOW, :]], axis=0),
          in_window & ((col >= WINDOW) | (n > 0)))

    def later_block(blk, carry):
        first = pl.multiple_of((blk - 1) * WINDOW, WINDOW)
        kv_rows = pl.ds(first, 2 * WINDOW)
        block(pl.ds(first + WINDOW, WINDOW), kc_ref[kv_rows, :], vc_ref[kv_rows, :], in_window)
        return carry

    lax.fori_loop(1, nblk, later_block, 0)


def _attn_prompt(zb3, zf3, sinks, gain):
    b, s, _ = zb3.shape
    nblk = ATTN_STEP_BLOCKS
    rows = nblk * WINDOW

    def cur(width_idx):
        return lambda bi, n: (bi, n, width_idx)

    def prev(width_idx):
        return lambda bi, n: (bi, jnp.maximum(n * nblk - 1, 0), width_idx)

    return pl.pallas_call(
        _attn_prompt_kernel,
        grid=(b, s // rows),
        in_specs=[
            pl.BlockSpec(memory_space=pltpu.SMEM),
            pl.BlockSpec((None, rows, ATTN_WIDTH), cur(ZB_QA)),
            pl.BlockSpec((None, rows, KV_WIDTH), cur(ZF_KA)),
            pl.BlockSpec((None, WINDOW, KV_WIDTH), prev(ZF_KA)),
            pl.BlockSpec((None, rows, KV_WIDTH), cur(ZF_VA)),
            pl.BlockSpec((None, WINDOW, KV_WIDTH), prev(ZF_VA)),
            pl.BlockSpec((1, ATTN_WIDTH), lambda bi, n: (0, 0)),
        ],
        out_specs=pl.BlockSpec((None, rows, ATTN_WIDTH), lambda bi, n: (bi, n, 0)),
        out_shape=jax.ShapeDtypeStruct((b, s, ATTN_WIDTH), BF16),
        compiler_params=_params("parallel", "arbitrary"),
        name="attn_prompt",
    )(sinks, zb3, zf3, zf3, zf3, zf3, gain)


def _attn_sample_kernel(sink_ref, q_ref, kn_ref, vn_ref, ck_ref, cv_ref, gain_ref,
                        o_ref, wk_ref, wv_ref, *, t_new):
    seqs = ck_ref.shape[0]
    t_bits = _log2(t_new)
    rows = ATTN_GROUP * t_new
    q_all = q_ref[...].astype(F32).reshape(seqs, t_new, ATTN_WIDTH)
    kn = kn_ref[...].reshape(seqs, t_new, KV_WIDTH)
    vn = vn_ref[...].reshape(seqs, t_new, KV_WIDTH)
    ck = ck_ref[...]
    cv = cv_ref[...]
    wk_ref[:, :WINDOW - t_new, :] = ck[:, t_new:, :]
    wk_ref[:, WINDOW - t_new:, :] = kn
    wv_ref[:, :WINDOW - t_new, :] = cv[:, t_new:, :]
    wv_ref[:, WINDOW - t_new:, :] = vn

    r_c = lax.broadcasted_iota(jnp.int32, (rows, WINDOW), 0)
    j_c = lax.broadcasted_iota(jnp.int32, (rows, WINDOW), 1)
    dist_c = (r_c & (t_new - 1)) + WINDOW - j_c
    neg_c = jnp.where(dist_c < WINDOW, -dist_c.astype(F32), MASKED)
    r_n = lax.broadcasted_iota(jnp.int32, (rows, t_new), 0)
    j_n = lax.broadcasted_iota(jnp.int32, (rows, t_new), 1)
    dist_n = (r_n & (t_new - 1)) - j_n
    neg_n = jnp.where(dist_n >= 0, -dist_n.astype(F32), MASKED)
    head_of_row = lax.broadcasted_iota(jnp.int32, (rows, 1), 0) >> t_bits
    scale = ATTN_HEAD_DIM ** -0.5

    outs = [None] * ATTN_HEADS
    for kvh in range(ATTN_KV_HEADS):
        slope = jnp.zeros((rows, 1), F32)
        sink = jnp.zeros((rows, 1), F32)
        for g in range(ATTN_GROUP):
            hd = kvh * ATTN_GROUP + g
            slope = jnp.where(head_of_row == g, ALIBI_SLOPES[hd], slope)
            sink = jnp.where(head_of_row == g, sink_ref[hd], sink)
        q = jnp.concatenate(
            [q_all[:, :, (kvh * ATTN_GROUP + g) * ATTN_HEAD_DIM:(kvh * ATTN_GROUP + g + 1) * ATTN_HEAD_DIM]
             for g in range(ATTN_GROUP)], axis=1).astype(BF16)
        sl = slice(kvh * ATTN_HEAD_DIM, (kvh + 1) * ATTN_HEAD_DIM)
        ck_h = (ck[:, :, sl] * scale).astype(BF16)
        kn_h = (kn[:, :, sl] * scale).astype(BF16)
        cv_h = cv[:, :, sl].astype(BF16)
        vn_h = vn[:, :, sl].astype(BF16)
        s_c = jnp.einsum('gqd,gkd->gqk', q, ck_h, preferred_element_type=F32) + (slope * neg_c)[None]
        s_n = jnp.einsum('gqd,gkd->gqk', q, kn_h, preferred_element_type=F32) + (slope * neg_n)[None]
        m = jnp.maximum(jnp.maximum(jnp.max(s_c, axis=-1, keepdims=True),
                                    jnp.max(s_n, axis=-1, keepdims=True)), sink[None])
        p_c = jnp.exp(s_c - m)
        p_n = jnp.exp(s_n - m)
        denom = (jnp.sum(p_c, axis=-1, keepdims=True) + jnp.sum(p_n, axis=-1, keepdims=True)
                 + jnp.exp(sink[None] - m))
        o = (jnp.einsum('gqk,gkd->gqd', p_c.astype(BF16), cv_h, preferred_element_type=F32)
             + jnp.einsum('gqk,gkd->gqd', p_n.astype(BF16), vn_h, preferred_element_type=F32)) / denom
        for g in range(ATTN_GROUP):
            outs[kvh * ATTN_GROUP + g] = o[:, g * t_new:(g + 1) * t_new, :]
    o_all = jnp.concatenate(outs, axis=2)
    o_ref[...] = _rms(o_all, gain_ref[...][None]).reshape(seqs * t_new, ATTN_WIDTH)


def _attn_sample(zb, zf, cache_k, cache_v, sinks, gain, t_new):
    b = cache_k.shape[0]
    gs = ATTN_SAMPLE_SEQS
    tok = lambda width, idx: pl.BlockSpec((gs * t_new, width), lambda i: (i, idx))
    cache_spec = pl.BlockSpec((gs, WINDOW, KV_WIDTH), lambda i: (i, 0, 0))
    return pl.pallas_call(
        functools.partial(_attn_sample_kernel, t_new=t_new),
        grid=(b // gs,),
        in_specs=[
            pl.BlockSpec(memory_space=pltpu.SMEM),
            tok(ATTN_WIDTH, ZB_QA), tok(KV_WIDTH, ZF_KA), tok(KV_WIDTH, ZF_VA),
            cache_spec, cache_spec,
            pl.BlockSpec((1, ATTN_WIDTH), lambda i: (0, 0)),
        ],
        out_specs=[tok(ATTN_WIDTH, 0), cache_spec, cache_spec],
        out_shape=[
            jax.ShapeDtypeStruct((b * t_new, ATTN_WIDTH), F32),
            jax.ShapeDtypeStruct((b, WINDOW, KV_WIDTH), F32),
            jax.ShapeDtypeStruct((b, WINDOW, KV_WIDTH), F32),
        ],
        compiler_params=_params("parallel"),
        name="attn_sample",
    )(sinks, zb, zf, zf, cache_k, cache_v, gain)


def _lower_bound(lbl_ref):
    logits = lbl_ref[...]
    e = jnp.exp(logits - jnp.max(logits, axis=0, keepdims=True))
    return e[0:1, :] / jnp.sum(e, axis=0, keepdims=True)


def _hgrn_features(q_raw, f_raw, lb):
    sg = jax.nn.sigmoid(f_raw)
    f = lb + (1.0 - lb) * sg
    k = (1.0 - lb) * (1.0 - sg)
    return _silu(q_raw), k, f, jnp.log2(f)


def _cumsum_rows(x, segment):
    row = lax.broadcasted_iota(jnp.int32, x.shape, 0) & (segment - 1)
    shift = 1
    while shift < segment:
        x = x + jnp.where(row >= shift, pltpu.roll(x, shift, axis=0), 0.0)
        shift *= 2
    return x


def _split3(x):
    hi = x.astype(BF16)
    r = x - hi.astype(F32)
    mid = r.astype(BF16)
    lo = (r - mid.astype(F32)).astype(BF16)
    return hi, mid, lo


def _cumsum_rows_mxu(x, tri):
    hi, mid, lo = _split3(x)
    return _dot(tri, hi) + _dot(tri, mid) + _dot(tri, lo)


def _column_broadcast(row_vec, n):
    rows = 16
    hi, mid, lo = _split3(row_vec)
    terms = jnp.concatenate([hi, mid, lo, jnp.zeros((rows - 3, n), BF16)], axis=0)
    ones = (lax.broadcasted_iota(jnp.int32, (rows, n), 0) < 3).astype(BF16)
    return _dot_tn(terms, ones)


def _hgrn_out(o, gain, g_raw):
    return _rms(o, gain) * _silu(g_raw)


def _pair_levels(c):
    t = lax.broadcasted_iota(jnp.int32, (c, c), 0)
    s = lax.broadcasted_iota(jnp.int32, (c, c), 1)
    masks = [t == s]
    half = 1
    while half < c:
        block = 2 * half
        same_block = (t >> _log2(block)) == (s >> _log2(block))
        masks.append(same_block & ((t & (block - 1)) >= half) & ((s & (block - 1)) < half))
        half = block
    return masks


def _pair_factors(q, k, f, log2f, l2):
    c, w = q.shape
    pos = lax.broadcasted_iota(jnp.int32, (c, w), 0) & 3
    e2 = jnp.exp2(jnp.where(pos == 0, pltpu.roll(log2f, c - 1, axis=0),
                            jnp.where(pos == 1, 0.0,
                                      jnp.where(pos == 2, log2f,
                                                log2f + pltpu.roll(log2f, 1, axis=0)))))
    qs = [q, q * f, q * e2]
    ks = [k, k, k * e2]
    half = 4
    while half < c:
        block = 2 * half
        ref = l2.reshape(c // block, block, w)[:, half - 1:half, :]
        ref = jnp.broadcast_to(ref, (c // block, block, w)).reshape(c, w)
        e = jnp.exp2(-jnp.abs(l2 - ref))
        qs.append(q * e)
        ks.append(k * e)
        half = block
    return [x.astype(BF16) for x in qs], [x.astype(BF16) for x in ks]


def _hgrn_prompt_kernel(lbl_ref, q_ref, f_ref, i_ref, g_ref, gain_ref, r_ref, s_ref):
    c = HGRN_CHUNK

    @pl.when(pl.program_id(1) == 0)
    def _():
        s_ref[...] = jnp.zeros_like(s_ref)

    lb = _lower_bound(lbl_ref)
    q_all, k_all, f_all, log2f_all = _hgrn_features(q_ref[...].astype(F32), f_ref[...], lb)
    masks = _pair_levels(c)
    tri = (lax.broadcasted_iota(jnp.int32, (c, c), 1)
           <= lax.broadcasted_iota(jnp.int32, (c, c), 0)).astype(BF16)
    gain = gain_ref[...]

    chunks = []
    for ci in range(HGRN_STEP_CHUNKS):
        rows = slice(ci * c, (ci + 1) * c)
        l2_c = _cumsum_rows_mxu(log2f_all[rows], tri)
        q_lv, k_lv = _pair_factors(q_all[rows], k_all[rows], f_all[rows], log2f_all[rows], l2_c)
        l_end = l2_c[c - 1:c, :]
        q_hat = (q_all[rows] * jnp.exp2(l2_c)).astype(BF16)
        k_end = (k_all[rows] * jnp.exp2(l_end - l2_c)).astype(BF16)
        chunks.append((rows, q_lv, k_lv, q_hat, k_end, jnp.exp2(l_end)))

    for rows, q_lv, k_lv, q_hat, k_end, decay_end in chunks:
        for h in range(HGRN_HEADS):
            sl = slice(h * HGRN_DK, (h + 1) * HGRN_DK)
            vb = i_ref[rows, sl]
            a = jnp.zeros((c, c), F32)
            for mask, q_f, k_f in zip(masks, q_lv, k_lv):
                a = jnp.where(mask, _dot_nt(q_f[:, sl], k_f[:, sl]), a)
            s_old = s_ref[h]
            o = _dot(a.astype(BF16), vb) + _dot(q_hat[:, sl], s_old.astype(BF16))
            s_ref[h] = (s_old * _column_broadcast(decay_end[:, sl], HGRN_DK)
                        + _dot_tn(k_end[:, sl], vb))
            r_ref[rows, sl] = _hgrn_out(o, gain, g_ref[rows, sl].astype(F32)).astype(r_ref.dtype)


def _hgrn_prompt(zb3, zf3, lb_logits, gain):
    b, s, _ = zb3.shape
    rows = HGRN_CHUNK * HGRN_STEP_CHUNKS
    seg = lambda idx: pl.BlockSpec((None, rows, HGRN_KW), lambda bi, ci: (bi, ci, idx))
    return pl.pallas_call(
        _hgrn_prompt_kernel,
        grid=(b, s // rows),
        in_specs=[
            pl.BlockSpec(lb_logits.shape, lambda bi, ci: (0, 0)),
            seg(ZB_QH), seg(ZF_FH), seg(ZB_IH), seg(ZB_GH),
            pl.BlockSpec((1, HGRN_DV), lambda bi, ci: (0, 0)),
        ],
        out_specs=[
            pl.BlockSpec((None, rows, HGRN_VW), lambda bi, ci: (bi, ci, 0)),
            pl.BlockSpec((None, HGRN_HEADS, HGRN_DK, HGRN_DV), lambda bi, ci: (bi, 0, 0, 0)),
        ],
        out_shape=[
            jax.ShapeDtypeStruct((b, s, HGRN_VW), BF16),
            jax.ShapeDtypeStruct((b, HGRN_HEADS, HGRN_DK, HGRN_DV), F32),
        ],
        compiler_params=_params("parallel", "arbitrary"),
        name="hgrn_prompt",
    )(lb_logits, zb3, zf3, zb3, zb3, gain)


def _hgrn_sample_kernel(lbl_ref, seg_ref, segt_ref, q_ref, f_ref, i_ref, g_ref, gain_ref, s0_ref,
                        r_ref, s_ref, *, t_new):
    seqs = s0_ref.shape[0]
    n = seqs * t_new
    width = HGRN_KW
    lb = _lower_bound(lbl_ref)
    q_all, k_all, _, log2f_all = _hgrn_features(q_ref[...].astype(F32), f_ref[...], lb)
    l2_all = _cumsum_rows(log2f_all, t_new)
    v_all = i_ref[...].astype(F32)
    g_all = g_ref[...].astype(F32)
    gain = gain_ref[...]

    def row_of_seq(x, t):
        x3 = x.reshape(seqs, t_new, width)
        return jnp.broadcast_to(x3[:, t:t + 1, :], (seqs, t_new, width)).reshape(n, width)

    src = lax.broadcasted_iota(jnp.int32, (n, width), 0) & (t_new - 1)
    w_parts = []
    for t in range(t_new):
        decay = jnp.exp2(jnp.minimum(row_of_seq(l2_all, t) - l2_all, 0.0))
        w_parts.append(jnp.where(src <= t, decay * k_all * row_of_seq(q_all, t), 0.0))
    w = jnp.concatenate(w_parts, axis=0).astype(BF16)
    a_heads = _dot(w, seg_ref[...])
    a_lanes = _dot(a_heads.astype(BF16), segt_ref[...])
    o_rows = []
    for t in range(t_new):
        p = (a_lanes[t * n:(t + 1) * n] * v_all).reshape(seqs, t_new, width)
        o_rows.append(jnp.sum(p, axis=1, keepdims=True))
    o_intra = jnp.concatenate(o_rows, axis=1).reshape(n, width)

    l_end = row_of_seq(l2_all, t_new - 1)
    q_hat = q_all * jnp.exp2(l2_all)
    k_end = k_all * jnp.exp2(l_end - l2_all)
    decay_end = jnp.exp2(l_end)
    for h in range(HGRN_HEADS):
        sl = slice(h * HGRN_DK, (h + 1) * HGRN_DK)
        o_parts = []
        for sq in range(seqs):
            rows = slice(sq * t_new, (sq + 1) * t_new)
            s_old = s0_ref[sq, h]
            o_parts.append(_dot(q_hat[rows, sl].astype(BF16), s_old.astype(BF16)))
            s_ref[sq, h] = (s_old * _column_broadcast(decay_end[sq * t_new:sq * t_new + 1, sl], HGRN_DK)
                            + _dot_tn(k_end[rows, sl].astype(BF16), v_all[rows, sl].astype(BF16)))
        o = o_intra[:, sl] + jnp.concatenate(o_parts, axis=0)
        r_ref[:, sl] = _hgrn_out(o, gain, g_all[:, sl])


def _hgrn_sample(zb, zf, state, lb_logits, gain, t_new):
    b = state.shape[0]
    gs = HGRN_SAMPLE_SEQS
    seg = lambda idx: pl.BlockSpec((gs * t_new, HGRN_KW), lambda i: (i, idx))
    state_spec = pl.BlockSpec((gs, HGRN_HEADS, HGRN_DK, HGRN_DV), lambda i: (i, 0, 0, 0))
    lane_head = jnp.arange(HGRN_KW, dtype=jnp.int32) // HGRN_DK
    head_seg = (lane_head[:, None] == jnp.arange(HGRN_DK, dtype=jnp.int32)[None, :]).astype(BF16)
    return pl.pallas_call(
        functools.partial(_hgrn_sample_kernel, t_new=t_new),
        grid=(b // gs,),
        in_specs=[
            pl.BlockSpec(lb_logits.shape, lambda i: (0, 0)),
            pl.BlockSpec((HGRN_KW, HGRN_DK), lambda i: (0, 0)),
            pl.BlockSpec((HGRN_DK, HGRN_KW), lambda i: (0, 0)),
            seg(ZB_QH), seg(ZF_FH), seg(ZB_IH), seg(ZB_GH),
            pl.BlockSpec((1, HGRN_DV), lambda i: (0, 0)),
            state_spec,
        ],
        out_specs=[seg(0), state_spec],
        out_shape=[
            jax.ShapeDtypeStruct((b * t_new, HGRN_VW), F32),
            jax.ShapeDtypeStruct((b, HGRN_HEADS, HGRN_DK, HGRN_DV), F32),
        ],
        compiler_params=_params("parallel"),
        name="hgrn_sample",
    )(lb_logits, head_seg, head_seg.T, zb, zf, zb, zb, gain, state)


def _mix_out_kernel(x_ref, a_ref, r_ref, wa_ref, wr_ref, gpost_ref, gmem_ref, wq_ref,
                    x2_ref, qm_ref):
    mixed = (_dot(a_ref[...].astype(BF16), wa_ref[...])
             + _dot(r_ref[...].astype(BF16), wr_ref[...]))
    x2 = x_ref[...] + _rms(mixed, gpost_ref[...])
    x2_ref[...] = x2
    qm_ref[...] = _dot(_rms(x2, gmem_ref[...]).astype(BF16), wq_ref[...])


def _mix_out(x, a, r, w_out, g_post, g_mem, w_q):
    t = x.shape[0]
    tm = MIX_TOKEN_TILE
    tok = lambda width: pl.BlockSpec((tm, width), lambda i: (i, 0))
    const = lambda shape, idx=(0, 0): pl.BlockSpec(shape, lambda i: idx)
    return pl.pallas_call(
        _mix_out_kernel,
        grid=(t // tm,),
        in_specs=[
            tok(D_MODEL), tok(ATTN_WIDTH), tok(HGRN_VW),
            const((ATTN_WIDTH, D_MODEL), (0, 0)),
            const((HGRN_VW, D_MODEL), (1, 0)),
            const((1, D_MODEL)), const((1, D_MODEL)),
            const((D_MODEL, MEM_WIDTH)),
        ],
        out_specs=[tok(D_MODEL), tok(MEM_WIDTH)],
        out_shape=[jax.ShapeDtypeStruct((t, D_MODEL), F32),
                   jax.ShapeDtypeStruct((t, MEM_WIDTH), F32)],
        compiler_params=_params("parallel"),
        name="mix_out",
    )(x, a, r, w_out, w_out, g_post, g_mem, w_q)


def _mem_kv_kernel(m_ref, g_ref, wk_ref, wv_ref, k_ref, v_ref, kb_ref, vb_ref):
    h = _rms(m_ref[...], g_ref[...]).astype(BF16)
    k = _dot(h, wk_ref[...])
    v = _dot(h, wv_ref[...])
    k_ref[...] = k
    v_ref[...] = v
    kb_ref[...] = k.astype(BF16)
    vb_ref[...] = v.astype(BF16)


def _mem_kv(mem, g, w_k, w_v):
    b, m, _ = mem.shape
    const = lambda shape: pl.BlockSpec(shape, lambda i: (0, 0))
    out = pl.BlockSpec((None, m, MEM_WIDTH), lambda i: (i, 0, 0))
    return pl.pallas_call(
        _mem_kv_kernel,
        grid=(b,),
        in_specs=[pl.BlockSpec((None, m, D_MODEL), lambda i: (i, 0, 0)), const((1, D_MODEL)),
                  const((D_MODEL, MEM_WIDTH)), const((D_MODEL, MEM_WIDTH))],
        out_specs=[out] * 4,
        out_shape=[jax.ShapeDtypeStruct((b, m, MEM_WIDTH), F32)] * 2
        + [jax.ShapeDtypeStruct((b, m, MEM_WIDTH), BF16)] * 2,
        compiler_params=_params("parallel"),
        name="mem_kv",
    )(mem, g, w_k, w_v)


def _mem_attn_finish(o, x_ref, wo_ref, g_ref, o_ref):
    seqs, tq, _ = x_ref.shape
    y = _dot(o.reshape(seqs * tq, MEM_WIDTH).astype(BF16), wo_ref[...])
    x = x_ref[...].reshape(seqs * tq, D_MODEL)
    o_ref[...] = (x + _rms(y, g_ref[...])).reshape(seqs, tq, D_MODEL)


def _mem_attn_kernel(x_ref, q_ref, mk_ref, mv_ref, wo_ref, g_ref, o_ref):
    scale = MEM_HEAD_DIM ** -0.5
    outs = []
    for h in range(MEM_HEADS):
        sl = slice(h * MEM_HEAD_DIM, (h + 1) * MEM_HEAD_DIM)
        q = q_ref[:, :, sl].astype(BF16)
        s = jnp.einsum('gqd,gkd->gqk', q, mk_ref[:, :, sl], preferred_element_type=F32) * scale
        p = jnp.exp(s - jnp.max(s, axis=-1, keepdims=True))
        denom = jnp.sum(p, axis=-1, keepdims=True)
        outs.append(jnp.einsum('gqk,gkd->gqd', p.astype(BF16), mv_ref[:, :, sl],
                               preferred_element_type=F32) / denom)
    _mem_attn_finish(jnp.concatenate(outs, axis=2), x_ref, wo_ref, g_ref, o_ref)


def _mem_attn_interleaved_kernel(x_ref, q_ref, mk_ref, mv_ref, wo_ref, g_ref, o_ref):
    seqs, tq, _ = q_ref.shape
    rows = MEM_HEADS * tq
    cols = mk_ref.shape[1]
    scale = MEM_HEAD_DIM ** -0.5
    q = jnp.concatenate([q_ref[:, :, h * MEM_HEAD_DIM:(h + 1) * MEM_HEAD_DIM]
                         for h in range(MEM_HEADS)], axis=1).astype(BF16)
    s = jnp.einsum('gqd,gkd->gqk', q, mk_ref[...].astype(BF16), preferred_element_type=F32) * scale
    row_head = lax.broadcasted_iota(jnp.int32, (rows, cols), 0) >> _log2(tq)
    col_head = lax.broadcasted_iota(jnp.int32, (rows, cols), 1) & (MEM_HEADS - 1)
    s = jnp.where((row_head == col_head)[None], s, -jnp.inf)
    p = jnp.exp(s - jnp.max(s, axis=-1, keepdims=True))
    denom = jnp.sum(p, axis=-1, keepdims=True)
    o = jnp.einsum('gqk,gkd->gqd', p.astype(BF16), mv_ref[...].astype(BF16),
                   preferred_element_type=F32) / denom
    o = jnp.concatenate([o[:, h * tq:(h + 1) * tq, :] for h in range(MEM_HEADS)], axis=2)
    _mem_attn_finish(o, x_ref, wo_ref, g_ref, o_ref)


def _mem_attn(body, x3, q3, mem_k, mem_v, w_o, g_post, seqs, tq):
    nseq, slen, _ = x3.shape
    tok = lambda width: pl.BlockSpec((seqs, tq, width), lambda i, j: (i, j, 0))
    mem = pl.BlockSpec((seqs,) + mem_k.shape[1:], lambda i, j: (i, 0, 0))
    const = lambda shape: pl.BlockSpec(shape, lambda i, j: (0, 0))
    return pl.pallas_call(
        body,
        grid=(nseq // seqs, slen // tq),
        in_specs=[tok(D_MODEL), tok(MEM_WIDTH), mem, mem,
                  const((MEM_WIDTH, D_MODEL)), const((1, D_MODEL))],
        out_specs=tok(D_MODEL),
        out_shape=jax.ShapeDtypeStruct((nseq, slen, D_MODEL), F32),
        compiler_params=_params("parallel", "arbitrary"),
        name="mem_attn",
    )(x3, q3, mem_k, mem_v, w_o, g_post)


def _row(g):
    return g.reshape(1, -1)


def _cast_tiles_kernel(order_ref, w_ref, o_ref):
    del order_ref
    o_ref[...] = w_ref[...].astype(BF16)


def _cast_col_tiles(w, tile, order=None):
    k, n = w.shape
    nt = n // tile
    order = jnp.arange(nt, dtype=jnp.int32) if order is None else jnp.asarray(order, jnp.int32)
    return pl.pallas_call(
        _cast_tiles_kernel,
        grid_spec=pltpu.PrefetchScalarGridSpec(
            num_scalar_prefetch=1,
            grid=(nt,),
            in_specs=[pl.BlockSpec((k, tile), lambda j, order_ref: (0, order_ref[j]))],
            out_specs=pl.BlockSpec((None, k, tile), lambda j, order_ref: (j, 0, 0)),
        ),
        out_shape=jax.ShapeDtypeStruct((nt, k, tile), BF16),
        compiler_params=_params("parallel"),
        name="cast_col_tiles",
    )(order, w)


def _trunk_front(x2d, w):
    x1 = _ffn(x2d, w['ffn1_pre'], w['ffn1_post'], w['ffn1_wg'], w['ffn1_wu'], w['ffn1_wd'])
    zb, zf = _in_proj(x1, w['mix_pre'], w['w_in'])
    return x1, zb, zf


def _trunk_back(x1, a, r, mem_body, mem_k, mem_v, w, nseq, seqs, tq):
    t = x1.shape[0]
    x2, qm = _mix_out(x1, a.reshape(t, ATTN_WIDTH), r.reshape(t, HGRN_VW), w['w_out'],
                      w['mix_post'], w['mem_pre'], w['w_mem_q'])
    x3 = _mem_attn(mem_body, x2.reshape(nseq, t // nseq, D_MODEL),
                   qm.reshape(nseq, t // nseq, MEM_WIDTH),
                   mem_k, mem_v, w['w_mem_o'], w['mem_post'], seqs, tq)
    return _ffn(x3.reshape(t, D_MODEL), w['ffn2_pre'], w['ffn2_post'],
                w['ffn2_wg'], w['ffn2_wu'], w['ffn2_wd'])


def kernel(x_prompt, x_sample, mem_prompt, cache_win_k, cache_win_v, state_hgrn, cache_mem_k, cache_mem_v, ffn1_norm_pre, ffn1_norm_post, ffn1_w_gate, ffn1_w_up, ffn1_w_down, mix_norm_pre, mix_norm_post, w_in, attn_sinks, hgrn_lb_logits, attn_out_gain, hgrn_out_gain, w_out, mem_norm_pre, mem_norm_post, mem_norm_kv, w_mem_q, w_mem_k, w_mem_v, w_mem_o, ffn2_norm_pre, ffn2_norm_post, ffn2_w_gate, ffn2_w_up, ffn2_w_down):
    bp, sp, _ = x_prompt.shape
    bs, ts, _ = x_sample.shape
    mt = mem_prompt.shape[1]
    l = 0

    seg_widths = dict(qa=ATTN_WIDTH, kv=2 * KV_WIDTH, qh=HGRN_KW, fh=HGRN_KW, ih=HGRN_VW, gh=HGRN_VW)
    seg_tiles, start = {}, 0
    for name in ('qa', 'kv', 'qh', 'fh', 'ih', 'gh'):
        n_tiles = seg_widths[name] // PROJ_COL_TILE
        seg_tiles[name] = list(range(start, start + n_tiles))
        start += n_tiles
    w_in_order = sum((seg_tiles[name] for name in ('qa', 'qh', 'ih', 'gh', 'fh', 'kv')), [])
    w = dict(
        ffn1_pre=_row(ffn1_norm_pre[l]), ffn1_post=_row(ffn1_norm_post[l]),
        ffn1_wg=_cast_col_tiles(ffn1_w_gate[l], FF_TILE), ffn1_wu=_cast_col_tiles(ffn1_w_up[l], FF_TILE),
        ffn1_wd=ffn1_w_down[l].astype(BF16).reshape(D_FF // FF_TILE, FF_TILE, D_MODEL),
        mix_pre=_row(mix_norm_pre[l]), mix_post=_row(mix_norm_post[l]),
        w_in=_cast_col_tiles(w_in[l], PROJ_COL_TILE, w_in_order), w_out=w_out[l].astype(BF16),
        mem_pre=_row(mem_norm_pre[l]), mem_post=_row(mem_norm_post[l]),
        w_mem_q=w_mem_q[l].astype(BF16), w_mem_o=w_mem_o[l].astype(BF16),
        ffn2_pre=_row(ffn2_norm_pre[l]), ffn2_post=_row(ffn2_norm_post[l]),
        ffn2_wg=_cast_col_tiles(ffn2_w_gate[l], FF_TILE), ffn2_wu=_cast_col_tiles(ffn2_w_up[l], FF_TILE),
        ffn2_wd=ffn2_w_down[l].astype(BF16).reshape(D_FF // FF_TILE, FF_TILE, D_MODEL),
    )
    sinks = attn_sinks[l]
    attn_gain = _row(attn_out_gain[l])
    hgrn_gain = _row(hgrn_out_gain[l])

    mk, mv, mk_b, mv_b = _mem_kv(mem_prompt, _row(mem_norm_kv[l]),
                                 w_mem_k[l].astype(BF16), w_mem_v[l].astype(BF16))
    x1, zb, zf = _trunk_front(x_prompt.reshape(bp * sp, D_MODEL), w)
    zb3 = zb.reshape(bp, sp, ZB_WIDTH)
    zf3 = zf.reshape(bp, sp, ZF_WIDTH)
    a = _attn_prompt(zb3, zf3, sinks, attn_gain)
    r, p_state = _hgrn_prompt(zb3, zf3, hgrn_lb_logits, hgrn_gain)
    y_p = _trunk_back(x1, a, r, _mem_attn_kernel, mk_b, mv_b, w, bp, 1, 512).reshape(bp, sp, D_MODEL)
    k_off = ZF_KA * KV_WIDTH
    v_off = ZF_VA * KV_WIDTH
    p_wk = zf3[:, sp - WINDOW:, k_off:k_off + KV_WIDTH]
    p_wv = zf3[:, sp - WINDOW:, v_off:v_off + KV_WIDTH]

    x1s, zbs, zfs = _trunk_front(x_sample.reshape(bs * ts, D_MODEL), w)
    a_s, s_wk, s_wv = _attn_sample(zbs, zfs, cache_win_k[l].reshape(bs, WINDOW, KV_WIDTH),
                                   cache_win_v[l].reshape(bs, WINDOW, KV_WIDTH), sinks, attn_gain, ts)
    r_s, s_state = _hgrn_sample(zbs, zfs, state_hgrn[l], hgrn_lb_logits, hgrn_gain, ts)
    mem_rows = cache_mem_k.shape[2] * MEM_HEADS
    y_s = _trunk_back(x1s, a_s, r_s, _mem_attn_interleaved_kernel,
                      cache_mem_k[l].reshape(bs, mem_rows, MEM_HEAD_DIM),
                      cache_mem_v[l].reshape(bs, mem_rows, MEM_HEAD_DIM),
                      w, bs, 8, ts).reshape(bs, ts, D_MODEL)

    kv5 = lambda t, n: t.reshape(1, n, WINDOW, ATTN_KV_HEADS, ATTN_HEAD_DIM)
    mem5 = lambda t: t.reshape(1, bp, mt, MEM_HEADS, MEM_HEAD_DIM)
    return (y_p, y_s, kv5(p_wk, bp), kv5(p_wv, bp), p_state[None], mem5(mk), mem5(mv),
            kv5(s_wk, bs), kv5(s_wv, bs), s_state[None])
```

```python
import functools

import jax
import jax.numpy as jnp
from jax import lax
from jax.experimental import pallas as pl
from jax.experimental.pallas import tpu as pltpu

F32 = jnp.float32
BF16 = jnp.bfloat16

D_MODEL = 2048
D_FF = 5632
ATTN_HEADS = 16
ATTN_KV_HEADS = 4
ATTN_GROUP = ATTN_HEADS // ATTN_KV_HEADS
ATTN_HEAD_DIM = 64
WINDOW = 128
ATTN_WIDTH = ATTN_HEADS * ATTN_HEAD_DIM
KV_WIDTH = ATTN_KV_HEADS * ATTN_HEAD_DIM
HGRN_HEADS = 8
HGRN_DK = 128
HGRN_DV = 128
HGRN_KW = HGRN_HEADS * HGRN_DK
HGRN_VW = HGRN_HEADS * HGRN_DV
IN_PROJ_WIDTH = ATTN_WIDTH + 2 * KV_WIDTH + 2 * HGRN_KW + 2 * HGRN_VW
MEM_HEADS = 4
MEM_HEAD_DIM = 128
MEM_WIDTH = MEM_HEADS * MEM_HEAD_DIM
FFN_RESIDUAL = 0.5
EPS = 1e-6

ZB_WIDTH = ATTN_WIDTH + HGRN_KW + 2 * HGRN_VW
ZF_WIDTH = HGRN_KW + 2 * KV_WIDTH
ZB_QA, ZB_QH, ZB_IH, ZB_GH = 0, 1, 2, 3
ZF_FH = 0
ZF_KA, ZF_VA = 4, 5

VMEM_LIMIT_BYTES = 56 * 1024 * 1024

FFN_TOKEN_TILE = 512
PROJ_TOKEN_TILE = 512
MIX_TOKEN_TILE = 512
FF_TILE = 512
RING_SLOTS = 4
RING_AHEAD = 2
CAST_AT_STEP = 3
PROJ_COL_TILE = 512
HGRN_CHUNK = 128
HGRN_STEP_CHUNKS = 2
HGRN_SAMPLE_SEQS = 4
ATTN_SAMPLE_SEQS = 8
ATTN_STEP_BLOCKS = 4
MASKED = -1e30

ALIBI_SLOPES = tuple(2.0 ** (-8.0 * (h + 1) / ATTN_HEADS) for h in range(ATTN_HEADS))


def _params(*semantics):
    return pltpu.CompilerParams(dimension_semantics=semantics,
                                vmem_limit_bytes=VMEM_LIMIT_BYTES)


def _rms(x, g):
    return x * lax.rsqrt(jnp.mean(x * x, axis=-1, keepdims=True) + EPS) * g


def _silu(x):
    return x * jax.nn.sigmoid(x)


def _dot(a, b):
    return jnp.dot(a, b, preferred_element_type=F32)


def _dot_nt(a, b):
    return lax.dot_general(a, b, (((1,), (1,)), ((), ())), preferred_element_type=F32)


def _dot_tn(a, b):
    return lax.dot_general(a, b, (((0,), (0,)), ((), ())), preferred_element_type=F32)


def _log2(n):
    assert n & (n - 1) == 0
    return n.bit_length() - 1


def _stream_weight_tiles(n, token_tiles, copies, step):
    i = pl.program_id(0)
    slots, ahead = RING_SLOTS, RING_AHEAD
    assert ahead < slots <= n

    def start(tile):
        for c in copies(tile, tile % slots):
            c.start()

    @pl.when(i == 0)
    def _():
        for tile in range(ahead):
            start(tile)

    for j in range(n):
        nxt = j + ahead
        if nxt < n:
            assert nxt % slots not in {jj % slots for jj in range(j, nxt)}
            start(nxt)
        else:
            assert (nxt - n) % slots not in {jj % slots for jj in range(j, n)}
            pl.when(i + 1 < token_tiles)(functools.partial(start, nxt - n))
        for c in copies(j, j % slots):
            c.wait()
        step(j, j % slots)


def _ffn_kernel(*refs, token_tiles, n_casts):
    m = n_casts
    x_ref, gpre_ref, gpost_ref, wg_hbm, wu_hbm, wd_hbm = refs[:6]
    cast_src = refs[6:6 + m]
    o_ref = refs[6 + m]
    cast_dst = refs[7 + m:7 + 2 * m]
    wg_buf, wu_buf, wd_buf, sem = refs[7 + 2 * m:11 + 2 * m]
    cast_bufs = refs[11 + 2 * m:11 + 4 * m]
    cast_sem = refs[11 + 4 * m] if m else None
    i = pl.program_id(0)
    tf = wg_buf.shape[2]

    def column_tile(w_hbm, tile):
        return w_hbm.at[tile] if len(w_hbm.shape) == 3 else w_hbm.at[:, pl.ds(tile * tf, tf)]

    def copies(tile, slot):
        return (pltpu.make_async_copy(column_tile(wg_hbm, tile), wg_buf.at[slot], sem.at[0, slot]),
                pltpu.make_async_copy(column_tile(wu_hbm, tile), wu_buf.at[slot], sem.at[1, slot]),
                pltpu.make_async_copy(wd_hbm.at[tile], wd_buf.at[slot], sem.at[2, slot]))

    def cast_in(k):
        buf = cast_bufs[2 * k]
        rows = pl.ds(pl.multiple_of(i * buf.shape[0], 16), buf.shape[0])
        return pltpu.make_async_copy(cast_src[k].at[rows, :], buf, cast_sem.at[k, 0])

    def cast_out(k):
        buf = cast_bufs[2 * k + 1]
        rows = pl.ds(pl.multiple_of(i * buf.shape[0], 16), buf.shape[0])
        return pltpu.make_async_copy(buf, cast_dst[k].at[rows, :], cast_sem.at[k, 1])

    for k in range(m):
        cast_in(k).start()

    h = _rms(x_ref[...], gpre_ref[...]).astype(BF16)

    def step(j, slot):
        k = j - CAST_AT_STEP
        if 0 <= k < m:
            cast_in(k).wait()
            cast_bufs[2 * k + 1][...] = cast_bufs[2 * k][...].astype(BF16)
            cast_out(k).start()
        g = _dot(h, wg_buf[slot])
        u = _dot(h, wu_buf[slot])
        part = _dot((_silu(g) * u).astype(BF16), wd_buf[slot])
        if j == 0:
            o_ref[...] = part
        else:
            o_ref[...] += part

    _stream_weight_tiles(wd_hbm.shape[0], token_tiles, copies, step)
    for k in range(m):
        cast_out(k).wait()
    o_ref[...] = x_ref[...] + _rms(o_ref[...], FFN_RESIDUAL * gpost_ref[...])


def _ffn(x, g_pre, g_post, wg, wu, wd, side_casts=()):
    t = x.shape[0]
    tm = FFN_TOKEN_TILE
    tiles = t // tm
    n, tf, _ = wd.shape
    m = len(side_casts)
    assert n >= CAST_AT_STEP + m
    any_space = pl.BlockSpec(memory_space=pl.ANY)
    cast_shapes, cast_scratch = [], []
    for w in side_casts:
        r, c = w.shape
        rows = r // tiles
        assert rows * tiles == r and rows % 16 == 0
        cast_shapes.append(jax.ShapeDtypeStruct((r, c), BF16))
        cast_scratch += [pltpu.VMEM((rows, c), F32), pltpu.VMEM((rows, c), BF16)]
    if m:
        cast_scratch.append(pltpu.SemaphoreType.DMA((m, 2)))
    outs = pl.pallas_call(
        functools.partial(_ffn_kernel, token_tiles=tiles, n_casts=m),
        grid=(tiles,),
        in_specs=[
            pl.BlockSpec((tm, D_MODEL), lambda i: (i, 0)),
            pl.BlockSpec((1, D_MODEL), lambda i: (0, 0)),
            pl.BlockSpec((1, D_MODEL), lambda i: (0, 0)),
            any_space, any_space, any_space,
        ] + [any_space] * m,
        out_specs=[pl.BlockSpec((tm, D_MODEL), lambda i: (i, 0))] + [any_space] * m,
        out_shape=[jax.ShapeDtypeStruct((t, D_MODEL), F32)] + cast_shapes,
        scratch_shapes=[
            pltpu.VMEM((RING_SLOTS, D_MODEL, tf), BF16),
            pltpu.VMEM((RING_SLOTS, D_MODEL, tf), BF16),
            pltpu.VMEM((RING_SLOTS, tf, D_MODEL), BF16),
            pltpu.SemaphoreType.DMA((3, RING_SLOTS)),
        ] + cast_scratch,
        compiler_params=_params("arbitrary"),
        name="ffn_half",
    )(x, g_pre, g_post, wg, wu, wd, *side_casts)
    return tuple(outs)


def _in_proj_kernel(x_ref, g_ref, w_hbm, zb_ref, zf_ref, w_buf, sem, *, token_tiles):
    tn = w_hbm.shape[2]
    nb = zb_ref.shape[1] // tn

    def copies(tile, slot):
        return (pltpu.make_async_copy(w_hbm.at[tile], w_buf.at[slot], sem.at[slot]),)

    h = _rms(x_ref[...], g_ref[...]).astype(BF16)

    def step(j, slot):
        z = _dot(h, w_buf[slot])
        if j < nb:
            zb_ref[:, j * tn:(j + 1) * tn] = z.astype(BF16)
        else:
            zf_ref[:, (j - nb) * tn:(j - nb + 1) * tn] = z

    _stream_weight_tiles(w_hbm.shape[0], token_tiles, copies, step)


def _in_proj(x, g, w):
    t = x.shape[0]
    tm = PROJ_TOKEN_TILE
    n, _, tn = w.shape
    return pl.pallas_call(
        functools.partial(_in_proj_kernel, token_tiles=t // tm),
        grid=(t // tm,),
        in_specs=[
            pl.BlockSpec((tm, D_MODEL), lambda i: (i, 0)),
            pl.BlockSpec((1, D_MODEL), lambda i: (0, 0)),
            pl.BlockSpec(memory_space=pl.ANY),
        ],
        out_specs=[
            pl.BlockSpec((tm, ZB_WIDTH), lambda i: (i, 0)),
            pl.BlockSpec((tm, ZF_WIDTH), lambda i: (i, 0)),
        ],
        out_shape=[jax.ShapeDtypeStruct((t, ZB_WIDTH), BF16),
                   jax.ShapeDtypeStruct((t, ZF_WIDTH), F32)],
        scratch_shapes=[pltpu.VMEM((RING_SLOTS, D_MODEL, tn), BF16),
                        pltpu.SemaphoreType.DMA((RING_SLOTS,))],
        compiler_params=_params("arbitrary"),
        name="in_proj",
    )(x, g, w)


def _attn_prompt_kernel(sink_ref, q_ref, kc_ref, kp_ref, vc_ref, vp_ref, gain_ref, o_ref):
    n = pl.program_id(1)
    nblk = q_ref.shape[0] // WINDOW
    scale = ATTN_HEAD_DIM ** -0.5
    row = lax.broadcasted_iota(jnp.int32, (WINDOW, 2 * WINDOW), 0)
    col = lax.broadcasted_iota(jnp.int32, (WINDOW, 2 * WINDOW), 1)
    dist = row + WINDOW - col
    in_window = (dist >= 0) & (dist < WINDOW)
    gain = gain_ref[...]

    def block(qrows, k, v, valid):
        k = (k * scale).astype(BF16)
        v = v.astype(BF16)
        neg_dist = jnp.where(valid, -dist.astype(F32), MASKED)
        outs = []
        for hd in range(ATTN_HEADS):
            kvh = hd // ATTN_GROUP
            kh = k[:, kvh * ATTN_HEAD_DIM:(kvh + 1) * ATTN_HEAD_DIM]
            vh = v[:, kvh * ATTN_HEAD_DIM:(kvh + 1) * ATTN_HEAD_DIM]
            qh = q_ref[qrows, hd * ATTN_HEAD_DIM:(hd + 1) * ATTN_HEAD_DIM]
            s = _dot_nt(qh, kh) + ALIBI_SLOPES[hd] * neg_dist
            sink = sink_ref[hd]
            m = jnp.maximum(jnp.max(s, axis=-1, keepdims=True), sink)
            p = jnp.exp(s - m)
            denom = jnp.sum(p, axis=-1, keepdims=True) + jnp.exp(sink - m)
            outs.append(_dot(p.astype(BF16), vh) / denom)
        o = jnp.concatenate(outs, axis=1)
        o_ref[qrows, :] = _rms(o, gain).astype(BF16)

    block(slice(0, WINDOW),
          jnp.concatenate([kp_ref[...], kc_ref[0:WINDOW, :]], axis=0),
          jnp.concatenate([vp_ref[...], vc_ref[0:WINDOW, :]], axis=0),
          in_window & ((col >= WINDOW) | (n > 0)))

    def later_block(blk, carry):
        first = pl.multiple_of((blk - 1) * WINDOW, WINDOW)
        kv_rows = pl.ds(first, 2 * WINDOW)
        block(pl.ds(first + WINDOW, WINDOW), kc_ref[kv_rows, :], vc_ref[kv_rows, :], in_window)
        return carry

    lax.fori_loop(1, nblk, later_block, 0)


def _attn_prompt(zb3, zf3, sinks, gain):
    b, s, _ = zb3.shape
    nblk = ATTN_STEP_BLOCKS
    rows = nblk * WINDOW

    def cur(width_idx):
        return lambda bi, n: (bi, n, width_idx)

    def prev(width_idx):
        return lambda bi, n: (bi, jnp.maximum(n * nblk - 1, 0), width_idx)

    return pl.pallas_call(
        _attn_prompt_kernel,
        grid=(b, s // rows),
        in_specs=[
            pl.BlockSpec(memory_space=pltpu.SMEM),
            pl.BlockSpec((None, rows, ATTN_WIDTH), cur(ZB_QA)),
            pl.BlockSpec((None, rows, KV_WIDTH), cur(ZF_KA)),
            pl.BlockSpec((None, WINDOW, KV_WIDTH), prev(ZF_KA)),
            pl.BlockSpec((None, rows, KV_WIDTH), cur(ZF_VA)),
            pl.BlockSpec((None, WINDOW, KV_WIDTH), prev(ZF_VA)),
            pl.BlockSpec((1, ATTN_WIDTH), lambda bi, n: (0, 0)),
        ],
        out_specs=pl.BlockSpec((None, rows, ATTN_WIDTH), lambda bi, n: (bi, n, 0)),
        out_shape=jax.ShapeDtypeStruct((b, s, ATTN_WIDTH), BF16),
        compiler_params=_params("parallel", "arbitrary"),
        name="attn_prompt",
    )(sinks, zb3, zf3, zf3, zf3, zf3, gain)


def _attn_sample_kernel(sink_ref, q_ref, kn_ref, vn_ref, ck_ref, cv_ref, gain_ref,
                        o_ref, wk_ref, wv_ref, *, t_new):
    seqs = ck_ref.shape[0]
    t_bits = _log2(t_new)
    rows = ATTN_GROUP * t_new
    q_all = q_ref[...].astype(F32).reshape(seqs, t_new, ATTN_WIDTH)
    kn = kn_ref[...].reshape(seqs, t_new, KV_WIDTH)
    vn = vn_ref[...].reshape(seqs, t_new, KV_WIDTH)
    ck = ck_ref[...]
    cv = cv_ref[...]
    wk_ref[:, :WINDOW - t_new, :] = ck[:, t_new:, :]
    wk_ref[:, WINDOW - t_new:, :] = kn
    wv_ref[:, :WINDOW - t_new, :] = cv[:, t_new:, :]
    wv_ref[:, WINDOW - t_new:, :] = vn

    r_c = lax.broadcasted_iota(jnp.int32, (rows, WINDOW), 0)
    j_c = lax.broadcasted_iota(jnp.int32, (rows, WINDOW), 1)
    dist_c = (r_c & (t_new - 1)) + WINDOW - j_c
    neg_c = jnp.where(dist_c < WINDOW, -dist_c.astype(F32), MASKED)
    r_n = lax.broadcasted_iota(jnp.int32, (rows, t_new), 0)
    j_n = lax.broadcasted_iota(jnp.int32, (rows, t_new), 1)
    dist_n = (r_n & (t_new - 1)) - j_n
    neg_n = jnp.where(dist_n >= 0, -dist_n.astype(F32), MASKED)
    head_of_row = lax.broadcasted_iota(jnp.int32, (rows, 1), 0) >> t_bits
    scale = ATTN_HEAD_DIM ** -0.5

    outs = [None] * ATTN_HEADS
    for kvh in range(ATTN_KV_HEADS):
        slope = jnp.zeros((rows, 1), F32)
        sink = jnp.zeros((rows, 1), F32)
        for g in range(ATTN_GROUP):
            hd = kvh * ATTN_GROUP + g
            slope = jnp.where(head_of_row == g, ALIBI_SLOPES[hd], slope)
            sink = jnp.where(head_of_row == g, sink_ref[hd], sink)
        q = jnp.concatenate(
            [q_all[:, :, (kvh * ATTN_GROUP + g) * ATTN_HEAD_DIM:(kvh * ATTN_GROUP + g + 1) * ATTN_HEAD_DIM]
             for g in range(ATTN_GROUP)], axis=1).astype(BF16)
        sl = slice(kvh * ATTN_HEAD_DIM, (kvh + 1) * ATTN_HEAD_DIM)
        ck_h = (ck[:, :, sl] * scale).astype(BF16)
        kn_h = (kn[:, :, sl] * scale).astype(BF16)
        cv_h = cv[:, :, sl].astype(BF16)
        vn_h = vn[:, :, sl].astype(BF16)
        s_c = jnp.einsum('gqd,gkd->gqk', q, ck_h, preferred_element_type=F32) + (slope * neg_c)[None]
        s_n = jnp.einsum('gqd,gkd->gqk', q, kn_h, preferred_element_type=F32) + (slope * neg_n)[None]
        m = jnp.maximum(jnp.maximum(jnp.max(s_c, axis=-1, keepdims=True),
                                    jnp.max(s_n, axis=-1, keepdims=True)), sink[None])
        p_c = jnp.exp(s_c - m)
        p_n = jnp.exp(s_n - m)
        denom = (jnp.sum(p_c, axis=-1, keepdims=True) + jnp.sum(p_n, axis=-1, keepdims=True)
                 + jnp.exp(sink[None] - m))
        o = (jnp.einsum('gqk,gkd->gqd', p_c.astype(BF16), cv_h, preferred_element_type=F32)
             + jnp.einsum('gqk,gkd->gqd', p_n.astype(BF16), vn_h, preferred_element_type=F32)) / denom
        for g in range(ATTN_GROUP):
            outs[kvh * ATTN_GROUP + g] = o[:, g * t_new:(g + 1) * t_new, :]
    o_all = jnp.concatenate(outs, axis=2)
    o_ref[...] = _rms(o_all, gain_ref[...][None]).reshape(seqs * t_new, ATTN_WIDTH)


def _attn_sample(zb, zf, cache_k, cache_v, sinks, gain, t_new):
    b = cache_k.shape[0]
    gs = ATTN_SAMPLE_SEQS
    tok = lambda width, idx: pl.BlockSpec((gs * t_new, width), lambda i: (i, idx))
    cache_spec = pl.BlockSpec((gs, WINDOW, KV_WIDTH), lambda i: (i, 0, 0))
    return pl.pallas_call(
        functools.partial(_attn_sample_kernel, t_new=t_new),
        grid=(b // gs,),
        in_specs=[
            pl.BlockSpec(memory_space=pltpu.SMEM),
            tok(ATTN_WIDTH, ZB_QA), tok(KV_WIDTH, ZF_KA), tok(KV_WIDTH, ZF_VA),
            cache_spec, cache_spec,
            pl.BlockSpec((1, ATTN_WIDTH), lambda i: (0, 0)),
        ],
        out_specs=[tok(ATTN_WIDTH, 0), cache_spec, cache_spec],
        out_shape=[
            jax.ShapeDtypeStruct((b * t_new, ATTN_WIDTH), F32),
            jax.ShapeDtypeStruct((b, WINDOW, KV_WIDTH), F32),
            jax.ShapeDtypeStruct((b, WINDOW, KV_WIDTH), F32),
        ],
        compiler_params=_params("parallel"),
        name="attn_sample",
    )(sinks, zb, zf, zf, cache_k, cache_v, gain)


def _lower_bound(lbl_ref):
    logits = lbl_ref[...]
    e = jnp.exp(logits - jnp.max(logits, axis=0, keepdims=True))
    return e[0:1, :] / jnp.sum(e, axis=0, keepdims=True)


def _hgrn_features(q_raw, f_raw, lb):
    sg = jax.nn.sigmoid(f_raw)
    f = lb + (1.0 - lb) * sg
    k = (1.0 - lb) * (1.0 - sg)
    return _silu(q_raw), k, f, jnp.log2(f)


def _cumsum_rows(x, segment):
    row = lax.broadcasted_iota(jnp.int32, x.shape, 0) & (segment - 1)
    shift = 1
    while shift < segment:
        x = x + jnp.where(row >= shift, pltpu.roll(x, shift, axis=0), 0.0)
        shift *= 2
    return x


def _split3(x):
    hi = x.astype(BF16)
    r = x - hi.astype(F32)
    mid = r.astype(BF16)
    lo = (r - mid.astype(F32)).astype(BF16)
    return hi, mid, lo


def _cumsum_rows_mxu(x, tri):
    hi, mid, lo = _split3(x)
    return _dot(tri, hi) + _dot(tri, mid) + _dot(tri, lo)


def _column_broadcast(row_vec, n):
    rows = 16
    hi, mid, lo = _split3(row_vec)
    terms = jnp.concatenate([hi, mid, lo, jnp.zeros((rows - 3, n), BF16)], axis=0)
    ones = (lax.broadcasted_iota(jnp.int32, (rows, n), 0) < 3).astype(BF16)
    return _dot_tn(terms, ones)


def _hgrn_out(o, gain, g_raw):
    return _rms(o, gain) * _silu(g_raw)


def _pair_levels(c):
    t = lax.broadcasted_iota(jnp.int32, (c, c), 0)
    s = lax.broadcasted_iota(jnp.int32, (c, c), 1)
    masks = [t == s]
    half = 1
    while half < c:
        block = 2 * half
        same_block = (t >> _log2(block)) == (s >> _log2(block))
        masks.append(same_block & ((t & (block - 1)) >= half) & ((s & (block - 1)) < half))
        half = block
    return masks


def _pair_factors(q, k, f, log2f, l2):
    c, w = q.shape
    pos = lax.broadcasted_iota(jnp.int32, (c, w), 0) & 3
    e2 = jnp.exp2(jnp.where(pos == 0, pltpu.roll(log2f, c - 1, axis=0),
                            jnp.where(pos == 1, 0.0,
                                      jnp.where(pos == 2, log2f,
                                                log2f + pltpu.roll(log2f, 1, axis=0)))))
    qs = [q, q * f, q * e2]
    ks = [k, k, k * e2]
    half = 4
    while half < c:
        block = 2 * half
        ref = l2.reshape(c // block, block, w)[:, half - 1:half, :]
        ref = jnp.broadcast_to(ref, (c // block, block, w)).reshape(c, w)
        e = jnp.exp2(-jnp.abs(l2 - ref))
        qs.append(q * e)
        ks.append(k * e)
        half = block
    return [x.astype(BF16) for x in qs], [x.astype(BF16) for x in ks]


def _hgrn_prompt_kernel(lbl_ref, q_ref, f_ref, i_ref, g_ref, gain_ref, r_ref, s_ref):
    c = HGRN_CHUNK

    @pl.when(pl.program_id(1) == 0)
    def _():
        s_ref[...] = jnp.zeros_like(s_ref)

    lb = _lower_bound(lbl_ref)
    q_all, k_all, f_all, log2f_all = _hgrn_features(q_ref[...].astype(F32), f_ref[...], lb)
    masks = _pair_levels(c)
    tri = (lax.broadcasted_iota(jnp.int32, (c, c), 1)
           <= lax.broadcasted_iota(jnp.int32, (c, c), 0)).astype(BF16)
    gain = gain_ref[...]

    chunks = []
    for ci in range(HGRN_STEP_CHUNKS):
        rows = slice(ci * c, (ci + 1) * c)
        l2_c = _cumsum_rows_mxu(log2f_all[rows], tri)
        q_lv, k_lv = _pair_factors(q_all[rows], k_all[rows], f_all[rows], log2f_all[rows], l2_c)
        l_end = l2_c[c - 1:c, :]
        q_hat = (q_all[rows] * jnp.exp2(l2_c)).astype(BF16)
        k_end = (k_all[rows] * jnp.exp2(l_end - l2_c)).astype(BF16)
        chunks.append((rows, q_lv, k_lv, q_hat, k_end, jnp.exp2(l_end)))

    for rows, q_lv, k_lv, q_hat, k_end, decay_end in chunks:
        for h in range(HGRN_HEADS):
            sl = slice(h * HGRN_DK, (h + 1) * HGRN_DK)
            vb = i_ref[rows, sl]
            a = jnp.zeros((c, c), F32)
            for mask, q_f, k_f in zip(masks, q_lv, k_lv):
                a = jnp.where(mask, _dot_nt(q_f[:, sl], k_f[:, sl]), a)
            s_old = s_ref[h]
            o = _dot(a.astype(BF16), vb) + _dot(q_hat[:, sl], s_old.astype(BF16))
            s_ref[h] = (s_old * _column_broadcast(decay_end[:, sl], HGRN_DK)
                        + _dot_tn(k_end[:, sl], vb))
            r_ref[rows, sl] = _hgrn_out(o, gain, g_ref[rows, sl].astype(F32)).astype(r_ref.dtype)


def _hgrn_prompt(zb3, zf3, lb_logits, gain):
    b, s, _ = zb3.shape
    rows = HGRN_CHUNK * HGRN_STEP_CHUNKS
    seg = lambda idx: pl.BlockSpec((None, rows, HGRN_KW), lambda bi, ci: (bi, ci, idx))
    return pl.pallas_call(
        _hgrn_prompt_kernel,
        grid=(b, s // rows),
        in_specs=[
            pl.BlockSpec(lb_logits.shape, lambda bi, ci: (0, 0)),
            seg(ZB_QH), seg(ZF_FH), seg(ZB_IH), seg(ZB_GH),
            pl.BlockSpec((1, HGRN_DV), lambda bi, ci: (0, 0)),
        ],
        out_specs=[
            pl.BlockSpec((None, rows, HGRN_VW), lambda bi, ci: (bi, ci, 0)),
            pl.BlockSpec((None, HGRN_HEADS, HGRN_DK, HGRN_DV), lambda bi, ci: (bi, 0, 0, 0)),
        ],
        out_shape=[
            jax.ShapeDtypeStruct((b, s, HGRN_VW), BF16),
            jax.ShapeDtypeStruct((b, HGRN_HEADS, HGRN_DK, HGRN_DV), F32),
        ],
        compiler_params=_params("parallel", "arbitrary"),
        name="hgrn_prompt",
    )(lb_logits, zb3, zf3, zb3, zb3, gain)


def _hgrn_sample_kernel(lbl_ref, seg_ref, segt_ref, q_ref, f_ref, i_ref, g_ref, gain_ref, s0_ref,
                        r_ref, s_ref, *, t_new):
    seqs = s0_ref.shape[0]
    n = seqs * t_new
    width = HGRN_KW
    lb = _lower_bound(lbl_ref)
    q_all, k_all, _, log2f_all = _hgrn_features(q_ref[...].astype(F32), f_ref[...], lb)
    l2_all = _cumsum_rows(log2f_all, t_new)
    v_all = i_ref[...].astype(F32)
    g_all = g_ref[...].astype(F32)
    gain = gain_ref[...]

    def row_of_seq(x, t):
        x3 = x.reshape(seqs, t_new, width)
        return jnp.broadcast_to(x3[:, t:t + 1, :], (seqs, t_new, width)).reshape(n, width)

    src = lax.broadcasted_iota(jnp.int32, (n, width), 0) & (t_new - 1)
    w_parts = []
    for t in range(t_new):
        decay = jnp.exp2(jnp.minimum(row_of_seq(l2_all, t) - l2_all, 0.0))
        w_parts.append(jnp.where(src <= t, decay * k_all * row_of_seq(q_all, t), 0.0))
    w = jnp.concatenate(w_parts, axis=0).astype(BF16)
    a_heads = _dot(w, seg_ref[...])
    a_lanes = _dot(a_heads.astype(BF16), segt_ref[...])
    o_rows = []
    for t in range(t_new):
        p = (a_lanes[t * n:(t + 1) * n] * v_all).reshape(seqs, t_new, width)
        o_rows.append(jnp.sum(p, axis=1, keepdims=True))
    o_intra = jnp.concatenate(o_rows, axis=1).reshape(n, width)

    l_end = row_of_seq(l2_all, t_new - 1)
    q_hat = q_all * jnp.exp2(l2_all)
    k_end = k_all * jnp.exp2(l_end - l2_all)
    decay_end = jnp.exp2(l_end)
    for h in range(HGRN_HEADS):
        sl = slice(h * HGRN_DK, (h + 1) * HGRN_DK)
        o_parts = []
        for sq in range(seqs):
            rows = slice(sq * t_new, (sq + 1) * t_new)
            s_old = s0_ref[sq, h]
            o_parts.append(_dot(q_hat[rows, sl].astype(BF16), s_old.astype(BF16)))
            s_ref[sq, h] = (s_old * _column_broadcast(decay_end[sq * t_new:sq * t_new + 1, sl], HGRN_DK)
                            + _dot_tn(k_end[rows, sl].astype(BF16), v_all[rows, sl].astype(BF16)))
        o = o_intra[:, sl] + jnp.concatenate(o_parts, axis=0)
        r_ref[:, sl] = _hgrn_out(o, gain, g_all[:, sl])


def _hgrn_sample(zb, zf, state, lb_logits, gain, t_new):
    b = state.shape[0]
    gs = HGRN_SAMPLE_SEQS
    seg = lambda idx: pl.BlockSpec((gs * t_new, HGRN_KW), lambda i: (i, idx))
    state_spec = pl.BlockSpec((gs, HGRN_HEADS, HGRN_DK, HGRN_DV), lambda i: (i, 0, 0, 0))
    lane_head = jnp.arange(HGRN_KW, dtype=jnp.int32) // HGRN_DK
    head_seg = (lane_head[:, None] == jnp.arange(HGRN_DK, dtype=jnp.int32)[None, :]).astype(BF16)
    return pl.pallas_call(
        functools.partial(_hgrn_sample_kernel, t_new=t_new),
        grid=(b // gs,),
        in_specs=[
            pl.BlockSpec(lb_logits.shape, lambda i: (0, 0)),
            pl.BlockSpec((HGRN_KW, HGRN_DK), lambda i: (0, 0)),
            pl.BlockSpec((HGRN_DK, HGRN_KW), lambda i: (0, 0)),
            seg(ZB_QH), seg(ZF_FH), seg(ZB_IH), seg(ZB_GH),
            pl.BlockSpec((1, HGRN_DV), lambda i: (0, 0)),
            state_spec,
        ],
        out_specs=[seg(0), state_spec],
        out_shape=[
            jax.ShapeDtypeStruct((b * t_new, HGRN_VW), F32),
            jax.ShapeDtypeStruct((b, HGRN_HEADS, HGRN_DK, HGRN_DV), F32),
        ],
        compiler_params=_params("parallel"),
        name="hgrn_sample",
    )(lb_logits, head_seg, head_seg.T, zb, zf, zb, zb, gain, state)


def _mix_out_kernel(x_ref, a_ref, r_ref, wa_ref, wr_ref, gpost_ref, gmem_ref, wq_ref,
                    x2_ref, qm_ref):
    mixed = (_dot(a_ref[...].astype(BF16), wa_ref[...])
             + _dot(r_ref[...].astype(BF16), wr_ref[...]))
    x2 = x_ref[...] + _rms(mixed, gpost_ref[...])
    x2_ref[...] = x2
    qm_ref[...] = _dot(_rms(x2, gmem_ref[...]).astype(BF16), wq_ref[...])


def _mix_out(x, a, r, w_out, g_post, g_mem, w_q):
    t = x.shape[0]
    tm = MIX_TOKEN_TILE
    tok = lambda width: pl.BlockSpec((tm, width), lambda i: (i, 0))
    const = lambda shape, idx=(0, 0): pl.BlockSpec(shape, lambda i: idx)
    return pl.pallas_call(
        _mix_out_kernel,
        grid=(t // tm,),
        in_specs=[
            tok(D_MODEL), tok(ATTN_WIDTH), tok(HGRN_VW),
            const((ATTN_WIDTH, D_MODEL), (0, 0)),
            const((HGRN_VW, D_MODEL), (1, 0)),
            const((1, D_MODEL)), const((1, D_MODEL)),
            const((D_MODEL, MEM_WIDTH)),
        ],
        out_specs=[tok(D_MODEL), tok(MEM_WIDTH)],
        out_shape=[jax.ShapeDtypeStruct((t, D_MODEL), F32),
                   jax.ShapeDtypeStruct((t, MEM_WIDTH), F32)],
        compiler_params=_params("parallel"),
        name="mix_out",
    )(x, a, r, w_out, w_out, g_post, g_mem, w_q)


def _mem_kv_kernel(m_ref, g_ref, wk_ref, wv_ref, k_ref, v_ref, kb_ref, vb_ref):
    h = _rms(m_ref[...], g_ref[...]).astype(BF16)
    k = _dot(h, wk_ref[...])
    v = _dot(h, wv_ref[...])
    k_ref[...] = k
    v_ref[...] = v
    kb_ref[...] = k.astype(BF16)
    vb_ref[...] = v.astype(BF16)


def _mem_kv(mem, g, w_k, w_v):
    b, m, _ = mem.shape
    const = lambda shape: pl.BlockSpec(shape, lambda i: (0, 0))
    out = pl.BlockSpec((None, m, MEM_WIDTH), lambda i: (i, 0, 0))
    return pl.pallas_call(
        _mem_kv_kernel,
        grid=(b,),
        in_specs=[pl.BlockSpec((None, m, D_MODEL), lambda i: (i, 0, 0)), const((1, D_MODEL)),
                  const((D_MODEL, MEM_WIDTH)), const((D_MODEL, MEM_WIDTH))],
        out_specs=[out] * 4,
        out_shape=[jax.ShapeDtypeStruct((b, m, MEM_WIDTH), F32)] * 2
        + [jax.ShapeDtypeStruct((b, m, MEM_WIDTH), BF16)] * 2,
        compiler_params=_params("parallel"),
        name="mem_kv",
    )(mem, g, w_k, w_v)


def _mem_attn_finish(o, x_ref, wo_ref, g_ref, o_ref):
    seqs, tq, _ = x_ref.shape
    y = _dot(o.reshape(seqs * tq, MEM_WIDTH).astype(BF16), wo_ref[...])
    x = x_ref[...].reshape(seqs * tq, D_MODEL)
    o_ref[...] = (x + _rms(y, g_ref[...])).reshape(seqs, tq, D_MODEL)


def _mem_attn_kernel(x_ref, q_ref, mk_ref, mv_ref, wo_ref, g_ref, o_ref):
    scale = MEM_HEAD_DIM ** -0.5
    outs = []
    for h in range(MEM_HEADS):
        sl = slice(h * MEM_HEAD_DIM, (h + 1) * MEM_HEAD_DIM)
        q = q_ref[:, :, sl].astype(BF16)
        s = jnp.einsum('gqd,gkd->gqk', q, mk_ref[:, :, sl], preferred_element_type=F32) * scale
        p = jnp.exp(s - jnp.max(s, axis=-1, keepdims=True))
        denom = jnp.sum(p, axis=-1, keepdims=True)
        outs.append(jnp.einsum('gqk,gkd->gqd', p.astype(BF16), mv_ref[:, :, sl],
                               preferred_element_type=F32) / denom)
    _mem_attn_finish(jnp.concatenate(outs, axis=2), x_ref, wo_ref, g_ref, o_ref)


def _mem_attn_interleaved_kernel(x_ref, q_ref, mk_ref, mv_ref, wo_ref, g_ref, o_ref):
    seqs, tq, _ = q_ref.shape
    rows = MEM_HEADS * tq
    cols = mk_ref.shape[1]
    scale = MEM_HEAD_DIM ** -0.5
    q = jnp.concatenate([q_ref[:, :, h * MEM_HEAD_DIM:(h + 1) * MEM_HEAD_DIM]
                         for h in range(MEM_HEADS)], axis=1).astype(BF16)
    s = jnp.einsum('gqd,gkd->gqk', q, mk_ref[...].astype(BF16), preferred_element_type=F32) * scale
    row_head = lax.broadcasted_iota(jnp.int32, (rows, cols), 0) >> _log2(tq)
    col_head = lax.broadcasted_iota(jnp.int32, (rows, cols), 1) & (MEM_HEADS - 1)
    s = jnp.where((row_head == col_head)[None], s, -jnp.inf)
    p = jnp.exp(s - jnp.max(s, axis=-1, keepdims=True))
    denom = jnp.sum(p, axis=-1, keepdims=True)
    o = jnp.einsum('gqk,gkd->gqd', p.astype(BF16), mv_ref[...].astype(BF16),
                   preferred_element_type=F32) / denom
    o = jnp.concatenate([o[:, h * tq:(h + 1) * tq, :] for h in range(MEM_HEADS)], axis=2)
    _mem_attn_finish(o, x_ref, wo_ref, g_ref, o_ref)


def _mem_attn(body, x3, q3, mem_k, mem_v, w_o, g_post, seqs, tq):
    nseq, slen, _ = x3.shape
    tok = lambda width: pl.BlockSpec((seqs, tq, width), lambda i, j: (i, j, 0))
    mem = pl.BlockSpec((seqs,) + mem_k.shape[1:], lambda i, j: (i, 0, 0))
    const = lambda shape: pl.BlockSpec(shape, lambda i, j: (0, 0))
    return pl.pallas_call(
        body,
        grid=(nseq // seqs, slen // tq),
        in_specs=[tok(D_MODEL), tok(MEM_WIDTH), mem, mem,
                  const((MEM_WIDTH, D_MODEL)), const((1, D_MODEL))],
        out_specs=tok(D_MODEL),
        out_shape=jax.ShapeDtypeStruct((nseq, slen, D_MODEL), F32),
        compiler_params=_params("parallel", "arbitrary"),
        name="mem_attn",
    )(x3, q3, mem_k, mem_v, w_o, g_post)


def _row(g):
    return g.reshape(1, -1)


def _cast_tiles_kernel(order_ref, w_ref, o_ref):
    del order_ref
    o_ref[...] = w_ref[...].astype(BF16)


def _cast_col_tiles(w, tile, order=None):
    k, n = w.shape
    nt = n // tile
    order = jnp.arange(nt, dtype=jnp.int32) if order is None else jnp.asarray(order, jnp.int32)
    return pl.pallas_call(
        _cast_tiles_kernel,
        grid_spec=pltpu.PrefetchScalarGridSpec(
            num_scalar_prefetch=1,
            grid=(nt,),
            in_specs=[pl.BlockSpec((k, tile), lambda j, order_ref: (0, order_ref[j]))],
            out_specs=pl.BlockSpec((None, k, tile), lambda j, order_ref: (j, 0, 0)),
        ),
        out_shape=jax.ShapeDtypeStruct((nt, k, tile), BF16),
        compiler_params=_params("parallel"),
        name="cast_col_tiles",
    )(order, w)


def _trunk_front(x2d, w, side_casts=()):
    x1, *converted = _ffn(x2d, w['ffn1_pre'], w['ffn1_post'], w['ffn1_wg'], w['ffn1_wu'],
                          w['ffn1_wd'], side_casts=side_casts)
    zb, zf = _in_proj(x1, w['mix_pre'], w['w_in'])
    return x1, zb, zf, converted


def _trunk_back(x1, a, r, mem_body, mem_k, mem_v, w, nseq, seqs, tq):
    t = x1.shape[0]
    x2, qm = _mix_out(x1, a.reshape(t, ATTN_WIDTH), r.reshape(t, HGRN_VW), w['w_out'],
                      w['mix_post'], w['mem_pre'], w['w_mem_q'])
    x3 = _mem_attn(mem_body, x2.reshape(nseq, t // nseq, D_MODEL),
                   qm.reshape(nseq, t // nseq, MEM_WIDTH),
                   mem_k, mem_v, w['w_mem_o'], w['mem_post'], seqs, tq)
    (y,) = _ffn(x3.reshape(t, D_MODEL), w['ffn2_pre'], w['ffn2_post'],
                w['ffn2_wg'], w['ffn2_wu'], w['ffn2_wd'])
    return y


def kernel(x_prompt, x_sample, mem_prompt, cache_win_k, cache_win_v, state_hgrn, cache_mem_k, cache_mem_v, ffn1_norm_pre, ffn1_norm_post, ffn1_w_gate, ffn1_w_up, ffn1_w_down, mix_norm_pre, mix_norm_post, w_in, attn_sinks, hgrn_lb_logits, attn_out_gain, hgrn_out_gain, w_out, mem_norm_pre, mem_norm_post, mem_norm_kv, w_mem_q, w_mem_k, w_mem_v, w_mem_o, ffn2_norm_pre, ffn2_norm_post, ffn2_w_gate, ffn2_w_up, ffn2_w_down):
    bp, sp, _ = x_prompt.shape
    bs, ts, _ = x_sample.shape
    mt = mem_prompt.shape[1]
    l = 0

    seg_widths = dict(qa=ATTN_WIDTH, kv=2 * KV_WIDTH, qh=HGRN_KW, fh=HGRN_KW, ih=HGRN_VW, gh=HGRN_VW)
    seg_tiles, start = {}, 0
    for name in ('qa', 'kv', 'qh', 'fh', 'ih', 'gh'):
        n_tiles = seg_widths[name] // PROJ_COL_TILE
        seg_tiles[name] = list(range(start, start + n_tiles))
        start += n_tiles
    w_in_order = sum((seg_tiles[name] for name in ('qa', 'qh', 'ih', 'gh', 'fh', 'kv')), [])
    w = dict(
        ffn1_pre=_row(ffn1_norm_pre[l]), ffn1_post=_row(ffn1_norm_post[l]),
        ffn1_wg=_cast_col_tiles(ffn1_w_gate[l], FF_TILE), ffn1_wu=_cast_col_tiles(ffn1_w_up[l], FF_TILE),
        ffn1_wd=ffn1_w_down[l].astype(BF16).reshape(D_FF // FF_TILE, FF_TILE, D_MODEL),
        mix_pre=_row(mix_norm_pre[l]), mix_post=_row(mix_norm_post[l]),
        w_in=_cast_col_tiles(w_in[l], PROJ_COL_TILE, w_in_order), w_out=w_out[l].astype(BF16),
        mem_pre=_row(mem_norm_pre[l]), mem_post=_row(mem_norm_post[l]),
        w_mem_q=w_mem_q[l].astype(BF16), w_mem_o=w_mem_o[l].astype(BF16),
        ffn2_pre=_row(ffn2_norm_pre[l]), ffn2_post=_row(ffn2_norm_post[l]),
    )
    ffn2_casts = (ffn2_w_gate[l], ffn2_w_up[l], ffn2_w_down[l])
    sinks = attn_sinks[l]
    attn_gain = _row(attn_out_gain[l])
    hgrn_gain = _row(hgrn_out_gain[l])

    mk, mv, mk_b, mv_b = _mem_kv(mem_prompt, _row(mem_norm_kv[l]),
                                 w_mem_k[l].astype(BF16), w_mem_v[l].astype(BF16))
    x1, zb, zf, (wg2, wu2, wd2) = _trunk_front(x_prompt.reshape(bp * sp, D_MODEL), w, ffn2_casts)
    w.update(ffn2_wg=wg2, ffn2_wu=wu2, ffn2_wd=wd2.reshape(D_FF // FF_TILE, FF_TILE, D_MODEL))
    zb3 = zb.reshape(bp, sp, ZB_WIDTH)
    zf3 = zf.reshape(bp, sp, ZF_WIDTH)
    a = _attn_prompt(zb3, zf3, sinks, attn_gain)
    r, p_state = _hgrn_prompt(zb3, zf3, hgrn_lb_logits, hgrn_gain)
    y_p = _trunk_back(x1, a, r, _mem_attn_kernel, mk_b, mv_b, w, bp, 1, 512).reshape(bp, sp, D_MODEL)
    k_off = ZF_KA * KV_WIDTH
    v_off = ZF_VA * KV_WIDTH
    p_wk = zf3[:, sp - WINDOW:, k_off:k_off + KV_WIDTH]
    p_wv = zf3[:, sp - WINDOW:, v_off:v_off + KV_WIDTH]

    x1s, zbs, zfs, _ = _trunk_front(x_sample.reshape(bs * ts, D_MODEL), w)
    a_s, s_wk, s_wv = _attn_sample(zbs, zfs, cache_win_k[l].reshape(bs, WINDOW, KV_WIDTH),
                                   cache_win_v[l].reshape(bs, WINDOW, KV_WIDTH), sinks, attn_gain, ts)
    r_s, s_state = _hgrn_sample(zbs, zfs, state_hgrn[l], hgrn_lb_logits, hgrn_gain, ts)
    mem_rows = cache_mem_k.shape[2] * MEM_HEADS
    y_s = _trunk_back(x1s, a_s, r_s, _mem_attn_interleaved_kernel,
                      cache_mem_k[l].reshape(bs, mem_rows, MEM_HEAD_DIM),
                      cache_mem_v[l].reshape(bs, mem_rows, MEM_HEAD_DIM),
                      w, bs, 8, ts).reshape(bs, ts, D_MODEL)

    kv5 = lambda t, n: t.reshape(1, n, WINDOW, ATTN_KV_HEADS, ATTN_HEAD_DIM)
    mem5 = lambda t: t.reshape(1, bp, mt, MEM_HEADS, MEM_HEAD_DIM)
    return (y_p, y_s, kv5(p_wk, bp), kv5(p_wv, bp), p_state[None], mem5(mk), mem5(mv),
            kv5(s_wk, bs), kv5(s_wv, bs), s_state[None])
```

```python
import functools

import jax
import jax.numpy as jnp
from jax import lax
from jax.experimental import pallas as pl
from jax.experimental.pallas import tpu as pltpu

F32 = jnp.float32
BF16 = jnp.bfloat16

D_MODEL = 2048
D_FF = 5632
ATTN_HEADS = 16
ATTN_KV_HEADS = 4
ATTN_GROUP = ATTN_HEADS // ATTN_KV_HEADS
ATTN_HEAD_DIM = 64
WINDOW = 128
ATTN_WIDTH = ATTN_HEADS * ATTN_HEAD_DIM
KV_WIDTH = ATTN_KV_HEADS * ATTN_HEAD_DIM
HGRN_HEADS = 8
HGRN_DK = 128
HGRN_DV = 128
HGRN_KW = HGRN_HEADS * HGRN_DK
HGRN_VW = HGRN_HEADS * HGRN_DV
IN_PROJ_WIDTH = ATTN_WIDTH + 2 * KV_WIDTH + 2 * HGRN_KW + 2 * HGRN_VW
MEM_HEADS = 4
MEM_HEAD_DIM = 128
MEM_WIDTH = MEM_HEADS * MEM_HEAD_DIM
FFN_RESIDUAL = 0.5
EPS = 1e-6

ZB_WIDTH = ATTN_WIDTH + HGRN_KW + 2 * HGRN_VW
ZF_WIDTH = HGRN_KW + 2 * KV_WIDTH
ZB_QA, ZB_QH, ZB_IH, ZB_GH = 0, 1, 2, 3
ZF_FH = 0
ZF_KA, ZF_VA = 4, 5

VMEM_LIMIT_BYTES = 56 * 1024 * 1024

FFN_TOKEN_TILE = 512
PROJ_TOKEN_TILE = 1024
MIX_TOKEN_TILE = 512
FF_TILE = 512
RING_SLOTS = 4
RING_AHEAD = 2
CAST_AT_STEP = 3
PROJ_COL_TILE = 512
HGRN_CHUNK = 128
HGRN_STEP_CHUNKS = 2
HGRN_SAMPLE_SEQS = 4
ATTN_SAMPLE_SEQS = 8
ATTN_STEP_BLOCKS = 4
MASKED = -1e30

ALIBI_SLOPES = tuple(2.0 ** (-8.0 * (h + 1) / ATTN_HEADS) for h in range(ATTN_HEADS))


def _params(*semantics):
    return pltpu.CompilerParams(dimension_semantics=semantics,
                                vmem_limit_bytes=VMEM_LIMIT_BYTES)


def _rms(x, g):
    return x * lax.rsqrt(jnp.mean(x * x, axis=-1, keepdims=True) + EPS) * g


def _silu(x):
    return x * jax.nn.sigmoid(x)


def _dot(a, b):
    return jnp.dot(a, b, preferred_element_type=F32)


def _dot_nt(a, b):
    return lax.dot_general(a, b, (((1,), (1,)), ((), ())), preferred_element_type=F32)


def _dot_tn(a, b):
    return lax.dot_general(a, b, (((0,), (0,)), ((), ())), preferred_element_type=F32)


def _log2(n):
    assert n & (n - 1) == 0
    return n.bit_length() - 1


def _stream_weight_tiles(n, token_tiles, copies, step):
    i = pl.program_id(0)
    slots, ahead = RING_SLOTS, RING_AHEAD
    assert ahead < slots <= n

    def start(tile):
        for c in copies(tile, tile % slots):
            c.start()

    @pl.when(i == 0)
    def _():
        for tile in range(ahead):
            start(tile)

    for j in range(n):
        nxt = j + ahead
        if nxt < n:
            assert nxt % slots not in {jj % slots for jj in range(j, nxt)}
            start(nxt)
        else:
            assert (nxt - n) % slots not in {jj % slots for jj in range(j, n)}
            pl.when(i + 1 < token_tiles)(functools.partial(start, nxt - n))
        for c in copies(j, j % slots):
            c.wait()
        step(j, j % slots)


def _ffn_kernel(*refs, token_tiles, n_casts):
    m = n_casts
    x_ref, gpre_ref, gpost_ref, wg_hbm, wu_hbm, wd_hbm = refs[:6]
    cast_src = refs[6:6 + m]
    o_ref = refs[6 + m]
    cast_dst = refs[7 + m:7 + 2 * m]
    wg_buf, wu_buf, wd_buf, sem = refs[7 + 2 * m:11 + 2 * m]
    cast_bufs = refs[11 + 2 * m:11 + 4 * m]
    cast_sem = refs[11 + 4 * m] if m else None
    i = pl.program_id(0)
    tf = wg_buf.shape[2]

    def column_tile(w_hbm, tile):
        return w_hbm.at[tile] if len(w_hbm.shape) == 3 else w_hbm.at[:, pl.ds(tile * tf, tf)]

    def copies(tile, slot):
        return (pltpu.make_async_copy(column_tile(wg_hbm, tile), wg_buf.at[slot], sem.at[0, slot]),
                pltpu.make_async_copy(column_tile(wu_hbm, tile), wu_buf.at[slot], sem.at[1, slot]),
                pltpu.make_async_copy(wd_hbm.at[tile], wd_buf.at[slot], sem.at[2, slot]))

    def cast_in(k):
        buf = cast_bufs[2 * k]
        rows = pl.ds(pl.multiple_of(i * buf.shape[0], 16), buf.shape[0])
        return pltpu.make_async_copy(cast_src[k].at[rows, :], buf, cast_sem.at[k, 0])

    def cast_out(k):
        buf = cast_bufs[2 * k + 1]
        rows = pl.ds(pl.multiple_of(i * buf.shape[0], 16), buf.shape[0])
        return pltpu.make_async_copy(buf, cast_dst[k].at[rows, :], cast_sem.at[k, 1])

    for k in range(m):
        cast_in(k).start()

    h = _rms(x_ref[...], gpre_ref[...]).astype(BF16)

    def step(j, slot):
        k = j - CAST_AT_STEP
        if 0 <= k < m:
            cast_in(k).wait()
            cast_bufs[2 * k + 1][...] = cast_bufs[2 * k][...].astype(BF16)
            cast_out(k).start()
        g = _dot(h, wg_buf[slot])
        u = _dot(h, wu_buf[slot])
        part = _dot((_silu(g) * u).astype(BF16), wd_buf[slot])
        if j == 0:
            o_ref[...] = part
        else:
            o_ref[...] += part

    _stream_weight_tiles(wd_hbm.shape[0], token_tiles, copies, step)
    for k in range(m):
        cast_out(k).wait()
    o_ref[...] = x_ref[...] + _rms(o_ref[...], FFN_RESIDUAL * gpost_ref[...])


def _ffn(x, g_pre, g_post, wg, wu, wd, side_casts=()):
    t = x.shape[0]
    tm = FFN_TOKEN_TILE
    tiles = t // tm
    n, tf, _ = wd.shape
    m = len(side_casts)
    assert n >= CAST_AT_STEP + m
    any_space = pl.BlockSpec(memory_space=pl.ANY)
    cast_shapes, cast_scratch = [], []
    for w in side_casts:
        r, c = w.shape
        rows = r // tiles
        assert rows * tiles == r and rows % 16 == 0
        cast_shapes.append(jax.ShapeDtypeStruct((r, c), BF16))
        cast_scratch += [pltpu.VMEM((rows, c), F32), pltpu.VMEM((rows, c), BF16)]
    if m:
        cast_scratch.append(pltpu.SemaphoreType.DMA((m, 2)))
    outs = pl.pallas_call(
        functools.partial(_ffn_kernel, token_tiles=tiles, n_casts=m),
        grid=(tiles,),
        in_specs=[
            pl.BlockSpec((tm, D_MODEL), lambda i: (i, 0)),
            pl.BlockSpec((1, D_MODEL), lambda i: (0, 0)),
            pl.BlockSpec((1, D_MODEL), lambda i: (0, 0)),
            any_space, any_space, any_space,
        ] + [any_space] * m,
        out_specs=[pl.BlockSpec((tm, D_MODEL), lambda i: (i, 0))] + [any_space] * m,
        out_shape=[jax.ShapeDtypeStruct((t, D_MODEL), F32)] + cast_shapes,
        scratch_shapes=[
            pltpu.VMEM((RING_SLOTS, D_MODEL, tf), BF16),
            pltpu.VMEM((RING_SLOTS, D_MODEL, tf), BF16),
            pltpu.VMEM((RING_SLOTS, tf, D_MODEL), BF16),
            pltpu.SemaphoreType.DMA((3, RING_SLOTS)),
        ] + cast_scratch,
        compiler_params=_params("arbitrary"),
        name="ffn_half",
    )(x, g_pre, g_post, wg, wu, wd, *side_casts)
    return tuple(outs)


def _in_proj_kernel(x_ref, g_ref, w_hbm, zb_hbm, zf_hbm, w_buf, sem, stage_b, stage_f, out_sem,
                    *, token_tiles):
    i = pl.program_id(0)
    n, _, tn = w_hbm.shape
    tm = x_ref.shape[0]
    nb = zb_hbm.shape[1] // tn
    rows = pl.ds(pl.multiple_of(i * tm, tm), tm)

    def copies(tile, slot):
        return (pltpu.make_async_copy(w_hbm.at[tile], w_buf.at[slot], sem.at[slot]),)

    def out_copy(j):
        if j < nb:
            s = j % 2
            return pltpu.make_async_copy(stage_b.at[s], zb_hbm.at[rows, pl.ds(j * tn, tn)],
                                         out_sem.at[0, s])
        s = (j - nb) % 2
        return pltpu.make_async_copy(stage_f.at[s], zf_hbm.at[rows, pl.ds((j - nb) * tn, tn)],
                                     out_sem.at[1, s])

    def same_stage(j):
        return range(0, nb) if j < nb else range(nb, n)

    h = _rms(x_ref[...], g_ref[...]).astype(BF16)

    def step(j, slot):
        z = _dot(h, w_buf[slot])
        if j - 2 in same_stage(j):
            out_copy(j - 2).wait()
        if j < nb:
            stage_b[j % 2] = z.astype(BF16)
        else:
            stage_f[(j - nb) % 2] = z
        out_copy(j).start()

    _stream_weight_tiles(n, token_tiles, copies, step)
    for j in range(n):
        if j + 2 not in same_stage(j):
            out_copy(j).wait()


def _in_proj(x, g, w):
    t = x.shape[0]
    tm = min(PROJ_TOKEN_TILE, t)
    n, _, tn = w.shape
    any_space = pl.BlockSpec(memory_space=pl.ANY)
    return pl.pallas_call(
        functools.partial(_in_proj_kernel, token_tiles=t // tm),
        grid=(t // tm,),
        in_specs=[
            pl.BlockSpec((tm, D_MODEL), lambda i: (i, 0)),
            pl.BlockSpec((1, D_MODEL), lambda i: (0, 0)),
            any_space,
        ],
        out_specs=[any_space, any_space],
        out_shape=[jax.ShapeDtypeStruct((t, ZB_WIDTH), BF16),
                   jax.ShapeDtypeStruct((t, ZF_WIDTH), F32)],
        scratch_shapes=[pltpu.VMEM((RING_SLOTS, D_MODEL, tn), BF16),
                        pltpu.SemaphoreType.DMA((RING_SLOTS,)),
                        pltpu.VMEM((2, tm, tn), BF16),
                        pltpu.VMEM((2, tm, tn), F32),
                        pltpu.SemaphoreType.DMA((2, 2))],
        compiler_params=_params("arbitrary"),
        name="in_proj",
    )(x, g, w)


def _attn_prompt_kernel(sink_ref, q_ref, kc_ref, kp_ref, vc_ref, vp_ref, gain_ref, o_ref):
    n = pl.program_id(1)
    nblk = q_ref.shape[0] // WINDOW
    scale = ATTN_HEAD_DIM ** -0.5
    row = lax.broadcasted_iota(jnp.int32, (WINDOW, 2 * WINDOW), 0)
    col = lax.broadcasted_iota(jnp.int32, (WINDOW, 2 * WINDOW), 1)
    dist = row + WINDOW - col
    in_window = (dist >= 0) & (dist < WINDOW)
    gain = gain_ref[...]

    def block(qrows, k, v, valid):
        k = (k * scale).astype(BF16)
        v = v.astype(BF16)
        neg_dist = jnp.where(valid, -dist.astype(F32), MASKED)
        outs = []
        for hd in range(ATTN_HEADS):
            kvh = hd // ATTN_GROUP
            kh = k[:, kvh * ATTN_HEAD_DIM:(kvh + 1) * ATTN_HEAD_DIM]
            vh = v[:, kvh * ATTN_HEAD_DIM:(kvh + 1) * ATTN_HEAD_DIM]
            qh = q_ref[qrows, hd * ATTN_HEAD_DIM:(hd + 1) * ATTN_HEAD_DIM]
            s = _dot_nt(qh, kh) + ALIBI_SLOPES[hd] * neg_dist
            sink = sink_ref[hd]
            m = jnp.maximum(jnp.max(s, axis=-1, keepdims=True), sink)
            p = jnp.exp(s - m)
            denom = jnp.sum(p, axis=-1, keepdims=True) + jnp.exp(sink - m)
            outs.append(_dot(p.astype(BF16), vh) / denom)
        o = jnp.concatenate(outs, axis=1)
        o_ref[qrows, :] = _rms(o, gain).astype(BF16)

    block(slice(0, WINDOW),
          jnp.concatenate([kp_ref[...], kc_ref[0:WINDOW, :]], axis=0),
          jnp.concatenate([vp_ref[...], vc_ref[0:WINDOW, :]], axis=0),
          in_window & ((col >= WINDOW) | (n > 0)))

    def later_block(blk, carry):
        first = pl.multiple_of((blk - 1) * WINDOW, WINDOW)
        kv_rows = pl.ds(first, 2 * WINDOW)
        block(pl.ds(first + WINDOW, WINDOW), kc_ref[kv_rows, :], vc_ref[kv_rows, :], in_window)
        return carry

    lax.fori_loop(1, nblk, later_block, 0)


def _attn_prompt(zb3, zf3, sinks, gain):
    b, s, _ = zb3.shape
    nblk = ATTN_STEP_BLOCKS
    rows = nblk * WINDOW

    def cur(width_idx):
        return lambda bi, n: (bi, n, width_idx)

    def prev(width_idx):
        return lambda bi, n: (bi, jnp.maximum(n * nblk - 1, 0), width_idx)

    return pl.pallas_call(
        _attn_prompt_kernel,
        grid=(b, s // rows),
        in_specs=[
            pl.BlockSpec(memory_space=pltpu.SMEM),
            pl.BlockSpec((None, rows, ATTN_WIDTH), cur(ZB_QA)),
            pl.BlockSpec((None, rows, KV_WIDTH), cur(ZF_KA)),
            pl.BlockSpec((None, WINDOW, KV_WIDTH), prev(ZF_KA)),
            pl.BlockSpec((None, rows, KV_WIDTH), cur(ZF_VA)),
            pl.BlockSpec((None, WINDOW, KV_WIDTH), prev(ZF_VA)),
            pl.BlockSpec((1, ATTN_WIDTH), lambda bi, n: (0, 0)),
        ],
        out_specs=pl.BlockSpec((None, rows, ATTN_WIDTH), lambda bi, n: (bi, n, 0)),
        out_shape=jax.ShapeDtypeStruct((b, s, ATTN_WIDTH), BF16),
        compiler_params=_params("parallel", "arbitrary"),
        name="attn_prompt",
    )(sinks, zb3, zf3, zf3, zf3, zf3, gain)


def _attn_sample_kernel(sink_ref, q_ref, kn_ref, vn_ref, ck_ref, cv_ref, gain_ref,
                        o_ref, wk_ref, wv_ref, *, t_new):
    seqs = ck_ref.shape[0]
    t_bits = _log2(t_new)
    rows = ATTN_GROUP * t_new
    q_all = q_ref[...].astype(F32).reshape(seqs, t_new, ATTN_WIDTH)
    kn = kn_ref[...].reshape(seqs, t_new, KV_WIDTH)
    vn = vn_ref[...].reshape(seqs, t_new, KV_WIDTH)
    ck = ck_ref[...]
    cv = cv_ref[...]
    wk_ref[:, :WINDOW - t_new, :] = ck[:, t_new:, :]
    wk_ref[:, WINDOW - t_new:, :] = kn
    wv_ref[:, :WINDOW - t_new, :] = cv[:, t_new:, :]
    wv_ref[:, WINDOW - t_new:, :] = vn

    r_c = lax.broadcasted_iota(jnp.int32, (rows, WINDOW), 0)
    j_c = lax.broadcasted_iota(jnp.int32, (rows, WINDOW), 1)
    dist_c = (r_c & (t_new - 1)) + WINDOW - j_c
    neg_c = jnp.where(dist_c < WINDOW, -dist_c.astype(F32), MASKED)
    r_n = lax.broadcasted_iota(jnp.int32, (rows, t_new), 0)
    j_n = lax.broadcasted_iota(jnp.int32, (rows, t_new), 1)
    dist_n = (r_n & (t_new - 1)) - j_n
    neg_n = jnp.where(dist_n >= 0, -dist_n.astype(F32), MASKED)
    head_of_row = lax.broadcasted_iota(jnp.int32, (rows, 1), 0) >> t_bits
    scale = ATTN_HEAD_DIM ** -0.5

    outs = [None] * ATTN_HEADS
    for kvh in range(ATTN_KV_HEADS):
        slope = jnp.zeros((rows, 1), F32)
        sink = jnp.zeros((rows, 1), F32)
        for g in range(ATTN_GROUP):
            hd = kvh * ATTN_GROUP + g
            slope = jnp.where(head_of_row == g, ALIBI_SLOPES[hd], slope)
            sink = jnp.where(head_of_row == g, sink_ref[hd], sink)
        q = jnp.concatenate(
            [q_all[:, :, (kvh * ATTN_GROUP + g) * ATTN_HEAD_DIM:(kvh * ATTN_GROUP + g + 1) * ATTN_HEAD_DIM]
             for g in range(ATTN_GROUP)], axis=1).astype(BF16)
        sl = slice(kvh * ATTN_HEAD_DIM, (kvh + 1) * ATTN_HEAD_DIM)
        ck_h = (ck[:, :, sl] * scale).astype(BF16)
        kn_h = (kn[:, :, sl] * scale).astype(BF16)
        cv_h = cv[:, :, sl].astype(BF16)
        vn_h = vn[:, :, sl].astype(BF16)
        s_c = jnp.einsum('gqd,gkd->gqk', q, ck_h, preferred_element_type=F32) + (slope * neg_c)[None]
        s_n = jnp.einsum('gqd,gkd->gqk', q, kn_h, preferred_element_type=F32) + (slope * neg_n)[None]
        m = jnp.maximum(jnp.maximum(jnp.max(s_c, axis=-1, keepdims=True),
                                    jnp.max(s_n, axis=-1, keepdims=True)), sink[None])
        p_c = jnp.exp(s_c - m)
        p_n = jnp.exp(s_n - m)
        denom = (jnp.sum(p_c, axis=-1, keepdims=True) + jnp.sum(p_n, axis=-1, keepdims=True)
                 + jnp.exp(sink[None] - m))
        o = (jnp.einsum('gqk,gkd->gqd', p_c.astype(BF16), cv_h, preferred_element_type=F32)
             + jnp.einsum('gqk,gkd->gqd', p_n.astype(BF16), vn_h, preferred_element_type=F32)) / denom
        for g in range(ATTN_GROUP):
            outs[kvh * ATTN_GROUP + g] = o[:, g * t_new:(g + 1) * t_new, :]
    o_all = jnp.concatenate(outs, axis=2)
    o_ref[...] = _rms(o_all, gain_ref[...][None]).reshape(seqs * t_new, ATTN_WIDTH)


def _attn_sample(zb, zf, cache_k, cache_v, sinks, gain, t_new):
    b = cache_k.shape[0]
    gs = ATTN_SAMPLE_SEQS
    tok = lambda width, idx: pl.BlockSpec((gs * t_new, width), lambda i: (i, idx))
    cache_spec = pl.BlockSpec((gs, WINDOW, KV_WIDTH), lambda i: (i, 0, 0))
    return pl.pallas_call(
        functools.partial(_attn_sample_kernel, t_new=t_new),
        grid=(b // gs,),
        in_specs=[
            pl.BlockSpec(memory_space=pltpu.SMEM),
            tok(ATTN_WIDTH, ZB_QA), tok(KV_WIDTH, ZF_KA), tok(KV_WIDTH, ZF_VA),
            cache_spec, cache_spec,
            pl.BlockSpec((1, ATTN_WIDTH), lambda i: (0, 0)),
        ],
        out_specs=[tok(ATTN_WIDTH, 0), cache_spec, cache_spec],
        out_shape=[
            jax.ShapeDtypeStruct((b * t_new, ATTN_WIDTH), F32),
            jax.ShapeDtypeStruct((b, WINDOW, KV_WIDTH), F32),
            jax.ShapeDtypeStruct((b, WINDOW, KV_WIDTH), F32),
        ],
        compiler_params=_params("parallel"),
        name="attn_sample",
    )(sinks, zb, zf, zf, cache_k, cache_v, gain)


def _lower_bound(lbl_ref):
    logits = lbl_ref[...]
    e = jnp.exp(logits - jnp.max(logits, axis=0, keepdims=True))
    return e[0:1, :] / jnp.sum(e, axis=0, keepdims=True)


def _hgrn_features(q_raw, f_raw, lb):
    sg = jax.nn.sigmoid(f_raw)
    f = lb + (1.0 - lb) * sg
    k = (1.0 - lb) * (1.0 - sg)
    return _silu(q_raw), k, f, jnp.log2(f)


def _cumsum_rows(x, segment):
    row = lax.broadcasted_iota(jnp.int32, x.shape, 0) & (segment - 1)
    shift = 1
    while shift < segment:
        x = x + jnp.where(row >= shift, pltpu.roll(x, shift, axis=0), 0.0)
        shift *= 2
    return x


def _split3(x):
    hi = x.astype(BF16)
    r = x - hi.astype(F32)
    mid = r.astype(BF16)
    lo = (r - mid.astype(F32)).astype(BF16)
    return hi, mid, lo


def _cumsum_rows_mxu(x, tri):
    hi, mid, lo = _split3(x)
    return _dot(tri, hi) + _dot(tri, mid) + _dot(tri, lo)


def _column_broadcast(row_vec, n):
    rows = 16
    hi, mid, lo = _split3(row_vec)
    terms = jnp.concatenate([hi, mid, lo, jnp.zeros((rows - 3, n), BF16)], axis=0)
    ones = (lax.broadcasted_iota(jnp.int32, (rows, n), 0) < 3).astype(BF16)
    return _dot_tn(terms, ones)


def _hgrn_out(o, gain, g_raw):
    return _rms(o, gain) * _silu(g_raw)


def _pair_levels(c):
    t = lax.broadcasted_iota(jnp.int32, (c, c), 0)
    s = lax.broadcasted_iota(jnp.int32, (c, c), 1)
    masks = [t == s]
    half = 1
    while half < c:
        block = 2 * half
        same_block = (t >> _log2(block)) == (s >> _log2(block))
        masks.append(same_block & ((t & (block - 1)) >= half) & ((s & (block - 1)) < half))
        half = block
    return masks


def _pair_factors(q, k, f, log2f, l2):
    c, w = q.shape
    pos = lax.broadcasted_iota(jnp.int32, (c, w), 0) & 3
    e2 = jnp.exp2(jnp.where(pos == 0, pltpu.roll(log2f, c - 1, axis=0),
                            jnp.where(pos == 1, 0.0,
                                      jnp.where(pos == 2, log2f,
                                                log2f + pltpu.roll(log2f, 1, axis=0)))))
    qs = [q, q * f, q * e2]
    ks = [k, k, k * e2]
    half = 4
    while half < c:
        block = 2 * half
        ref = l2.reshape(c // block, block, w)[:, half - 1:half, :]
        ref = jnp.broadcast_to(ref, (c // block, block, w)).reshape(c, w)
        e = jnp.exp2(-jnp.abs(l2 - ref))
        qs.append(q * e)
        ks.append(k * e)
        half = block
    return [x.astype(BF16) for x in qs], [x.astype(BF16) for x in ks]


def _hgrn_prompt_kernel(lbl_ref, q_ref, f_ref, i_ref, g_ref, gain_ref, r_ref, s_ref):
    c = HGRN_CHUNK

    @pl.when(pl.program_id(1) == 0)
    def _():
        s_ref[...] = jnp.zeros_like(s_ref)

    lb = _lower_bound(lbl_ref)
    q_all, k_all, f_all, log2f_all = _hgrn_features(q_ref[...].astype(F32), f_ref[...], lb)
    masks = _pair_levels(c)
    tri = (lax.broadcasted_iota(jnp.int32, (c, c), 1)
           <= lax.broadcasted_iota(jnp.int32, (c, c), 0)).astype(BF16)
    gain = gain_ref[...]

    chunks = []
    for ci in range(HGRN_STEP_CHUNKS):
        rows = slice(ci * c, (ci + 1) * c)
        l2_c = _cumsum_rows_mxu(log2f_all[rows], tri)
        q_lv, k_lv = _pair_factors(q_all[rows], k_all[rows], f_all[rows], log2f_all[rows], l2_c)
        l_end = l2_c[c - 1:c, :]
        q_hat = (q_all[rows] * jnp.exp2(l2_c)).astype(BF16)
        k_end = (k_all[rows] * jnp.exp2(l_end - l2_c)).astype(BF16)
        chunks.append((rows, q_lv, k_lv, q_hat, k_end, jnp.exp2(l_end)))

    for rows, q_lv, k_lv, q_hat, k_end, decay_end in chunks:
        for h in range(HGRN_HEADS):
            sl = slice(h * HGRN_DK, (h + 1) * HGRN_DK)
            vb = i_ref[rows, sl]
            a = jnp.zeros((c, c), F32)
            for mask, q_f, k_f in zip(masks, q_lv, k_lv):
                a = jnp.where(mask, _dot_nt(q_f[:, sl], k_f[:, sl]), a)
            s_old = s_ref[h]
            o = _dot(a.astype(BF16), vb) + _dot(q_hat[:, sl], s_old.astype(BF16))
            s_ref[h] = (s_old * _column_broadcast(decay_end[:, sl], HGRN_DK)
                        + _dot_tn(k_end[:, sl], vb))
            r_ref[rows, sl] = _hgrn_out(o, gain, g_ref[rows, sl].astype(F32)).astype(r_ref.dtype)


def _hgrn_prompt(zb3, zf3, lb_logits, gain):
    b, s, _ = zb3.shape
    rows = HGRN_CHUNK * HGRN_STEP_CHUNKS
    seg = lambda idx: pl.BlockSpec((None, rows, HGRN_KW), lambda bi, ci: (bi, ci, idx))
    return pl.pallas_call(
        _hgrn_prompt_kernel,
        grid=(b, s // rows),
        in_specs=[
            pl.BlockSpec(lb_logits.shape, lambda bi, ci: (0, 0)),
            seg(ZB_QH), seg(ZF_FH), seg(ZB_IH), seg(ZB_GH),
            pl.BlockSpec((1, HGRN_DV), lambda bi, ci: (0, 0)),
        ],
        out_specs=[
            pl.BlockSpec((None, rows, HGRN_VW), lambda bi, ci: (bi, ci, 0)),
            pl.BlockSpec((None, HGRN_HEADS, HGRN_DK, HGRN_DV), lambda bi, ci: (bi, 0, 0, 0)),
        ],
        out_shape=[
            jax.ShapeDtypeStruct((b, s, HGRN_VW), BF16),
            jax.ShapeDtypeStruct((b, HGRN_HEADS, HGRN_DK, HGRN_DV), F32),
        ],
        compiler_params=_params("parallel", "arbitrary"),
        name="hgrn_prompt",
    )(lb_logits, zb3, zf3, zb3, zb3, gain)


def _hgrn_sample_kernel(lbl_ref, seg_ref, segt_ref, q_ref, f_ref, i_ref, g_ref, gain_ref, s0_ref,
                        r_ref, s_ref, *, t_new):
    seqs = s0_ref.shape[0]
    n = seqs * t_new
    width = HGRN_KW
    lb = _lower_bound(lbl_ref)
    q_all, k_all, _, log2f_all = _hgrn_features(q_ref[...].astype(F32), f_ref[...], lb)
    l2_all = _cumsum_rows(log2f_all, t_new)
    v_all = i_ref[...].astype(F32)
    g_all = g_ref[...].astype(F32)
    gain = gain_ref[...]

    def row_of_seq(x, t):
        x3 = x.reshape(seqs, t_new, width)
        return jnp.broadcast_to(x3[:, t:t + 1, :], (seqs, t_new, width)).reshape(n, width)

    src = lax.broadcasted_iota(jnp.int32, (n, width), 0) & (t_new - 1)
    w_parts = []
    for t in range(t_new):
        decay = jnp.exp2(jnp.minimum(row_of_seq(l2_all, t) - l2_all, 0.0))
        w_parts.append(jnp.where(src <= t, decay * k_all * row_of_seq(q_all, t), 0.0))
    w = jnp.concatenate(w_parts, axis=0).astype(BF16)
    a_heads = _dot(w, seg_ref[...])
    a_lanes = _dot(a_heads.astype(BF16), segt_ref[...])
    o_rows = []
    for t in range(t_new):
        p = (a_lanes[t * n:(t + 1) * n] * v_all).reshape(seqs, t_new, width)
        o_rows.append(jnp.sum(p, axis=1, keepdims=True))
    o_intra = jnp.concatenate(o_rows, axis=1).reshape(n, width)

    l_end = row_of_seq(l2_all, t_new - 1)
    q_hat = q_all * jnp.exp2(l2_all)
    k_end = k_all * jnp.exp2(l_end - l2_all)
    decay_end = jnp.exp2(l_end)
    for h in range(HGRN_HEADS):
        sl = slice(h * HGRN_DK, (h + 1) * HGRN_DK)
        o_parts = []
        for sq in range(seqs):
            rows = slice(sq * t_new, (sq + 1) * t_new)
            s_old = s0_ref[sq, h]
            o_parts.append(_dot(q_hat[rows, sl].astype(BF16), s_old.astype(BF16)))
            s_ref[sq, h] = (s_old * _column_broadcast(decay_end[sq * t_new:sq * t_new + 1, sl], HGRN_DK)
                            + _dot_tn(k_end[rows, sl].astype(BF16), v_all[rows, sl].astype(BF16)))
        o = o_intra[:, sl] + jnp.concatenate(o_parts, axis=0)
        r_ref[:, sl] = _hgrn_out(o, gain, g_all[:, sl])


def _hgrn_sample(zb, zf, state, lb_logits, gain, t_new):
    b = state.shape[0]
    gs = HGRN_SAMPLE_SEQS
    seg = lambda idx: pl.BlockSpec((gs * t_new, HGRN_KW), lambda i: (i, idx))
    state_spec = pl.BlockSpec((gs, HGRN_HEADS, HGRN_DK, HGRN_DV), lambda i: (i, 0, 0, 0))
    lane_head = jnp.arange(HGRN_KW, dtype=jnp.int32) // HGRN_DK
    head_seg = (lane_head[:, None] == jnp.arange(HGRN_DK, dtype=jnp.int32)[None, :]).astype(BF16)
    return pl.pallas_call(
        functools.partial(_hgrn_sample_kernel, t_new=t_new),
        grid=(b // gs,),
        in_specs=[
            pl.BlockSpec(lb_logits.shape, lambda i: (0, 0)),
            pl.BlockSpec((HGRN_KW, HGRN_DK), lambda i: (0, 0)),
            pl.BlockSpec((HGRN_DK, HGRN_KW), lambda i: (0, 0)),
            seg(ZB_QH), seg(ZF_FH), seg(ZB_IH), seg(ZB_GH),
            pl.BlockSpec((1, HGRN_DV), lambda i: (0, 0)),
            state_spec,
        ],
        out_specs=[seg(0), state_spec],
        out_shape=[
            jax.ShapeDtypeStruct((b * t_new, HGRN_VW), F32),
            jax.ShapeDtypeStruct((b, HGRN_HEADS, HGRN_DK, HGRN_DV), F32),
        ],
        compiler_params=_params("parallel"),
        name="hgrn_sample",
    )(lb_logits, head_seg, head_seg.T, zb, zf, zb, zb, gain, state)


def _mix_out_kernel(x_ref, a_ref, r_ref, wa_ref, wr_ref, gpost_ref, gmem_ref, wq_ref,
                    x2_ref, qm_ref):
    mixed = (_dot(a_ref[...].astype(BF16), wa_ref[...])
             + _dot(r_ref[...].astype(BF16), wr_ref[...]))
    x2 = x_ref[...] + _rms(mixed, gpost_ref[...])
    x2_ref[...] = x2
    qm_ref[...] = _dot(_rms(x2, gmem_ref[...]).astype(BF16), wq_ref[...])


def _mix_out(x, a, r, w_out, g_post, g_mem, w_q):
    t = x.shape[0]
    tm = MIX_TOKEN_TILE
    tok = lambda width: pl.BlockSpec((tm, width), lambda i: (i, 0))
    const = lambda shape, idx=(0, 0): pl.BlockSpec(shape, lambda i: idx)
    return pl.pallas_call(
        _mix_out_kernel,
        grid=(t // tm,),
        in_specs=[
            tok(D_MODEL), tok(ATTN_WIDTH), tok(HGRN_VW),
            const((ATTN_WIDTH, D_MODEL), (0, 0)),
            const((HGRN_VW, D_MODEL), (1, 0)),
            const((1, D_MODEL)), const((1, D_MODEL)),
            const((D_MODEL, MEM_WIDTH)),
        ],
        out_specs=[tok(D_MODEL), tok(MEM_WIDTH)],
        out_shape=[jax.ShapeDtypeStruct((t, D_MODEL), F32),
                   jax.ShapeDtypeStruct((t, MEM_WIDTH), F32)],
        compiler_params=_params("parallel"),
        name="mix_out",
    )(x, a, r, w_out, w_out, g_post, g_mem, w_q)


def _mem_kv_kernel(m_ref, g_ref, wk_ref, wv_ref, k_ref, v_ref, kb_ref, vb_ref):
    h = _rms(m_ref[...], g_ref[...]).astype(BF16)
    k = _dot(h, wk_ref[...])
    v = _dot(h, wv_ref[...])
    k_ref[...] = k
    v_ref[...] = v
    kb_ref[...] = k.astype(BF16)
    vb_ref[...] = v.astype(BF16)


def _mem_kv(mem, g, w_k, w_v):
    b, m, _ = mem.shape
    const = lambda shape: pl.BlockSpec(shape, lambda i: (0, 0))
    out = pl.BlockSpec((None, m, MEM_WIDTH), lambda i: (i, 0, 0))
    return pl.pallas_call(
        _mem_kv_kernel,
        grid=(b,),
        in_specs=[pl.BlockSpec((None, m, D_MODEL), lambda i: (i, 0, 0)), const((1, D_MODEL)),
                  const((D_MODEL, MEM_WIDTH)), const((D_MODEL, MEM_WIDTH))],
        out_specs=[out] * 4,
        out_shape=[jax.ShapeDtypeStruct((b, m, MEM_WIDTH), F32)] * 2
        + [jax.ShapeDtypeStruct((b, m, MEM_WIDTH), BF16)] * 2,
        compiler_params=_params("parallel"),
        name="mem_kv",
    )(mem, g, w_k, w_v)


def _mem_attn_finish(o, x_ref, wo_ref, g_ref, o_ref):
    seqs, tq, _ = x_ref.shape
    y = _dot(o.reshape(seqs * tq, MEM_WIDTH).astype(BF16), wo_ref[...])
    x = x_ref[...].reshape(seqs * tq, D_MODEL)
    o_ref[...] = (x + _rms(y, g_ref[...])).reshape(seqs, tq, D_MODEL)


def _mem_attn_kernel(x_ref, q_ref, mk_ref, mv_ref, wo_ref, g_ref, o_ref):
    scale = MEM_HEAD_DIM ** -0.5
    outs = []
    for h in range(MEM_HEADS):
        sl = slice(h * MEM_HEAD_DIM, (h + 1) * MEM_HEAD_DIM)
        q = q_ref[:, :, sl].astype(BF16)
        s = jnp.einsum('gqd,gkd->gqk', q, mk_ref[:, :, sl], preferred_element_type=F32) * scale
        p = jnp.exp(s - jnp.max(s, axis=-1, keepdims=True))
        denom = jnp.sum(p, axis=-1, keepdims=True)
        outs.append(jnp.einsum('gqk,gkd->gqd', p.astype(BF16), mv_ref[:, :, sl],
                               preferred_element_type=F32) / denom)
    _mem_attn_finish(jnp.concatenate(outs, axis=2), x_ref, wo_ref, g_ref, o_ref)


def _mem_attn_interleaved_kernel(x_ref, q_ref, mk_ref, mv_ref, wo_ref, g_ref, o_ref):
    seqs, tq, _ = q_ref.shape
    rows = MEM_HEADS * tq
    cols = mk_ref.shape[1]
    scale = MEM_HEAD_DIM ** -0.5
    q = jnp.concatenate([q_ref[:, :, h * MEM_HEAD_DIM:(h + 1) * MEM_HEAD_DIM]
                         for h in range(MEM_HEADS)], axis=1).astype(BF16)
    s = jnp.einsum('gqd,gkd->gqk', q, mk_ref[...].astype(BF16), preferred_element_type=F32) * scale
    row_head = lax.broadcasted_iota(jnp.int32, (rows, cols), 0) >> _log2(tq)
    col_head = lax.broadcasted_iota(jnp.int32, (rows, cols), 1) & (MEM_HEADS - 1)
    s = jnp.where((row_head == col_head)[None], s, -jnp.inf)
    p = jnp.exp(s - jnp.max(s, axis=-1, keepdims=True))
    denom = jnp.sum(p, axis=-1, keepdims=True)
    o = jnp.einsum('gqk,gkd->gqd', p.astype(BF16), mv_ref[...].astype(BF16),
                   preferred_element_type=F32) / denom
    o = jnp.concatenate([o[:, h * tq:(h + 1) * tq, :] for h in range(MEM_HEADS)], axis=2)
    _mem_attn_finish(o, x_ref, wo_ref, g_ref, o_ref)


def _mem_attn(body, x3, q3, mem_k, mem_v, w_o, g_post, seqs, tq):
    nseq, slen, _ = x3.shape
    tok = lambda width: pl.BlockSpec((seqs, tq, width), lambda i, j: (i, j, 0))
    mem = pl.BlockSpec((seqs,) + mem_k.shape[1:], lambda i, j: (i, 0, 0))
    const = lambda shape: pl.BlockSpec(shape, lambda i, j: (0, 0))
    return pl.pallas_call(
        body,
        grid=(nseq // seqs, slen // tq),
        in_specs=[tok(D_MODEL), tok(MEM_WIDTH), mem, mem,
                  const((MEM_WIDTH, D_MODEL)), const((1, D_MODEL))],
        out_specs=tok(D_MODEL),
        out_shape=jax.ShapeDtypeStruct((nseq, slen, D_MODEL), F32),
        compiler_params=_params("parallel", "arbitrary"),
        name="mem_attn",
    )(x3, q3, mem_k, mem_v, w_o, g_post)


def _row(g):
    return g.reshape(1, -1)


def _cast_tiles_kernel(order_ref, w_ref, o_ref):
    del order_ref
    o_ref[...] = w_ref[...].astype(BF16)


def _cast_col_tiles(w, tile, order=None):
    k, n = w.shape
    nt = n // tile
    order = jnp.arange(nt, dtype=jnp.int32) if order is None else jnp.asarray(order, jnp.int32)
    return pl.pallas_call(
        _cast_tiles_kernel,
        grid_spec=pltpu.PrefetchScalarGridSpec(
            num_scalar_prefetch=1,
            grid=(nt,),
            in_specs=[pl.BlockSpec((k, tile), lambda j, order_ref: (0, order_ref[j]))],
            out_specs=pl.BlockSpec((None, k, tile), lambda j, order_ref: (j, 0, 0)),
        ),
        out_shape=jax.ShapeDtypeStruct((nt, k, tile), BF16),
        compiler_params=_params("parallel"),
        name="cast_col_tiles",
    )(order, w)


def _trunk_front(x2d, w, side_casts=()):
    x1, *converted = _ffn(x2d, w['ffn1_pre'], w['ffn1_post'], w['ffn1_wg'], w['ffn1_wu'],
                          w['ffn1_wd'], side_casts=side_casts)
    zb, zf = _in_proj(x1, w['mix_pre'], w['w_in'])
    return x1, zb, zf, converted


def _trunk_back(x1, a, r, mem_body, mem_k, mem_v, w, nseq, seqs, tq):
    t = x1.shape[0]
    x2, qm = _mix_out(x1, a.reshape(t, ATTN_WIDTH), r.reshape(t, HGRN_VW), w['w_out'],
                      w['mix_post'], w['mem_pre'], w['w_mem_q'])
    x3 = _mem_attn(mem_body, x2.reshape(nseq, t // nseq, D_MODEL),
                   qm.reshape(nseq, t // nseq, MEM_WIDTH),
                   mem_k, mem_v, w['w_mem_o'], w['mem_post'], seqs, tq)
    (y,) = _ffn(x3.reshape(t, D_MODEL), w['ffn2_pre'], w['ffn2_post'],
                w['ffn2_wg'], w['ffn2_wu'], w['ffn2_wd'])
    return y


def kernel(x_prompt, x_sample, mem_prompt, cache_win_k, cache_win_v, state_hgrn, cache_mem_k, cache_mem_v, ffn1_norm_pre, ffn1_norm_post, ffn1_w_gate, ffn1_w_up, ffn1_w_down, mix_norm_pre, mix_norm_post, w_in, attn_sinks, hgrn_lb_logits, attn_out_gain, hgrn_out_gain, w_out, mem_norm_pre, mem_norm_post, mem_norm_kv, w_mem_q, w_mem_k, w_mem_v, w_mem_o, ffn2_norm_pre, ffn2_norm_post, ffn2_w_gate, ffn2_w_up, ffn2_w_down):
    bp, sp, _ = x_prompt.shape
    bs, ts, _ = x_sample.shape
    mt = mem_prompt.shape[1]
    l = 0

    seg_widths = dict(qa=ATTN_WIDTH, kv=2 * KV_WIDTH, qh=HGRN_KW, fh=HGRN_KW, ih=HGRN_VW, gh=HGRN_VW)
    seg_tiles, start = {}, 0
    for name in ('qa', 'kv', 'qh', 'fh', 'ih', 'gh'):
        n_tiles = seg_widths[name] // PROJ_COL_TILE
        seg_tiles[name] = list(range(start, start + n_tiles))
        start += n_tiles
    w_in_order = sum((seg_tiles[name] for name in ('qa', 'qh', 'ih', 'gh', 'fh', 'kv')), [])
    w = dict(
        ffn1_pre=_row(ffn1_norm_pre[l]), ffn1_post=_row(ffn1_norm_post[l]),
        ffn1_wg=_cast_col_tiles(ffn1_w_gate[l], FF_TILE), ffn1_wu=_cast_col_tiles(ffn1_w_up[l], FF_TILE),
        ffn1_wd=ffn1_w_down[l].astype(BF16).reshape(D_FF // FF_TILE, FF_TILE, D_MODEL),
        mix_pre=_row(mix_norm_pre[l]), mix_post=_row(mix_norm_post[l]),
        w_in=_cast_col_tiles(w_in[l], PROJ_COL_TILE, w_in_order), w_out=w_out[l].astype(BF16),
        mem_pre=_row(mem_norm_pre[l]), mem_post=_row(mem_norm_post[l]),
        w_mem_q=w_mem_q[l].astype(BF16), w_mem_o=w_mem_o[l].astype(BF16),
        ffn2_pre=_row(ffn2_norm_pre[l]), ffn2_post=_row(ffn2_norm_post[l]),
    )
    ffn2_casts = (ffn2_w_gate[l], ffn2_w_up[l], ffn2_w_down[l])
    sinks = attn_sinks[l]
    attn_gain = _row(attn_out_gain[l])
    hgrn_gain = _row(hgrn_out_gain[l])

    mk, mv, mk_b, mv_b = _mem_kv(mem_prompt, _row(mem_norm_kv[l]),
                                 w_mem_k[l].astype(BF16), w_mem_v[l].astype(BF16))
    x1, zb, zf, (wg2, wu2, wd2) = _trunk_front(x_prompt.reshape(bp * sp, D_MODEL), w, ffn2_casts)
    w.update(ffn2_wg=wg2, ffn2_wu=wu2, ffn2_wd=wd2.reshape(D_FF // FF_TILE, FF_TILE, D_MODEL))
    zb3 = zb.reshape(bp, sp, ZB_WIDTH)
    zf3 = zf.reshape(bp, sp, ZF_WIDTH)
    a = _attn_prompt(zb3, zf3, sinks, attn_gain)
    r, p_state = _hgrn_prompt(zb3, zf3, hgrn_lb_logits, hgrn_gain)
    y_p = _trunk_back(x1, a, r, _mem_attn_kernel, mk_b, mv_b, w, bp, 1, 512).reshape(bp, sp, D_MODEL)
    k_off = ZF_KA * KV_WIDTH
    v_off = ZF_VA * KV_WIDTH
    p_wk = zf3[:, sp - WINDOW:, k_off:k_off + KV_WIDTH]
    p_wv = zf3[:, sp - WINDOW:, v_off:v_off + KV_WIDTH]

    x1s, zbs, zfs, _ = _trunk_front(x_sample.reshape(bs * ts, D_MODEL), w)
    a_s, s_wk, s_wv = _attn_sample(zbs, zfs, cache_win_k[l].reshape(bs, WINDOW, KV_WIDTH),
                                   cache_win_v[l].reshape(bs, WINDOW, KV_WIDTH), sinks, attn_gain, ts)
    r_s, s_state = _hgrn_sample(zbs, zfs, state_hgrn[l], hgrn_lb_logits, hgrn_gain, ts)
    mem_rows = cache_mem_k.shape[2] * MEM_HEADS
    y_s = _trunk_back(x1s, a_s, r_s, _mem_attn_interleaved_kernel,
                      cache_mem_k[l].reshape(bs, mem_rows, MEM_HEAD_DIM),
                      cache_mem_v[l].reshape(bs, mem_rows, MEM_HEAD_DIM),
                      w, bs, 8, ts).reshape(bs, ts, D_MODEL)

    kv5 = lambda t, n: t.reshape(1, n, WINDOW, ATTN_KV_HEADS, ATTN_HEAD_DIM)
    mem5 = lambda t: t.reshape(1, bp, mt, MEM_HEADS, MEM_HEAD_DIM)
    return (y_p, y_s, kv5(p_wk, bp), kv5(p_wv, bp), p_state[None], mem5(mk), mem5(mv),
            kv5(s_wk, bs), kv5(s_wv, bs), s_state[None])
```

```python
import functools

import jax
import jax.numpy as jnp
from jax import lax
from jax.experimental import pallas as pl
from jax.experimental.pallas import tpu as pltpu

F32 = jnp.float32
BF16 = jnp.bfloat16

D_MODEL = 2048
D_FF = 5632
ATTN_HEADS = 16
ATTN_KV_HEADS = 4
ATTN_GROUP = ATTN_HEADS // ATTN_KV_HEADS
ATTN_HEAD_DIM = 64
WINDOW = 128
ATTN_WIDTH = ATTN_HEADS * ATTN_HEAD_DIM
KV_WIDTH = ATTN_KV_HEADS * ATTN_HEAD_DIM
HGRN_HEADS = 8
HGRN_DK = 128
HGRN_DV = 128
HGRN_KW = HGRN_HEADS * HGRN_DK
HGRN_VW = HGRN_HEADS * HGRN_DV
IN_PROJ_WIDTH = ATTN_WIDTH + 2 * KV_WIDTH + 2 * HGRN_KW + 2 * HGRN_VW
MEM_HEADS = 4
MEM_HEAD_DIM = 128
MEM_WIDTH = MEM_HEADS * MEM_HEAD_DIM
FFN_RESIDUAL = 0.5
EPS = 1e-6

ZB_WIDTH = ATTN_WIDTH + HGRN_KW + 2 * HGRN_VW
ZF_WIDTH = HGRN_KW + 2 * KV_WIDTH
ZB_QA, ZB_QH, ZB_IH, ZB_GH = 0, 1, 2, 3
ZF_FH = 0
ZF_KA, ZF_VA = 4, 5

VMEM_LIMIT_BYTES = 56 * 1024 * 1024

FFN_TOKEN_TILE = 512
PROJ_TOKEN_TILE = 1024
MIX_TOKEN_TILE = 512
FF_TILE = 512
RING_SLOTS = 4
RING_AHEAD = 2
CAST_AT_STEP = 3
PROJ_COL_TILE = 512
HGRN_CHUNK = 128
HGRN_STEP_CHUNKS = 4
HGRN_SAMPLE_SEQS = 4
ATTN_SAMPLE_SEQS = 8
ATTN_STEP_BLOCKS = 8
MASKED = -1e30

ALIBI_SLOPES = tuple(2.0 ** (-8.0 * (h + 1) / ATTN_HEADS) for h in range(ATTN_HEADS))


def _params(*semantics):
    return pltpu.CompilerParams(dimension_semantics=semantics,
                                vmem_limit_bytes=VMEM_LIMIT_BYTES)


def _rms(x, g):
    return x * lax.rsqrt(jnp.mean(x * x, axis=-1, keepdims=True) + EPS) * g


def _silu(x):
    return x * jax.nn.sigmoid(x)


def _dot(a, b):
    return jnp.dot(a, b, preferred_element_type=F32)


def _dot_nt(a, b):
    return lax.dot_general(a, b, (((1,), (1,)), ((), ())), preferred_element_type=F32)


def _dot_tn(a, b):
    return lax.dot_general(a, b, (((0,), (0,)), ((), ())), preferred_element_type=F32)


def _log2(n):
    assert n & (n - 1) == 0
    return n.bit_length() - 1


def _stream_weight_tiles(n, token_tiles, copies, step):
    i = pl.program_id(0)
    slots, ahead = RING_SLOTS, RING_AHEAD
    assert ahead < slots <= n

    def start(tile):
        for c in copies(tile, tile % slots):
            c.start()

    @pl.when(i == 0)
    def _():
        for tile in range(ahead):
            start(tile)

    for j in range(n):
        nxt = j + ahead
        if nxt < n:
            assert nxt % slots not in {jj % slots for jj in range(j, nxt)}
            start(nxt)
        else:
            assert (nxt - n) % slots not in {jj % slots for jj in range(j, n)}
            pl.when(i + 1 < token_tiles)(functools.partial(start, nxt - n))
        for c in copies(j, j % slots):
            c.wait()
        step(j, j % slots)


def _ffn_kernel(*refs, token_tiles, n_casts):
    m = n_casts
    x_ref, gpre_ref, gpost_ref, wg_hbm, wu_hbm, wd_hbm = refs[:6]
    cast_src = refs[6:6 + m]
    o_ref = refs[6 + m]
    cast_dst = refs[7 + m:7 + 2 * m]
    wg_buf, wu_buf, wd_buf, sem = refs[7 + 2 * m:11 + 2 * m]
    cast_bufs = refs[11 + 2 * m:11 + 4 * m]
    cast_sem = refs[11 + 4 * m] if m else None
    i = pl.program_id(0)
    tf = wg_buf.shape[2]

    def column_tile(w_hbm, tile):
        return w_hbm.at[tile] if len(w_hbm.shape) == 3 else w_hbm.at[:, pl.ds(tile * tf, tf)]

    def copies(tile, slot):
        return (pltpu.make_async_copy(column_tile(wg_hbm, tile), wg_buf.at[slot], sem.at[0, slot]),
                pltpu.make_async_copy(column_tile(wu_hbm, tile), wu_buf.at[slot], sem.at[1, slot]),
                pltpu.make_async_copy(wd_hbm.at[tile], wd_buf.at[slot], sem.at[2, slot]))

    def cast_in(k):
        buf = cast_bufs[2 * k]
        rows = pl.ds(pl.multiple_of(i * buf.shape[0], 16), buf.shape[0])
        return pltpu.make_async_copy(cast_src[k].at[rows, :], buf, cast_sem.at[k, 0])

    def cast_out(k):
        buf = cast_bufs[2 * k + 1]
        rows = pl.ds(pl.multiple_of(i * buf.shape[0], 16), buf.shape[0])
        return pltpu.make_async_copy(buf, cast_dst[k].at[rows, :], cast_sem.at[k, 1])

    for k in range(m):
        cast_in(k).start()

    h = _rms(x_ref[...], gpre_ref[...]).astype(BF16)

    def step(j, slot):
        k = j - CAST_AT_STEP
        if 0 <= k < m:
            cast_in(k).wait()
            cast_bufs[2 * k + 1][...] = cast_bufs[2 * k][...].astype(BF16)
            cast_out(k).start()
        g = _dot(h, wg_buf[slot])
        u = _dot(h, wu_buf[slot])
        part = _dot((_silu(g) * u).astype(BF16), wd_buf[slot])
        if j == 0:
            o_ref[...] = part
        else:
            o_ref[...] += part

    _stream_weight_tiles(wd_hbm.shape[0], token_tiles, copies, step)
    for k in range(m):
        cast_out(k).wait()
    o_ref[...] = x_ref[...] + _rms(o_ref[...], FFN_RESIDUAL * gpost_ref[...])


def _ffn(x, g_pre, g_post, wg, wu, wd, side_casts=()):
    t = x.shape[0]
    tm = FFN_TOKEN_TILE
    tiles = t // tm
    n, tf, _ = wd.shape
    m = len(side_casts)
    assert n >= CAST_AT_STEP + m
    any_space = pl.BlockSpec(memory_space=pl.ANY)
    cast_shapes, cast_scratch = [], []
    for w in side_casts:
        r, c = w.shape
        rows = r // tiles
        assert rows * tiles == r and rows % 16 == 0
        cast_shapes.append(jax.ShapeDtypeStruct((r, c), BF16))
        cast_scratch += [pltpu.VMEM((rows, c), F32), pltpu.VMEM((rows, c), BF16)]
    if m:
        cast_scratch.append(pltpu.SemaphoreType.DMA((m, 2)))
    outs = pl.pallas_call(
        functools.partial(_ffn_kernel, token_tiles=tiles, n_casts=m),
        grid=(tiles,),
        in_specs=[
            pl.BlockSpec((tm, D_MODEL), lambda i: (i, 0)),
            pl.BlockSpec((1, D_MODEL), lambda i: (0, 0)),
            pl.BlockSpec((1, D_MODEL), lambda i: (0, 0)),
            any_space, any_space, any_space,
        ] + [any_space] * m,
        out_specs=[pl.BlockSpec((tm, D_MODEL), lambda i: (i, 0))] + [any_space] * m,
        out_shape=[jax.ShapeDtypeStruct((t, D_MODEL), F32)] + cast_shapes,
        scratch_shapes=[
            pltpu.VMEM((RING_SLOTS, D_MODEL, tf), BF16),
            pltpu.VMEM((RING_SLOTS, D_MODEL, tf), BF16),
            pltpu.VMEM((RING_SLOTS, tf, D_MODEL), BF16),
            pltpu.SemaphoreType.DMA((3, RING_SLOTS)),
        ] + cast_scratch,
        compiler_params=_params("arbitrary"),
        name="ffn_half",
    )(x, g_pre, g_post, wg, wu, wd, *side_casts)
    return tuple(outs)


def _in_proj_kernel(x_ref, g_ref, w_hbm, zb_hbm, zf_hbm, w_buf, sem, stage_b, stage_f, out_sem,
                    *, token_tiles):
    i = pl.program_id(0)
    n, _, tn = w_hbm.shape
    tm = x_ref.shape[0]
    nb = zb_hbm.shape[1] // tn
    rows = pl.ds(pl.multiple_of(i * tm, tm), tm)

    def copies(tile, slot):
        return (pltpu.make_async_copy(w_hbm.at[tile], w_buf.at[slot], sem.at[slot]),)

    def out_copy(j):
        if j < nb:
            s = j % 2
            return pltpu.make_async_copy(stage_b.at[s], zb_hbm.at[rows, pl.ds(j * tn, tn)],
                                         out_sem.at[0, s])
        s = (j - nb) % 2
        return pltpu.make_async_copy(stage_f.at[s], zf_hbm.at[rows, pl.ds((j - nb) * tn, tn)],
                                     out_sem.at[1, s])

    def same_stage(j):
        return range(0, nb) if j < nb else range(nb, n)

    h = _rms(x_ref[...], g_ref[...]).astype(BF16)

    def step(j, slot):
        z = _dot(h, w_buf[slot])
        if j - 2 in same_stage(j):
            out_copy(j - 2).wait()
        if j < nb:
            stage_b[j % 2] = z.astype(BF16)
        else:
            stage_f[(j - nb) % 2] = z
        out_copy(j).start()

    _stream_weight_tiles(n, token_tiles, copies, step)
    for j in range(n):
        if j + 2 not in same_stage(j):
            out_copy(j).wait()


def _in_proj(x, g, w):
    t = x.shape[0]
    tm = min(PROJ_TOKEN_TILE, t)
    n, _, tn = w.shape
    any_space = pl.BlockSpec(memory_space=pl.ANY)
    return pl.pallas_call(
        functools.partial(_in_proj_kernel, token_tiles=t // tm),
        grid=(t // tm,),
        in_specs=[
            pl.BlockSpec((tm, D_MODEL), lambda i: (i, 0)),
            pl.BlockSpec((1, D_MODEL), lambda i: (0, 0)),
            any_space,
        ],
        out_specs=[any_space, any_space],
        out_shape=[jax.ShapeDtypeStruct((t, ZB_WIDTH), BF16),
                   jax.ShapeDtypeStruct((t, ZF_WIDTH), F32)],
        scratch_shapes=[pltpu.VMEM((RING_SLOTS, D_MODEL, tn), BF16),
                        pltpu.SemaphoreType.DMA((RING_SLOTS,)),
                        pltpu.VMEM((2, tm, tn), BF16),
                        pltpu.VMEM((2, tm, tn), F32),
                        pltpu.SemaphoreType.DMA((2, 2))],
        compiler_params=_params("arbitrary"),
        name="in_proj",
    )(x, g, w)


def _attn_prompt_kernel(sink_ref, q_ref, kc_ref, kp_ref, vc_ref, vp_ref, gain_ref, o_ref):
    n = pl.program_id(1)
    nblk = q_ref.shape[0] // WINDOW
    scale = ATTN_HEAD_DIM ** -0.5
    row = lax.broadcasted_iota(jnp.int32, (WINDOW, 2 * WINDOW), 0)
    col = lax.broadcasted_iota(jnp.int32, (WINDOW, 2 * WINDOW), 1)
    dist = row + WINDOW - col
    in_window = (dist >= 0) & (dist < WINDOW)
    gain = gain_ref[...]

    def block(qrows, k, v, valid):
        k = (k * scale).astype(BF16)
        v = v.astype(BF16)
        neg_dist = jnp.where(valid, -dist.astype(F32), MASKED)
        outs = []
        for hd in range(ATTN_HEADS):
            kvh = hd // ATTN_GROUP
            kh = k[:, kvh * ATTN_HEAD_DIM:(kvh + 1) * ATTN_HEAD_DIM]
            vh = v[:, kvh * ATTN_HEAD_DIM:(kvh + 1) * ATTN_HEAD_DIM]
            qh = q_ref[qrows, hd * ATTN_HEAD_DIM:(hd + 1) * ATTN_HEAD_DIM]
            s = _dot_nt(qh, kh) + ALIBI_SLOPES[hd] * neg_dist
            sink = sink_ref[hd]
            m = jnp.maximum(jnp.max(s, axis=-1, keepdims=True), sink)
            p = jnp.exp(s - m)
            denom = jnp.sum(p, axis=-1, keepdims=True) + jnp.exp(sink - m)
            outs.append(_dot(p.astype(BF16), vh) / denom)
        o = jnp.concatenate(outs, axis=1)
        o_ref[qrows, :] = _rms(o, gain).astype(BF16)

    block(slice(0, WINDOW),
          jnp.concatenate([kp_ref[...], kc_ref[0:WINDOW, :]], axis=0),
          jnp.concatenate([vp_ref[...], vc_ref[0:WINDOW, :]], axis=0),
          in_window & ((col >= WINDOW) | (n > 0)))

    def later_block(blk, carry):
        first = pl.multiple_of((blk - 1) * WINDOW, WINDOW)
        kv_rows = pl.ds(first, 2 * WINDOW)
        block(pl.ds(first + WINDOW, WINDOW), kc_ref[kv_rows, :], vc_ref[kv_rows, :], in_window)
        return carry

    lax.fori_loop(1, nblk, later_block, 0)


def _attn_prompt(zb3, zf3, sinks, gain):
    b, s, _ = zb3.shape
    nblk = ATTN_STEP_BLOCKS
    rows = nblk * WINDOW

    def cur(width_idx):
        return lambda bi, n: (bi, n, width_idx)

    def prev(width_idx):
        return lambda bi, n: (bi, jnp.maximum(n * nblk - 1, 0), width_idx)

    return pl.pallas_call(
        _attn_prompt_kernel,
        grid=(b, s // rows),
        in_specs=[
            pl.BlockSpec(memory_space=pltpu.SMEM),
            pl.BlockSpec((None, rows, ATTN_WIDTH), cur(ZB_QA)),
            pl.BlockSpec((None, rows, KV_WIDTH), cur(ZF_KA)),
            pl.BlockSpec((None, WINDOW, KV_WIDTH), prev(ZF_KA)),
            pl.BlockSpec((None, rows, KV_WIDTH), cur(ZF_VA)),
            pl.BlockSpec((None, WINDOW, KV_WIDTH), prev(ZF_VA)),
            pl.BlockSpec((1, ATTN_WIDTH), lambda bi, n: (0, 0)),
        ],
        out_specs=pl.BlockSpec((None, rows, ATTN_WIDTH), lambda bi, n: (bi, n, 0)),
        out_shape=jax.ShapeDtypeStruct((b, s, ATTN_WIDTH), BF16),
        compiler_params=_params("parallel", "arbitrary"),
        name="attn_prompt",
    )(sinks, zb3, zf3, zf3, zf3, zf3, gain)


def _attn_sample_kernel(sink_ref, q_ref, kn_ref, vn_ref, ck_ref, cv_ref, gain_ref,
                        o_ref, wk_ref, wv_ref, *, t_new):
    seqs = ck_ref.shape[0]
    t_bits = _log2(t_new)
    rows = ATTN_GROUP * t_new
    q_all = q_ref[...].astype(F32).reshape(seqs, t_new, ATTN_WIDTH)
    kn = kn_ref[...].reshape(seqs, t_new, KV_WIDTH)
    vn = vn_ref[...].reshape(seqs, t_new, KV_WIDTH)
    ck = ck_ref[...]
    cv = cv_ref[...]
    wk_ref[:, :WINDOW - t_new, :] = ck[:, t_new:, :]
    wk_ref[:, WINDOW - t_new:, :] = kn
    wv_ref[:, :WINDOW - t_new, :] = cv[:, t_new:, :]
    wv_ref[:, WINDOW - t_new:, :] = vn

    r_c = lax.broadcasted_iota(jnp.int32, (rows, WINDOW), 0)
    j_c = lax.broadcasted_iota(jnp.int32, (rows, WINDOW), 1)
    dist_c = (r_c & (t_new - 1)) + WINDOW - j_c
    neg_c = jnp.where(dist_c < WINDOW, -dist_c.astype(F32), MASKED)
    r_n = lax.broadcasted_iota(jnp.int32, (rows, t_new), 0)
    j_n = lax.broadcasted_iota(jnp.int32, (rows, t_new), 1)
    dist_n = (r_n & (t_new - 1)) - j_n
    neg_n = jnp.where(dist_n >= 0, -dist_n.astype(F32), MASKED)
    head_of_row = lax.broadcasted_iota(jnp.int32, (rows, 1), 0) >> t_bits
    scale = ATTN_HEAD_DIM ** -0.5

    outs = [None] * ATTN_HEADS
    for kvh in range(ATTN_KV_HEADS):
        slope = jnp.zeros((rows, 1), F32)
        sink = jnp.zeros((rows, 1), F32)
        for g in range(ATTN_GROUP):
            hd = kvh * ATTN_GROUP + g
            slope = jnp.where(head_of_row == g, ALIBI_SLOPES[hd], slope)
            sink = jnp.where(head_of_row == g, sink_ref[hd], sink)
        q = jnp.concatenate(
            [q_all[:, :, (kvh * ATTN_GROUP + g) * ATTN_HEAD_DIM:(kvh * ATTN_GROUP + g + 1) * ATTN_HEAD_DIM]
             for g in range(ATTN_GROUP)], axis=1).astype(BF16)
        sl = slice(kvh * ATTN_HEAD_DIM, (kvh + 1) * ATTN_HEAD_DIM)
        ck_h = (ck[:, :, sl] * scale).astype(BF16)
        kn_h = (kn[:, :, sl] * scale).astype(BF16)
        cv_h = cv[:, :, sl].astype(BF16)
        vn_h = vn[:, :, sl].astype(BF16)
        s_c = jnp.einsum('gqd,gkd->gqk', q, ck_h, preferred_element_type=F32) + (slope * neg_c)[None]
        s_n = jnp.einsum('gqd,gkd->gqk', q, kn_h, preferred_element_type=F32) + (slope * neg_n)[None]
        m = jnp.maximum(jnp.maximum(jnp.max(s_c, axis=-1, keepdims=True),
                                    jnp.max(s_n, axis=-1, keepdims=True)), sink[None])
        p_c = jnp.exp(s_c - m)
        p_n = jnp.exp(s_n - m)
        denom = (jnp.sum(p_c, axis=-1, keepdims=True) + jnp.sum(p_n, axis=-1, keepdims=True)
                 + jnp.exp(sink[None] - m))
        o = (jnp.einsum('gqk,gkd->gqd', p_c.astype(BF16), cv_h, preferred_element_type=F32)
             + jnp.einsum('gqk,gkd->gqd', p_n.astype(BF16), vn_h, preferred_element_type=F32)) / denom
        for g in range(ATTN_GROUP):
            outs[kvh * ATTN_GROUP + g] = o[:, g * t_new:(g + 1) * t_new, :]
    o_all = jnp.concatenate(outs, axis=2)
    o_ref[...] = _rms(o_all, gain_ref[...][None]).reshape(seqs * t_new, ATTN_WIDTH)


def _attn_sample(zb, zf, cache_k, cache_v, sinks, gain, t_new):
    b = cache_k.shape[0]
    gs = ATTN_SAMPLE_SEQS
    tok = lambda width, idx: pl.BlockSpec((gs * t_new, width), lambda i: (i, idx))
    cache_spec = pl.BlockSpec((gs, WINDOW, KV_WIDTH), lambda i: (i, 0, 0))
    return pl.pallas_call(
        functools.partial(_attn_sample_kernel, t_new=t_new),
        grid=(b // gs,),
        in_specs=[
            pl.BlockSpec(memory_space=pltpu.SMEM),
            tok(ATTN_WIDTH, ZB_QA), tok(KV_WIDTH, ZF_KA), tok(KV_WIDTH, ZF_VA),
            cache_spec, cache_spec,
            pl.BlockSpec((1, ATTN_WIDTH), lambda i: (0, 0)),
        ],
        out_specs=[tok(ATTN_WIDTH, 0), cache_spec, cache_spec],
        out_shape=[
            jax.ShapeDtypeStruct((b * t_new, ATTN_WIDTH), F32),
            jax.ShapeDtypeStruct((b, WINDOW, KV_WIDTH), F32),
            jax.ShapeDtypeStruct((b, WINDOW, KV_WIDTH), F32),
        ],
        compiler_params=_params("parallel"),
        name="attn_sample",
    )(sinks, zb, zf, zf, cache_k, cache_v, gain)


def _lower_bound(lbl_ref):
    logits = lbl_ref[...]
    e = jnp.exp(logits - jnp.max(logits, axis=0, keepdims=True))
    return e[0:1, :] / jnp.sum(e, axis=0, keepdims=True)


def _hgrn_features(q_raw, f_raw, lb):
    sg = jax.nn.sigmoid(f_raw)
    f = lb + (1.0 - lb) * sg
    k = (1.0 - lb) * (1.0 - sg)
    return _silu(q_raw), k, f, jnp.log2(f)


def _cumsum_rows(x, segment):
    row = lax.broadcasted_iota(jnp.int32, x.shape, 0) & (segment - 1)
    shift = 1
    while shift < segment:
        x = x + jnp.where(row >= shift, pltpu.roll(x, shift, axis=0), 0.0)
        shift *= 2
    return x


def _split3(x):
    hi = x.astype(BF16)
    r = x - hi.astype(F32)
    mid = r.astype(BF16)
    lo = (r - mid.astype(F32)).astype(BF16)
    return hi, mid, lo


def _cumsum_rows_mxu(x, tri):
    hi, mid, lo = _split3(x)
    return _dot(tri, hi) + _dot(tri, mid) + _dot(tri, lo)


def _column_broadcast(row_vec, n):
    rows = 16
    hi, mid, lo = _split3(row_vec)
    terms = jnp.concatenate([hi, mid, lo, jnp.zeros((rows - 3, n), BF16)], axis=0)
    ones = (lax.broadcasted_iota(jnp.int32, (rows, n), 0) < 3).astype(BF16)
    return _dot_tn(terms, ones)


def _hgrn_out(o, gain, g_raw):
    return _rms(o, gain) * _silu(g_raw)


def _pair_levels(c):
    t = lax.broadcasted_iota(jnp.int32, (c, c), 0)
    s = lax.broadcasted_iota(jnp.int32, (c, c), 1)
    masks = [t == s]
    half = 1
    while half < c:
        block = 2 * half
        same_block = (t >> _log2(block)) == (s >> _log2(block))
        masks.append(same_block & ((t & (block - 1)) >= half) & ((s & (block - 1)) < half))
        half = block
    return masks


def _pair_factors(q, k, f, log2f, l2):
    c, w = q.shape
    pos = lax.broadcasted_iota(jnp.int32, (c, w), 0) & 3
    e2 = jnp.exp2(jnp.where(pos == 0, pltpu.roll(log2f, c - 1, axis=0),
                            jnp.where(pos == 1, 0.0,
                                      jnp.where(pos == 2, log2f,
                                                log2f + pltpu.roll(log2f, 1, axis=0)))))
    qb, kb = q.astype(BF16), k.astype(BF16)
    e2 = e2.astype(BF16)
    qs = [qb, qb * f.astype(BF16), qb * e2]
    ks = [kb, kb, kb * e2]
    half = 4
    while half < c:
        block = 2 * half
        ref = l2.reshape(c // block, block, w)[:, half - 1:half, :]
        ref = jnp.broadcast_to(ref, (c // block, block, w)).reshape(c, w)
        e = jnp.exp2(-jnp.abs(l2 - ref)).astype(BF16)
        qs.append(qb * e)
        ks.append(kb * e)
        half = block
    return qs, ks


def _hgrn_prompt_kernel(lbl_ref, q_ref, f_ref, i_ref, g_ref, gain_ref, r_ref, s_ref):
    c = HGRN_CHUNK

    @pl.when(pl.program_id(1) == 0)
    def _():
        s_ref[...] = jnp.zeros_like(s_ref)

    lb = _lower_bound(lbl_ref)
    q_all, k_all, f_all, log2f_all = _hgrn_features(q_ref[...].astype(F32), f_ref[...], lb)
    masks = _pair_levels(c)
    tri = (lax.broadcasted_iota(jnp.int32, (c, c), 1)
           <= lax.broadcasted_iota(jnp.int32, (c, c), 0)).astype(BF16)
    gain = gain_ref[...]

    chunks = []
    for ci in range(HGRN_STEP_CHUNKS):
        rows = slice(ci * c, (ci + 1) * c)
        l2_c = _cumsum_rows_mxu(log2f_all[rows], tri)
        q_lv, k_lv = _pair_factors(q_all[rows], k_all[rows], f_all[rows], log2f_all[rows], l2_c)
        l_end = l2_c[c - 1:c, :]
        q_hat = (q_all[rows] * jnp.exp2(l2_c)).astype(BF16)
        k_end = (k_all[rows] * jnp.exp2(l_end - l2_c)).astype(BF16)
        chunks.append((rows, q_lv, k_lv, q_hat, k_end, jnp.exp2(l_end)))

    for rows, q_lv, k_lv, q_hat, k_end, decay_end in chunks:
        for h in range(HGRN_HEADS):
            sl = slice(h * HGRN_DK, (h + 1) * HGRN_DK)
            vb = i_ref[rows, sl]
            a = jnp.zeros((c, c), F32)
            for mask, q_f, k_f in zip(masks, q_lv, k_lv):
                a = jnp.where(mask, _dot_nt(q_f[:, sl], k_f[:, sl]), a)
            s_old = s_ref[h]
            o = _dot(a.astype(BF16), vb) + _dot(q_hat[:, sl], s_old.astype(BF16))
            s_ref[h] = (s_old * _column_broadcast(decay_end[:, sl], HGRN_DK)
                        + _dot_tn(k_end[:, sl], vb))
            r_ref[rows, sl] = _hgrn_out(o, gain, g_ref[rows, sl].astype(F32)).astype(r_ref.dtype)


def _hgrn_prompt(zb3, zf3, lb_logits, gain):
    b, s, _ = zb3.shape
    rows = HGRN_CHUNK * HGRN_STEP_CHUNKS
    seg = lambda idx: pl.BlockSpec((None, rows, HGRN_KW), lambda bi, ci: (bi, ci, idx))
    return pl.pallas_call(
        _hgrn_prompt_kernel,
        grid=(b, s // rows),
        in_specs=[
            pl.BlockSpec(lb_logits.shape, lambda bi, ci: (0, 0)),
            seg(ZB_QH), seg(ZF_FH), seg(ZB_IH), seg(ZB_GH),
            pl.BlockSpec((1, HGRN_DV), lambda bi, ci: (0, 0)),
        ],
        out_specs=[
            pl.BlockSpec((None, rows, HGRN_VW), lambda bi, ci: (bi, ci, 0)),
            pl.BlockSpec((None, HGRN_HEADS, HGRN_DK, HGRN_DV), lambda bi, ci: (bi, 0, 0, 0)),
        ],
        out_shape=[
            jax.ShapeDtypeStruct((b, s, HGRN_VW), BF16),
            jax.ShapeDtypeStruct((b, HGRN_HEADS, HGRN_DK, HGRN_DV), F32),
        ],
        compiler_params=_params("parallel", "arbitrary"),
        name="hgrn_prompt",
    )(lb_logits, zb3, zf3, zb3, zb3, gain)


def _hgrn_sample_kernel(lbl_ref, seg_ref, segt_ref, q_ref, f_ref, i_ref, g_ref, gain_ref, s0_ref,
                        r_ref, s_ref, *, t_new):
    seqs = s0_ref.shape[0]
    n = seqs * t_new
    width = HGRN_KW
    lb = _lower_bound(lbl_ref)
    q_all, k_all, _, log2f_all = _hgrn_features(q_ref[...].astype(F32), f_ref[...], lb)
    l2_all = _cumsum_rows(log2f_all, t_new)
    v_all = i_ref[...].astype(F32)
    g_all = g_ref[...].astype(F32)
    gain = gain_ref[...]

    def row_of_seq(x, t):
        x3 = x.reshape(seqs, t_new, width)
        return jnp.broadcast_to(x3[:, t:t + 1, :], (seqs, t_new, width)).reshape(n, width)

    src = lax.broadcasted_iota(jnp.int32, (n, width), 0) & (t_new - 1)
    w_parts = []
    for t in range(t_new):
        decay = jnp.exp2(jnp.minimum(row_of_seq(l2_all, t) - l2_all, 0.0))
        w_parts.append(jnp.where(src <= t, decay * k_all * row_of_seq(q_all, t), 0.0))
    w = jnp.concatenate(w_parts, axis=0).astype(BF16)
    a_heads = _dot(w, seg_ref[...])
    a_lanes = _dot(a_heads.astype(BF16), segt_ref[...])
    o_rows = []
    for t in range(t_new):
        p = (a_lanes[t * n:(t + 1) * n] * v_all).reshape(seqs, t_new, width)
        o_rows.append(jnp.sum(p, axis=1, keepdims=True))
    o_intra = jnp.concatenate(o_rows, axis=1).reshape(n, width)

    l_end = row_of_seq(l2_all, t_new - 1)
    q_hat = q_all * jnp.exp2(l2_all)
    k_end = k_all * jnp.exp2(l_end - l2_all)
    decay_end = jnp.exp2(l_end)
    for h in range(HGRN_HEADS):
        sl = slice(h * HGRN_DK, (h + 1) * HGRN_DK)
        o_parts = []
        for sq in range(seqs):
            rows = slice(sq * t_new, (sq + 1) * t_new)
            s_old = s0_ref[sq, h]
            o_parts.append(_dot(q_hat[rows, sl].astype(BF16), s_old.astype(BF16)))
            s_ref[sq, h] = (s_old * _column_broadcast(decay_end[sq * t_new:sq * t_new + 1, sl], HGRN_DK)
                            + _dot_tn(k_end[rows, sl].astype(BF16), v_all[rows, sl].astype(BF16)))
        o = o_intra[:, sl] + jnp.concatenate(o_parts, axis=0)
        r_ref[:, sl] = _hgrn_out(o, gain, g_all[:, sl])


def _hgrn_sample(zb, zf, state, lb_logits, gain, t_new):
    b = state.shape[0]
    gs = HGRN_SAMPLE_SEQS
    seg = lambda idx: pl.BlockSpec((gs * t_new, HGRN_KW), lambda i: (i, idx))
    state_spec = pl.BlockSpec((gs, HGRN_HEADS, HGRN_DK, HGRN_DV), lambda i: (i, 0, 0, 0))
    lane_head = jnp.arange(HGRN_KW, dtype=jnp.int32) // HGRN_DK
    head_seg = (lane_head[:, None] == jnp.arange(HGRN_DK, dtype=jnp.int32)[None, :]).astype(BF16)
    return pl.pallas_call(
        functools.partial(_hgrn_sample_kernel, t_new=t_new),
        grid=(b // gs,),
        in_specs=[
            pl.BlockSpec(lb_logits.shape, lambda i: (0, 0)),
            pl.BlockSpec((HGRN_KW, HGRN_DK), lambda i: (0, 0)),
            pl.BlockSpec((HGRN_DK, HGRN_KW), lambda i: (0, 0)),
            seg(ZB_QH), seg(ZF_FH), seg(ZB_IH), seg(ZB_GH),
            pl.BlockSpec((1, HGRN_DV), lambda i: (0, 0)),
            state_spec,
        ],
        out_specs=[seg(0), state_spec],
        out_shape=[
            jax.ShapeDtypeStruct((b * t_new, HGRN_VW), F32),
            jax.ShapeDtypeStruct((b, HGRN_HEADS, HGRN_DK, HGRN_DV), F32),
        ],
        compiler_params=_params("parallel"),
        name="hgrn_sample",
    )(lb_logits, head_seg, head_seg.T, zb, zf, zb, zb, gain, state)


def _mix_out_kernel(x_ref, a_ref, r_ref, wa_ref, wr_ref, gpost_ref, gmem_ref, wq_ref,
                    x2_ref, qm_ref):
    mixed = (_dot(a_ref[...].astype(BF16), wa_ref[...])
             + _dot(r_ref[...].astype(BF16), wr_ref[...]))
    x2 = x_ref[...] + _rms(mixed, gpost_ref[...])
    x2_ref[...] = x2
    qm_ref[...] = _dot(_rms(x2, gmem_ref[...]).astype(BF16), wq_ref[...])


def _mix_out(x, a, r, w_out, g_post, g_mem, w_q):
    t = x.shape[0]
    tm = MIX_TOKEN_TILE
    tok = lambda width: pl.BlockSpec((tm, width), lambda i: (i, 0))
    const = lambda shape, idx=(0, 0): pl.BlockSpec(shape, lambda i: idx)
    return pl.pallas_call(
        _mix_out_kernel,
        grid=(t // tm,),
        in_specs=[
            tok(D_MODEL), tok(ATTN_WIDTH), tok(HGRN_VW),
            const((ATTN_WIDTH, D_MODEL), (0, 0)),
            const((HGRN_VW, D_MODEL), (1, 0)),
            const((1, D_MODEL)), const((1, D_MODEL)),
            const((D_MODEL, MEM_WIDTH)),
        ],
        out_specs=[tok(D_MODEL), tok(MEM_WIDTH)],
        out_shape=[jax.ShapeDtypeStruct((t, D_MODEL), F32),
                   jax.ShapeDtypeStruct((t, MEM_WIDTH), F32)],
        compiler_params=_params("parallel"),
        name="mix_out",
    )(x, a, r, w_out, w_out, g_post, g_mem, w_q)


def _mem_kv_kernel(m_ref, g_ref, wk_ref, wv_ref, k_ref, v_ref, kb_ref, vb_ref):
    h = _rms(m_ref[...], g_ref[...]).astype(BF16)
    k = _dot(h, wk_ref[...])
    v = _dot(h, wv_ref[...])
    k_ref[...] = k
    v_ref[...] = v
    kb_ref[...] = k.astype(BF16)
    vb_ref[...] = v.astype(BF16)


def _mem_kv(mem, g, w_k, w_v):
    b, m, _ = mem.shape
    const = lambda shape: pl.BlockSpec(shape, lambda i: (0, 0))
    out = pl.BlockSpec((None, m, MEM_WIDTH), lambda i: (i, 0, 0))
    return pl.pallas_call(
        _mem_kv_kernel,
        grid=(b,),
        in_specs=[pl.BlockSpec((None, m, D_MODEL), lambda i: (i, 0, 0)), const((1, D_MODEL)),
                  const((D_MODEL, MEM_WIDTH)), const((D_MODEL, MEM_WIDTH))],
        out_specs=[out] * 4,
        out_shape=[jax.ShapeDtypeStruct((b, m, MEM_WIDTH), F32)] * 2
        + [jax.ShapeDtypeStruct((b, m, MEM_WIDTH), BF16)] * 2,
        compiler_params=_params("parallel"),
        name="mem_kv",
    )(mem, g, w_k, w_v)


def _mem_attn_finish(o, x_ref, wo_ref, g_ref, o_ref):
    seqs, tq, _ = x_ref.shape
    y = _dot(o.reshape(seqs * tq, MEM_WIDTH).astype(BF16), wo_ref[...])
    x = x_ref[...].reshape(seqs * tq, D_MODEL)
    o_ref[...] = (x + _rms(y, g_ref[...])).reshape(seqs, tq, D_MODEL)


def _mem_attn_kernel(x_ref, q_ref, mk_ref, mv_ref, wo_ref, g_ref, o_ref):
    scale = MEM_HEAD_DIM ** -0.5
    outs = []
    for h in range(MEM_HEADS):
        sl = slice(h * MEM_HEAD_DIM, (h + 1) * MEM_HEAD_DIM)
        q = q_ref[:, :, sl].astype(BF16)
        s = jnp.einsum('gqd,gkd->gqk', q, mk_ref[:, :, sl], preferred_element_type=F32) * scale
        p = jnp.exp(s - jnp.max(s, axis=-1, keepdims=True))
        denom = jnp.sum(p, axis=-1, keepdims=True)
        outs.append(jnp.einsum('gqk,gkd->gqd', p.astype(BF16), mv_ref[:, :, sl],
                               preferred_element_type=F32) / denom)
    _mem_attn_finish(jnp.concatenate(outs, axis=2), x_ref, wo_ref, g_ref, o_ref)


def _mem_attn_interleaved_kernel(x_ref, q_ref, mk_ref, mv_ref, wo_ref, g_ref, o_ref):
    seqs, tq, _ = q_ref.shape
    rows = MEM_HEADS * tq
    cols = mk_ref.shape[1]
    scale = MEM_HEAD_DIM ** -0.5
    q = jnp.concatenate([q_ref[:, :, h * MEM_HEAD_DIM:(h + 1) * MEM_HEAD_DIM]
                         for h in range(MEM_HEADS)], axis=1).astype(BF16)
    s = jnp.einsum('gqd,gkd->gqk', q, mk_ref[...].astype(BF16), preferred_element_type=F32) * scale
    row_head = lax.broadcasted_iota(jnp.int32, (rows, cols), 0) >> _log2(tq)
    col_head = lax.broadcasted_iota(jnp.int32, (rows, cols), 1) & (MEM_HEADS - 1)
    s = jnp.where((row_head == col_head)[None], s, -jnp.inf)
    p = jnp.exp(s - jnp.max(s, axis=-1, keepdims=True))
    denom = jnp.sum(p, axis=-1, keepdims=True)
    o = jnp.einsum('gqk,gkd->gqd', p.astype(BF16), mv_ref[...].astype(BF16),
                   preferred_element_type=F32) / denom
    o = jnp.concatenate([o[:, h * tq:(h + 1) * tq, :] for h in range(MEM_HEADS)], axis=2)
    _mem_attn_finish(o, x_ref, wo_ref, g_ref, o_ref)


def _mem_attn(body, x3, q3, mem_k, mem_v, w_o, g_post, seqs, tq):
    nseq, slen, _ = x3.shape
    tok = lambda width: pl.BlockSpec((seqs, tq, width), lambda i, j: (i, j, 0))
    mem = pl.BlockSpec((seqs,) + mem_k.shape[1:], lambda i, j: (i, 0, 0))
    const = lambda shape: pl.BlockSpec(shape, lambda i, j: (0, 0))
    return pl.pallas_call(
        body,
        grid=(nseq // seqs, slen // tq),
        in_specs=[tok(D_MODEL), tok(MEM_WIDTH), mem, mem,
                  const((MEM_WIDTH, D_MODEL)), const((1, D_MODEL))],
        out_specs=tok(D_MODEL),
        out_shape=jax.ShapeDtypeStruct((nseq, slen, D_MODEL), F32),
        compiler_params=_params("parallel", "arbitrary"),
        name="mem_attn",
    )(x3, q3, mem_k, mem_v, w_o, g_post)


def _row(g):
    return g.reshape(1, -1)


def _cast_tiles_kernel(order_ref, w_ref, o_ref):
    del order_ref
    o_ref[...] = w_ref[...].astype(BF16)


def _cast_col_tiles(w, tile, order=None):
    k, n = w.shape
    nt = n // tile
    order = jnp.arange(nt, dtype=jnp.int32) if order is None else jnp.asarray(order, jnp.int32)
    return pl.pallas_call(
        _cast_tiles_kernel,
        grid_spec=pltpu.PrefetchScalarGridSpec(
            num_scalar_prefetch=1,
            grid=(nt,),
            in_specs=[pl.BlockSpec((k, tile), lambda j, order_ref: (0, order_ref[j]))],
            out_specs=pl.BlockSpec((None, k, tile), lambda j, order_ref: (j, 0, 0)),
        ),
        out_shape=jax.ShapeDtypeStruct((nt, k, tile), BF16),
        compiler_params=_params("parallel"),
        name="cast_col_tiles",
    )(order, w)


def _trunk_front(x2d, w, side_casts=()):
    x1, *converted = _ffn(x2d, w['ffn1_pre'], w['ffn1_post'], w['ffn1_wg'], w['ffn1_wu'],
                          w['ffn1_wd'], side_casts=side_casts)
    zb, zf = _in_proj(x1, w['mix_pre'], w['w_in'])
    return x1, zb, zf, converted


def _trunk_back(x1, a, r, mem_body, mem_k, mem_v, w, nseq, seqs, tq):
    t = x1.shape[0]
    x2, qm = _mix_out(x1, a.reshape(t, ATTN_WIDTH), r.reshape(t, HGRN_VW), w['w_out'],
                      w['mix_post'], w['mem_pre'], w['w_mem_q'])
    x3 = _mem_attn(mem_body, x2.reshape(nseq, t // nseq, D_MODEL),
                   qm.reshape(nseq, t // nseq, MEM_WIDTH),
                   mem_k, mem_v, w['w_mem_o'], w['mem_post'], seqs, tq)
    (y,) = _ffn(x3.reshape(t, D_MODEL), w['ffn2_pre'], w['ffn2_post'],
                w['ffn2_wg'], w['ffn2_wu'], w['ffn2_wd'])
    return y


def kernel(x_prompt, x_sample, mem_prompt, cache_win_k, cache_win_v, state_hgrn, cache_mem_k, cache_mem_v, ffn1_norm_pre, ffn1_norm_post, ffn1_w_gate, ffn1_w_up, ffn1_w_down, mix_norm_pre, mix_norm_post, w_in, attn_sinks, hgrn_lb_logits, attn_out_gain, hgrn_out_gain, w_out, mem_norm_pre, mem_norm_post, mem_norm_kv, w_mem_q, w_mem_k, w_mem_v, w_mem_o, ffn2_norm_pre, ffn2_norm_post, ffn2_w_gate, ffn2_w_up, ffn2_w_down):
    bp, sp, _ = x_prompt.shape
    bs, ts, _ = x_sample.shape
    mt = mem_prompt.shape[1]
    l = 0

    seg_widths = dict(qa=ATTN_WIDTH, kv=2 * KV_WIDTH, qh=HGRN_KW, fh=HGRN_KW, ih=HGRN_VW, gh=HGRN_VW)
    seg_tiles, start = {}, 0
    for name in ('qa', 'kv', 'qh', 'fh', 'ih', 'gh'):
        n_tiles = seg_widths[name] // PROJ_COL_TILE
        seg_tiles[name] = list(range(start, start + n_tiles))
        start += n_tiles
    w_in_order = sum((seg_tiles[name] for name in ('qa', 'qh', 'ih', 'gh', 'fh', 'kv')), [])
    w = dict(
        ffn1_pre=_row(ffn1_norm_pre[l]), ffn1_post=_row(ffn1_norm_post[l]),
        ffn1_wg=_cast_col_tiles(ffn1_w_gate[l], FF_TILE), ffn1_wu=_cast_col_tiles(ffn1_w_up[l], FF_TILE),
        ffn1_wd=ffn1_w_down[l].astype(BF16).reshape(D_FF // FF_TILE, FF_TILE, D_MODEL),
        mix_pre=_row(mix_norm_pre[l]), mix_post=_row(mix_norm_post[l]),
        w_in=_cast_col_tiles(w_in[l], PROJ_COL_TILE, w_in_order), w_out=w_out[l].astype(BF16),
        mem_pre=_row(mem_norm_pre[l]), mem_post=_row(mem_norm_post[l]),
        w_mem_q=w_mem_q[l].astype(BF16), w_mem_o=w_mem_o[l].astype(BF16),
        ffn2_pre=_row(ffn2_norm_pre[l]), ffn2_post=_row(ffn2_norm_post[l]),
    )
    ffn2_casts = (ffn2_w_gate[l], ffn2_w_up[l], ffn2_w_down[l])
    sinks = attn_sinks[l]
    attn_gain = _row(attn_out_gain[l])
    hgrn_gain = _row(hgrn_out_gain[l])

    mk, mv, mk_b, mv_b = _mem_kv(mem_prompt, _row(mem_norm_kv[l]),
                                 w_mem_k[l].astype(BF16), w_mem_v[l].astype(BF16))
    x1, zb, zf, (wg2, wu2, wd2) = _trunk_front(x_prompt.reshape(bp * sp, D_MODEL), w, ffn2_casts)
    w.update(ffn2_wg=wg2, ffn2_wu=wu2, ffn2_wd=wd2.reshape(D_FF // FF_TILE, FF_TILE, D_MODEL))
    zb3 = zb.reshape(bp, sp, ZB_WIDTH)
    zf3 = zf.reshape(bp, sp, ZF_WIDTH)
    a = _attn_prompt(zb3, zf3, sinks, attn_gain)
    r, p_state = _hgrn_prompt(zb3, zf3, hgrn_lb_logits, hgrn_gain)
    y_p = _trunk_back(x1, a, r, _mem_attn_kernel, mk_b, mv_b, w, bp, 1, 512).reshape(bp, sp, D_MODEL)
    k_off = ZF_KA * KV_WIDTH
    v_off = ZF_VA * KV_WIDTH
    p_wk = zf3[:, sp - WINDOW:, k_off:k_off + KV_WIDTH]
    p_wv = zf3[:, sp - WINDOW:, v_off:v_off + KV_WIDTH]

    x1s, zbs, zfs, _ = _trunk_front(x_sample.reshape(bs * ts, D_MODEL), w)
    a_s, s_wk, s_wv = _attn_sample(zbs, zfs, cache_win_k[l].reshape(bs, WINDOW, KV_WIDTH),
                                   cache_win_v[l].reshape(bs, WINDOW, KV_WIDTH), sinks, attn_gain, ts)
    r_s, s_state = _hgrn_sample(zbs, zfs, state_hgrn[l], hgrn_lb_logits, hgrn_gain, ts)
    mem_rows = cache_mem_k.shape[2] * MEM_HEADS
    y_s = _trunk_back(x1s, a_s, r_s, _mem_attn_interleaved_kernel,
                      cache_mem_k[l].reshape(bs, mem_rows, MEM_HEAD_DIM),
                      cache_mem_v[l].reshape(bs, mem_rows, MEM_HEAD_DIM),
                      w, bs, 8, ts).reshape(bs, ts, D_MODEL)

    kv5 = lambda t, n: t.reshape(1, n, WINDOW, ATTN_KV_HEADS, ATTN_HEAD_DIM)
    mem5 = lambda t: t.reshape(1, bp, mt, MEM_HEADS, MEM_HEAD_DIM)
    return (y_p, y_s, kv5(p_wk, bp), kv5(p_wv, bp), p_state[None], mem5(mk), mem5(mv),
            kv5(s_wk, bs), kv5(s_wv, bs), s_state[None])
```

```python
import functools

import jax
import jax.numpy as jnp
from jax import lax
from jax.experimental import pallas as pl
from jax.experimental.pallas import tpu as pltpu

F32 = jnp.float32
BF16 = jnp.bfloat16

D_MODEL = 2048
D_FF = 5632
ATTN_HEADS = 16
ATTN_KV_HEADS = 4
ATTN_GROUP = ATTN_HEADS // ATTN_KV_HEADS
ATTN_HEAD_DIM = 64
WINDOW = 128
ATTN_WIDTH = ATTN_HEADS * ATTN_HEAD_DIM
KV_WIDTH = ATTN_KV_HEADS * ATTN_HEAD_DIM
HGRN_HEADS = 8
HGRN_DK = 128
HGRN_DV = 128
HGRN_KW = HGRN_HEADS * HGRN_DK
HGRN_VW = HGRN_HEADS * HGRN_DV
IN_PROJ_WIDTH = ATTN_WIDTH + 2 * KV_WIDTH + 2 * HGRN_KW + 2 * HGRN_VW
MEM_HEADS = 4
MEM_HEAD_DIM = 128
MEM_WIDTH = MEM_HEADS * MEM_HEAD_DIM
FFN_RESIDUAL = 0.5
EPS = 1e-6

ZB_WIDTH = ATTN_WIDTH + HGRN_KW + 2 * HGRN_VW
ZF_WIDTH = HGRN_KW + 2 * KV_WIDTH
ZB_QA, ZB_QH, ZB_IH, ZB_GH = 0, 1, 2, 3
ZF_FH = 0
ZF_KA, ZF_VA = 4, 5

VMEM_LIMIT_BYTES = 56 * 1024 * 1024

FFN_TOKEN_TILE = 512
PROJ_TOKEN_TILE = 1024
MIX_TOKEN_TILE = 512
FF_TILE = 512
RING_SLOTS = 4
RING_AHEAD = 2
CAST_AT_STEP = 3
PROJ_COL_TILE = 512
HGRN_CHUNK = 128
HGRN_STEP_CHUNKS = 4
HGRN_SAMPLE_SEQS = 4
ATTN_SAMPLE_SEQS = 8
ATTN_STEP_BLOCKS = 8
MASKED = -1e30

ALIBI_SLOPES = tuple(2.0 ** (-8.0 * (h + 1) / ATTN_HEADS) for h in range(ATTN_HEADS))


def _params(*semantics):
    return pltpu.CompilerParams(dimension_semantics=semantics,
                                vmem_limit_bytes=VMEM_LIMIT_BYTES)


def _rms(x, g):
    return x * lax.rsqrt(jnp.mean(x * x, axis=-1, keepdims=True) + EPS) * g


def _silu(x):
    return x * jax.nn.sigmoid(x)


def _dot(a, b):
    return jnp.dot(a, b, preferred_element_type=F32)


def _dot_nt(a, b):
    return lax.dot_general(a, b, (((1,), (1,)), ((), ())), preferred_element_type=F32)


def _dot_tn(a, b):
    return lax.dot_general(a, b, (((0,), (0,)), ((), ())), preferred_element_type=F32)


def _log2(n):
    assert n & (n - 1) == 0
    return n.bit_length() - 1


def _stream_weight_tiles(n, token_tiles, copies, step):
    i = pl.program_id(0)
    slots, ahead = RING_SLOTS, RING_AHEAD
    assert ahead < slots <= n

    def start(tile):
        for c in copies(tile, tile % slots):
            c.start()

    @pl.when(i == 0)
    def _():
        for tile in range(ahead):
            start(tile)

    for j in range(n):
        nxt = j + ahead
        if nxt < n:
            assert nxt % slots not in {jj % slots for jj in range(j, nxt)}
            start(nxt)
        else:
            assert (nxt - n) % slots not in {jj % slots for jj in range(j, n)}
            pl.when(i + 1 < token_tiles)(functools.partial(start, nxt - n))
        for c in copies(j, j % slots):
            c.wait()
        step(j, j % slots)


def _ffn_kernel(*refs, token_tiles, n_casts):
    m = n_casts
    x_ref, gpre_ref, gpost_ref, wg_hbm, wu_hbm, wd_hbm = refs[:6]
    cast_src = refs[6:6 + m]
    o_ref = refs[6 + m]
    cast_dst = refs[7 + m:7 + 2 * m]
    wg_buf, wu_buf, wd_buf, sem = refs[7 + 2 * m:11 + 2 * m]
    cast_bufs = refs[11 + 2 * m:11 + 4 * m]
    cast_sem = refs[11 + 4 * m] if m else None
    i = pl.program_id(0)
    tf = wg_buf.shape[2]

    def column_tile(w_hbm, tile):
        return w_hbm.at[tile] if len(w_hbm.shape) == 3 else w_hbm.at[:, pl.ds(tile * tf, tf)]

    def copies(tile, slot):
        return (pltpu.make_async_copy(column_tile(wg_hbm, tile), wg_buf.at[slot], sem.at[0, slot]),
                pltpu.make_async_copy(column_tile(wu_hbm, tile), wu_buf.at[slot], sem.at[1, slot]),
                pltpu.make_async_copy(wd_hbm.at[tile], wd_buf.at[slot], sem.at[2, slot]))

    def cast_in(k):
        buf = cast_bufs[2 * k]
        rows = pl.ds(pl.multiple_of(i * buf.shape[0], 16), buf.shape[0])
        return pltpu.make_async_copy(cast_src[k].at[rows, :], buf, cast_sem.at[k, 0])

    def cast_out(k):
        buf = cast_bufs[2 * k + 1]
        rows = pl.ds(pl.multiple_of(i * buf.shape[0], 16), buf.shape[0])
        return pltpu.make_async_copy(buf, cast_dst[k].at[rows, :], cast_sem.at[k, 1])

    for k in range(m):
        cast_in(k).start()

    h = _rms(x_ref[...], gpre_ref[...]).astype(BF16)

    def step(j, slot):
        k = j - CAST_AT_STEP
        if 0 <= k < m:
            cast_in(k).wait()
            cast_bufs[2 * k + 1][...] = cast_bufs[2 * k][...].astype(BF16)
            cast_out(k).start()
        g = _dot(h, wg_buf[slot])
        u = _dot(h, wu_buf[slot])
        part = _dot((_silu(g) * u).astype(BF16), wd_buf[slot])
        if j == 0:
            o_ref[...] = part
        else:
            o_ref[...] += part

    _stream_weight_tiles(wd_hbm.shape[0], token_tiles, copies, step)
    for k in range(m):
        cast_out(k).wait()
    o_ref[...] = x_ref[...] + _rms(o_ref[...], FFN_RESIDUAL * gpost_ref[...])


def _ffn(x, g_pre, g_post, wg, wu, wd, side_casts=()):
    t = x.shape[0]
    tm = FFN_TOKEN_TILE
    tiles = t // tm
    n, tf, _ = wd.shape
    m = len(side_casts)
    assert n >= CAST_AT_STEP + m
    any_space = pl.BlockSpec(memory_space=pl.ANY)
    cast_shapes, cast_scratch = [], []
    for w in side_casts:
        r, c = w.shape
        rows = r // tiles
        assert rows * tiles == r and rows % 16 == 0
        cast_shapes.append(jax.ShapeDtypeStruct((r, c), BF16))
        cast_scratch += [pltpu.VMEM((rows, c), F32), pltpu.VMEM((rows, c), BF16)]
    if m:
        cast_scratch.append(pltpu.SemaphoreType.DMA((m, 2)))
    outs = pl.pallas_call(
        functools.partial(_ffn_kernel, token_tiles=tiles, n_casts=m),
        grid=(tiles,),
        in_specs=[
            pl.BlockSpec((tm, D_MODEL), lambda i: (i, 0)),
            pl.BlockSpec((1, D_MODEL), lambda i: (0, 0)),
            pl.BlockSpec((1, D_MODEL), lambda i: (0, 0)),
            any_space, any_space, any_space,
        ] + [any_space] * m,
        out_specs=[pl.BlockSpec((tm, D_MODEL), lambda i: (i, 0))] + [any_space] * m,
        out_shape=[jax.ShapeDtypeStruct((t, D_MODEL), F32)] + cast_shapes,
        scratch_shapes=[
            pltpu.VMEM((RING_SLOTS, D_MODEL, tf), BF16),
            pltpu.VMEM((RING_SLOTS, D_MODEL, tf), BF16),
            pltpu.VMEM((RING_SLOTS, tf, D_MODEL), BF16),
            pltpu.SemaphoreType.DMA((3, RING_SLOTS)),
        ] + cast_scratch,
        compiler_params=_params("arbitrary"),
        name="ffn_half",
    )(x, g_pre, g_post, wg, wu, wd, *side_casts)
    return tuple(outs)


def _in_proj_kernel(x_ref, g_ref, w_hbm, zb_hbm, zf_hbm, w_buf, sem, stage_b, stage_f, out_sem,
                    *, token_tiles):
    i = pl.program_id(0)
    n, _, tn = w_hbm.shape
    tm = x_ref.shape[0]
    nb = zb_hbm.shape[1] // tn
    rows = pl.ds(pl.multiple_of(i * tm, tm), tm)

    def copies(tile, slot):
        return (pltpu.make_async_copy(w_hbm.at[tile], w_buf.at[slot], sem.at[slot]),)

    def out_copy(j):
        if j < nb:
            s = j % 2
            return pltpu.make_async_copy(stage_b.at[s], zb_hbm.at[rows, pl.ds(j * tn, tn)],
                                         out_sem.at[0, s])
        s = (j - nb) % 2
        return pltpu.make_async_copy(stage_f.at[s], zf_hbm.at[rows, pl.ds((j - nb) * tn, tn)],
                                     out_sem.at[1, s])

    def same_stage(j):
        return range(0, nb) if j < nb else range(nb, n)

    h = _rms(x_ref[...], g_ref[...]).astype(BF16)

    def step(j, slot):
        z = _dot(h, w_buf[slot])
        if j - 2 in same_stage(j):
            out_copy(j - 2).wait()
        if j < nb:
            stage_b[j % 2] = z.astype(BF16)
        else:
            stage_f[(j - nb) % 2] = z
        out_copy(j).start()

    _stream_weight_tiles(n, token_tiles, copies, step)
    for j in range(n):
        if j + 2 not in same_stage(j):
            out_copy(j).wait()


def _in_proj(x, g, w):
    t = x.shape[0]
    tm = min(PROJ_TOKEN_TILE, t)
    n, _, tn = w.shape
    any_space = pl.BlockSpec(memory_space=pl.ANY)
    return pl.pallas_call(
        functools.partial(_in_proj_kernel, token_tiles=t // tm),
        grid=(t // tm,),
        in_specs=[
            pl.BlockSpec((tm, D_MODEL), lambda i: (i, 0)),
            pl.BlockSpec((1, D_MODEL), lambda i: (0, 0)),
            any_space,
        ],
        out_specs=[any_space, any_space],
        out_shape=[jax.ShapeDtypeStruct((t, ZB_WIDTH), BF16),
                   jax.ShapeDtypeStruct((t, ZF_WIDTH), F32)],
        scratch_shapes=[pltpu.VMEM((RING_SLOTS, D_MODEL, tn), BF16),
                        pltpu.SemaphoreType.DMA((RING_SLOTS,)),
                        pltpu.VMEM((2, tm, tn), BF16),
                        pltpu.VMEM((2, tm, tn), F32),
                        pltpu.SemaphoreType.DMA((2, 2))],
        compiler_params=_params("arbitrary"),
        name="in_proj",
    )(x, g, w)


def _attn_prompt_kernel(sink_ref, q_ref, kc_ref, kp_ref, vc_ref, vp_ref, gain_ref, o_ref):
    n = pl.program_id(1)
    nblk = q_ref.shape[0] // WINDOW
    scale = ATTN_HEAD_DIM ** -0.5
    row = lax.broadcasted_iota(jnp.int32, (WINDOW, 2 * WINDOW), 0)
    col = lax.broadcasted_iota(jnp.int32, (WINDOW, 2 * WINDOW), 1)
    dist = row + WINDOW - col
    in_window = (dist >= 0) & (dist < WINDOW)
    gain = gain_ref[...]

    def block(qrows, k, v, valid):
        k = (k * scale).astype(BF16)
        v = v.astype(BF16)
        neg_dist = jnp.where(valid, -dist.astype(F32), MASKED)
        outs = []
        for hd in range(ATTN_HEADS):
            kvh = hd // ATTN_GROUP
            kh = k[:, kvh * ATTN_HEAD_DIM:(kvh + 1) * ATTN_HEAD_DIM]
            vh = v[:, kvh * ATTN_HEAD_DIM:(kvh + 1) * ATTN_HEAD_DIM]
            qh = q_ref[qrows, hd * ATTN_HEAD_DIM:(hd + 1) * ATTN_HEAD_DIM]
            s = _dot_nt(qh, kh) + ALIBI_SLOPES[hd] * neg_dist
            sink = sink_ref[hd]
            m = jnp.maximum(jnp.max(s, axis=-1, keepdims=True), sink)
            p = jnp.exp(s - m)
            denom = jnp.sum(p, axis=-1, keepdims=True) + jnp.exp(sink - m)
            outs.append(_dot(p.astype(BF16), vh) / denom)
        o = jnp.concatenate(outs, axis=1)
        o_ref[qrows, :] = _rms(o, gain).astype(BF16)

    block(slice(0, WINDOW),
          jnp.concatenate([kp_ref[...], kc_ref[0:WINDOW, :]], axis=0),
          jnp.concatenate([vp_ref[...], vc_ref[0:WINDOW, :]], axis=0),
          in_window & ((col >= WINDOW) | (n > 0)))

    def later_block(blk, carry):
        first = pl.multiple_of((blk - 1) * WINDOW, WINDOW)
        kv_rows = pl.ds(first, 2 * WINDOW)
        block(pl.ds(first + WINDOW, WINDOW), kc_ref[kv_rows, :], vc_ref[kv_rows, :], in_window)
        return carry

    lax.fori_loop(1, nblk, later_block, 0)


def _attn_prompt(zb3, zf3, sinks, gain):
    b, s, _ = zb3.shape
    nblk = ATTN_STEP_BLOCKS
    rows = nblk * WINDOW

    def cur(width_idx):
        return lambda bi, n: (bi, n, width_idx)

    def prev(width_idx):
        return lambda bi, n: (bi, jnp.maximum(n * nblk - 1, 0), width_idx)

    return pl.pallas_call(
        _attn_prompt_kernel,
        grid=(b, s // rows),
        in_specs=[
            pl.BlockSpec(memory_space=pltpu.SMEM),
            pl.BlockSpec((None, rows, ATTN_WIDTH), cur(ZB_QA)),
            pl.BlockSpec((None, rows, KV_WIDTH), cur(ZF_KA)),
            pl.BlockSpec((None, WINDOW, KV_WIDTH), prev(ZF_KA)),
            pl.BlockSpec((None, rows, KV_WIDTH), cur(ZF_VA)),
            pl.BlockSpec((None, WINDOW, KV_WIDTH), prev(ZF_VA)),
            pl.BlockSpec((1, ATTN_WIDTH), lambda bi, n: (0, 0)),
        ],
        out_specs=pl.BlockSpec((None, rows, ATTN_WIDTH), lambda bi, n: (bi, n, 0)),
        out_shape=jax.ShapeDtypeStruct((b, s, ATTN_WIDTH), BF16),
        compiler_params=_params("parallel", "arbitrary"),
        name="attn_prompt",
    )(sinks, zb3, zf3, zf3, zf3, zf3, gain)


def _attn_sample_kernel(sink_ref, q_ref, kn_ref, vn_ref, ck_ref, cv_ref, gain_ref,
                        o_ref, wk_ref, wv_ref, *, t_new):
    seqs = ck_ref.shape[0]
    t_bits = _log2(t_new)
    rows = ATTN_GROUP * t_new
    q_all = q_ref[...].astype(F32).reshape(seqs, t_new, ATTN_WIDTH)
    kn = kn_ref[...].reshape(seqs, t_new, KV_WIDTH)
    vn = vn_ref[...].reshape(seqs, t_new, KV_WIDTH)
    ck = ck_ref[...]
    cv = cv_ref[...]
    wk_ref[:, :WINDOW - t_new, :] = ck[:, t_new:, :]
    wk_ref[:, WINDOW - t_new:, :] = kn
    wv_ref[:, :WINDOW - t_new, :] = cv[:, t_new:, :]
    wv_ref[:, WINDOW - t_new:, :] = vn

    r_c = lax.broadcasted_iota(jnp.int32, (rows, WINDOW), 0)
    j_c = lax.broadcasted_iota(jnp.int32, (rows, WINDOW), 1)
    dist_c = (r_c & (t_new - 1)) + WINDOW - j_c
    neg_c = jnp.where(dist_c < WINDOW, -dist_c.astype(F32), MASKED)
    r_n = lax.broadcasted_iota(jnp.int32, (rows, t_new), 0)
    j_n = lax.broadcasted_iota(jnp.int32, (rows, t_new), 1)
    dist_n = (r_n & (t_new - 1)) - j_n
    neg_n = jnp.where(dist_n >= 0, -dist_n.astype(F32), MASKED)
    head_of_row = lax.broadcasted_iota(jnp.int32, (rows, 1), 0) >> t_bits
    scale = ATTN_HEAD_DIM ** -0.5

    outs = [None] * ATTN_HEADS
    for kvh in range(ATTN_KV_HEADS):
        slope = jnp.zeros((rows, 1), F32)
        sink = jnp.zeros((rows, 1), F32)
        for g in range(ATTN_GROUP):
            hd = kvh * ATTN_GROUP + g
            slope = jnp.where(head_of_row == g, ALIBI_SLOPES[hd], slope)
            sink = jnp.where(head_of_row == g, sink_ref[hd], sink)
        q = jnp.concatenate(
            [q_all[:, :, (kvh * ATTN_GROUP + g) * ATTN_HEAD_DIM:(kvh * ATTN_GROUP + g + 1) * ATTN_HEAD_DIM]
             for g in range(ATTN_GROUP)], axis=1).astype(BF16)
        sl = slice(kvh * ATTN_HEAD_DIM, (kvh + 1) * ATTN_HEAD_DIM)
        ck_h = (ck[:, :, sl] * scale).astype(BF16)
        kn_h = (kn[:, :, sl] * scale).astype(BF16)
        cv_h = cv[:, :, sl].astype(BF16)
        vn_h = vn[:, :, sl].astype(BF16)
        s_c = jnp.einsum('gqd,gkd->gqk', q, ck_h, preferred_element_type=F32) + (slope * neg_c)[None]
        s_n = jnp.einsum('gqd,gkd->gqk', q, kn_h, preferred_element_type=F32) + (slope * neg_n)[None]
        m = jnp.maximum(jnp.maximum(jnp.max(s_c, axis=-1, keepdims=True),
                                    jnp.max(s_n, axis=-1, keepdims=True)), sink[None])
        p_c = jnp.exp(s_c - m)
        p_n = jnp.exp(s_n - m)
        denom = (jnp.sum(p_c, axis=-1, keepdims=True) + jnp.sum(p_n, axis=-1, keepdims=True)
                 + jnp.exp(sink[None] - m))
        o = (jnp.einsum('gqk,gkd->gqd', p_c.astype(BF16), cv_h, preferred_element_type=F32)
             + jnp.einsum('gqk,gkd->gqd', p_n.astype(BF16), vn_h, preferred_element_type=F32)) / denom
        for g in range(ATTN_GROUP):
            outs[kvh * ATTN_GROUP + g] = o[:, g * t_new:(g + 1) * t_new, :]
    o_all = jnp.concatenate(outs, axis=2)
    o_ref[...] = _rms(o_all, gain_ref[...][None]).reshape(seqs * t_new, ATTN_WIDTH)


def _attn_sample(zb, zf, cache_k, cache_v, sinks, gain, t_new):
    b = cache_k.shape[0]
    gs = ATTN_SAMPLE_SEQS
    tok = lambda width, idx: pl.BlockSpec((gs * t_new, width), lambda i: (i, idx))
    cache_spec = pl.BlockSpec((gs, WINDOW, KV_WIDTH), lambda i: (i, 0, 0))
    return pl.pallas_call(
        functools.partial(_attn_sample_kernel, t_new=t_new),
        grid=(b // gs,),
        in_specs=[
            pl.BlockSpec(memory_space=pltpu.SMEM),
            tok(ATTN_WIDTH, ZB_QA), tok(KV_WIDTH, ZF_KA), tok(KV_WIDTH, ZF_VA),
            cache_spec, cache_spec,
            pl.BlockSpec((1, ATTN_WIDTH), lambda i: (0, 0)),
        ],
        out_specs=[tok(ATTN_WIDTH, 0), cache_spec, cache_spec],
        out_shape=[
            jax.ShapeDtypeStruct((b * t_new, ATTN_WIDTH), F32),
            jax.ShapeDtypeStruct((b, WINDOW, KV_WIDTH), F32),
            jax.ShapeDtypeStruct((b, WINDOW, KV_WIDTH), F32),
        ],
        compiler_params=_params("parallel"),
        name="attn_sample",
    )(sinks, zb, zf, zf, cache_k, cache_v, gain)


def _lower_bound(lbl_ref):
    logits = lbl_ref[...]
    e = jnp.exp(logits - jnp.max(logits, axis=0, keepdims=True))
    return e[0:1, :] / jnp.sum(e, axis=0, keepdims=True)


def _hgrn_features(q_raw, f_raw, lb):
    sg = jax.nn.sigmoid(f_raw)
    f = lb + (1.0 - lb) * sg
    k = (1.0 - lb) * (1.0 - sg)
    return _silu(q_raw), k, f, jnp.log2(f)


def _cumsum_rows(x, segment):
    row = lax.broadcasted_iota(jnp.int32, x.shape, 0) & (segment - 1)
    shift = 1
    while shift < segment:
        x = x + jnp.where(row >= shift, pltpu.roll(x, shift, axis=0), 0.0)
        shift *= 2
    return x


def _split3(x):
    hi = x.astype(BF16)
    r = x - hi.astype(F32)
    mid = r.astype(BF16)
    lo = (r - mid.astype(F32)).astype(BF16)
    return hi, mid, lo


def _cumsum_rows_mxu(x, tri):
    hi, mid, lo = _split3(x)
    return _dot(tri, hi) + _dot(tri, mid) + _dot(tri, lo)


def _column_broadcast(row_vec, n):
    rows = 16
    hi, mid, lo = _split3(row_vec)
    terms = jnp.concatenate([hi, mid, lo, jnp.zeros((rows - 3, n), BF16)], axis=0)
    ones = (lax.broadcasted_iota(jnp.int32, (rows, n), 0) < 3).astype(BF16)
    return _dot_tn(terms, ones)


def _hgrn_out(o, gain, g_raw):
    return _rms(o, gain) * _silu(g_raw)


def _pair_levels(c):
    t = lax.broadcasted_iota(jnp.int32, (c, c), 0)
    s = lax.broadcasted_iota(jnp.int32, (c, c), 1)
    masks = [t == s]
    half = 1
    while half < c:
        block = 2 * half
        same_block = (t >> _log2(block)) == (s >> _log2(block))
        masks.append(same_block & ((t & (block - 1)) >= half) & ((s & (block - 1)) < half))
        half = block
    return masks


def _pair_factors(q, k, f, log2f, l2):
    c, w = q.shape
    pos = lax.broadcasted_iota(jnp.int32, (c, w), 0) & 3
    e2 = jnp.exp2(jnp.where(pos == 0, pltpu.roll(log2f, c - 1, axis=0),
                            jnp.where(pos == 1, 0.0,
                                      jnp.where(pos == 2, log2f,
                                                log2f + pltpu.roll(log2f, 1, axis=0)))))
    qb, kb = q.astype(BF16), k.astype(BF16)
    e2 = e2.astype(BF16)
    qs = [qb, qb * f.astype(BF16), qb * e2]
    ks = [kb, kb, kb * e2]
    half = 4
    while half < c:
        block = 2 * half
        ref = l2.reshape(c // block, block, w)[:, half - 1:half, :]
        ref = jnp.broadcast_to(ref, (c // block, block, w)).reshape(c, w)
        e = jnp.exp2(-jnp.abs(l2 - ref)).astype(BF16)
        qs.append(qb * e)
        ks.append(kb * e)
        half = block
    return qs, ks


def _hgrn_prompt_kernel(lbl_ref, q_ref, f_ref, i_ref, g_ref, gain_ref, r_ref, s_ref):
    c = HGRN_CHUNK

    @pl.when(pl.program_id(1) == 0)
    def _():
        s_ref[...] = jnp.zeros_like(s_ref)

    lb = _lower_bound(lbl_ref)
    q_all, k_all, f_all, log2f_all = _hgrn_features(q_ref[...].astype(F32), f_ref[...], lb)
    masks = _pair_levels(c)
    tri = (lax.broadcasted_iota(jnp.int32, (c, c), 1)
           <= lax.broadcasted_iota(jnp.int32, (c, c), 0)).astype(BF16)
    gain = gain_ref[...]

    chunks = []
    for ci in range(HGRN_STEP_CHUNKS):
        rows = slice(ci * c, (ci + 1) * c)
        l2_c = _cumsum_rows_mxu(log2f_all[rows], tri)
        q_lv, k_lv = _pair_factors(q_all[rows], k_all[rows], f_all[rows], log2f_all[rows], l2_c)
        l_end = l2_c[c - 1:c, :]
        q_hat = (q_all[rows] * jnp.exp2(l2_c)).astype(BF16)
        k_end = (k_all[rows] * jnp.exp2(l_end - l2_c)).astype(BF16)
        chunks.append((rows, q_lv, k_lv, q_hat, k_end, jnp.exp2(l_end)))

    for rows, q_lv, k_lv, q_hat, k_end, decay_end in chunks:
        for h in range(HGRN_HEADS):
            sl = slice(h * HGRN_DK, (h + 1) * HGRN_DK)
            vb = i_ref[rows, sl]
            a = jnp.zeros((c, c), F32)
            for mask, q_f, k_f in zip(masks, q_lv, k_lv):
                a = jnp.where(mask, _dot_nt(q_f[:, sl], k_f[:, sl]), a)
            s_old = s_ref[h]
            o = _dot(a.astype(BF16), vb) + _dot(q_hat[:, sl], s_old.astype(BF16))
            s_ref[h] = (s_old * _column_broadcast(decay_end[:, sl], HGRN_DK)
                        + _dot_tn(k_end[:, sl], vb))
            r_ref[rows, sl] = _hgrn_out(o, gain, g_ref[rows, sl].astype(F32)).astype(r_ref.dtype)


def _hgrn_prompt(zb3, zf3, lb_logits, gain):
    b, s, _ = zb3.shape
    rows = HGRN_CHUNK * HGRN_STEP_CHUNKS
    seg = lambda idx: pl.BlockSpec((None, rows, HGRN_KW), lambda bi, ci: (bi, ci, idx))
    return pl.pallas_call(
        _hgrn_prompt_kernel,
        grid=(b, s // rows),
        in_specs=[
            pl.BlockSpec(lb_logits.shape, lambda bi, ci: (0, 0)),
            seg(ZB_QH), seg(ZF_FH), seg(ZB_IH), seg(ZB_GH),
            pl.BlockSpec((1, HGRN_DV), lambda bi, ci: (0, 0)),
        ],
        out_specs=[
            pl.BlockSpec((None, rows, HGRN_VW), lambda bi, ci: (bi, ci, 0)),
            pl.BlockSpec((None, HGRN_HEADS, HGRN_DK, HGRN_DV), lambda bi, ci: (bi, 0, 0, 0)),
        ],
        out_shape=[
            jax.ShapeDtypeStruct((b, s, HGRN_VW), BF16),
            jax.ShapeDtypeStruct((b, HGRN_HEADS, HGRN_DK, HGRN_DV), F32),
        ],
        compiler_params=_params("parallel", "arbitrary"),
        name="hgrn_prompt",
    )(lb_logits, zb3, zf3, zb3, zb3, gain)


def _hgrn_sample_kernel(lbl_ref, seg_ref, segt_ref, q_ref, f_ref, i_ref, g_ref, gain_ref, s0_ref,
                        r_ref, s_ref, *, t_new):
    seqs = s0_ref.shape[0]
    n = seqs * t_new
    width = HGRN_KW
    lb = _lower_bound(lbl_ref)
    q_all, k_all, _, log2f_all = _hgrn_features(q_ref[...].astype(F32), f_ref[...], lb)
    l2_all = _cumsum_rows(log2f_all, t_new)
    v_all = i_ref[...].astype(F32)
    g_all = g_ref[...].astype(F32)
    gain = gain_ref[...]

    def row_of_seq(x, t):
        x3 = x.reshape(seqs, t_new, width)
        return jnp.broadcast_to(x3[:, t:t + 1, :], (seqs, t_new, width)).reshape(n, width)

    src = lax.broadcasted_iota(jnp.int32, (n, width), 0) & (t_new - 1)
    w_parts = []
    for t in range(t_new):
        decay = jnp.exp2(jnp.minimum(row_of_seq(l2_all, t) - l2_all, 0.0))
        w_parts.append(jnp.where(src <= t, decay * k_all * row_of_seq(q_all, t), 0.0))
    w = jnp.concatenate(w_parts, axis=0).astype(BF16)
    a_heads = _dot(w, seg_ref[...])
    a_lanes = _dot(a_heads.astype(BF16), segt_ref[...])
    o_rows = []
    for t in range(t_new):
        p = (a_lanes[t * n:(t + 1) * n] * v_all).reshape(seqs, t_new, width)
        o_rows.append(jnp.sum(p, axis=1, keepdims=True))
    o_intra = jnp.concatenate(o_rows, axis=1).reshape(n, width)

    l_end = row_of_seq(l2_all, t_new - 1)
    q_hat = q_all * jnp.exp2(l2_all)
    k_end = k_all * jnp.exp2(l_end - l2_all)
    decay_end = jnp.exp2(l_end)
    for h in range(HGRN_HEADS):
        sl = slice(h * HGRN_DK, (h + 1) * HGRN_DK)
        o_parts = []
        for sq in range(seqs):
            rows = slice(sq * t_new, (sq + 1) * t_new)
            s_old = s0_ref[sq, h]
            o_parts.append(_dot(q_hat[rows, sl].astype(BF16), s_old.astype(BF16)))
            s_ref[sq, h] = (s_old * _column_broadcast(decay_end[sq * t_new:sq * t_new + 1, sl], HGRN_DK)
                            + _dot_tn(k_end[rows, sl].astype(BF16), v_all[rows, sl].astype(BF16)))
        o = o_intra[:, sl] + jnp.concatenate(o_parts, axis=0)
        r_ref[:, sl] = _hgrn_out(o, gain, g_all[:, sl])


def _hgrn_sample(zb, zf, state, lb_logits, gain, t_new):
    b = state.shape[0]
    gs = HGRN_SAMPLE_SEQS
    seg = lambda idx: pl.BlockSpec((gs * t_new, HGRN_KW), lambda i: (i, idx))
    state_spec = pl.BlockSpec((gs, HGRN_HEADS, HGRN_DK, HGRN_DV), lambda i: (i, 0, 0, 0))
    lane_head = jnp.arange(HGRN_KW, dtype=jnp.int32) // HGRN_DK
    head_seg = (lane_head[:, None] == jnp.arange(HGRN_DK, dtype=jnp.int32)[None, :]).astype(BF16)
    return pl.pallas_call(
        functools.partial(_hgrn_sample_kernel, t_new=t_new),
        grid=(b // gs,),
        in_specs=[
            pl.BlockSpec(lb_logits.shape, lambda i: (0, 0)),
            pl.BlockSpec((HGRN_KW, HGRN_DK), lambda i: (0, 0)),
            pl.BlockSpec((HGRN_DK, HGRN_KW), lambda i: (0, 0)),
            seg(ZB_QH), seg(ZF_FH), seg(ZB_IH), seg(ZB_GH),
            pl.BlockSpec((1, HGRN_DV), lambda i: (0, 0)),
            state_spec,
        ],
        out_specs=[seg(0), state_spec],
        out_shape=[
            jax.ShapeDtypeStruct((b * t_new, HGRN_VW), F32),
            jax.ShapeDtypeStruct((b, HGRN_HEADS, HGRN_DK, HGRN_DV), F32),
        ],
        compiler_params=_params("parallel"),
        name="hgrn_sample",
    )(lb_logits, head_seg, head_seg.T, zb, zf, zb, zb, gain, state)


def _mix_out_kernel(x_ref, a_ref, r_ref, wa_ref, wr_ref, gpost_ref, gmem_ref, wq_ref,
                    x2_ref, qm_ref):
    mixed = (_dot(a_ref[...].astype(BF16), wa_ref[...])
             + _dot(r_ref[...].astype(BF16), wr_ref[...]))
    x2 = x_ref[...] + _rms(mixed, gpost_ref[...])
    x2_ref[...] = x2
    qm_ref[...] = _dot(_rms(x2, gmem_ref[...]).astype(BF16), wq_ref[...])


def _mix_out(x, a, r, w_out, g_post, g_mem, w_q):
    t = x.shape[0]
    tm = MIX_TOKEN_TILE
    tok = lambda width: pl.BlockSpec((tm, width), lambda i: (i, 0))
    const = lambda shape, idx=(0, 0): pl.BlockSpec(shape, lambda i: idx)
    return pl.pallas_call(
        _mix_out_kernel,
        grid=(t // tm,),
        in_specs=[
            tok(D_MODEL), tok(ATTN_WIDTH), tok(HGRN_VW),
            const((ATTN_WIDTH, D_MODEL), (0, 0)),
            const((HGRN_VW, D_MODEL), (1, 0)),
            const((1, D_MODEL)), const((1, D_MODEL)),
            const((D_MODEL, MEM_WIDTH)),
        ],
        out_specs=[tok(D_MODEL), tok(MEM_WIDTH)],
        out_shape=[jax.ShapeDtypeStruct((t, D_MODEL), F32),
                   jax.ShapeDtypeStruct((t, MEM_WIDTH), F32)],
        compiler_params=_params("parallel"),
        name="mix_out",
    )(x, a, r, w_out, w_out, g_post, g_mem, w_q)


def _mem_kv_kernel(m_ref, g_ref, wk_ref, wv_ref, k_ref, v_ref, kb_ref, vb_ref):
    h = _rms(m_ref[...], g_ref[...]).astype(BF16)
    k = _dot(h, wk_ref[...])
    v = _dot(h, wv_ref[...])
    k_ref[...] = k
    v_ref[...] = v
    kb_ref[...] = k.astype(BF16)
    vb_ref[...] = v.astype(BF16)


def _mem_kv(mem, g, w_k, w_v):
    b, m, _ = mem.shape
    const = lambda shape: pl.BlockSpec(shape, lambda i: (0, 0))
    out = pl.BlockSpec((None, m, MEM_WIDTH), lambda i: (i, 0, 0))
    return pl.pallas_call(
        _mem_kv_kernel,
        grid=(b,),
        in_specs=[pl.BlockSpec((None, m, D_MODEL), lambda i: (i, 0, 0)), const((1, D_MODEL)),
                  const((D_MODEL, MEM_WIDTH)), const((D_MODEL, MEM_WIDTH))],
        out_specs=[out] * 4,
        out_shape=[jax.ShapeDtypeStruct((b, m, MEM_WIDTH), F32)] * 2
        + [jax.ShapeDtypeStruct((b, m, MEM_WIDTH), BF16)] * 2,
        compiler_params=_params("parallel"),
        name="mem_kv",
    )(mem, g, w_k, w_v)


def _mem_attn_finish(o, x_ref, wo_ref, g_ref, o_ref):
    seqs, tq, _ = x_ref.shape
    y = _dot(o.reshape(seqs * tq, MEM_WIDTH).astype(BF16), wo_ref[...])
    x = x_ref[...].reshape(seqs * tq, D_MODEL)
    o_ref[...] = (x + _rms(y, g_ref[...])).reshape(seqs, tq, D_MODEL)


def _mem_attn_kernel(x_ref, q_ref, mk_ref, mv_ref, wo_ref, g_ref, o_ref):
    scale = MEM_HEAD_DIM ** -0.5
    outs = []
    for h in range(MEM_HEADS):
        sl = slice(h * MEM_HEAD_DIM, (h + 1) * MEM_HEAD_DIM)
        q = q_ref[:, :, sl].astype(BF16)
        s = jnp.einsum('gqd,gkd->gqk', q, mk_ref[:, :, sl], preferred_element_type=F32) * scale
        p = jnp.exp(s - jnp.max(s, axis=-1, keepdims=True))
        denom = jnp.sum(p, axis=-1, keepdims=True)
        outs.append(jnp.einsum('gqk,gkd->gqd', p.astype(BF16), mv_ref[:, :, sl],
                               preferred_element_type=F32) / denom)
    _mem_attn_finish(jnp.concatenate(outs, axis=2), x_ref, wo_ref, g_ref, o_ref)


def _mem_attn_interleaved_kernel(x_ref, q_ref, mk_ref, mv_ref, wo_ref, g_ref, o_ref):
    seqs, tq, _ = q_ref.shape
    rows = MEM_HEADS * tq
    cols = mk_ref.shape[1]
    scale = MEM_HEAD_DIM ** -0.5
    q = jnp.concatenate([q_ref[:, :, h * MEM_HEAD_DIM:(h + 1) * MEM_HEAD_DIM]
                         for h in range(MEM_HEADS)], axis=1).astype(BF16)
    s = jnp.einsum('gqd,gkd->gqk', q, mk_ref[...].astype(BF16), preferred_element_type=F32) * scale
    row_head = lax.broadcasted_iota(jnp.int32, (rows, cols), 0) >> _log2(tq)
    col_head = lax.broadcasted_iota(jnp.int32, (rows, cols), 1) & (MEM_HEADS - 1)
    s = jnp.where((row_head == col_head)[None], s, -jnp.inf)
    p = jnp.exp(s - jnp.max(s, axis=-1, keepdims=True))
    denom = jnp.sum(p, axis=-1, keepdims=True)
    o = jnp.einsum('gqk,gkd->gqd', p.astype(BF16), mv_ref[...].astype(BF16),
                   preferred_element_type=F32) / denom
    o = jnp.concatenate([o[:, h * tq:(h + 1) * tq, :] for h in range(MEM_HEADS)], axis=2)
    _mem_attn_finish(o, x_ref, wo_ref, g_ref, o_ref)


def _mem_attn(body, x3, q3, mem_k, mem_v, w_o, g_post, seqs, tq):
    nseq, slen, _ = x3.shape
    tok = lambda width: pl.BlockSpec((seqs, tq, width), lambda i, j: (i, j, 0))
    mem = pl.BlockSpec((seqs,) + mem_k.shape[1:], lambda i, j: (i, 0, 0))
    const = lambda shape: pl.BlockSpec(shape, lambda i, j: (0, 0))
    return pl.pallas_call(
        body,
        grid=(nseq // seqs, slen // tq),
        in_specs=[tok(D_MODEL), tok(MEM_WIDTH), mem, mem,
                  const((MEM_WIDTH, D_MODEL)), const((1, D_MODEL))],
        out_specs=tok(D_MODEL),
        out_shape=jax.ShapeDtypeStruct((nseq, slen, D_MODEL), F32),
        compiler_params=_params("parallel", "arbitrary"),
        name="mem_attn",
    )(x3, q3, mem_k, mem_v, w_o, g_post)


def _row(g):
    return g.reshape(1, -1)


def _cast_tiles_kernel(order_ref, w_ref, o_ref):
    del order_ref
    o_ref[...] = w_ref[...].astype(BF16)


def _cast_col_tiles(w, tile, order=None):
    k, n = w.shape
    nt = n // tile
    order = jnp.arange(nt, dtype=jnp.int32) if order is None else jnp.asarray(order, jnp.int32)
    return pl.pallas_call(
        _cast_tiles_kernel,
        grid_spec=pltpu.PrefetchScalarGridSpec(
            num_scalar_prefetch=1,
            grid=(nt,),
            in_specs=[pl.BlockSpec((k, tile), lambda j, order_ref: (0, order_ref[j]))],
            out_specs=pl.BlockSpec((None, k, tile), lambda j, order_ref: (j, 0, 0)),
        ),
        out_shape=jax.ShapeDtypeStruct((nt, k, tile), BF16),
        compiler_params=_params("parallel"),
        name="cast_col_tiles",
    )(order, w)


def _trunk_front(x2d, w, side_casts=()):
    x1, *converted = _ffn(x2d, w['ffn1_pre'], w['ffn1_post'], w['ffn1_wg'], w['ffn1_wu'],
                          w['ffn1_wd'], side_casts=side_casts)
    zb, zf = _in_proj(x1, w['mix_pre'], w['w_in'])
    return x1, zb, zf, converted


def _trunk_back(x1, a, r, mem_body, mem_k, mem_v, w, nseq, seqs, tq):
    t = x1.shape[0]
    x2, qm = _mix_out(x1, a.reshape(t, ATTN_WIDTH), r.reshape(t, HGRN_VW), w['w_out'],
                      w['mix_post'], w['mem_pre'], w['w_mem_q'])
    x3 = _mem_attn(mem_body, x2.reshape(nseq, t // nseq, D_MODEL),
                   qm.reshape(nseq, t // nseq, MEM_WIDTH),
                   mem_k, mem_v, w['w_mem_o'], w['mem_post'], seqs, tq)
    (y,) = _ffn(x3.reshape(t, D_MODEL), w['ffn2_pre'], w['ffn2_post'],
                w['ffn2_wg'], w['ffn2_wu'], w['ffn2_wd'])
    return y


def kernel(x_prompt, x_sample, mem_prompt, cache_win_k, cache_win_v, state_hgrn, cache_mem_k, cache_mem_v, ffn1_norm_pre, ffn1_norm_post, ffn1_w_gate, ffn1_w_up, ffn1_w_down, mix_norm_pre, mix_norm_post, w_in, attn_sinks, hgrn_lb_logits, attn_out_gain, hgrn_out_gain, w_out, mem_norm_pre, mem_norm_post, mem_norm_kv, w_mem_q, w_mem_k, w_mem_v, w_mem_o, ffn2_norm_pre, ffn2_norm_post, ffn2_w_gate, ffn2_w_up, ffn2_w_down):
    bp, sp, _ = x_prompt.shape
    bs, ts, _ = x_sample.shape
    mt = mem_prompt.shape[1]
    assert w_in.shape[0] == 1 and hgrn_lb_logits.shape[0] == 2, "one layer (depth 1) is implemented"
    assert (bp * sp) % PROJ_TOKEN_TILE == 0 and (bs * ts) % FFN_TOKEN_TILE == 0
    assert sp % (ATTN_STEP_BLOCKS * WINDOW) == 0 and sp % (HGRN_STEP_CHUNKS * HGRN_CHUNK) == 0
    assert bs % ATTN_SAMPLE_SEQS == 0 and bs % HGRN_SAMPLE_SEQS == 0 and bs % 8 == 0
    l = 0

    seg_widths = dict(qa=ATTN_WIDTH, kv=2 * KV_WIDTH, qh=HGRN_KW, fh=HGRN_KW, ih=HGRN_VW, gh=HGRN_VW)
    seg_tiles, start = {}, 0
    for name in ('qa', 'kv', 'qh', 'fh', 'ih', 'gh'):
        n_tiles = seg_widths[name] // PROJ_COL_TILE
        seg_tiles[name] = list(range(start, start + n_tiles))
        start += n_tiles
    w_in_order = sum((seg_tiles[name] for name in ('qa', 'qh', 'ih', 'gh', 'fh', 'kv')), [])
    w = dict(
        ffn1_pre=_row(ffn1_norm_pre[l]), ffn1_post=_row(ffn1_norm_post[l]),
        ffn1_wg=_cast_col_tiles(ffn1_w_gate[l], FF_TILE), ffn1_wu=_cast_col_tiles(ffn1_w_up[l], FF_TILE),
        ffn1_wd=ffn1_w_down[l].astype(BF16).reshape(D_FF // FF_TILE, FF_TILE, D_MODEL),
        mix_pre=_row(mix_norm_pre[l]), mix_post=_row(mix_norm_post[l]),
        w_in=_cast_col_tiles(w_in[l], PROJ_COL_TILE, w_in_order), w_out=w_out[l].astype(BF16),
        mem_pre=_row(mem_norm_pre[l]), mem_post=_row(mem_norm_post[l]),
        w_mem_q=w_mem_q[l].astype(BF16), w_mem_o=w_mem_o[l].astype(BF16),
        ffn2_pre=_row(ffn2_norm_pre[l]), ffn2_post=_row(ffn2_norm_post[l]),
    )
    ffn2_casts = (ffn2_w_gate[l], ffn2_w_up[l], ffn2_w_down[l])
    sinks = attn_sinks[l]
    attn_gain = _row(attn_out_gain[l])
    hgrn_gain = _row(hgrn_out_gain[l])

    mk, mv, mk_b, mv_b = _mem_kv(mem_prompt, _row(mem_norm_kv[l]),
                                 w_mem_k[l].astype(BF16), w_mem_v[l].astype(BF16))
    x1, zb, zf, (wg2, wu2, wd2) = _trunk_front(x_prompt.reshape(bp * sp, D_MODEL), w, ffn2_casts)
    w.update(ffn2_wg=wg2, ffn2_wu=wu2, ffn2_wd=wd2.reshape(D_FF // FF_TILE, FF_TILE, D_MODEL))
    zb3 = zb.reshape(bp, sp, ZB_WIDTH)
    zf3 = zf.reshape(bp, sp, ZF_WIDTH)
    a = _attn_prompt(zb3, zf3, sinks, attn_gain)
    r, p_state = _hgrn_prompt(zb3, zf3, hgrn_lb_logits, hgrn_gain)
    y_p = _trunk_back(x1, a, r, _mem_attn_kernel, mk_b, mv_b, w, bp, 1, 512).reshape(bp, sp, D_MODEL)
    k_off = ZF_KA * KV_WIDTH
    v_off = ZF_VA * KV_WIDTH
    p_wk = zf3[:, sp - WINDOW:, k_off:k_off + KV_WIDTH]
    p_wv = zf3[:, sp - WINDOW:, v_off:v_off + KV_WIDTH]

    x1s, zbs, zfs, _ = _trunk_front(x_sample.reshape(bs * ts, D_MODEL), w)
    a_s, s_wk, s_wv = _attn_sample(zbs, zfs, cache_win_k[l].reshape(bs, WINDOW, KV_WIDTH),
                                   cache_win_v[l].reshape(bs, WINDOW, KV_WIDTH), sinks, attn_gain, ts)
    r_s, s_state = _hgrn_sample(zbs, zfs, state_hgrn[l], hgrn_lb_logits, hgrn_gain, ts)
    mem_rows = cache_mem_k.shape[2] * MEM_HEADS
    y_s = _trunk_back(x1s, a_s, r_s, _mem_attn_interleaved_kernel,
                      cache_mem_k[l].reshape(bs, mem_rows, MEM_HEAD_DIM),
                      cache_mem_v[l].reshape(bs, mem_rows, MEM_HEAD_DIM),
                      w, bs, 8, ts).reshape(bs, ts, D_MODEL)

    kv5 = lambda t, n: t.reshape(1, n, WINDOW, ATTN_KV_HEADS, ATTN_HEAD_DIM)
    mem5 = lambda t: t.reshape(1, bp, mt, MEM_HEADS, MEM_HEAD_DIM)
    return (y_p, y_s, kv5(p_wk, bp), kv5(p_wv, bp), p_state[None], mem5(mk), mem5(mv),
            kv5(s_wk, bs), kv5(s_wv, bs), s_state[None])
```

```python
import functools

import jax
import jax.numpy as jnp
from jax import lax
from jax.experimental import pallas as pl
from jax.experimental.pallas import tpu as pltpu

F32 = jnp.float32
BF16 = jnp.bfloat16

D_MODEL = 2048
D_FF = 5632
ATTN_HEADS = 16
ATTN_KV_HEADS = 4
ATTN_GROUP = ATTN_HEADS // ATTN_KV_HEADS
ATTN_HEAD_DIM = 64
WINDOW = 128
ATTN_WIDTH = ATTN_HEADS * ATTN_HEAD_DIM
KV_WIDTH = ATTN_KV_HEADS * ATTN_HEAD_DIM
HGRN_HEADS = 8
HGRN_DK = 128
HGRN_DV = 128
HGRN_KW = HGRN_HEADS * HGRN_DK
HGRN_VW = HGRN_HEADS * HGRN_DV
IN_PROJ_WIDTH = ATTN_WIDTH + 2 * KV_WIDTH + 2 * HGRN_KW + 2 * HGRN_VW
MEM_HEADS = 4
MEM_HEAD_DIM = 128
MEM_WIDTH = MEM_HEADS * MEM_HEAD_DIM
FFN_RESIDUAL = 0.5
EPS = 1e-6

ZB_WIDTH = ATTN_WIDTH + HGRN_KW + 2 * HGRN_VW
ZF_WIDTH = HGRN_KW + 2 * KV_WIDTH
ZB_QA, ZB_QH, ZB_IH, ZB_GH = 0, 1, 2, 3
ZF_FH = 0
ZF_KA, ZF_VA = 4, 5

VMEM_LIMIT_BYTES = 56 * 1024 * 1024

FFN_TOKEN_TILE = 512
PROJ_TOKEN_TILE = 1024
MIX_TOKEN_TILE = 512
FF_TILE = 512
RING_SLOTS = 4
RING_AHEAD = 2
CAST_AT_STEP = 3
PROJ_COL_TILE = 512
HGRN_CHUNK = 128
HGRN_STEP_CHUNKS = 4
HGRN_SAMPLE_SEQS = 4
ATTN_SAMPLE_SEQS = 8
ATTN_STEP_BLOCKS = 8
MASKED = -1e30

ALIBI_SLOPES = tuple(2.0 ** (-8.0 * (h + 1) / ATTN_HEADS) for h in range(ATTN_HEADS))


def _params(*semantics, fuse_inputs=None):
    return pltpu.CompilerParams(dimension_semantics=semantics,
                                vmem_limit_bytes=VMEM_LIMIT_BYTES,
                                allow_input_fusion=fuse_inputs)


def _rms(x, g):
    return x * lax.rsqrt(jnp.mean(x * x, axis=-1, keepdims=True) + EPS) * g


def _silu(x):
    return x * jax.nn.sigmoid(x)


def _dot(a, b):
    return jnp.dot(a, b, preferred_element_type=F32)


def _dot_nt(a, b):
    return lax.dot_general(a, b, (((1,), (1,)), ((), ())), preferred_element_type=F32)


def _dot_tn(a, b):
    return lax.dot_general(a, b, (((0,), (0,)), ((), ())), preferred_element_type=F32)


def _log2(n):
    assert n & (n - 1) == 0
    return n.bit_length() - 1


def _stream_weight_tiles(n, token_tiles, copies, step):
    i = pl.program_id(0)
    slots, ahead = RING_SLOTS, RING_AHEAD
    assert ahead < slots <= n

    def start(tile):
        for c in copies(tile, tile % slots):
            c.start()

    @pl.when(i == 0)
    def _():
        for tile in range(ahead):
            start(tile)

    for j in range(n):
        nxt = j + ahead
        if nxt < n:
            assert nxt % slots not in {jj % slots for jj in range(j, nxt)}
            start(nxt)
        else:
            assert (nxt - n) % slots not in {jj % slots for jj in range(j, n)}
            pl.when(i + 1 < token_tiles)(functools.partial(start, nxt - n))
        for c in copies(j, j % slots):
            c.wait()
        step(j, j % slots)


def _ffn_kernel(*refs, token_tiles, n_casts):
    m = n_casts
    x_ref, gpre_ref, gpost_ref, wg_hbm, wu_hbm, wd_hbm = refs[:6]
    cast_src = refs[6:6 + m]
    o_ref = refs[6 + m]
    cast_dst = refs[7 + m:7 + 2 * m]
    wg_buf, wu_buf, wd_buf, sem = refs[7 + 2 * m:11 + 2 * m]
    cast_bufs = refs[11 + 2 * m:11 + 4 * m]
    cast_sem = refs[11 + 4 * m] if m else None
    i = pl.program_id(0)
    tf = wg_buf.shape[2]

    def column_tile(w_hbm, tile):
        return w_hbm.at[tile] if len(w_hbm.shape) == 3 else w_hbm.at[:, pl.ds(tile * tf, tf)]

    def copies(tile, slot):
        return (pltpu.make_async_copy(column_tile(wg_hbm, tile), wg_buf.at[slot], sem.at[0, slot]),
                pltpu.make_async_copy(column_tile(wu_hbm, tile), wu_buf.at[slot], sem.at[1, slot]),
                pltpu.make_async_copy(wd_hbm.at[tile], wd_buf.at[slot], sem.at[2, slot]))

    def cast_in(k):
        buf = cast_bufs[2 * k]
        rows = pl.ds(pl.multiple_of(i * buf.shape[0], 16), buf.shape[0])
        return pltpu.make_async_copy(cast_src[k].at[rows, :], buf, cast_sem.at[k, 0])

    def cast_out(k):
        buf = cast_bufs[2 * k + 1]
        rows = pl.ds(pl.multiple_of(i * buf.shape[0], 16), buf.shape[0])
        return pltpu.make_async_copy(buf, cast_dst[k].at[rows, :], cast_sem.at[k, 1])

    for k in range(m):
        cast_in(k).start()

    h = _rms(x_ref[...], gpre_ref[...]).astype(BF16)

    def step(j, slot):
        k = j - CAST_AT_STEP
        if 0 <= k < m:
            cast_in(k).wait()
            cast_bufs[2 * k + 1][...] = cast_bufs[2 * k][...].astype(BF16)
            cast_out(k).start()
        g = _dot(h, wg_buf[slot])
        u = _dot(h, wu_buf[slot])
        part = _dot((_silu(g) * u).astype(BF16), wd_buf[slot])
        if j == 0:
            o_ref[...] = part
        else:
            o_ref[...] += part

    _stream_weight_tiles(wd_hbm.shape[0], token_tiles, copies, step)
    for k in range(m):
        cast_out(k).wait()
    o_ref[...] = x_ref[...] + _rms(o_ref[...], FFN_RESIDUAL * gpost_ref[...])


def _ffn(x, g_pre, g_post, wg, wu, wd, side_casts=()):
    t = x.shape[0]
    tm = FFN_TOKEN_TILE
    tiles = t // tm
    n, tf, _ = wd.shape
    m = len(side_casts)
    assert n >= CAST_AT_STEP + m
    any_space = pl.BlockSpec(memory_space=pl.ANY)
    cast_shapes, cast_scratch = [], []
    for w in side_casts:
        r, c = w.shape
        rows = r // tiles
        assert rows * tiles == r and rows % 16 == 0
        cast_shapes.append(jax.ShapeDtypeStruct((r, c), BF16))
        cast_scratch += [pltpu.VMEM((rows, c), F32), pltpu.VMEM((rows, c), BF16)]
    if m:
        cast_scratch.append(pltpu.SemaphoreType.DMA((m, 2)))
    outs = pl.pallas_call(
        functools.partial(_ffn_kernel, token_tiles=tiles, n_casts=m),
        grid=(tiles,),
        in_specs=[
            pl.BlockSpec((tm, D_MODEL), lambda i: (i, 0)),
            pl.BlockSpec((1, D_MODEL), lambda i: (0, 0)),
            pl.BlockSpec((1, D_MODEL), lambda i: (0, 0)),
            any_space, any_space, any_space,
        ] + [any_space] * m,
        out_specs=[pl.BlockSpec((tm, D_MODEL), lambda i: (i, 0))] + [any_space] * m,
        out_shape=[jax.ShapeDtypeStruct((t, D_MODEL), F32)] + cast_shapes,
        scratch_shapes=[
            pltpu.VMEM((RING_SLOTS, D_MODEL, tf), BF16),
            pltpu.VMEM((RING_SLOTS, D_MODEL, tf), BF16),
            pltpu.VMEM((RING_SLOTS, tf, D_MODEL), BF16),
            pltpu.SemaphoreType.DMA((3, RING_SLOTS)),
        ] + cast_scratch,
        compiler_params=_params("arbitrary"),
        name="ffn_half",
    )(x, g_pre, g_post, wg, wu, wd, *side_casts)
    return tuple(outs)


def _in_proj_kernel(x_ref, g_ref, w_hbm, zb_hbm, zf_hbm, w_buf, sem, stage_b, stage_f, out_sem,
                    *, token_tiles):
    i = pl.program_id(0)
    n, _, tn = w_hbm.shape
    tm = x_ref.shape[0]
    nb = zb_hbm.shape[1] // tn
    rows = pl.ds(pl.multiple_of(i * tm, tm), tm)

    def copies(tile, slot):
        return (pltpu.make_async_copy(w_hbm.at[tile], w_buf.at[slot], sem.at[slot]),)

    def out_copy(j):
        if j < nb:
            s = j % 2
            return pltpu.make_async_copy(stage_b.at[s], zb_hbm.at[rows, pl.ds(j * tn, tn)],
                                         out_sem.at[0, s])
        s = (j - nb) % 2
        return pltpu.make_async_copy(stage_f.at[s], zf_hbm.at[rows, pl.ds((j - nb) * tn, tn)],
                                     out_sem.at[1, s])

    def same_stage(j):
        return range(0, nb) if j < nb else range(nb, n)

    h = _rms(x_ref[...], g_ref[...]).astype(BF16)

    def step(j, slot):
        z = _dot(h, w_buf[slot])
        if j - 2 in same_stage(j):
            out_copy(j - 2).wait()
        if j < nb:
            stage_b[j % 2] = z.astype(BF16)
        else:
            stage_f[(j - nb) % 2] = z
        out_copy(j).start()

    _stream_weight_tiles(n, token_tiles, copies, step)
    for j in range(n):
        if j + 2 not in same_stage(j):
            out_copy(j).wait()


def _in_proj(x, g, w):
    t = x.shape[0]
    tm = min(PROJ_TOKEN_TILE, t)
    n, _, tn = w.shape
    any_space = pl.BlockSpec(memory_space=pl.ANY)
    return pl.pallas_call(
        functools.partial(_in_proj_kernel, token_tiles=t // tm),
        grid=(t // tm,),
        in_specs=[
            pl.BlockSpec((tm, D_MODEL), lambda i: (i, 0)),
            pl.BlockSpec((1, D_MODEL), lambda i: (0, 0)),
            any_space,
        ],
        out_specs=[any_space, any_space],
        out_shape=[jax.ShapeDtypeStruct((t, ZB_WIDTH), BF16),
                   jax.ShapeDtypeStruct((t, ZF_WIDTH), F32)],
        scratch_shapes=[pltpu.VMEM((RING_SLOTS, D_MODEL, tn), BF16),
                        pltpu.SemaphoreType.DMA((RING_SLOTS,)),
                        pltpu.VMEM((2, tm, tn), BF16),
                        pltpu.VMEM((2, tm, tn), F32),
                        pltpu.SemaphoreType.DMA((2, 2))],
        compiler_params=_params("arbitrary"),
        name="in_proj",
    )(x, g, w)


def _attn_prompt_kernel(sink_ref, q_ref, kc_ref, kp_ref, vc_ref, vp_ref, gain_ref, o_ref):
    n = pl.program_id(1)
    nblk = q_ref.shape[0] // WINDOW
    scale = ATTN_HEAD_DIM ** -0.5
    row = lax.broadcasted_iota(jnp.int32, (WINDOW, 2 * WINDOW), 0)
    col = lax.broadcasted_iota(jnp.int32, (WINDOW, 2 * WINDOW), 1)
    dist = row + WINDOW - col
    in_window = (dist >= 0) & (dist < WINDOW)
    gain = gain_ref[...]

    def block(qrows, k, v, valid):
        k = (k * scale).astype(BF16)
        v = v.astype(BF16)
        neg_dist = jnp.where(valid, -dist.astype(F32), MASKED)
        outs = []
        for hd in range(ATTN_HEADS):
            kvh = hd // ATTN_GROUP
            kh = k[:, kvh * ATTN_HEAD_DIM:(kvh + 1) * ATTN_HEAD_DIM]
            vh = v[:, kvh * ATTN_HEAD_DIM:(kvh + 1) * ATTN_HEAD_DIM]
            qh = q_ref[qrows, hd * ATTN_HEAD_DIM:(hd + 1) * ATTN_HEAD_DIM]
            s = _dot_nt(qh, kh) + ALIBI_SLOPES[hd] * neg_dist
            sink = sink_ref[hd]
            m = jnp.maximum(jnp.max(s, axis=-1, keepdims=True), sink)
            p = jnp.exp(s - m)
            denom = jnp.sum(p, axis=-1, keepdims=True) + jnp.exp(sink - m)
            outs.append(_dot(p.astype(BF16), vh) / denom)
        o = jnp.concatenate(outs, axis=1)
        o_ref[qrows, :] = _rms(o, gain).astype(BF16)

    block(slice(0, WINDOW),
          jnp.concatenate([kp_ref[...], kc_ref[0:WINDOW, :]], axis=0),
          jnp.concatenate([vp_ref[...], vc_ref[0:WINDOW, :]], axis=0),
          in_window & ((col >= WINDOW) | (n > 0)))

    def later_block(blk, carry):
        first = pl.multiple_of((blk - 1) * WINDOW, WINDOW)
        kv_rows = pl.ds(first, 2 * WINDOW)
        block(pl.ds(first + WINDOW, WINDOW), kc_ref[kv_rows, :], vc_ref[kv_rows, :], in_window)
        return carry

    lax.fori_loop(1, nblk, later_block, 0)


def _attn_prompt(zb3, zf3, sinks, gain):
    b, s, _ = zb3.shape
    nblk = ATTN_STEP_BLOCKS
    rows = nblk * WINDOW

    def cur(width_idx):
        return lambda bi, n: (bi, n, width_idx)

    def prev(width_idx):
        return lambda bi, n: (bi, jnp.maximum(n * nblk - 1, 0), width_idx)

    return pl.pallas_call(
        _attn_prompt_kernel,
        grid=(b, s // rows),
        in_specs=[
            pl.BlockSpec(memory_space=pltpu.SMEM),
            pl.BlockSpec((None, rows, ATTN_WIDTH), cur(ZB_QA)),
            pl.BlockSpec((None, rows, KV_WIDTH), cur(ZF_KA)),
            pl.BlockSpec((None, WINDOW, KV_WIDTH), prev(ZF_KA)),
            pl.BlockSpec((None, rows, KV_WIDTH), cur(ZF_VA)),
            pl.BlockSpec((None, WINDOW, KV_WIDTH), prev(ZF_VA)),
            pl.BlockSpec((1, ATTN_WIDTH), lambda bi, n: (0, 0)),
        ],
        out_specs=pl.BlockSpec((None, rows, ATTN_WIDTH), lambda bi, n: (bi, n, 0)),
        out_shape=jax.ShapeDtypeStruct((b, s, ATTN_WIDTH), BF16),
        compiler_params=_params("parallel", "arbitrary"),
        name="attn_prompt",
    )(sinks, zb3, zf3, zf3, zf3, zf3, gain)


def _attn_sample_kernel(sink_ref, q_ref, kn_ref, vn_ref, ck_ref, cv_ref, gain_ref,
                        o_ref, wk_ref, wv_ref, *, t_new):
    seqs = ck_ref.shape[0]
    t_bits = _log2(t_new)
    rows = ATTN_GROUP * t_new
    q_all = q_ref[...].astype(F32).reshape(seqs, t_new, ATTN_WIDTH)
    kn = kn_ref[...].reshape(seqs, t_new, KV_WIDTH)
    vn = vn_ref[...].reshape(seqs, t_new, KV_WIDTH)
    ck = ck_ref[...]
    cv = cv_ref[...]
    wk_ref[:, :WINDOW - t_new, :] = ck[:, t_new:, :]
    wk_ref[:, WINDOW - t_new:, :] = kn
    wv_ref[:, :WINDOW - t_new, :] = cv[:, t_new:, :]
    wv_ref[:, WINDOW - t_new:, :] = vn

    r_c = lax.broadcasted_iota(jnp.int32, (rows, WINDOW), 0)
    j_c = lax.broadcasted_iota(jnp.int32, (rows, WINDOW), 1)
    dist_c = (r_c & (t_new - 1)) + WINDOW - j_c
    neg_c = jnp.where(dist_c < WINDOW, -dist_c.astype(F32), MASKED)
    r_n = lax.broadcasted_iota(jnp.int32, (rows, t_new), 0)
    j_n = lax.broadcasted_iota(jnp.int32, (rows, t_new), 1)
    dist_n = (r_n & (t_new - 1)) - j_n
    neg_n = jnp.where(dist_n >= 0, -dist_n.astype(F32), MASKED)
    head_of_row = lax.broadcasted_iota(jnp.int32, (rows, 1), 0) >> t_bits
    scale = ATTN_HEAD_DIM ** -0.5

    outs = [None] * ATTN_HEADS
    for kvh in range(ATTN_KV_HEADS):
        slope = jnp.zeros((rows, 1), F32)
        sink = jnp.zeros((rows, 1), F32)
        for g in range(ATTN_GROUP):
            hd = kvh * ATTN_GROUP + g
            slope = jnp.where(head_of_row == g, ALIBI_SLOPES[hd], slope)
            sink = jnp.where(head_of_row == g, sink_ref[hd], sink)
        q = jnp.concatenate(
            [q_all[:, :, (kvh * ATTN_GROUP + g) * ATTN_HEAD_DIM:(kvh * ATTN_GROUP + g + 1) * ATTN_HEAD_DIM]
             for g in range(ATTN_GROUP)], axis=1).astype(BF16)
        sl = slice(kvh * ATTN_HEAD_DIM, (kvh + 1) * ATTN_HEAD_DIM)
        ck_h = (ck[:, :, sl] * scale).astype(BF16)
        kn_h = (kn[:, :, sl] * scale).astype(BF16)
        cv_h = cv[:, :, sl].astype(BF16)
        vn_h = vn[:, :, sl].astype(BF16)
        s_c = jnp.einsum('gqd,gkd->gqk', q, ck_h, preferred_element_type=F32) + (slope * neg_c)[None]
        s_n = jnp.einsum('gqd,gkd->gqk', q, kn_h, preferred_element_type=F32) + (slope * neg_n)[None]
        m = jnp.maximum(jnp.maximum(jnp.max(s_c, axis=-1, keepdims=True),
                                    jnp.max(s_n, axis=-1, keepdims=True)), sink[None])
        p_c = jnp.exp(s_c - m)
        p_n = jnp.exp(s_n - m)
        denom = (jnp.sum(p_c, axis=-1, keepdims=True) + jnp.sum(p_n, axis=-1, keepdims=True)
                 + jnp.exp(sink[None] - m))
        o = (jnp.einsum('gqk,gkd->gqd', p_c.astype(BF16), cv_h, preferred_element_type=F32)
             + jnp.einsum('gqk,gkd->gqd', p_n.astype(BF16), vn_h, preferred_element_type=F32)) / denom
        for g in range(ATTN_GROUP):
            outs[kvh * ATTN_GROUP + g] = o[:, g * t_new:(g + 1) * t_new, :]
    o_all = jnp.concatenate(outs, axis=2)
    o_ref[...] = _rms(o_all, gain_ref[...][None]).reshape(seqs * t_new, ATTN_WIDTH)


def _attn_sample(zb, zf, cache_k, cache_v, sinks, gain, t_new):
    b = cache_k.shape[0]
    gs = ATTN_SAMPLE_SEQS
    tok = lambda width, idx: pl.BlockSpec((gs * t_new, width), lambda i: (i, idx))
    cache_spec = pl.BlockSpec((gs, WINDOW, KV_WIDTH), lambda i: (i, 0, 0))
    return pl.pallas_call(
        functools.partial(_attn_sample_kernel, t_new=t_new),
        grid=(b // gs,),
        in_specs=[
            pl.BlockSpec(memory_space=pltpu.SMEM),
            tok(ATTN_WIDTH, ZB_QA), tok(KV_WIDTH, ZF_KA), tok(KV_WIDTH, ZF_VA),
            cache_spec, cache_spec,
            pl.BlockSpec((1, ATTN_WIDTH), lambda i: (0, 0)),
        ],
        out_specs=[tok(ATTN_WIDTH, 0), cache_spec, cache_spec],
        out_shape=[
            jax.ShapeDtypeStruct((b * t_new, ATTN_WIDTH), F32),
            jax.ShapeDtypeStruct((b, WINDOW, KV_WIDTH), F32),
            jax.ShapeDtypeStruct((b, WINDOW, KV_WIDTH), F32),
        ],
        compiler_params=_params("parallel"),
        name="attn_sample",
    )(sinks, zb, zf, zf, cache_k, cache_v, gain)


def _lower_bound(lbl_ref):
    logits = lbl_ref[...]
    e = jnp.exp(logits - jnp.max(logits, axis=0, keepdims=True))
    return e[0:1, :] / jnp.sum(e, axis=0, keepdims=True)


def _hgrn_features(q_raw, f_raw, lb):
    sg = jax.nn.sigmoid(f_raw)
    f = lb + (1.0 - lb) * sg
    k = (1.0 - lb) * (1.0 - sg)
    return _silu(q_raw), k, f, jnp.log2(f)


def _cumsum_rows(x, segment):
    row = lax.broadcasted_iota(jnp.int32, x.shape, 0) & (segment - 1)
    shift = 1
    while shift < segment:
        x = x + jnp.where(row >= shift, pltpu.roll(x, shift, axis=0), 0.0)
        shift *= 2
    return x


def _split3(x):
    hi = x.astype(BF16)
    r = x - hi.astype(F32)
    mid = r.astype(BF16)
    lo = (r - mid.astype(F32)).astype(BF16)
    return hi, mid, lo


def _cumsum_rows_mxu(x, tri):
    hi, mid, lo = _split3(x)
    return _dot(tri, hi) + _dot(tri, mid) + _dot(tri, lo)


def _column_broadcast(row_vec, n):
    rows = 16
    hi, mid, lo = _split3(row_vec)
    terms = jnp.concatenate([hi, mid, lo, jnp.zeros((rows - 3, n), BF16)], axis=0)
    ones = (lax.broadcasted_iota(jnp.int32, (rows, n), 0) < 3).astype(BF16)
    return _dot_tn(terms, ones)


def _hgrn_out(o, gain, g_raw):
    return _rms(o, gain) * _silu(g_raw)


def _pair_levels(c):
    t = lax.broadcasted_iota(jnp.int32, (c, c), 0)
    s = lax.broadcasted_iota(jnp.int32, (c, c), 1)
    masks = [t == s]
    half = 1
    while half < c:
        block = 2 * half
        same_block = (t >> _log2(block)) == (s >> _log2(block))
        masks.append(same_block & ((t & (block - 1)) >= half) & ((s & (block - 1)) < half))
        half = block
    return masks


def _pair_factors(q, k, f, log2f, l2):
    c, w = q.shape
    pos = lax.broadcasted_iota(jnp.int32, (c, w), 0) & 3
    e2 = jnp.exp2(jnp.where(pos == 0, pltpu.roll(log2f, c - 1, axis=0),
                            jnp.where(pos == 1, 0.0,
                                      jnp.where(pos == 2, log2f,
                                                log2f + pltpu.roll(log2f, 1, axis=0)))))
    qb, kb = q.astype(BF16), k.astype(BF16)
    e2 = e2.astype(BF16)
    qs = [qb, qb * f.astype(BF16), qb * e2]
    ks = [kb, kb, kb * e2]
    half = 4
    while half < c:
        block = 2 * half
        ref = l2.reshape(c // block, block, w)[:, half - 1:half, :]
        ref = jnp.broadcast_to(ref, (c // block, block, w)).reshape(c, w)
        e = jnp.exp2(-jnp.abs(l2 - ref)).astype(BF16)
        qs.append(qb * e)
        ks.append(kb * e)
        half = block
    return qs, ks


def _hgrn_prompt_kernel(lbl_ref, q_ref, f_ref, i_ref, g_ref, gain_ref, r_ref, s_ref):
    c = HGRN_CHUNK

    @pl.when(pl.program_id(1) == 0)
    def _():
        s_ref[...] = jnp.zeros_like(s_ref)

    lb = _lower_bound(lbl_ref)
    q_all, k_all, f_all, log2f_all = _hgrn_features(q_ref[...].astype(F32), f_ref[...], lb)
    masks = _pair_levels(c)
    tri = (lax.broadcasted_iota(jnp.int32, (c, c), 1)
           <= lax.broadcasted_iota(jnp.int32, (c, c), 0)).astype(BF16)
    gain = gain_ref[...]

    chunks = []
    for ci in range(HGRN_STEP_CHUNKS):
        rows = slice(ci * c, (ci + 1) * c)
        l2_c = _cumsum_rows_mxu(log2f_all[rows], tri)
        q_lv, k_lv = _pair_factors(q_all[rows], k_all[rows], f_all[rows], log2f_all[rows], l2_c)
        l_end = l2_c[c - 1:c, :]
        q_hat = (q_all[rows] * jnp.exp2(l2_c)).astype(BF16)
        k_end = (k_all[rows] * jnp.exp2(l_end - l2_c)).astype(BF16)
        chunks.append((rows, q_lv, k_lv, q_hat, k_end, jnp.exp2(l_end)))

    for rows, q_lv, k_lv, q_hat, k_end, decay_end in chunks:
        for h in range(HGRN_HEADS):
            sl = slice(h * HGRN_DK, (h + 1) * HGRN_DK)
            vb = i_ref[rows, sl]
            a = jnp.zeros((c, c), F32)
            for mask, q_f, k_f in zip(masks, q_lv, k_lv):
                a = jnp.where(mask, _dot_nt(q_f[:, sl], k_f[:, sl]), a)
            s_old = s_ref[h]
            o = _dot(a.astype(BF16), vb) + _dot(q_hat[:, sl], s_old.astype(BF16))
            s_ref[h] = (s_old * _column_broadcast(decay_end[:, sl], HGRN_DK)
                        + _dot_tn(k_end[:, sl], vb))
            r_ref[rows, sl] = _hgrn_out(o, gain, g_ref[rows, sl].astype(F32)).astype(r_ref.dtype)


def _hgrn_prompt(zb3, zf3, lb_logits, gain):
    b, s, _ = zb3.shape
    rows = HGRN_CHUNK * HGRN_STEP_CHUNKS
    seg = lambda idx: pl.BlockSpec((None, rows, HGRN_KW), lambda bi, ci: (bi, ci, idx))
    return pl.pallas_call(
        _hgrn_prompt_kernel,
        grid=(b, s // rows),
        in_specs=[
            pl.BlockSpec(lb_logits.shape, lambda bi, ci: (0, 0)),
            seg(ZB_QH), seg(ZF_FH), seg(ZB_IH), seg(ZB_GH),
            pl.BlockSpec((1, HGRN_DV), lambda bi, ci: (0, 0)),
        ],
        out_specs=[
            pl.BlockSpec((None, rows, HGRN_VW), lambda bi, ci: (bi, ci, 0)),
            pl.BlockSpec((None, HGRN_HEADS, HGRN_DK, HGRN_DV), lambda bi, ci: (bi, 0, 0, 0)),
        ],
        out_shape=[
            jax.ShapeDtypeStruct((b, s, HGRN_VW), BF16),
            jax.ShapeDtypeStruct((b, HGRN_HEADS, HGRN_DK, HGRN_DV), F32),
        ],
        compiler_params=_params("parallel", "arbitrary"),
        name="hgrn_prompt",
    )(lb_logits, zb3, zf3, zb3, zb3, gain)


def _hgrn_sample_kernel(lbl_ref, seg_ref, segt_ref, q_ref, f_ref, i_ref, g_ref, gain_ref, s0_ref,
                        r_ref, s_ref, *, t_new):
    seqs = s0_ref.shape[0]
    n = seqs * t_new
    width = HGRN_KW
    lb = _lower_bound(lbl_ref)
    q_all, k_all, _, log2f_all = _hgrn_features(q_ref[...].astype(F32), f_ref[...], lb)
    l2_all = _cumsum_rows(log2f_all, t_new)
    v_all = i_ref[...].astype(F32)
    g_all = g_ref[...].astype(F32)
    gain = gain_ref[...]

    def row_of_seq(x, t):
        x3 = x.reshape(seqs, t_new, width)
        return jnp.broadcast_to(x3[:, t:t + 1, :], (seqs, t_new, width)).reshape(n, width)

    src = lax.broadcasted_iota(jnp.int32, (n, width), 0) & (t_new - 1)
    w_parts = []
    for t in range(t_new):
        decay = jnp.exp2(jnp.minimum(row_of_seq(l2_all, t) - l2_all, 0.0))
        w_parts.append(jnp.where(src <= t, decay * k_all * row_of_seq(q_all, t), 0.0))
    w = jnp.concatenate(w_parts, axis=0).astype(BF16)
    a_heads = _dot(w, seg_ref[...])
    a_lanes = _dot(a_heads.astype(BF16), segt_ref[...])
    o_rows = []
    for t in range(t_new):
        p = (a_lanes[t * n:(t + 1) * n] * v_all).reshape(seqs, t_new, width)
        o_rows.append(jnp.sum(p, axis=1, keepdims=True))
    o_intra = jnp.concatenate(o_rows, axis=1).reshape(n, width)

    l_end = row_of_seq(l2_all, t_new - 1)
    q_hat = q_all * jnp.exp2(l2_all)
    k_end = k_all * jnp.exp2(l_end - l2_all)
    decay_end = jnp.exp2(l_end)
    for h in range(HGRN_HEADS):
        sl = slice(h * HGRN_DK, (h + 1) * HGRN_DK)
        o_parts = []
        for sq in range(seqs):
            rows = slice(sq * t_new, (sq + 1) * t_new)
            s_old = s0_ref[sq, h]
            o_parts.append(_dot(q_hat[rows, sl].astype(BF16), s_old.astype(BF16)))
            s_ref[sq, h] = (s_old * _column_broadcast(decay_end[sq * t_new:sq * t_new + 1, sl], HGRN_DK)
                            + _dot_tn(k_end[rows, sl].astype(BF16), v_all[rows, sl].astype(BF16)))
        o = o_intra[:, sl] + jnp.concatenate(o_parts, axis=0)
        r_ref[:, sl] = _hgrn_out(o, gain, g_all[:, sl])


def _hgrn_sample(zb, zf, state, lb_logits, gain, t_new):
    b = state.shape[0]
    gs = HGRN_SAMPLE_SEQS
    seg = lambda idx: pl.BlockSpec((gs * t_new, HGRN_KW), lambda i: (i, idx))
    state_spec = pl.BlockSpec((gs, HGRN_HEADS, HGRN_DK, HGRN_DV), lambda i: (i, 0, 0, 0))
    lane_head = jnp.arange(HGRN_KW, dtype=jnp.int32) // HGRN_DK
    head_seg = (lane_head[:, None] == jnp.arange(HGRN_DK, dtype=jnp.int32)[None, :]).astype(BF16)
    return pl.pallas_call(
        functools.partial(_hgrn_sample_kernel, t_new=t_new),
        grid=(b // gs,),
        in_specs=[
            pl.BlockSpec(lb_logits.shape, lambda i: (0, 0)),
            pl.BlockSpec((HGRN_KW, HGRN_DK), lambda i: (0, 0)),
            pl.BlockSpec((HGRN_DK, HGRN_KW), lambda i: (0, 0)),
            seg(ZB_QH), seg(ZF_FH), seg(ZB_IH), seg(ZB_GH),
            pl.BlockSpec((1, HGRN_DV), lambda i: (0, 0)),
            state_spec,
        ],
        out_specs=[seg(0), state_spec],
        out_shape=[
            jax.ShapeDtypeStruct((b * t_new, HGRN_VW), F32),
            jax.ShapeDtypeStruct((b, HGRN_HEADS, HGRN_DK, HGRN_DV), F32),
        ],
        compiler_params=_params("parallel"),
        name="hgrn_sample",
    )(lb_logits, head_seg, head_seg.T, zb, zf, zb, zb, gain, state)


def _mix_out_kernel(x_ref, a_ref, r_ref, wa_ref, wr_ref, gpost_ref, gmem_ref, wq_ref,
                    x2_ref, qm_ref):
    mixed = (_dot(a_ref[...].astype(BF16), wa_ref[...])
             + _dot(r_ref[...].astype(BF16), wr_ref[...]))
    x2 = x_ref[...] + _rms(mixed, gpost_ref[...])
    x2_ref[...] = x2
    qm_ref[...] = _dot(_rms(x2, gmem_ref[...]).astype(BF16), wq_ref[...])


def _mix_out(x, a, r, w_out, g_post, g_mem, w_q):
    t = x.shape[0]
    tm = MIX_TOKEN_TILE
    tok = lambda width: pl.BlockSpec((tm, width), lambda i: (i, 0))
    const = lambda shape, idx=(0, 0): pl.BlockSpec(shape, lambda i: idx)
    return pl.pallas_call(
        _mix_out_kernel,
        grid=(t // tm,),
        in_specs=[
            tok(D_MODEL), tok(ATTN_WIDTH), tok(HGRN_VW),
            const((ATTN_WIDTH, D_MODEL), (0, 0)),
            const((HGRN_VW, D_MODEL), (1, 0)),
            const((1, D_MODEL)), const((1, D_MODEL)),
            const((D_MODEL, MEM_WIDTH)),
        ],
        out_specs=[tok(D_MODEL), tok(MEM_WIDTH)],
        out_shape=[jax.ShapeDtypeStruct((t, D_MODEL), F32),
                   jax.ShapeDtypeStruct((t, MEM_WIDTH), F32)],
        compiler_params=_params("parallel", fuse_inputs=[False, False, False, True, True,
                                                         False, False, True]),
        name="mix_out",
    )(x, a, r, w_out, w_out, g_post, g_mem, w_q)


def _mem_kv_kernel(m_ref, g_ref, wk_ref, wv_ref, k_ref, v_ref, kb_ref, vb_ref):
    h = _rms(m_ref[...], g_ref[...]).astype(BF16)
    k = _dot(h, wk_ref[...])
    v = _dot(h, wv_ref[...])
    k_ref[...] = k
    v_ref[...] = v
    kb_ref[...] = k.astype(BF16)
    vb_ref[...] = v.astype(BF16)


def _mem_kv(mem, g, w_k, w_v):
    b, m, _ = mem.shape
    const = lambda shape: pl.BlockSpec(shape, lambda i: (0, 0))
    out = pl.BlockSpec((None, m, MEM_WIDTH), lambda i: (i, 0, 0))
    return pl.pallas_call(
        _mem_kv_kernel,
        grid=(b,),
        in_specs=[pl.BlockSpec((None, m, D_MODEL), lambda i: (i, 0, 0)), const((1, D_MODEL)),
                  const((D_MODEL, MEM_WIDTH)), const((D_MODEL, MEM_WIDTH))],
        out_specs=[out] * 4,
        out_shape=[jax.ShapeDtypeStruct((b, m, MEM_WIDTH), F32)] * 2
        + [jax.ShapeDtypeStruct((b, m, MEM_WIDTH), BF16)] * 2,
        compiler_params=_params("parallel", fuse_inputs=[False, False, True, True]),
        name="mem_kv",
    )(mem, g, w_k, w_v)


def _mem_attn_finish(o, x_ref, wo_ref, g_ref, o_ref):
    seqs, tq, _ = x_ref.shape
    y = _dot(o.reshape(seqs * tq, MEM_WIDTH).astype(BF16), wo_ref[...])
    x = x_ref[...].reshape(seqs * tq, D_MODEL)
    o_ref[...] = (x + _rms(y, g_ref[...])).reshape(seqs, tq, D_MODEL)


def _mem_attn_kernel(x_ref, q_ref, mk_ref, mv_ref, wo_ref, g_ref, o_ref):
    scale = MEM_HEAD_DIM ** -0.5
    outs = []
    for h in range(MEM_HEADS):
        sl = slice(h * MEM_HEAD_DIM, (h + 1) * MEM_HEAD_DIM)
        q = q_ref[:, :, sl].astype(BF16)
        s = jnp.einsum('gqd,gkd->gqk', q, mk_ref[:, :, sl], preferred_element_type=F32) * scale
        p = jnp.exp(s - jnp.max(s, axis=-1, keepdims=True))
        denom = jnp.sum(p, axis=-1, keepdims=True)
        outs.append(jnp.einsum('gqk,gkd->gqd', p.astype(BF16), mv_ref[:, :, sl],
                               preferred_element_type=F32) / denom)
    _mem_attn_finish(jnp.concatenate(outs, axis=2), x_ref, wo_ref, g_ref, o_ref)


def _mem_attn_interleaved_kernel(x_ref, q_ref, mk_ref, mv_ref, wo_ref, g_ref, o_ref):
    seqs, tq, _ = q_ref.shape
    rows = MEM_HEADS * tq
    cols = mk_ref.shape[1]
    scale = MEM_HEAD_DIM ** -0.5
    q = jnp.concatenate([q_ref[:, :, h * MEM_HEAD_DIM:(h + 1) * MEM_HEAD_DIM]
                         for h in range(MEM_HEADS)], axis=1).astype(BF16)
    s = jnp.einsum('gqd,gkd->gqk', q, mk_ref[...].astype(BF16), preferred_element_type=F32) * scale
    row_head = lax.broadcasted_iota(jnp.int32, (rows, cols), 0) >> _log2(tq)
    col_head = lax.broadcasted_iota(jnp.int32, (rows, cols), 1) & (MEM_HEADS - 1)
    s = jnp.where((row_head == col_head)[None], s, -jnp.inf)
    p = jnp.exp(s - jnp.max(s, axis=-1, keepdims=True))
    denom = jnp.sum(p, axis=-1, keepdims=True)
    o = jnp.einsum('gqk,gkd->gqd', p.astype(BF16), mv_ref[...].astype(BF16),
                   preferred_element_type=F32) / denom
    o = jnp.concatenate([o[:, h * tq:(h + 1) * tq, :] for h in range(MEM_HEADS)], axis=2)
    _mem_attn_finish(o, x_ref, wo_ref, g_ref, o_ref)


def _mem_attn(body, x3, q3, mem_k, mem_v, w_o, g_post, seqs, tq):
    nseq, slen, _ = x3.shape
    tok = lambda width: pl.BlockSpec((seqs, tq, width), lambda i, j: (i, j, 0))
    mem = pl.BlockSpec((seqs,) + mem_k.shape[1:], lambda i, j: (i, 0, 0))
    const = lambda shape: pl.BlockSpec(shape, lambda i, j: (0, 0))
    return pl.pallas_call(
        body,
        grid=(nseq // seqs, slen // tq),
        in_specs=[tok(D_MODEL), tok(MEM_WIDTH), mem, mem,
                  const((MEM_WIDTH, D_MODEL)), const((1, D_MODEL))],
        out_specs=tok(D_MODEL),
        out_shape=jax.ShapeDtypeStruct((nseq, slen, D_MODEL), F32),
        compiler_params=_params("parallel", "arbitrary",
                                fuse_inputs=[False, False, False, False, True, False]),
        name="mem_attn",
    )(x3, q3, mem_k, mem_v, w_o, g_post)


def _row(g):
    return g.reshape(1, -1)


def _cast_tiles_kernel(order_ref, w_ref, o_ref):
    del order_ref
    o_ref[...] = w_ref[...].astype(BF16)


def _cast_col_tiles(w, tile, order=None):
    k, n = w.shape
    nt = n // tile
    order = jnp.arange(nt, dtype=jnp.int32) if order is None else jnp.asarray(order, jnp.int32)
    return pl.pallas_call(
        _cast_tiles_kernel,
        grid_spec=pltpu.PrefetchScalarGridSpec(
            num_scalar_prefetch=1,
            grid=(nt,),
            in_specs=[pl.BlockSpec((k, tile), lambda j, order_ref: (0, order_ref[j]))],
            out_specs=pl.BlockSpec((None, k, tile), lambda j, order_ref: (j, 0, 0)),
        ),
        out_shape=jax.ShapeDtypeStruct((nt, k, tile), BF16),
        compiler_params=_params("parallel"),
        name="cast_col_tiles",
    )(order, w)


def _trunk_front(x2d, w, side_casts=()):
    x1, *converted = _ffn(x2d, w['ffn1_pre'], w['ffn1_post'], w['ffn1_wg'], w['ffn1_wu'],
                          w['ffn1_wd'], side_casts=side_casts)
    zb, zf = _in_proj(x1, w['mix_pre'], w['w_in'])
    return x1, zb, zf, converted


def _trunk_back(x1, a, r, mem_body, mem_k, mem_v, w, nseq, seqs, tq):
    t = x1.shape[0]
    x2, qm = _mix_out(x1, a.reshape(t, ATTN_WIDTH), r.reshape(t, HGRN_VW), w['w_out'],
                      w['mix_post'], w['mem_pre'], w['w_mem_q'])
    x3 = _mem_attn(mem_body, x2.reshape(nseq, t // nseq, D_MODEL),
                   qm.reshape(nseq, t // nseq, MEM_WIDTH),
                   mem_k, mem_v, w['w_mem_o'], w['mem_post'], seqs, tq)
    (y,) = _ffn(x3.reshape(t, D_MODEL), w['ffn2_pre'], w['ffn2_post'],
                w['ffn2_wg'], w['ffn2_wu'], w['ffn2_wd'])
    return y


def kernel(x_prompt, x_sample, mem_prompt, cache_win_k, cache_win_v, state_hgrn, cache_mem_k, cache_mem_v, ffn1_norm_pre, ffn1_norm_post, ffn1_w_gate, ffn1_w_up, ffn1_w_down, mix_norm_pre, mix_norm_post, w_in, attn_sinks, hgrn_lb_logits, attn_out_gain, hgrn_out_gain, w_out, mem_norm_pre, mem_norm_post, mem_norm_kv, w_mem_q, w_mem_k, w_mem_v, w_mem_o, ffn2_norm_pre, ffn2_norm_post, ffn2_w_gate, ffn2_w_up, ffn2_w_down):
    bp, sp, _ = x_prompt.shape
    bs, ts, _ = x_sample.shape
    mt = mem_prompt.shape[1]
    assert w_in.shape[0] == 1 and hgrn_lb_logits.shape[0] == 2, "one layer (depth 1) is implemented"
    assert (bp * sp) % PROJ_TOKEN_TILE == 0 and (bs * ts) % FFN_TOKEN_TILE == 0
    assert sp % (ATTN_STEP_BLOCKS * WINDOW) == 0 and sp % (HGRN_STEP_CHUNKS * HGRN_CHUNK) == 0
    assert bs % ATTN_SAMPLE_SEQS == 0 and bs % HGRN_SAMPLE_SEQS == 0 and bs % 8 == 0
    l = 0

    seg_widths = dict(qa=ATTN_WIDTH, kv=2 * KV_WIDTH, qh=HGRN_KW, fh=HGRN_KW, ih=HGRN_VW, gh=HGRN_VW)
    seg_tiles, start = {}, 0
    for name in ('qa', 'kv', 'qh', 'fh', 'ih', 'gh'):
        n_tiles = seg_widths[name] // PROJ_COL_TILE
        seg_tiles[name] = list(range(start, start + n_tiles))
        start += n_tiles
    w_in_order = sum((seg_tiles[name] for name in ('qa', 'qh', 'ih', 'gh', 'fh', 'kv')), [])
    w = dict(
        ffn1_pre=_row(ffn1_norm_pre[l]), ffn1_post=_row(ffn1_norm_post[l]),
        ffn1_wg=_cast_col_tiles(ffn1_w_gate[l], FF_TILE), ffn1_wu=_cast_col_tiles(ffn1_w_up[l], FF_TILE),
        ffn1_wd=ffn1_w_down[l].astype(BF16).reshape(D_FF // FF_TILE, FF_TILE, D_MODEL),
        mix_pre=_row(mix_norm_pre[l]), mix_post=_row(mix_norm_post[l]),
        w_in=_cast_col_tiles(w_in[l], PROJ_COL_TILE, w_in_order), w_out=w_out[l].astype(BF16),
        mem_pre=_row(mem_norm_pre[l]), mem_post=_row(mem_norm_post[l]),
        w_mem_q=w_mem_q[l].astype(BF16), w_mem_o=w_mem_o[l].astype(BF16),
        ffn2_pre=_row(ffn2_norm_pre[l]), ffn2_post=_row(ffn2_norm_post[l]),
    )
    ffn2_casts = (ffn2_w_gate[l], ffn2_w_up[l], ffn2_w_down[l])
    sinks = attn_sinks[l]
    attn_gain = _row(attn_out_gain[l])
    hgrn_gain = _row(hgrn_out_gain[l])

    mk, mv, mk_b, mv_b = _mem_kv(mem_prompt, _row(mem_norm_kv[l]),
                                 w_mem_k[l].astype(BF16), w_mem_v[l].astype(BF16))
    x1, zb, zf, (wg2, wu2, wd2) = _trunk_front(x_prompt.reshape(bp * sp, D_MODEL), w, ffn2_casts)
    w.update(ffn2_wg=wg2, ffn2_wu=wu2, ffn2_wd=wd2.reshape(D_FF // FF_TILE, FF_TILE, D_MODEL))
    zb3 = zb.reshape(bp, sp, ZB_WIDTH)
    zf3 = zf.reshape(bp, sp, ZF_WIDTH)
    a = _attn_prompt(zb3, zf3, sinks, attn_gain)
    r, p_state = _hgrn_prompt(zb3, zf3, hgrn_lb_logits, hgrn_gain)
    y_p = _trunk_back(x1, a, r, _mem_attn_kernel, mk_b, mv_b, w, bp, 1, 512).reshape(bp, sp, D_MODEL)
    k_off = ZF_KA * KV_WIDTH
    v_off = ZF_VA * KV_WIDTH
    p_wk = zf3[:, sp - WINDOW:, k_off:k_off + KV_WIDTH]
    p_wv = zf3[:, sp - WINDOW:, v_off:v_off + KV_WIDTH]

    x1s, zbs, zfs, _ = _trunk_front(x_sample.reshape(bs * ts, D_MODEL), w)
    a_s, s_wk, s_wv = _attn_sample(zbs, zfs, cache_win_k[l].reshape(bs, WINDOW, KV_WIDTH),
                                   cache_win_v[l].reshape(bs, WINDOW, KV_WIDTH), sinks, attn_gain, ts)
    r_s, s_state = _hgrn_sample(zbs, zfs, state_hgrn[l], hgrn_lb_logits, hgrn_gain, ts)
    mem_rows = cache_mem_k.shape[2] * MEM_HEADS
    y_s = _trunk_back(x1s, a_s, r_s, _mem_attn_interleaved_kernel,
                      cache_mem_k[l].reshape(bs, mem_rows, MEM_HEAD_DIM),
                      cache_mem_v[l].reshape(bs, mem_rows, MEM_HEAD_DIM),
                      w, bs, 8, ts).reshape(bs, ts, D_MODEL)

    kv5 = lambda t, n: t.reshape(1, n, WINDOW, ATTN_KV_HEADS, ATTN_HEAD_DIM)
    mem5 = lambda t: t.reshape(1, bp, mt, MEM_HEADS, MEM_HEAD_DIM)
    return (y_p, y_s, kv5(p_wk, bp), kv5(p_wv, bp), p_state[None], mem5(mk), mem5(mv),
            kv5(s_wk, bs), kv5(s_wv, bs), s_state[None])
```

```python
import functools

import jax
import jax.numpy as jnp
from jax import lax
from jax.experimental import pallas as pl
from jax.experimental.pallas import tpu as pltpu

F32 = jnp.float32
BF16 = jnp.bfloat16

D_MODEL = 2048
D_FF = 5632
ATTN_HEADS = 16
ATTN_KV_HEADS = 4
ATTN_GROUP = ATTN_HEADS // ATTN_KV_HEADS
ATTN_HEAD_DIM = 64
WINDOW = 128
ATTN_WIDTH = ATTN_HEADS * ATTN_HEAD_DIM
KV_WIDTH = ATTN_KV_HEADS * ATTN_HEAD_DIM
HGRN_HEADS = 8
HGRN_DK = 128
HGRN_DV = 128
HGRN_KW = HGRN_HEADS * HGRN_DK
HGRN_VW = HGRN_HEADS * HGRN_DV
IN_PROJ_WIDTH = ATTN_WIDTH + 2 * KV_WIDTH + 2 * HGRN_KW + 2 * HGRN_VW
MEM_HEADS = 4
MEM_HEAD_DIM = 128
MEM_WIDTH = MEM_HEADS * MEM_HEAD_DIM
FFN_RESIDUAL = 0.5
EPS = 1e-6

ZB_WIDTH = ATTN_WIDTH + HGRN_KW + 2 * HGRN_VW
ZF_WIDTH = HGRN_KW + 2 * KV_WIDTH
ZB_QA, ZB_QH, ZB_IH, ZB_GH = 0, 1, 2, 3
ZF_FH = 0
ZF_KA, ZF_VA = 4, 5

VMEM_LIMIT_BYTES = 56 * 1024 * 1024

FFN_TOKEN_TILE = 512
PROJ_TOKEN_TILE = 1024
MIX_TOKEN_TILE = 512
FF_TILE = 512
RING_SLOTS = 4
RING_AHEAD = 2
CAST_AT_STEP = 3
PROJ_COL_TILE = 512
HGRN_CHUNK = 128
HGRN_STEP_CHUNKS = 4
HGRN_SAMPLE_SEQS = 4
ATTN_SAMPLE_SEQS = 8
ATTN_STEP_BLOCKS = 8
MASKED = -1e30

ALIBI_SLOPES = tuple(2.0 ** (-8.0 * (h + 1) / ATTN_HEADS) for h in range(ATTN_HEADS))


def _params(*semantics, fuse_inputs=None):
    return pltpu.CompilerParams(dimension_semantics=semantics,
                                vmem_limit_bytes=VMEM_LIMIT_BYTES,
                                allow_input_fusion=fuse_inputs)


def _rms(x, g):
    return x * lax.rsqrt(jnp.mean(x * x, axis=-1, keepdims=True) + EPS) * g


def _silu(x):
    return x * jax.nn.sigmoid(x)


def _dot(a, b):
    return jnp.dot(a, b, preferred_element_type=F32)


def _dot_nt(a, b):
    return lax.dot_general(a, b, (((1,), (1,)), ((), ())), preferred_element_type=F32)


def _dot_tn(a, b):
    return lax.dot_general(a, b, (((0,), (0,)), ((), ())), preferred_element_type=F32)


def _log2(n):
    assert n & (n - 1) == 0
    return n.bit_length() - 1


def _stream_weight_tiles(n, token_tiles, copies, step):
    i = pl.program_id(0)
    slots, ahead = RING_SLOTS, RING_AHEAD
    assert ahead < slots <= n

    def start(tile):
        for c in copies(tile, tile % slots):
            c.start()

    @pl.when(i == 0)
    def _():
        for tile in range(ahead):
            start(tile)

    for j in range(n):
        nxt = j + ahead
        if nxt < n:
            assert nxt % slots not in {jj % slots for jj in range(j, nxt)}
            start(nxt)
        else:
            assert (nxt - n) % slots not in {jj % slots for jj in range(j, n)}
            pl.when(i + 1 < token_tiles)(functools.partial(start, nxt - n))
        for c in copies(j, j % slots):
            c.wait()
        step(j, j % slots)


def _ffn_kernel(*refs, token_tiles, n_casts):
    m = n_casts
    x_ref, gpre_ref, gpost_ref, wg_hbm, wu_hbm, wd_hbm = refs[:6]
    cast_src = refs[6:6 + m]
    o_ref = refs[6 + m]
    cast_dst = refs[7 + m:7 + 2 * m]
    wg_buf, wu_buf, wd_buf, sem = refs[7 + 2 * m:11 + 2 * m]
    cast_bufs = refs[11 + 2 * m:11 + 4 * m]
    cast_sem = refs[11 + 4 * m] if m else None
    i = pl.program_id(0)
    tf = wg_buf.shape[2]

    def column_tile(w_hbm, tile):
        return w_hbm.at[tile] if len(w_hbm.shape) == 3 else w_hbm.at[:, pl.ds(tile * tf, tf)]

    def copies(tile, slot):
        return (pltpu.make_async_copy(column_tile(wg_hbm, tile), wg_buf.at[slot], sem.at[0, slot]),
                pltpu.make_async_copy(column_tile(wu_hbm, tile), wu_buf.at[slot], sem.at[1, slot]),
                pltpu.make_async_copy(wd_hbm.at[tile], wd_buf.at[slot], sem.at[2, slot]))

    def cast_in(k):
        buf = cast_bufs[2 * k]
        rows = pl.ds(pl.multiple_of(i * buf.shape[0], 16), buf.shape[0])
        return pltpu.make_async_copy(cast_src[k].at[rows, :], buf, cast_sem.at[k, 0])

    def cast_out(k):
        buf = cast_bufs[2 * k + 1]
        rows = pl.ds(pl.multiple_of(i * buf.shape[0], 16), buf.shape[0])
        return pltpu.make_async_copy(buf, cast_dst[k].at[rows, :], cast_sem.at[k, 1])

    for k in range(m):
        cast_in(k).start()

    h = _rms(x_ref[...], gpre_ref[...]).astype(BF16)

    def step(j, slot):
        k = j - CAST_AT_STEP
        if 0 <= k < m:
            cast_in(k).wait()
            cast_bufs[2 * k + 1][...] = cast_bufs[2 * k][...].astype(BF16)
            cast_out(k).start()
        g = _dot(h, wg_buf[slot])
        u = _dot(h, wu_buf[slot])
        part = _dot((_silu(g) * u).astype(BF16), wd_buf[slot])
        if j == 0:
            o_ref[...] = part
        else:
            o_ref[...] += part

    _stream_weight_tiles(wd_hbm.shape[0], token_tiles, copies, step)
    for k in range(m):
        cast_out(k).wait()
    o_ref[...] = x_ref[...] + _rms(o_ref[...], FFN_RESIDUAL * gpost_ref[...])


def _ffn(x, g_pre, g_post, wg, wu, wd, side_casts=()):
    t = x.shape[0]
    tm = FFN_TOKEN_TILE
    tiles = t // tm
    n, tf, _ = wd.shape
    m = len(side_casts)
    assert n >= CAST_AT_STEP + m
    any_space = pl.BlockSpec(memory_space=pl.ANY)
    cast_shapes, cast_scratch = [], []
    for w in side_casts:
        r, c = w.shape
        rows = r // tiles
        assert rows * tiles == r and rows % 16 == 0
        cast_shapes.append(jax.ShapeDtypeStruct((r, c), BF16))
        cast_scratch += [pltpu.VMEM((rows, c), F32), pltpu.VMEM((rows, c), BF16)]
    if m:
        cast_scratch.append(pltpu.SemaphoreType.DMA((m, 2)))
    outs = pl.pallas_call(
        functools.partial(_ffn_kernel, token_tiles=tiles, n_casts=m),
        grid=(tiles,),
        in_specs=[
            pl.BlockSpec((tm, D_MODEL), lambda i: (i, 0)),
            pl.BlockSpec((1, D_MODEL), lambda i: (0, 0)),
            pl.BlockSpec((1, D_MODEL), lambda i: (0, 0)),
            any_space, any_space, any_space,
        ] + [any_space] * m,
        out_specs=[pl.BlockSpec((tm, D_MODEL), lambda i: (i, 0))] + [any_space] * m,
        out_shape=[jax.ShapeDtypeStruct((t, D_MODEL), F32)] + cast_shapes,
        scratch_shapes=[
            pltpu.VMEM((RING_SLOTS, D_MODEL, tf), BF16),
            pltpu.VMEM((RING_SLOTS, D_MODEL, tf), BF16),
            pltpu.VMEM((RING_SLOTS, tf, D_MODEL), BF16),
            pltpu.SemaphoreType.DMA((3, RING_SLOTS)),
        ] + cast_scratch,
        compiler_params=_params("arbitrary"),
        name="ffn_half",
    )(x, g_pre, g_post, wg, wu, wd, *side_casts)
    return tuple(outs)


def _in_proj_kernel(x_ref, g_ref, w_hbm, zb_hbm, zf_hbm, w_buf, sem, stage_b, stage_f, out_sem,
                    *, token_tiles):
    i = pl.program_id(0)
    n, _, tn = w_hbm.shape
    tm = x_ref.shape[0]
    nb = zb_hbm.shape[1] // tn
    rows = pl.ds(pl.multiple_of(i * tm, tm), tm)

    def copies(tile, slot):
        return (pltpu.make_async_copy(w_hbm.at[tile], w_buf.at[slot], sem.at[slot]),)

    def out_copy(j):
        if j < nb:
            s = j % 2
            return pltpu.make_async_copy(stage_b.at[s], zb_hbm.at[rows, pl.ds(j * tn, tn)],
                                         out_sem.at[0, s])
        s = (j - nb) % 2
        return pltpu.make_async_copy(stage_f.at[s], zf_hbm.at[rows, pl.ds((j - nb) * tn, tn)],
                                     out_sem.at[1, s])

    def same_stage(j):
        return range(0, nb) if j < nb else range(nb, n)

    h = _rms(x_ref[...], g_ref[...]).astype(BF16)

    def step(j, slot):
        z = _dot(h, w_buf[slot])
        if j - 2 in same_stage(j):
            out_copy(j - 2).wait()
        if j < nb:
            stage_b[j % 2] = z.astype(BF16)
        else:
            stage_f[(j - nb) % 2] = z
        out_copy(j).start()

    _stream_weight_tiles(n, token_tiles, copies, step)
    for j in range(n):
        if j + 2 not in same_stage(j):
            out_copy(j).wait()


def _in_proj(x, g, w):
    t = x.shape[0]
    tm = min(PROJ_TOKEN_TILE, t)
    n, _, tn = w.shape
    any_space = pl.BlockSpec(memory_space=pl.ANY)
    return pl.pallas_call(
        functools.partial(_in_proj_kernel, token_tiles=t // tm),
        grid=(t // tm,),
        in_specs=[
            pl.BlockSpec((tm, D_MODEL), lambda i: (i, 0)),
            pl.BlockSpec((1, D_MODEL), lambda i: (0, 0)),
            any_space,
        ],
        out_specs=[any_space, any_space],
        out_shape=[jax.ShapeDtypeStruct((t, ZB_WIDTH), BF16),
                   jax.ShapeDtypeStruct((t, ZF_WIDTH), F32)],
        scratch_shapes=[pltpu.VMEM((RING_SLOTS, D_MODEL, tn), BF16),
                        pltpu.SemaphoreType.DMA((RING_SLOTS,)),
                        pltpu.VMEM((2, tm, tn), BF16),
                        pltpu.VMEM((2, tm, tn), F32),
                        pltpu.SemaphoreType.DMA((2, 2))],
        compiler_params=_params("arbitrary"),
        name="in_proj",
    )(x, g, w)


def _attn_prompt_kernel(sink_ref, q_ref, kc_ref, kp_ref, vc_ref, vp_ref, gain_ref, o_ref):
    n = pl.program_id(1)
    nblk = q_ref.shape[0] // WINDOW
    scale = ATTN_HEAD_DIM ** -0.5
    row = lax.broadcasted_iota(jnp.int32, (WINDOW, 2 * WINDOW), 0)
    col = lax.broadcasted_iota(jnp.int32, (WINDOW, 2 * WINDOW), 1)
    dist = row + WINDOW - col
    in_window = (dist >= 0) & (dist < WINDOW)
    gain = gain_ref[...]

    def block(qrows, k, v, valid):
        k = (k * scale).astype(BF16)
        v = v.astype(BF16)
        neg_dist = jnp.where(valid, -dist.astype(F32), MASKED)
        outs = []
        for hd in range(ATTN_HEADS):
            kvh = hd // ATTN_GROUP
            kh = k[:, kvh * ATTN_HEAD_DIM:(kvh + 1) * ATTN_HEAD_DIM]
            vh = v[:, kvh * ATTN_HEAD_DIM:(kvh + 1) * ATTN_HEAD_DIM]
            qh = q_ref[qrows, hd * ATTN_HEAD_DIM:(hd + 1) * ATTN_HEAD_DIM]
            s = _dot_nt(qh, kh) + ALIBI_SLOPES[hd] * neg_dist
            sink = sink_ref[hd]
            m = jnp.maximum(jnp.max(s, axis=-1, keepdims=True), sink)
            p = jnp.exp(s - m)
            denom = jnp.sum(p, axis=-1, keepdims=True) + jnp.exp(sink - m)
            outs.append(_dot(p.astype(BF16), vh) / denom)
        o = jnp.concatenate(outs, axis=1)
        o_ref[qrows, :] = _rms(o, gain).astype(BF16)

    block(slice(0, WINDOW),
          jnp.concatenate([kp_ref[...], kc_ref[0:WINDOW, :]], axis=0),
          jnp.concatenate([vp_ref[...], vc_ref[0:WINDOW, :]], axis=0),
          in_window & ((col >= WINDOW) | (n > 0)))

    def later_block(blk, carry):
        first = pl.multiple_of((blk - 1) * WINDOW, WINDOW)
        kv_rows = pl.ds(first, 2 * WINDOW)
        block(pl.ds(first + WINDOW, WINDOW), kc_ref[kv_rows, :], vc_ref[kv_rows, :], in_window)
        return carry

    lax.fori_loop(1, nblk, later_block, 0)


def _attn_prompt(zb3, zf3, sinks, gain):
    b, s, _ = zb3.shape
    nblk = ATTN_STEP_BLOCKS
    rows = nblk * WINDOW

    def cur(width_idx):
        return lambda bi, n: (bi, n, width_idx)

    def prev(width_idx):
        return lambda bi, n: (bi, jnp.maximum(n * nblk - 1, 0), width_idx)

    return pl.pallas_call(
        _attn_prompt_kernel,
        grid=(b, s // rows),
        in_specs=[
            pl.BlockSpec(memory_space=pltpu.SMEM),
            pl.BlockSpec((None, rows, ATTN_WIDTH), cur(ZB_QA)),
            pl.BlockSpec((None, rows, KV_WIDTH), cur(ZF_KA)),
            pl.BlockSpec((None, WINDOW, KV_WIDTH), prev(ZF_KA)),
            pl.BlockSpec((None, rows, KV_WIDTH), cur(ZF_VA)),
            pl.BlockSpec((None, WINDOW, KV_WIDTH), prev(ZF_VA)),
            pl.BlockSpec((1, ATTN_WIDTH), lambda bi, n: (0, 0)),
        ],
        out_specs=pl.BlockSpec((None, rows, ATTN_WIDTH), lambda bi, n: (bi, n, 0)),
        out_shape=jax.ShapeDtypeStruct((b, s, ATTN_WIDTH), BF16),
        compiler_params=_params("parallel", "arbitrary"),
        name="attn_prompt",
    )(sinks, zb3, zf3, zf3, zf3, zf3, gain)


def _attn_sample_kernel(sink_ref, q_ref, kn_ref, vn_ref, ck_ref, cv_ref, gain_ref,
                        o_ref, wk_ref, wv_ref, *, t_new):
    seqs = ck_ref.shape[0]
    t_bits = _log2(t_new)
    rows = ATTN_GROUP * t_new
    q_all = q_ref[...].astype(F32).reshape(seqs, t_new, ATTN_WIDTH)
    kn = kn_ref[...].reshape(seqs, t_new, KV_WIDTH)
    vn = vn_ref[...].reshape(seqs, t_new, KV_WIDTH)
    ck = ck_ref[...]
    cv = cv_ref[...]
    wk_ref[:, :WINDOW - t_new, :] = ck[:, t_new:, :]
    wk_ref[:, WINDOW - t_new:, :] = kn
    wv_ref[:, :WINDOW - t_new, :] = cv[:, t_new:, :]
    wv_ref[:, WINDOW - t_new:, :] = vn

    r_c = lax.broadcasted_iota(jnp.int32, (rows, WINDOW), 0)
    j_c = lax.broadcasted_iota(jnp.int32, (rows, WINDOW), 1)
    dist_c = (r_c & (t_new - 1)) + WINDOW - j_c
    neg_c = jnp.where(dist_c < WINDOW, -dist_c.astype(F32), MASKED)
    r_n = lax.broadcasted_iota(jnp.int32, (rows, t_new), 0)
    j_n = lax.broadcasted_iota(jnp.int32, (rows, t_new), 1)
    dist_n = (r_n & (t_new - 1)) - j_n
    neg_n = jnp.where(dist_n >= 0, -dist_n.astype(F32), MASKED)
    head_of_row = lax.broadcasted_iota(jnp.int32, (rows, 1), 0) >> t_bits
    scale = ATTN_HEAD_DIM ** -0.5

    outs = [None] * ATTN_HEADS
    for kvh in range(ATTN_KV_HEADS):
        slope = jnp.zeros((rows, 1), F32)
        sink = jnp.zeros((rows, 1), F32)
        for g in range(ATTN_GROUP):
            hd = kvh * ATTN_GROUP + g
            slope = jnp.where(head_of_row == g, ALIBI_SLOPES[hd], slope)
            sink = jnp.where(head_of_row == g, sink_ref[hd], sink)
        q = jnp.concatenate(
            [q_all[:, :, (kvh * ATTN_GROUP + g) * ATTN_HEAD_DIM:(kvh * ATTN_GROUP + g + 1) * ATTN_HEAD_DIM]
             for g in range(ATTN_GROUP)], axis=1).astype(BF16)
        sl = slice(kvh * ATTN_HEAD_DIM, (kvh + 1) * ATTN_HEAD_DIM)
        ck_h = (ck[:, :, sl] * scale).astype(BF16)
        kn_h = (kn[:, :, sl] * scale).astype(BF16)
        cv_h = cv[:, :, sl].astype(BF16)
        vn_h = vn[:, :, sl].astype(BF16)
        s_c = jnp.einsum('gqd,gkd->gqk', q, ck_h, preferred_element_type=F32) + (slope * neg_c)[None]
        s_n = jnp.einsum('gqd,gkd->gqk', q, kn_h, preferred_element_type=F32) + (slope * neg_n)[None]
        m = jnp.maximum(jnp.maximum(jnp.max(s_c, axis=-1, keepdims=True),
                                    jnp.max(s_n, axis=-1, keepdims=True)), sink[None])
        p_c = jnp.exp(s_c - m)
        p_n = jnp.exp(s_n - m)
        denom = (jnp.sum(p_c, axis=-1, keepdims=True) + jnp.sum(p_n, axis=-1, keepdims=True)
                 + jnp.exp(sink[None] - m))
        o = (jnp.einsum('gqk,gkd->gqd', p_c.astype(BF16), cv_h, preferred_element_type=F32)
             + jnp.einsum('gqk,gkd->gqd', p_n.astype(BF16), vn_h, preferred_element_type=F32)) / denom
        for g in range(ATTN_GROUP):
            outs[kvh * ATTN_GROUP + g] = o[:, g * t_new:(g + 1) * t_new, :]
    o_all = jnp.concatenate(outs, axis=2)
    o_ref[...] = _rms(o_all, gain_ref[...][None]).reshape(seqs * t_new, ATTN_WIDTH)


def _attn_sample(zb, zf, cache_k, cache_v, sinks, gain, t_new):
    b = cache_k.shape[0]
    gs = ATTN_SAMPLE_SEQS
    tok = lambda width, idx: pl.BlockSpec((gs * t_new, width), lambda i: (i, idx))
    cache_spec = pl.BlockSpec((gs, WINDOW, KV_WIDTH), lambda i: (i, 0, 0))
    return pl.pallas_call(
        functools.partial(_attn_sample_kernel, t_new=t_new),
        grid=(b // gs,),
        in_specs=[
            pl.BlockSpec(memory_space=pltpu.SMEM),
            tok(ATTN_WIDTH, ZB_QA), tok(KV_WIDTH, ZF_KA), tok(KV_WIDTH, ZF_VA),
            cache_spec, cache_spec,
            pl.BlockSpec((1, ATTN_WIDTH), lambda i: (0, 0)),
        ],
        out_specs=[tok(ATTN_WIDTH, 0), cache_spec, cache_spec],
        out_shape=[
            jax.ShapeDtypeStruct((b * t_new, ATTN_WIDTH), F32),
            jax.ShapeDtypeStruct((b, WINDOW, KV_WIDTH), F32),
            jax.ShapeDtypeStruct((b, WINDOW, KV_WIDTH), F32),
        ],
        compiler_params=_params("parallel"),
        name="attn_sample",
    )(sinks, zb, zf, zf, cache_k, cache_v, gain)


def _lower_bound(lbl_ref):
    logits = lbl_ref[...]
    e = jnp.exp(logits - jnp.max(logits, axis=0, keepdims=True))
    return e[0:1, :] / jnp.sum(e, axis=0, keepdims=True)


def _hgrn_features(q_raw, f_raw, lb):
    sg = jax.nn.sigmoid(f_raw)
    f = lb + (1.0 - lb) * sg
    k = (1.0 - lb) * (1.0 - sg)
    return _silu(q_raw), k, f, jnp.log2(f)


def _cumsum_rows(x, segment):
    row = lax.broadcasted_iota(jnp.int32, x.shape, 0) & (segment - 1)
    shift = 1
    while shift < segment:
        x = x + jnp.where(row >= shift, pltpu.roll(x, shift, axis=0), 0.0)
        shift *= 2
    return x


def _split3(x):
    hi = x.astype(BF16)
    r = x - hi.astype(F32)
    mid = r.astype(BF16)
    lo = (r - mid.astype(F32)).astype(BF16)
    return hi, mid, lo


def _cumsum_rows_mxu(x, tri):
    hi, mid, lo = _split3(x)
    return _dot(tri, hi) + _dot(tri, mid) + _dot(tri, lo)


def _column_broadcast(row_vec, n):
    rows = 16
    hi, mid, lo = _split3(row_vec)
    terms = jnp.concatenate([hi, mid, lo, jnp.zeros((rows - 3, n), BF16)], axis=0)
    ones = (lax.broadcasted_iota(jnp.int32, (rows, n), 0) < 3).astype(BF16)
    return _dot_tn(terms, ones)


def _hgrn_out(o, gain, g_raw):
    return _rms(o, gain) * _silu(g_raw)


def _pair_levels(c):
    t = lax.broadcasted_iota(jnp.int32, (c, c), 0)
    s = lax.broadcasted_iota(jnp.int32, (c, c), 1)
    masks = [t == s]
    half = 1
    while half < c:
        block = 2 * half
        same_block = (t >> _log2(block)) == (s >> _log2(block))
        masks.append(same_block & ((t & (block - 1)) >= half) & ((s & (block - 1)) < half))
        half = block
    return masks


def _pair_factors(q, k, f, log2f, l2):
    c, w = q.shape
    pos = lax.broadcasted_iota(jnp.int32, (c, w), 0) & 3
    e2 = jnp.exp2(jnp.where(pos == 0, pltpu.roll(log2f, c - 1, axis=0),
                            jnp.where(pos == 1, 0.0,
                                      jnp.where(pos == 2, log2f,
                                                log2f + pltpu.roll(log2f, 1, axis=0)))))
    qb, kb = q.astype(BF16), k.astype(BF16)
    e2 = e2.astype(BF16)
    qs = [qb, qb * f.astype(BF16), qb * e2]
    ks = [kb, kb, kb * e2]
    half = 4
    while half < c:
        block = 2 * half
        ref = l2.reshape(c // block, block, w)[:, half - 1:half, :]
        ref = jnp.broadcast_to(ref, (c // block, block, w)).reshape(c, w)
        e = jnp.exp2(-jnp.abs(l2 - ref)).astype(BF16)
        qs.append(qb * e)
        ks.append(kb * e)
        half = block
    return qs, ks


def _hgrn_prompt_kernel(lbl_ref, q_ref, f_ref, i_ref, g_ref, gain_ref, r_ref, s_ref):
    c = HGRN_CHUNK

    @pl.when(pl.program_id(1) == 0)
    def _():
        s_ref[...] = jnp.zeros_like(s_ref)

    lb = _lower_bound(lbl_ref)
    q_all, k_all, f_all, log2f_all = _hgrn_features(q_ref[...].astype(F32), f_ref[...], lb)
    masks = _pair_levels(c)
    tri = (lax.broadcasted_iota(jnp.int32, (c, c), 1)
           <= lax.broadcasted_iota(jnp.int32, (c, c), 0)).astype(BF16)
    gain = gain_ref[...]

    chunks = []
    for ci in range(HGRN_STEP_CHUNKS):
        rows = slice(ci * c, (ci + 1) * c)
        l2_c = _cumsum_rows_mxu(log2f_all[rows], tri)
        q_lv, k_lv = _pair_factors(q_all[rows], k_all[rows], f_all[rows], log2f_all[rows], l2_c)
        l_end = l2_c[c - 1:c, :]
        q_hat = (q_all[rows] * jnp.exp2(l2_c)).astype(BF16)
        k_end = (k_all[rows] * jnp.exp2(l_end - l2_c)).astype(BF16)
        chunks.append((rows, q_lv, k_lv, q_hat, k_end, jnp.exp2(l_end)))

    for rows, q_lv, k_lv, q_hat, k_end, decay_end in chunks:
        for h in range(HGRN_HEADS):
            sl = slice(h * HGRN_DK, (h + 1) * HGRN_DK)
            vb = i_ref[rows, sl]
            a = jnp.zeros((c, c), F32)
            for mask, q_f, k_f in zip(masks, q_lv, k_lv):
                a = jnp.where(mask, _dot_nt(q_f[:, sl], k_f[:, sl]), a)
            s_old = s_ref[h]
            o = _dot(a.astype(BF16), vb) + _dot(q_hat[:, sl], s_old.astype(BF16))
            s_ref[h] = (s_old * _column_broadcast(decay_end[:, sl], HGRN_DK)
                        + _dot_tn(k_end[:, sl], vb))
            r_ref[rows, sl] = _hgrn_out(o, gain, g_ref[rows, sl].astype(F32)).astype(r_ref.dtype)


def _hgrn_prompt(zb3, zf3, lb_logits, gain):
    b, s, _ = zb3.shape
    rows = HGRN_CHUNK * HGRN_STEP_CHUNKS
    seg = lambda idx: pl.BlockSpec((None, rows, HGRN_KW), lambda bi, ci: (bi, ci, idx))
    return pl.pallas_call(
        _hgrn_prompt_kernel,
        grid=(b, s // rows),
        in_specs=[
            pl.BlockSpec(lb_logits.shape, lambda bi, ci: (0, 0)),
            seg(ZB_QH), seg(ZF_FH), seg(ZB_IH), seg(ZB_GH),
            pl.BlockSpec((1, HGRN_DV), lambda bi, ci: (0, 0)),
        ],
        out_specs=[
            pl.BlockSpec((None, rows, HGRN_VW), lambda bi, ci: (bi, ci, 0)),
            pl.BlockSpec((None, HGRN_HEADS, HGRN_DK, HGRN_DV), lambda bi, ci: (bi, 0, 0, 0)),
        ],
        out_shape=[
            jax.ShapeDtypeStruct((b, s, HGRN_VW), BF16),
            jax.ShapeDtypeStruct((b, HGRN_HEADS, HGRN_DK, HGRN_DV), F32),
        ],
        compiler_params=_params("parallel", "arbitrary"),
        name="hgrn_prompt",
    )(lb_logits, zb3, zf3, zb3, zb3, gain)


def _hgrn_sample_kernel(lbl_ref, seg_ref, segt_ref, q_ref, f_ref, i_ref, g_ref, gain_ref, s0_ref,
                        r_ref, s_ref, *, t_new):
    seqs = s0_ref.shape[0]
    n = seqs * t_new
    width = HGRN_KW
    lb = _lower_bound(lbl_ref)
    q_all, k_all, _, log2f_all = _hgrn_features(q_ref[...].astype(F32), f_ref[...], lb)
    l2_all = _cumsum_rows(log2f_all, t_new)
    v_all = i_ref[...].astype(F32)
    g_all = g_ref[...].astype(F32)
    gain = gain_ref[...]

    def row_of_seq(x, t):
        x3 = x.reshape(seqs, t_new, width)
        return jnp.broadcast_to(x3[:, t:t + 1, :], (seqs, t_new, width)).reshape(n, width)

    src = lax.broadcasted_iota(jnp.int32, (n, width), 0) & (t_new - 1)
    w_parts = []
    for t in range(t_new):
        decay = jnp.exp2(jnp.minimum(row_of_seq(l2_all, t) - l2_all, 0.0))
        w_parts.append(jnp.where(src <= t, decay * k_all * row_of_seq(q_all, t), 0.0))
    w = jnp.concatenate(w_parts, axis=0).astype(BF16)
    a_heads = _dot(w, seg_ref[...])
    a_lanes = _dot(a_heads.astype(BF16), segt_ref[...])
    o_rows = []
    for t in range(t_new):
        p = (a_lanes[t * n:(t + 1) * n] * v_all).reshape(seqs, t_new, width)
        o_rows.append(jnp.sum(p, axis=1, keepdims=True))
    o_intra = jnp.concatenate(o_rows, axis=1).reshape(n, width)

    l_end = row_of_seq(l2_all, t_new - 1)
    q_hat = q_all * jnp.exp2(l2_all)
    k_end = k_all * jnp.exp2(l_end - l2_all)
    decay_end = jnp.exp2(l_end)
    for h in range(HGRN_HEADS):
        sl = slice(h * HGRN_DK, (h + 1) * HGRN_DK)
        o_parts = []
        for sq in range(seqs):
            rows = slice(sq * t_new, (sq + 1) * t_new)
            s_old = s0_ref[sq, h]
            o_parts.append(_dot(q_hat[rows, sl].astype(BF16), s_old.astype(BF16)))
            s_ref[sq, h] = (s_old * _column_broadcast(decay_end[sq * t_new:sq * t_new + 1, sl], HGRN_DK)
                            + _dot_tn(k_end[rows, sl].astype(BF16), v_all[rows, sl].astype(BF16)))
        o = o_intra[:, sl] + jnp.concatenate(o_parts, axis=0)
        r_ref[:, sl] = _hgrn_out(o, gain, g_all[:, sl])


def _hgrn_sample(zb, zf, state, lb_logits, gain, t_new):
    b = state.shape[0]
    gs = HGRN_SAMPLE_SEQS
    seg = lambda idx: pl.BlockSpec((gs * t_new, HGRN_KW), lambda i: (i, idx))
    state_spec = pl.BlockSpec((gs, HGRN_HEADS, HGRN_DK, HGRN_DV), lambda i: (i, 0, 0, 0))
    lane_head = jnp.arange(HGRN_KW, dtype=jnp.int32) // HGRN_DK
    head_seg = (lane_head[:, None] == jnp.arange(HGRN_DK, dtype=jnp.int32)[None, :]).astype(BF16)
    return pl.pallas_call(
        functools.partial(_hgrn_sample_kernel, t_new=t_new),
        grid=(b // gs,),
        in_specs=[
            pl.BlockSpec(lb_logits.shape, lambda i: (0, 0)),
            pl.BlockSpec((HGRN_KW, HGRN_DK), lambda i: (0, 0)),
            pl.BlockSpec((HGRN_DK, HGRN_KW), lambda i: (0, 0)),
            seg(ZB_QH), seg(ZF_FH), seg(ZB_IH), seg(ZB_GH),
            pl.BlockSpec((1, HGRN_DV), lambda i: (0, 0)),
            state_spec,
        ],
        out_specs=[seg(0), state_spec],
        out_shape=[
            jax.ShapeDtypeStruct((b * t_new, HGRN_VW), F32),
            jax.ShapeDtypeStruct((b, HGRN_HEADS, HGRN_DK, HGRN_DV), F32),
        ],
        compiler_params=_params("parallel"),
        name="hgrn_sample",
    )(lb_logits, head_seg, head_seg.T, zb, zf, zb, zb, gain, state)


def _mix_out_kernel(x_ref, a_ref, r_ref, wa_ref, wr_ref, gpost_ref, gmem_ref, wq_ref,
                    x2_ref, qm_ref):
    mixed = (_dot(a_ref[...].astype(BF16), wa_ref[...])
             + _dot(r_ref[...].astype(BF16), wr_ref[...]))
    x2 = x_ref[...] + _rms(mixed, gpost_ref[...])
    x2_ref[...] = x2
    qm_ref[...] = _dot(_rms(x2, gmem_ref[...]).astype(BF16), wq_ref[...])


def _mix_out(x, a, r, w_out, g_post, g_mem, w_q):
    t = x.shape[0]
    tm = MIX_TOKEN_TILE
    tok = lambda width: pl.BlockSpec((tm, width), lambda i: (i, 0))
    const = lambda shape, idx=(0, 0): pl.BlockSpec(shape, lambda i: idx)
    return pl.pallas_call(
        _mix_out_kernel,
        grid=(t // tm,),
        in_specs=[
            tok(D_MODEL), tok(ATTN_WIDTH), tok(HGRN_VW),
            const((ATTN_WIDTH, D_MODEL), (0, 0)),
            const((HGRN_VW, D_MODEL), (1, 0)),
            const((1, D_MODEL)), const((1, D_MODEL)),
            const((D_MODEL, MEM_WIDTH)),
        ],
        out_specs=[tok(D_MODEL), tok(MEM_WIDTH)],
        out_shape=[jax.ShapeDtypeStruct((t, D_MODEL), F32),
                   jax.ShapeDtypeStruct((t, MEM_WIDTH), F32)],
        compiler_params=_params("parallel", fuse_inputs=[False, False, False, True, True,
                                                         False, False, True]),
        name="mix_out",
    )(x, a, r, w_out, w_out, g_post, g_mem, w_q)


def _mem_kv_kernel(m_ref, g_ref, wk_ref, wv_ref, k_ref, v_ref, kb_ref, vb_ref):
    h = _rms(m_ref[...], g_ref[...]).astype(BF16)
    k = _dot(h, wk_ref[...])
    v = _dot(h, wv_ref[...])
    k_ref[...] = k
    v_ref[...] = v
    kb_ref[...] = k.astype(BF16)
    vb_ref[...] = v.astype(BF16)


def _mem_kv(mem, g, w_k, w_v):
    b, m, _ = mem.shape
    const = lambda shape: pl.BlockSpec(shape, lambda i: (0, 0))
    out = pl.BlockSpec((None, m, MEM_WIDTH), lambda i: (i, 0, 0))
    return pl.pallas_call(
        _mem_kv_kernel,
        grid=(b,),
        in_specs=[pl.BlockSpec((None, m, D_MODEL), lambda i: (i, 0, 0)), const((1, D_MODEL)),
                  const((D_MODEL, MEM_WIDTH)), const((D_MODEL, MEM_WIDTH))],
        out_specs=[out] * 4,
        out_shape=[jax.ShapeDtypeStruct((b, m, MEM_WIDTH), F32)] * 2
        + [jax.ShapeDtypeStruct((b, m, MEM_WIDTH), BF16)] * 2,
        compiler_params=_params("parallel", fuse_inputs=[False, False, True, True]),
        name="mem_kv",
    )(mem, g, w_k, w_v)


def _mem_attn_finish(o, x_ref, wo_ref, g_ref, o_ref):
    seqs, tq, _ = x_ref.shape
    y = _dot(o.reshape(seqs * tq, MEM_WIDTH).astype(BF16), wo_ref[...])
    x = x_ref[...].reshape(seqs * tq, D_MODEL)
    o_ref[...] = (x + _rms(y, g_ref[...])).reshape(seqs, tq, D_MODEL)


def _mem_attn_kernel(x_ref, q_ref, mk_ref, mv_ref, wo_ref, g_ref, o_ref):
    scale = MEM_HEAD_DIM ** -0.5
    outs = []
    for h in range(MEM_HEADS):
        sl = slice(h * MEM_HEAD_DIM, (h + 1) * MEM_HEAD_DIM)
        q = q_ref[:, :, sl].astype(BF16)
        s = jnp.einsum('gqd,gkd->gqk', q, mk_ref[:, :, sl], preferred_element_type=F32) * scale
        p = jnp.exp(s - jnp.max(s, axis=-1, keepdims=True))
        denom = jnp.sum(p, axis=-1, keepdims=True)
        outs.append(jnp.einsum('gqk,gkd->gqd', p.astype(BF16), mv_ref[:, :, sl],
                               preferred_element_type=F32) / denom)
    _mem_attn_finish(jnp.concatenate(outs, axis=2), x_ref, wo_ref, g_ref, o_ref)


def _mem_attn_interleaved_kernel(x_ref, q_ref, mk_ref, mv_ref, wo_ref, g_ref, o_ref):
    seqs, tq, _ = q_ref.shape
    rows = MEM_HEADS * tq
    cols = mk_ref.shape[1]
    scale = MEM_HEAD_DIM ** -0.5
    q = jnp.concatenate([q_ref[:, :, h * MEM_HEAD_DIM:(h + 1) * MEM_HEAD_DIM]
                         for h in range(MEM_HEADS)], axis=1).astype(BF16)
    s = jnp.einsum('gqd,gkd->gqk', q, mk_ref[...].astype(BF16), preferred_element_type=F32) * scale
    row_head = lax.broadcasted_iota(jnp.int32, (rows, cols), 0) >> _log2(tq)
    col_head = lax.broadcasted_iota(jnp.int32, (rows, cols), 1) & (MEM_HEADS - 1)
    s = jnp.where((row_head == col_head)[None], s, -jnp.inf)
    p = jnp.exp(s - jnp.max(s, axis=-1, keepdims=True))
    denom = jnp.sum(p, axis=-1, keepdims=True)
    o = jnp.einsum('gqk,gkd->gqd', p.astype(BF16), mv_ref[...].astype(BF16),
                   preferred_element_type=F32) / denom
    o = jnp.concatenate([o[:, h * tq:(h + 1) * tq, :] for h in range(MEM_HEADS)], axis=2)
    _mem_attn_finish(o, x_ref, wo_ref, g_ref, o_ref)


def _mem_attn(body, x3, q3, mem_k, mem_v, w_o, g_post, seqs, tq):
    nseq, slen, _ = x3.shape
    tok = lambda width: pl.BlockSpec((seqs, tq, width), lambda i, j: (i, j, 0))
    mem = pl.BlockSpec((seqs,) + mem_k.shape[1:], lambda i, j: (i, 0, 0))
    const = lambda shape: pl.BlockSpec(shape, lambda i, j: (0, 0))
    return pl.pallas_call(
        body,
        grid=(nseq // seqs, slen // tq),
        in_specs=[tok(D_MODEL), tok(MEM_WIDTH), mem, mem,
                  const((MEM_WIDTH, D_MODEL)), const((1, D_MODEL))],
        out_specs=tok(D_MODEL),
        out_shape=jax.ShapeDtypeStruct((nseq, slen, D_MODEL), F32),
        compiler_params=_params("parallel", "arbitrary",
                                fuse_inputs=[False, False, False, False, True, False]),
        name="mem_attn",
    )(x3, q3, mem_k, mem_v, w_o, g_post)


def _row(g):
    return g.reshape(1, -1)


def _cast_tiles_kernel(order_ref, w_ref, o_ref):
    del order_ref
    o_ref[...] = w_ref[...].astype(BF16)


def _cast_col_tiles(w, tile, order=None):
    k, n = w.shape
    nt = n // tile
    order = jnp.arange(nt, dtype=jnp.int32) if order is None else jnp.asarray(order, jnp.int32)
    return pl.pallas_call(
        _cast_tiles_kernel,
        grid_spec=pltpu.PrefetchScalarGridSpec(
            num_scalar_prefetch=1,
            grid=(nt,),
            in_specs=[pl.BlockSpec((k, tile), lambda j, order_ref: (0, order_ref[j]))],
            out_specs=pl.BlockSpec((None, k, tile), lambda j, order_ref: (j, 0, 0)),
        ),
        out_shape=jax.ShapeDtypeStruct((nt, k, tile), BF16),
        compiler_params=_params("parallel"),
        name="cast_col_tiles",
    )(order, w)


def _trunk_front(x2d, w, side_casts=()):
    x1, *converted = _ffn(x2d, w['ffn1_pre'], w['ffn1_post'], w['ffn1_wg'], w['ffn1_wu'],
                          w['ffn1_wd'], side_casts=side_casts)
    zb, zf = _in_proj(x1, w['mix_pre'], w['w_in'])
    return x1, zb, zf, converted


def _trunk_back(x1, a, r, mem_body, mem_k, mem_v, w, nseq, seqs, tq):
    t = x1.shape[0]
    x2, qm = _mix_out(x1, a.reshape(t, ATTN_WIDTH), r.reshape(t, HGRN_VW), w['w_out'],
                      w['mix_post'], w['mem_pre'], w['w_mem_q'])
    x3 = _mem_attn(mem_body, x2.reshape(nseq, t // nseq, D_MODEL),
                   qm.reshape(nseq, t // nseq, MEM_WIDTH),
                   mem_k, mem_v, w['w_mem_o'], w['mem_post'], seqs, tq)
    (y,) = _ffn(x3.reshape(t, D_MODEL), w['ffn2_pre'], w['ffn2_post'],
                w['ffn2_wg'], w['ffn2_wu'], w['ffn2_wd'])
    return y


def kernel(x_prompt, x_sample, mem_prompt, cache_win_k, cache_win_v, state_hgrn, cache_mem_k, cache_mem_v, ffn1_norm_pre, ffn1_norm_post, ffn1_w_gate, ffn1_w_up, ffn1_w_down, mix_norm_pre, mix_norm_post, w_in, attn_sinks, hgrn_lb_logits, attn_out_gain, hgrn_out_gain, w_out, mem_norm_pre, mem_norm_post, mem_norm_kv, w_mem_q, w_mem_k, w_mem_v, w_mem_o, ffn2_norm_pre, ffn2_norm_post, ffn2_w_gate, ffn2_w_up, ffn2_w_down):
    bp, sp, _ = x_prompt.shape
    bs, ts, _ = x_sample.shape
    mt = mem_prompt.shape[1]
    assert w_in.shape[0] == 1 and hgrn_lb_logits.shape[0] == 2, "one layer (depth 1) is implemented"
    assert (bp * sp) % PROJ_TOKEN_TILE == 0 and (bs * ts) % FFN_TOKEN_TILE == 0
    assert sp % (ATTN_STEP_BLOCKS * WINDOW) == 0 and sp % (HGRN_STEP_CHUNKS * HGRN_CHUNK) == 0
    assert bs % ATTN_SAMPLE_SEQS == 0 and bs % HGRN_SAMPLE_SEQS == 0 and bs % 8 == 0
    l = 0

    seg_widths = dict(qa=ATTN_WIDTH, kv=2 * KV_WIDTH, qh=HGRN_KW, fh=HGRN_KW, ih=HGRN_VW, gh=HGRN_VW)
    seg_tiles, start = {}, 0
    for name in ('qa', 'kv', 'qh', 'fh', 'ih', 'gh'):
        n_tiles = seg_widths[name] // PROJ_COL_TILE
        seg_tiles[name] = list(range(start, start + n_tiles))
        start += n_tiles
    w_in_order = sum((seg_tiles[name] for name in ('qa', 'qh', 'ih', 'gh', 'fh', 'kv')), [])
    w = dict(
        ffn1_pre=_row(ffn1_norm_pre[l]), ffn1_post=_row(ffn1_norm_post[l]),
        ffn1_wg=_cast_col_tiles(ffn1_w_gate[l], FF_TILE), ffn1_wu=_cast_col_tiles(ffn1_w_up[l], FF_TILE),
        ffn1_wd=ffn1_w_down[l].astype(BF16).reshape(D_FF // FF_TILE, FF_TILE, D_MODEL),
        mix_pre=_row(mix_norm_pre[l]), mix_post=_row(mix_norm_post[l]),
        w_in=_cast_col_tiles(w_in[l], PROJ_COL_TILE, w_in_order), w_out=w_out[l].astype(BF16),
        mem_pre=_row(mem_norm_pre[l]), mem_post=_row(mem_norm_post[l]),
        w_mem_q=w_mem_q[l].astype(BF16), w_mem_o=w_mem_o[l].astype(BF16),
        ffn2_pre=_row(ffn2_norm_pre[l]), ffn2_post=_row(ffn2_norm_post[l]),
    )
    ffn2_casts = (ffn2_w_gate[l], ffn2_w_up[l], ffn2_w_down[l])
    sinks = attn_sinks[l]
    attn_gain = _row(attn_out_gain[l])
    hgrn_gain = _row(hgrn_out_gain[l])

    mk, mv, mk_b, mv_b = _mem_kv(mem_prompt, _row(mem_norm_kv[l]),
                                 w_mem_k[l].astype(BF16), w_mem_v[l].astype(BF16))
    x1, zb, zf, (wg2, wu2, wd2) = _trunk_front(x_prompt.reshape(bp * sp, D_MODEL), w, ffn2_casts)
    w.update(ffn2_wg=wg2, ffn2_wu=wu2, ffn2_wd=wd2.reshape(D_FF // FF_TILE, FF_TILE, D_MODEL))
    zb3 = zb.reshape(bp, sp, ZB_WIDTH)
    zf3 = zf.reshape(bp, sp, ZF_WIDTH)
    a = _attn_prompt(zb3, zf3, sinks, attn_gain)
    r, p_state = _hgrn_prompt(zb3, zf3, hgrn_lb_logits, hgrn_gain)
    y_p = _trunk_back(x1, a, r, _mem_attn_kernel, mk_b, mv_b, w, bp, 1, 1024).reshape(bp, sp, D_MODEL)
    k_off = ZF_KA * KV_WIDTH
    v_off = ZF_VA * KV_WIDTH
    p_wk = zf3[:, sp - WINDOW:, k_off:k_off + KV_WIDTH]
    p_wv = zf3[:, sp - WINDOW:, v_off:v_off + KV_WIDTH]

    x1s, zbs, zfs, _ = _trunk_front(x_sample.reshape(bs * ts, D_MODEL), w)
    a_s, s_wk, s_wv = _attn_sample(zbs, zfs, cache_win_k[l].reshape(bs, WINDOW, KV_WIDTH),
                                   cache_win_v[l].reshape(bs, WINDOW, KV_WIDTH), sinks, attn_gain, ts)
    r_s, s_state = _hgrn_sample(zbs, zfs, state_hgrn[l], hgrn_lb_logits, hgrn_gain, ts)
    mem_rows = cache_mem_k.shape[2] * MEM_HEADS
    y_s = _trunk_back(x1s, a_s, r_s, _mem_attn_interleaved_kernel,
                      cache_mem_k[l].reshape(bs, mem_rows, MEM_HEAD_DIM),
                      cache_mem_v[l].reshape(bs, mem_rows, MEM_HEAD_DIM),
                      w, bs, 8, ts).reshape(bs, ts, D_MODEL)

    kv5 = lambda t, n: t.reshape(1, n, WINDOW, ATTN_KV_HEADS, ATTN_HEAD_DIM)
    mem5 = lambda t: t.reshape(1, bp, mt, MEM_HEADS, MEM_HEAD_DIM)
    return (y_p, y_s, kv5(p_wk, bp), kv5(p_wv, bp), p_state[None], mem5(mk), mem5(mv),
            kv5(s_wk, bs), kv5(s_wv, bs), s_state[None])
```
